```python
import math
import jax, jax.numpy as jnp
from jax import lax
import numpy as np

D_MODEL = 2048
BATCH = 4
SEQ = 2048
DEPTH = 4

N_MIXERS = 4
MIX_WIDTH = D_MODEL
GROUP_W = MIX_WIDTH // N_MIXERS
S5_CH_PER_GROUP = 16
S5_GROUPS = GROUP_W // S5_CH_PER_GROUP
S5_STATE = 64
S5_DT_MIN = 0.001
S5_DT_MAX = 0.1
POOL_WINDOWS = (2, 4, 8, 16)
POOL_CH = GROUP_W // len(POOL_WINDOWS)
CONV_WIDTH = 31
ATT_HEADS = 8
ATT_HEAD_DIM = GROUP_W // ATT_HEADS
DILATED_PATTERNS = ((128, 1), (512, 4), (2048, 16))
ATT_BLOCK = 128
REL_BUCKETS = 32
REL_MAX_DIST = 2048
MEM_LEN = 256
X_HEADS = 4
X_HEAD_DIM = 128
X_WIDTH = X_HEADS * X_HEAD_DIM
D_FF = 4 * D_MODEL
NORM_EPS = 1e-6
NEG_INF = -1e30
IN_WIDTH = GROUP_W + GROUP_W + 2 * GROUP_W + 3 * GROUP_W

kernel_name = 'hymba_style_multimixer_trunk'

F32 = jnp.float32


def rmsnorm(x, g):
    xf = x.astype(F32)
    y = xf * lax.rsqrt(jnp.mean(xf * xf, axis=-1, keepdims=True) + NORM_EPS)
    return (y * g.astype(F32)).astype(x.dtype)


def layernorm(x, g, b):
    xf = x.astype(F32)
    xc = xf - jnp.mean(xf, axis=-1, keepdims=True)
    y = xc * lax.rsqrt(jnp.mean(xc * xc, axis=-1, keepdims=True) + NORM_EPS)
    return (y * g.astype(F32) + b.astype(F32)).astype(x.dtype)


def group_rmsnorm(y, g, out_dtype):
    Bsz, L, _ = y.shape
    yf = y.astype(F32).reshape(Bsz, L, N_MIXERS, GROUP_W)
    yf = yf * lax.rsqrt(jnp.mean(yf * yf, axis=-1, keepdims=True) + NORM_EPS)
    return (yf.reshape(Bsz, L, MIX_WIDTH) * g.astype(F32)).astype(out_dtype)


def _cmul(ar, ai, br, bi):
    return ar * br - ai * bi, ar * bi + ai * br


def s5_mixer(u, lam_re, lam_im, log_dt, b_re, b_im, c_re, c_im, d_skip, w_glu):
    Bsz, L, _ = u.shape
    uf = u.astype(F32).reshape(Bsz, L, S5_GROUPS, S5_CH_PER_GROUP)
    lr = lam_re.astype(F32)
    li = lam_im.astype(F32)
    dt = jnp.exp(log_dt.astype(F32))[:, None]
    mag = jnp.exp(lr * dt)
    ab_r, ab_i = mag * jnp.cos(li * dt), mag * jnp.sin(li * dt)
    den = lr * lr + li * li
    nr, ni = ab_r - 1.0, ab_i
    f_r = (nr * lr + ni * li) / den
    f_i = (ni * lr - nr * li) / den
    bb_r, bb_i = _cmul(f_r[..., None], f_i[..., None], b_re.astype(F32), b_im.astype(F32))
    bu_r = jnp.einsum('gnc,blgc->blgn', bb_r, uf)
    bu_i = jnp.einsum('gnc,blgc->blgn', bb_i, uf)
    a_r = jnp.broadcast_to(ab_r, bu_r.shape)
    a_i = jnp.broadcast_to(ab_i, bu_i.shape)

    def combine(e1, e2):
        a1r, a1i, b1r, b1i = e1
        a2r, a2i, b2r, b2i = e2
        ar, ai = _cmul(a2r, a2i, a1r, a1i)
        br, bi = _cmul(a2r, a2i, b1r, b1i)
        return ar, ai, br + b2r, bi + b2i

    _, _, xr, xi = lax.associative_scan(combine, (a_r, a_i, bu_r, bu_i), axis=1)
    y = jnp.einsum('gcn,blgn->blgc', c_re.astype(F32), xr) - jnp.einsum('gcn,blgn->blgc', c_im.astype(F32), xi)
    y = y.reshape(Bsz, L, GROUP_W) + d_skip.astype(F32) * uf.reshape(Bsz, L, GROUP_W)
    g = jax.nn.gelu(y)
    out = g * jax.nn.sigmoid(jnp.einsum('blc,cd->bld', g, w_glu.astype(F32)))
    return out.astype(u.dtype)


def pool_mixer(u, pool_w, pool_scale):
    Bsz, L, _ = u.shape
    uf = u.astype(F32).reshape(Bsz, L, len(POOL_WINDOWS), POOL_CH)
    cs = jnp.cumsum(uf, axis=1)
    t = jnp.arange(L)
    pooled = []
    for gi, w in enumerate(POOL_WINDOWS):
        c = cs[:, :, gi]
        shifted = jnp.pad(c, ((0, 0), (w, 0), (0, 0)))[:, :L]
        cnt = jnp.minimum(t + 1, w).astype(F32)[None, :, None]
        pooled.append((c - shifted) / cnt - uf[:, :, gi])
    p = jnp.stack(pooled, axis=2)
    y = jnp.einsum('blgc,gcd->blgd', p, pool_w.astype(F32)).reshape(Bsz, L, GROUP_W)
    return (y * pool_scale.astype(F32)).astype(u.dtype)


def conv_mixer(u, w_dw, b_dw, ln_g, ln_b, w_pw):
    val, gate = jnp.split(u, 2, axis=-1)
    h = val * jax.nn.sigmoid(gate)
    h = lax.conv_general_dilated(h, w_dw[:, None, :], window_strides=(1,),
                                 padding=((CONV_WIDTH - 1, 0),),
                                 dimension_numbers=('NWC', 'WIO', 'NWC'),
                                 feature_group_count=GROUP_W) + b_dw
    h = jax.nn.silu(layernorm(h, ln_g, ln_b))
    return jnp.einsum('blc,cd->bld', h, w_pw)


def _t5_bucket(dist):
    n = np.maximum(dist, 0)
    max_exact = REL_BUCKETS // 2
    large = max_exact + (np.log(np.maximum(n, 1) / max_exact) / np.log(REL_MAX_DIST / max_exact)
                         * (REL_BUCKETS - max_exact)).astype(np.int64)
    large = np.minimum(large, REL_BUCKETS - 1)
    return np.where(n < max_exact, n, large).astype(np.int32)


def _dilated_branch(q, k, v, rel_bias, window, dilation):
    Bsz, L, H, E = q.shape
    Ls = L // dilation
    nb = -(-Ls // ATT_BLOCK)
    pad = nb * ATT_BLOCK - Ls

    def to_sub(t):
        t = t.reshape(Bsz, Ls, dilation, H, E).transpose(0, 2, 3, 1, 4)
        return jnp.pad(t, ((0, 0), (0, 0), (0, 0), (0, pad), (0, 0)))

    def band(t):
        t = jnp.pad(t, ((0, 0), (0, 0), (0, 0), (ATT_BLOCK, 0), (0, 0)))
        t = t.reshape(Bsz, dilation, H, nb + 1, ATT_BLOCK, E)
        return jnp.concatenate([t[:, :, :, :-1], t[:, :, :, 1:]], axis=4)

    qb = to_sub(q).reshape(Bsz, dilation, H, nb, ATT_BLOCK, E)
    kb = band(to_sub(k))
    vb = band(to_sub(v))
    s = jnp.einsum('bdhnqe,bdhnke->bdhnqk', qb, kb, preferred_element_type=F32) * (E ** -0.5)
    a_idx = np.arange(ATT_BLOCK)[:, None]
    b_idx = np.arange(2 * ATT_BLOCK)[None, :]
    sub_dist = a_idx + ATT_BLOCK - b_idx
    key_idx = np.arange(nb)[:, None, None] * ATT_BLOCK - ATT_BLOCK + b_idx[None]
    valid = (sub_dist >= 0) & (sub_dist <= window // dilation) & (key_idx >= 0)
    bucket = _t5_bucket(sub_dist * dilation)
    bias = jnp.transpose(rel_bias[jnp.asarray(bucket)], (2, 0, 1)).astype(F32)
    s = s + bias[None, None, :, None]
    s = jnp.where(jnp.asarray(valid)[None, None, None], s, NEG_INF)
    m = jnp.max(s, axis=-1, keepdims=True)
    p = jnp.exp(s - m)
    den = jnp.sum(p, axis=-1, keepdims=True)
    o = jnp.einsum('bdhnqk,bdhnke->bdhnqe', p, vb.astype(F32)) / den
    lse = (m + jnp.log(den))[..., 0]
    o = o.reshape(Bsz, dilation, H, nb * ATT_BLOCK, E)[:, :, :, :Ls]
    o = o.transpose(0, 3, 1, 2, 4).reshape(Bsz, L, H, E)
    lse = lse.reshape(Bsz, dilation, H, nb * ATT_BLOCK)[:, :, :, :Ls]
    lse = lse.transpose(0, 3, 1, 2).reshape(Bsz, L, H)
    return o, lse


def dilated_attention(qkv, rel_bias):
    Bsz, L, _ = qkv.shape
    q, k, v = [t.reshape(Bsz, L, ATT_HEADS, ATT_HEAD_DIM) for t in jnp.split(qkv, 3, axis=-1)]
    outs, lses = [], []
    for window, dilation in DILATED_PATTERNS:
        o, lse = _dilated_branch(q, k, v, rel_bias, window, dilation)
        outs.append(o)
        lses.append(lse)
    wts = jax.nn.softmax(jnp.stack(lses, axis=0), axis=0)
    o = jnp.einsum('pblh,pblhe->blhe', wts, jnp.stack(outs, axis=0))
    return o.reshape(Bsz, L, GROUP_W).astype(qkv.dtype)


def cross_attention(h, mem_n, w_xq, w_xk, w_xv, w_xo):
    Bsz, L, _ = h.shape
    M = mem_n.shape[1]
    q = jnp.einsum('bld,de->ble', h, w_xq).reshape(Bsz, L, X_HEADS, X_HEAD_DIM)
    k = jnp.einsum('bmd,de->bme', mem_n, w_xk).reshape(Bsz, M, X_HEADS, X_HEAD_DIM)
    v = jnp.einsum('bmd,de->bme', mem_n, w_xv).reshape(Bsz, M, X_HEADS, X_HEAD_DIM)
    s = jnp.einsum('blhe,bmhe->bhlm', q, k, preferred_element_type=F32) * (X_HEAD_DIM ** -0.5)
    p = jax.nn.softmax(s, axis=-1)
    o = jnp.einsum('bhlm,bmhe->blhe', p, v.astype(F32)).reshape(Bsz, L, X_WIDTH).astype(h.dtype)
    return jnp.einsum('ble,ed->bld', o, w_xo)


def setup_inputs(seed: int = 0) -> dict:
    key = jax.random.key(seed)
    ks = iter(jax.random.split(key, 48))

    def nrm(shape, scale):
        return jax.random.normal(next(ks), shape, F32) * scale

    def gain(shape):
        return 1.0 + nrm(shape, 0.02)

    x = nrm((BATCH, SEQ, D_MODEL), 1.0)
    mem = nrm((BATCH, MEM_LEN, D_MODEL), 1.0)
    rel_bias = nrm((REL_BUCKETS, ATT_HEADS), 0.2)
    mem_norm_g = gain((D_MODEL,))
    norm_mix_g = gain((DEPTH, D_MODEL))
    w_in = nrm((DEPTH, D_MODEL, IN_WIDTH), D_MODEL ** -0.5)
    s5_lam_re = -0.5 * jnp.exp(nrm((DEPTH, S5_GROUPS, S5_STATE), 0.02))
    s5_lam_im = math.pi * jnp.arange(S5_STATE, dtype=F32) + nrm((DEPTH, S5_GROUPS, S5_STATE), 0.02)
    s5_log_dt = jax.random.uniform(next(ks), (DEPTH, S5_GROUPS), F32,
                                   math.log(S5_DT_MIN), math.log(S5_DT_MAX))
    s5_b_re = nrm((DEPTH, S5_GROUPS, S5_STATE, S5_CH_PER_GROUP), (2 * S5_CH_PER_GROUP) ** -0.5)
    s5_b_im = nrm((DEPTH, S5_GROUPS, S5_STATE, S5_CH_PER_GROUP), (2 * S5_CH_PER_GROUP) ** -0.5)
    s5_c_re = nrm((DEPTH, S5_GROUPS, S5_CH_PER_GROUP, S5_STATE), 0.5)
    s5_c_im = nrm((DEPTH, S5_GROUPS, S5_CH_PER_GROUP, S5_STATE), 0.5)
    s5_d = nrm((DEPTH, GROUP_W), 1.0)
    s5_w_glu = nrm((DEPTH, GROUP_W, GROUP_W), GROUP_W ** -0.5)
    pool_w = nrm((DEPTH, len(POOL_WINDOWS), POOL_CH, POOL_CH), POOL_CH ** -0.5)
    pool_scale = gain((DEPTH, GROUP_W))
    conv_w_dw = nrm((DEPTH, CONV_WIDTH, GROUP_W), CONV_WIDTH ** -0.5)
    conv_b_dw = nrm((DEPTH, GROUP_W), 0.02)
    conv_ln_g = gain((DEPTH, GROUP_W))
    conv_ln_b = nrm((DEPTH, GROUP_W), 0.02)
    conv_w_pw = nrm((DEPTH, GROUP_W, GROUP_W), GROUP_W ** -0.5)
    grp_norm_g = gain((DEPTH, MIX_WIDTH))
    w_out = nrm((DEPTH, MIX_WIDTH, D_MODEL), MIX_WIDTH ** -0.5)
    norm_x_g = gain((DEPTH, D_MODEL))
    w_xq = nrm((DEPTH, D_MODEL, X_WIDTH), D_MODEL ** -0.5)
    w_xk = nrm((DEPTH, D_MODEL, X_WIDTH), D_MODEL ** -0.5)
    w_xv = nrm((DEPTH, D_MODEL, X_WIDTH), D_MODEL ** -0.5)
    w_xo = nrm((DEPTH, X_WIDTH, D_MODEL), X_WIDTH ** -0.5)
    norm_mlp_g = gain((DEPTH, D_MODEL))
    w_up = nrm((DEPTH, D_MODEL, D_FF), D_MODEL ** -0.5)
    w_down = nrm((DEPTH, D_FF, D_MODEL), D_FF ** -0.5)
    norm_final_g = gain((D_MODEL,))
    return {'x': x, 'mem': mem, 'rel_bias': rel_bias, 'mem_norm_g': mem_norm_g,
            'norm_mix_g': norm_mix_g, 'w_in': w_in,
            's5_lam_re': s5_lam_re, 's5_lam_im': s5_lam_im, 's5_log_dt': s5_log_dt,
            's5_b_re': s5_b_re, 's5_b_im': s5_b_im, 's5_c_re': s5_c_re, 's5_c_im': s5_c_im,
            's5_d': s5_d, 's5_w_glu': s5_w_glu,
            'pool_w': pool_w, 'pool_scale': pool_scale,
            'conv_w_dw': conv_w_dw, 'conv_b_dw': conv_b_dw, 'conv_ln_g': conv_ln_g,
            'conv_ln_b': conv_ln_b, 'conv_w_pw': conv_w_pw,
            'grp_norm_g': grp_norm_g, 'w_out': w_out,
            'norm_x_g': norm_x_g, 'w_xq': w_xq, 'w_xk': w_xk, 'w_xv': w_xv, 'w_xo': w_xo,
            'norm_mlp_g': norm_mlp_g, 'w_up': w_up, 'w_down': w_down,
            'norm_final_g': norm_final_g}


def reference(x, mem, rel_bias, mem_norm_g, norm_mix_g, w_in,
              s5_lam_re, s5_lam_im, s5_log_dt, s5_b_re, s5_b_im, s5_c_re, s5_c_im, s5_d, s5_w_glu,
              pool_w, pool_scale, conv_w_dw, conv_b_dw, conv_ln_g, conv_ln_b, conv_w_pw,
              grp_norm_g, w_out, norm_x_g, w_xq, w_xk, w_xv, w_xo,
              norm_mlp_g, w_up, w_down, norm_final_g):
    mem_n = rmsnorm(mem, mem_norm_g)
    h = x
    for l in range(DEPTH):
        xn = rmsnorm(h, norm_mix_g[l])
        proj = jnp.einsum('bld,dc->blc', xn, w_in[l])
        u_a, u_b, u_c, qkv = jnp.split(proj, [GROUP_W, 2 * GROUP_W, 4 * GROUP_W], axis=-1)
        y_a = s5_mixer(u_a, s5_lam_re[l], s5_lam_im[l], s5_log_dt[l], s5_b_re[l], s5_b_im[l],
                       s5_c_re[l], s5_c_im[l], s5_d[l], s5_w_glu[l])
        y_b = pool_mixer(u_b, pool_w[l], pool_scale[l])
        y_c = conv_mixer(u_c, conv_w_dw[l], conv_b_dw[l], conv_ln_g[l], conv_ln_b[l], conv_w_pw[l])
        y_d = dilated_attention(qkv, rel_bias)
        y = group_rmsnorm(jnp.concatenate([y_a, y_b, y_c, y_d], axis=-1), grp_norm_g[l], h.dtype)
        h = h + jnp.einsum('blc,cd->bld', y, w_out[l])
        h = h + cross_attention(rmsnorm(h, norm_x_g[l]), mem_n, w_xq[l], w_xk[l], w_xv[l], w_xo[l])
        hn = rmsnorm(h, norm_mlp_g[l])
        h = h + jnp.einsum('blf,fd->bld', jnp.square(jax.nn.relu(jnp.einsum('bld,df->blf', hn, w_up[l]))), w_down[l])
    return rmsnorm(h, norm_final_g)
```

```python
import functools
import math

import numpy as np
import jax
import jax.numpy as jnp
from jax import lax
from jax.experimental import pallas as pl
from jax.experimental.pallas import tpu as pltpu

F32 = jnp.float32
BF16 = jnp.bfloat16

D_MODEL = 2048
DEPTH = 4
GROUP_W = 512
S5_GROUPS = 32
S5_CH = 16
S5_STATE = 64
S5_NSTATE = S5_GROUPS * S5_STATE
S5_SCAN_LANE_BLOCKS = 4
POOL_WINDOWS = (2, 4, 8, 16)
POOL_CH = 128
CONV_WIDTH = 31
ATT_HEADS = 8
ATT_HEAD_DIM = 64
DILATED_PATTERNS = ((128, 1), (512, 4), (2048, 16))
ATT_BLOCK = 128
REL_BUCKETS = 32
REL_MAX_DIST = 2048
MEM_LEN = 256
X_HEADS = 4
X_HEAD_DIM = 128
X_WIDTH = 512
D_FF = 4 * D_MODEL
NORM_EPS = 1e-6
NEG_INF = -1e30
IN_WIDTH = 7 * GROUP_W

SUBLANES = 8
LANES = 128
VMEM_LIMIT = 52 * 1024 * 1024


def _params(*sem):
    return pltpu.CompilerParams(dimension_semantics=sem, vmem_limit_bytes=VMEM_LIMIT)


def _rms(x, g):
    return x * lax.rsqrt(jnp.mean(x * x, axis=-1, keepdims=True) + NORM_EPS) * g


def _norm_matmul_kernel(x_ref, g_ref, w_ref, o_ref, xn_ref):
    @pl.when(pl.program_id(1) == 0)
    def _():
        xn_ref[...] = _rms(x_ref[...], g_ref[...]).astype(BF16)

    o_ref[...] = jnp.dot(xn_ref[...], w_ref[...], preferred_element_type=F32)


def norm_matmul(x, g, w, layer, tm=1024, tn=512):
    m, d = x.shape
    n = w.shape[-1]
    return pl.pallas_call(
        _norm_matmul_kernel,
        grid=(m // tm, n // tn),
        in_specs=[
            pl.BlockSpec((tm, d), lambda i, j: (i, 0)),
            pl.BlockSpec((None, 1, d), lambda i, j: (layer, 0, 0)),
            pl.BlockSpec((None, d, tn), lambda i, j: (layer, 0, j)),
        ],
        out_specs=pl.BlockSpec((tm, tn), lambda i, j: (i, j)),
        out_shape=jax.ShapeDtypeStruct((m, n), F32),
        scratch_shapes=[pltpu.VMEM((tm, d), BF16)],
        compiler_params=_params("parallel", "arbitrary"),
        name="norm_matmul",
    )(x, g, w)


def _s5_param_kernel(lr_ref, li_ref, ldt_ref, lrx_ref, lix_ref, ldtx_ref, br_ref, bi_ref,
                     pr_ref, pi_ref, bbr_ref, bbi_ref):
    def discretise(lr, li, ldt):
        dt = jnp.exp(ldt)
        mag = jnp.exp(lr * dt)
        return mag * jnp.cos(li * dt), mag * jnp.sin(li * dt)

    ar, ai = discretise(lr_ref[...], li_ref[...], ldt_ref[...])
    cr, ci = ar, ai
    for j in range(SUBLANES):
        pr_ref[j] = cr
        pi_ref[j] = ci
        cr, ci = cr * ar - ci * ai, cr * ai + ci * ar

    lr, li = lrx_ref[...], lix_ref[...]
    ar, ai = discretise(lr, li, ldtx_ref[...])
    den = lr * lr + li * li
    nr, ni = ar - 1.0, ai
    f_r = (nr * lr + ni * li) / den
    f_i = (ni * lr - nr * li) / den
    b_r, b_i = br_ref[...], bi_ref[...]
    bbr_ref[...] = f_r * b_r - f_i * b_i
    bbi_ref[...] = f_r * b_i + f_i * b_r


def s5_params(lam_re, lam_im, log_dt, b_re, b_im):
    depth = lam_re.shape[0]
    gc = S5_GROUPS * S5_CH
    rep = lambda a: jnp.repeat(a, S5_CH, axis=1)
    ldt = log_dt[..., None]
    bt = lambda b: jnp.transpose(b, (0, 1, 3, 2)).reshape(depth, gc, S5_STATE)
    small = pl.BlockSpec((None, S5_GROUPS, S5_STATE), lambda l: (l, 0, 0))
    small1 = pl.BlockSpec((None, S5_GROUPS, 1), lambda l: (l, 0, 0))
    big = pl.BlockSpec((None, gc, S5_STATE), lambda l: (l, 0, 0))
    big1 = pl.BlockSpec((None, gc, 1), lambda l: (l, 0, 0))
    powspec = pl.BlockSpec((None, SUBLANES, S5_GROUPS, S5_STATE), lambda l: (l, 0, 0, 0))
    return pl.pallas_call(
        _s5_param_kernel,
        grid=(depth,),
        in_specs=[small, small, small1, big, big, big1, big, big],
        out_specs=[powspec, powspec, big, big],
        out_shape=[
            jax.ShapeDtypeStruct((depth, SUBLANES, S5_GROUPS, S5_STATE), F32),
            jax.ShapeDtypeStruct((depth, SUBLANES, S5_GROUPS, S5_STATE), F32),
            jax.ShapeDtypeStruct((depth, gc, S5_STATE), F32),
            jax.ShapeDtypeStruct((depth, gc, S5_STATE), F32),
        ],
        compiler_params=_params("arbitrary"),
        name="s5_params",
    )(lam_re, lam_im, ldt, rep(lam_re), rep(lam_im), rep(ldt), bt(b_re), bt(b_im))


def s5_layouts(pow_r, pow_i, bb_r, bb_i, c_re, c_im):
    depth = pow_r.shape[0]
    eye = jnp.eye(S5_GROUPS, dtype=F32)[:, None, :, None]

    def bdiag_b(bb):
        t = bb.reshape(depth, S5_GROUPS, S5_CH, 1, S5_STATE) * eye[None]
        return t.reshape(depth, GROUP_W, S5_NSTATE)

    def bdiag_c(c):
        t = jnp.transpose(c, (0, 1, 3, 2)).reshape(depth, S5_GROUPS, S5_STATE, 1, S5_CH) * eye[None]
        return t.reshape(depth, S5_NSTATE, GROUP_W)

    bmat = jnp.concatenate([bdiag_b(bb_r), bdiag_b(bb_i)], axis=2).astype(BF16)
    cmat = jnp.concatenate([bdiag_c(c_re), -bdiag_c(c_im)], axis=1).astype(BF16)

    row = jnp.arange(SUBLANES)[None, :, None]

    def coef(p):
        p = p.reshape(depth, SUBLANES, S5_NSTATE)
        tiles = [jnp.where(row >= s, p[:, s - 1][:, None, :], 0.0) for s in (1, 2, 4)]
        return jnp.stack(tiles + [p], axis=1)

    return bmat, cmat, coef(pow_r), coef(pow_i)


def _s5_kernel(u_ref, bmat_ref, cmat_ref, cfr_ref, cfi_ref, d_ref, wglu_ref, o_ref, xs_ref, *, tl):
    n = S5_NSTATE
    l = pl.program_id(1)

    @pl.when(l == 0)
    def _():
        xs_ref[0:SUBLANES, :] = jnp.zeros((SUBLANES, 2 * n), F32)

    @pl.when(l > 0)
    def _():
        xs_ref[0:SUBLANES, :] = xs_ref[tl:tl + SUBLANES, :]

    u = u_ref[...]
    xs_ref[SUBLANES:, :] = jnp.dot(u.astype(BF16), bmat_ref[...], preferred_element_type=F32)

    def scan_lanes(lb0):
        cols = [(slice(lb * LANES, (lb + 1) * LANES), slice(n + lb * LANES, n + (lb + 1) * LANES))
                for lb in range(lb0, lb0 + S5_SCAN_LANE_BLOCKS)]

        def tile_scan(rt, prev):
            base = pl.multiple_of(SUBLANES + rt * SUBLANES, SUBLANES)
            new = []
            for q, (cr_, ci_) in enumerate(cols):
                zr = xs_ref[pl.ds(base, SUBLANES), cr_]
                zi = xs_ref[pl.ds(base, SUBLANES), ci_]
                for k, s in enumerate((1, 2, 4)):
                    sr = pltpu.roll(zr, s, 0)
                    si = pltpu.roll(zi, s, 0)
                    ar = cfr_ref[k, :, cr_]
                    ai = cfi_ref[k, :, cr_]
                    zr, zi = zr + (ar * sr - ai * si), zi + (ar * si + ai * sr)
                pr = jnp.broadcast_to(prev[2 * q][SUBLANES - 1:SUBLANES, :], (SUBLANES, LANES))
                pi = jnp.broadcast_to(prev[2 * q + 1][SUBLANES - 1:SUBLANES, :], (SUBLANES, LANES))
                ar = cfr_ref[3, :, cr_]
                ai = cfi_ref[3, :, cr_]
                xr = zr + (ar * pr - ai * pi)
                xi = zi + (ar * pi + ai * pr)
                xs_ref[pl.ds(base, SUBLANES), cr_] = xr
                xs_ref[pl.ds(base, SUBLANES), ci_] = xi
                new += [xr, xi]
            return tuple(new)

        init = []
        for cr_, ci_ in cols:
            init += [xs_ref[0:SUBLANES, cr_], xs_ref[0:SUBLANES, ci_]]
        lax.fori_loop(0, tl // SUBLANES, tile_scan, tuple(init))

    for lb0 in range(0, n // LANES, S5_SCAN_LANE_BLOCKS):
        scan_lanes(lb0)

    y = jnp.dot(xs_ref[SUBLANES:, :].astype(BF16), cmat_ref[...], preferred_element_type=F32)
    y = y + d_ref[...] * u
    g = jax.nn.gelu(y)
    gate = jnp.dot(g.astype(BF16), wglu_ref[...], preferred_element_type=F32)
    o_ref[...] = g * jax.nn.sigmoid(gate)


def s5_mixer(proj, bmat, cmat, coef_r, coef_i, d_skip, w_glu, layer, bsz, seq, tl=512):
    nl = seq // tl
    n2 = 2 * S5_NSTATE
    return pl.pallas_call(
        functools.partial(_s5_kernel, tl=tl),
        grid=(bsz, nl),
        in_specs=[
            pl.BlockSpec((tl, GROUP_W), lambda b, l: (b * nl + l, 0)),
            pl.BlockSpec((None, GROUP_W, n2), lambda b, l: (layer, 0, 0)),
            pl.BlockSpec((None, n2, GROUP_W), lambda b, l: (layer, 0, 0)),
            pl.BlockSpec((None, 4, SUBLANES, S5_NSTATE), lambda b, l: (layer, 0, 0, 0)),
            pl.BlockSpec((None, 4, SUBLANES, S5_NSTATE), lambda b, l: (layer, 0, 0, 0)),
            pl.BlockSpec((None, 1, GROUP_W), lambda b, l: (layer, 0, 0)),
            pl.BlockSpec((None, GROUP_W, GROUP_W), lambda b, l: (layer, 0, 0)),
        ],
        out_specs=pl.BlockSpec((tl, GROUP_W), lambda b, l: (b * nl + l, 0)),
        out_shape=jax.ShapeDtypeStruct((bsz * seq, GROUP_W), F32),
        scratch_shapes=[pltpu.VMEM((SUBLANES + tl, n2), F32)],
        compiler_params=_params("parallel", "arbitrary"),
        name="s5_mixer",
    )(proj, bmat, cmat, coef_r, coef_i, d_skip, w_glu)


POOL_HALO = 16


def _pool_kernel(u_ref, pw_ref, sc_ref, o_ref, ext_ref, *, tl):
    l = pl.program_id(1)

    @pl.when(l == 0)
    def _():
        ext_ref[0:POOL_HALO, :] = jnp.zeros((POOL_HALO, GROUP_W), F32)

    @pl.when(l > 0)
    def _():
        ext_ref[0:POOL_HALO, :] = ext_ref[tl:tl + POOL_HALO, :]

    ext_ref[POOL_HALO:, :] = u_ref[...]
    t = l * tl + lax.broadcasted_iota(jnp.int32, (tl, 1), 0)
    for gi, w in enumerate(POOL_WINDOWS):
        cols = slice(gi * POOL_CH, (gi + 1) * POOL_CH)
        acc = ext_ref[pl.ds(POOL_HALO, tl), cols]
        for k in range(1, w):
            acc = acc + ext_ref[pl.ds(POOL_HALO - k, tl), cols]
        cnt = jnp.minimum(t + 1, w).astype(F32)
        p = acc / cnt - u_ref[:, cols]
        y = jnp.dot(p.astype(BF16), pw_ref[gi], preferred_element_type=F32)
        o_ref[:, cols] = y * sc_ref[:, cols]


def pool_mixer(proj, pool_w, pool_scale, layer, bsz, seq, tl=512):
    nl = seq // tl
    ng = len(POOL_WINDOWS)
    return pl.pallas_call(
        functools.partial(_pool_kernel, tl=tl),
        grid=(bsz, nl),
        in_specs=[
            pl.BlockSpec((tl, GROUP_W), lambda b, l: (b * nl + l, 1)),
            pl.BlockSpec((None, ng, POOL_CH, POOL_CH), lambda b, l: (layer, 0, 0, 0)),
            pl.BlockSpec((None, 1, GROUP_W), lambda b, l: (layer, 0, 0)),
        ],
        out_specs=pl.BlockSpec((tl, GROUP_W), lambda b, l: (b * nl + l, 0)),
        out_shape=jax.ShapeDtypeStruct((bsz * seq, GROUP_W), F32),
        scratch_shapes=[pltpu.VMEM((POOL_HALO + tl, GROUP_W), F32)],
        compiler_params=_params("parallel", "arbitrary"),
        name="pool_mixer",
    )(proj, pool_w, pool_scale)


CONV_HALO = 32
CONV_ROWS = 32


def _conv_kernel(val_ref, gate_ref, wdw_ref, bdw_ref, lng_ref, lnb_ref, wpw_ref, o_ref,
                 ext_ref, cv_ref, *, tl):
    l = pl.program_id(1)

    @pl.when(l == 0)
    def _():
        ext_ref[0:CONV_HALO, :] = jnp.zeros((CONV_HALO, GROUP_W), F32)

    @pl.when(l > 0)
    def _():
        ext_ref[0:CONV_HALO, :] = ext_ref[tl:tl + CONV_HALO, :]

    ext_ref[CONV_HALO:, :] = val_ref[...] * jax.nn.sigmoid(gate_ref[...])

    off = CONV_HALO - (CONV_WIDTH - 1)

    for r0 in range(0, tl, CONV_ROWS):
        acc = jnp.broadcast_to(bdw_ref[...], (CONV_ROWS, GROUP_W))
        for k in range(CONV_WIDTH):
            acc = acc + wdw_ref[k:k + 1, :] * ext_ref[r0 + off + k:r0 + off + k + CONV_ROWS, :]
        cv_ref[r0:r0 + CONV_ROWS, :] = acc

    h = cv_ref[...]
    hc = h - jnp.mean(h, axis=-1, keepdims=True)
    y = hc * lax.rsqrt(jnp.mean(hc * hc, axis=-1, keepdims=True) + NORM_EPS)
    y = y * lng_ref[...] + lnb_ref[...]
    y = jax.nn.silu(y)
    o_ref[...] = jnp.dot(y.astype(BF16), wpw_ref[...], preferred_element_type=F32)


def conv_mixer(proj, w_dw, b_dw, ln_g, ln_b, w_pw, layer, bsz, seq, tl=256):
    nl = seq // tl
    vec = pl.BlockSpec((None, 1, GROUP_W), lambda b, l: (layer, 0, 0))
    return pl.pallas_call(
        functools.partial(_conv_kernel, tl=tl),
        grid=(bsz, nl),
        in_specs=[
            pl.BlockSpec((tl, GROUP_W), lambda b, l: (b * nl + l, 2)),
            pl.BlockSpec((tl, GROUP_W), lambda b, l: (b * nl + l, 3)),
            pl.BlockSpec((None, CONV_WIDTH, GROUP_W), lambda b, l: (layer, 0, 0)),
            vec, vec, vec,
            pl.BlockSpec((None, GROUP_W, GROUP_W), lambda b, l: (layer, 0, 0)),
        ],
        out_specs=pl.BlockSpec((tl, GROUP_W), lambda b, l: (b * nl + l, 0)),
        out_shape=jax.ShapeDtypeStruct((bsz * seq, GROUP_W), F32),
        scratch_shapes=[pltpu.VMEM((CONV_HALO + tl, GROUP_W), F32),
                        pltpu.VMEM((tl, GROUP_W), F32)],
        compiler_params=_params("parallel", "arbitrary"),
        name="conv_mixer",
    )(proj, proj, w_dw, b_dw, ln_g, ln_b, w_pw)


def _t5_bucket(dist):
    n = np.maximum(dist, 0)
    max_exact = REL_BUCKETS // 2
    large = max_exact + (np.log(np.maximum(n, 1) / max_exact) / np.log(REL_MAX_DIST / max_exact)
                         * (REL_BUCKETS - max_exact)).astype(np.int64)
    large = np.minimum(large, REL_BUCKETS - 1)
    return np.where(n < max_exact, n, large).astype(np.int32)


def attention_bias_tiles(rel_bias, seq):
    nblk = seq // ATT_BLOCK
    a = np.arange(ATT_BLOCK)[:, None]
    c = np.arange(ATT_BLOCK)[None, :]
    d = np.arange(nblk)[:, None, None] * ATT_BLOCK + a[None] - c[None]
    mult = np.zeros(d.shape, np.int64)
    for window, dil in DILATED_PATTERNS:
        mult += (d >= 0) & (d % dil == 0) & (d // dil <= window // dil)
    valid = mult > 0
    logm = np.log(np.maximum(mult, 1)).astype(np.float32)
    bucket = _t5_bucket(d)
    bias = jnp.transpose(rel_bias[jnp.asarray(bucket)], (3, 0, 1, 2)).astype(F32)
    return jnp.where(jnp.asarray(valid)[None], bias + jnp.asarray(logm)[None], NEG_INF)


def _attn_kernel(q_ref, k_ref, v_ref, bt_ref, o_ref, kb_ref, vb_ref):
    i = pl.program_id(2)

    @pl.when(i == 0)
    def _():
        kb_ref[...] = k_ref[...].astype(BF16)
        vb_ref[...] = v_ref[...].astype(BF16)

    q = q_ref[...] * (ATT_HEAD_DIM ** -0.5)
    lane = lax.broadcasted_iota(jnp.int32, (ATT_BLOCK, LANES), 1)
    outs = []
    for hh in range(LANES // ATT_HEAD_DIM):
        in_head = (lane >= hh * ATT_HEAD_DIM) & (lane < (hh + 1) * ATT_HEAD_DIM)
        qh = jnp.where(in_head, q, 0.0).astype(BF16)

        def body(jj, carry, hh=hh, qh=qh):
            m, den, acc = carry
            r0 = pl.multiple_of((i - jj) * ATT_BLOCK, ATT_BLOCK)
            kblk = kb_ref[pl.ds(r0, ATT_BLOCK), :]
            s = lax.dot_general(qh, kblk, (((1,), (1,)), ((), ())), preferred_element_type=F32)
            s = s + bt_ref[hh, jj]
            m_new = jnp.maximum(m, jnp.max(s, axis=-1, keepdims=True))
            p = jnp.exp(s - m_new)
            alpha = jnp.exp(m - m_new)
            den = alpha * den + jnp.sum(p, axis=-1, keepdims=True)
            pv = jnp.dot(p.astype(BF16), vb_ref[pl.ds(r0, ATT_BLOCK), :], preferred_element_type=F32)
            return m_new, den, alpha * acc + pv

        init = (jnp.full((ATT_BLOCK, 1), NEG_INF, F32), jnp.zeros((ATT_BLOCK, 1), F32),
                jnp.zeros((ATT_BLOCK, LANES), F32))
        _, den, acc = lax.fori_loop(0, i + 1, body, init)
        outs.append((acc / den, in_head))
    o = outs[0][0]
    for oh, in_head in outs[1:]:
        o = jnp.where(in_head, oh, o)
    o_ref[...] = o


def dilated_attention(proj, bias_tiles, bsz, seq):
    nq = seq // ATT_BLOCK
    hpb = LANES // ATT_HEAD_DIM
    nhp = ATT_HEADS // hpb
    qcol = 4 * GROUP_W // LANES
    kcol = 5 * GROUP_W // LANES
    vcol = 6 * GROUP_W // LANES
    return pl.pallas_call(
        _attn_kernel,
        grid=(bsz, nhp, nq),
        in_specs=[
            pl.BlockSpec((ATT_BLOCK, LANES), lambda b, h, i: (b * nq + i, qcol + h)),
            pl.BlockSpec((seq, LANES), lambda b, h, i: (b, kcol + h)),
            pl.BlockSpec((seq, LANES), lambda b, h, i: (b, vcol + h)),
            pl.BlockSpec((hpb, nq, ATT_BLOCK, ATT_BLOCK), lambda b, h, i: (h, 0, 0, 0)),
        ],
        out_specs=pl.BlockSpec((ATT_BLOCK, LANES), lambda b, h, i: (b * nq + i, h)),
        out_shape=jax.ShapeDtypeStruct((bsz * seq, GROUP_W), F32),
        scratch_shapes=[pltpu.VMEM((seq, LANES), BF16), pltpu.VMEM((seq, LANES), BF16)],
        compiler_params=_params("parallel", "parallel", "arbitrary"),
        name="dilated_attention",
    )(proj, proj, proj, bias_tiles)


def _mix_out_kernel(ya_ref, yb_ref, yc_ref, yd_ref, g_ref, w_ref, h_ref, o_ref, yn_ref):
    @pl.when(pl.program_id(1) == 0)
    def _():
        for gi, y_ref in enumerate((ya_ref, yb_ref, yc_ref, yd_ref)):
            cols = slice(gi * GROUP_W, (gi + 1) * GROUP_W)
            yn_ref[:, cols] = _rms(y_ref[...], g_ref[:, cols]).astype(BF16)

    o_ref[...] = h_ref[...] + jnp.dot(yn_ref[...], w_ref[...], preferred_element_type=F32)


def mix_out(ya, yb, yc, yd, g, w, h, layer, tm=512, tn=1024):
    m, d = h.shape
    yspec = pl.BlockSpec((tm, GROUP_W), lambda i, j: (i, 0))
    return pl.pallas_call(
        _mix_out_kernel,
        grid=(m // tm, d // tn),
        in_specs=[
            yspec, yspec, yspec, yspec,
            pl.BlockSpec((None, 1, d), lambda i, j: (layer, 0, 0)),
            pl.BlockSpec((None, d, tn), lambda i, j: (layer, 0, j)),
            pl.BlockSpec((tm, tn), lambda i, j: (i, j)),
        ],
        out_specs=pl.BlockSpec((tm, tn), lambda i, j: (i, j)),
        out_shape=jax.ShapeDtypeStruct((m, d), F32),
        scratch_shapes=[pltpu.VMEM((tm, d), BF16)],
        compiler_params=_params("parallel", "arbitrary"),
        name="mix_out",
    )(ya, yb, yc, yd, g, w, h)


def _mem_kv_kernel(mem_ref, g_ref, wk_ref, wv_ref, k_ref, v_ref):
    mn = _rms(mem_ref[...], g_ref[...]).astype(BF16)
    k_ref[...] = jnp.dot(mn, wk_ref[...], preferred_element_type=F32).astype(BF16)
    v_ref[...] = jnp.dot(mn, wv_ref[...], preferred_element_type=F32).astype(BF16)


def mem_kv(mem, g, wk, wv):
    m, d = mem.shape
    depth = wk.shape[0]
    wspec = pl.BlockSpec((None, d, X_WIDTH), lambda l: (l, 0, 0))
    ospec = pl.BlockSpec((None, m, X_WIDTH), lambda l: (l, 0, 0))
    return pl.pallas_call(
        _mem_kv_kernel,
        grid=(depth,),
        in_specs=[pl.BlockSpec((m, d), lambda l: (0, 0)), pl.BlockSpec((1, d), lambda l: (0, 0)),
                  wspec, wspec],
        out_specs=[ospec, ospec],
        out_shape=[jax.ShapeDtypeStruct((depth, m, X_WIDTH), BF16)] * 2,
        compiler_params=_params("arbitrary"),
        name="mem_kv",
    )(mem, g, wk, wv)


def _xattn_kernel(h_ref, g_ref, wq_ref, k_ref, v_ref, wo_ref, o_ref):
    h = h_ref[...]
    hn = _rms(h, g_ref[...]).astype(BF16)
    q = jnp.dot(hn, wq_ref[...], preferred_element_type=F32) * (X_HEAD_DIM ** -0.5)
    outs = []
    for hd in range(X_HEADS):
        cols = slice(hd * X_HEAD_DIM, (hd + 1) * X_HEAD_DIM)
        s = lax.dot_general(q[:, cols].astype(BF16), k_ref[:, cols], (((1,), (1,)), ((), ())),
                            preferred_element_type=F32)
        p = jnp.exp(s - jnp.max(s, axis=-1, keepdims=True))
        p = p / jnp.sum(p, axis=-1, keepdims=True)
        outs.append(jnp.dot(p.astype(BF16), v_ref[:, cols], preferred_element_type=F32))
    o = jnp.concatenate(outs, axis=-1).astype(BF16)
    o_ref[...] = h + jnp.dot(o, wo_ref[...], preferred_element_type=F32)


def cross_attention(h, g, wq, k, v, wo, layer, bsz, seq, tm=512):
    d = h.shape[-1]
    nl = seq // tm
    mlen = k.shape[1] // bsz
    return pl.pallas_call(
        _xattn_kernel,
        grid=(bsz, nl),
        in_specs=[
            pl.BlockSpec((tm, d), lambda b, l: (b * nl + l, 0)),
            pl.BlockSpec((None, 1, d), lambda b, l: (layer, 0, 0)),
            pl.BlockSpec((None, d, X_WIDTH), lambda b, l: (layer, 0, 0)),
            pl.BlockSpec((None, mlen, X_WIDTH), lambda b, l: (layer, b, 0)),
            pl.BlockSpec((None, mlen, X_WIDTH), lambda b, l: (layer, b, 0)),
            pl.BlockSpec((None, X_WIDTH, d), lambda b, l: (layer, 0, 0)),
        ],
        out_specs=pl.BlockSpec((tm, d), lambda b, l: (b * nl + l, 0)),
        out_shape=jax.ShapeDtypeStruct(h.shape, F32),
        compiler_params=_params("parallel", "arbitrary"),
        name="cross_attention",
    )(h, g, wq, k, v, wo)


def _mlp_kernel(h_ref, g_ref, wu_ref, wd_ref, o_ref, hn_ref):
    @pl.when(pl.program_id(1) == 0)
    def _():
        h = h_ref[...]
        hn_ref[...] = _rms(h, g_ref[...]).astype(BF16)
        o_ref[...] = h

    a = jnp.dot(hn_ref[...], wu_ref[...], preferred_element_type=F32)
    a = jnp.square(jnp.maximum(a, 0.0)).astype(BF16)
    o_ref[...] += jnp.dot(a, wd_ref[...], preferred_element_type=F32)


def mlp(h, g, wu, wd, layer, tm=1024, tf=512):
    m, d = h.shape
    f = wu.shape[-1]
    return pl.pallas_call(
        _mlp_kernel,
        grid=(m // tm, f // tf),
        in_specs=[
            pl.BlockSpec((tm, d), lambda i, j: (i, 0)),
            pl.BlockSpec((None, 1, d), lambda i, j: (layer, 0, 0)),
            pl.BlockSpec((None, d, tf), lambda i, j: (layer, 0, j)),
            pl.BlockSpec((None, tf, d), lambda i, j: (layer, j, 0)),
        ],
        out_specs=pl.BlockSpec((tm, d), lambda i, j: (i, 0)),
        out_shape=jax.ShapeDtypeStruct((m, d), F32),
        scratch_shapes=[pltpu.VMEM((tm, d), BF16)],
        compiler_params=_params("parallel", "arbitrary"),
        name="mlp",
    )(h, g, wu, wd)


def _final_norm_kernel(h_ref, g_ref, o_ref):
    o_ref[...] = _rms(h_ref[...], g_ref[...])


def final_norm(h, g, tm=512):
    m, d = h.shape
    return pl.pallas_call(
        _final_norm_kernel,
        grid=(m // tm,),
        in_specs=[pl.BlockSpec((tm, d), lambda i: (i, 0)), pl.BlockSpec((1, d), lambda i: (0, 0))],
        out_specs=pl.BlockSpec((tm, d), lambda i: (i, 0)),
        out_shape=jax.ShapeDtypeStruct((m, d), F32),
        compiler_params=_params("parallel"),
        name="final_norm",
    )(h, g)


def kernel(x, mem, rel_bias, mem_norm_g, norm_mix_g, w_in, s5_lam_re, s5_lam_im, s5_log_dt, s5_b_re, s5_b_im, s5_c_re, s5_c_im, s5_d, s5_w_glu, pool_w, pool_scale, conv_w_dw, conv_b_dw, conv_ln_g, conv_ln_b, conv_w_pw, grp_norm_g, w_out, norm_x_g, w_xq, w_xk, w_xv, w_xo, norm_mlp_g, w_up, w_down, norm_final_g):
    bsz, seq, d = x.shape
    depth = w_in.shape[0]
    bf = lambda a: a.astype(BF16)
    vec = lambda a: a[:, None, :]

    k_mem, v_mem = mem_kv(mem.reshape(bsz * mem.shape[1], d), mem_norm_g[None, :], bf(w_xk), bf(w_xv))
    pow_r, pow_i, bb_r, bb_i = s5_params(s5_lam_re, s5_lam_im, s5_log_dt, s5_b_re, s5_b_im)
    bmat, cmat, coef_r, coef_i = s5_layouts(pow_r, pow_i, bb_r, bb_i, s5_c_re, s5_c_im)
    bias_tiles = attention_bias_tiles(rel_bias, seq)

    w_in_b, w_out_b, w_up_b, w_down_b = bf(w_in), bf(w_out), bf(w_up), bf(w_down)
    w_xq_b, w_xo_b = bf(w_xq), bf(w_xo)
    w_glu_b, pool_w_b, w_pw_b = bf(s5_w_glu), bf(pool_w), bf(conv_w_pw)
    norm_mix, s5_dv, pool_sc = vec(norm_mix_g), vec(s5_d), vec(pool_scale)
    b_dw, ln_g, ln_b = vec(conv_b_dw), vec(conv_ln_g), vec(conv_ln_b)
    grp_g, norm_x, norm_mlp = vec(grp_norm_g), vec(norm_x_g), vec(norm_mlp_g)

    h = x.reshape(bsz * seq, d)
    for l in range(depth):
        proj = norm_matmul(h, norm_mix, w_in_b, l)
        y_a = s5_mixer(proj, bmat, cmat, coef_r, coef_i, s5_dv, w_glu_b, l, bsz, seq)
        y_b = pool_mixer(proj, pool_w_b, pool_sc, l, bsz, seq)
        y_c = conv_mixer(proj, conv_w_dw, b_dw, ln_g, ln_b, w_pw_b, l, bsz, seq)
        y_d = dilated_attention(proj, bias_tiles, bsz, seq)
        h = mix_out(y_a, y_b, y_c, y_d, grp_g, w_out_b, h, l)
        h = cross_attention(h, norm_x, w_xq_b, k_mem, v_mem, w_xo_b, l, bsz, seq)
        h = mlp(h, norm_mlp, w_up_b, w_down_b, l)
    return final_norm(h, norm_final_g[None, :]).reshape(bsz, seq, d)
```

```python
import functools
import math

import numpy as np
import jax
import jax.numpy as jnp
from jax import lax
from jax.experimental import pallas as pl
from jax.experimental.pallas import tpu as pltpu

F32 = jnp.float32
BF16 = jnp.bfloat16

D_MODEL = 2048
DEPTH = 4
GROUP_W = 512
S5_GROUPS = 32
S5_CH = 16
S5_STATE = 64
S5_NSTATE = S5_GROUPS * S5_STATE
S5_SCAN_LANE_BLOCKS = 4
POOL_WINDOWS = (2, 4, 8, 16)
POOL_CH = 128
CONV_WIDTH = 31
ATT_HEADS = 8
ATT_HEAD_DIM = 64
DILATED_PATTERNS = ((128, 1), (512, 4), (2048, 16))
ATT_BLOCK = 128
REL_BUCKETS = 32
REL_MAX_DIST = 2048
MEM_LEN = 256
X_HEADS = 4
X_HEAD_DIM = 128
X_WIDTH = 512
D_FF = 4 * D_MODEL
NORM_EPS = 1e-6
NEG_INF = -1e30
IN_WIDTH = 7 * GROUP_W

SUBLANES = 8
LANES = 128
VMEM_LIMIT = 52 * 1024 * 1024


def _params(*sem):
    return pltpu.CompilerParams(dimension_semantics=sem, vmem_limit_bytes=VMEM_LIMIT)


def _rms(x, g):
    return x * lax.rsqrt(jnp.mean(x * x, axis=-1, keepdims=True) + NORM_EPS) * g


def _norm_matmul_kernel(x_ref, g_ref, w_ref, o_ref, xn_ref):
    @pl.when(pl.program_id(1) == 0)
    def _():
        xn_ref[...] = _rms(x_ref[...], g_ref[...]).astype(BF16)

    o_ref[...] = jnp.dot(xn_ref[...], w_ref[...], preferred_element_type=F32)


def norm_matmul(x, g, w, layer, tm=1024, tn=512):
    m, d = x.shape
    n = w.shape[-1]
    return pl.pallas_call(
        _norm_matmul_kernel,
        grid=(m // tm, n // tn),
        in_specs=[
            pl.BlockSpec((tm, d), lambda i, j: (i, 0)),
            pl.BlockSpec((None, 1, d), lambda i, j: (layer, 0, 0)),
            pl.BlockSpec((None, d, tn), lambda i, j: (layer, 0, j)),
        ],
        out_specs=pl.BlockSpec((tm, tn), lambda i, j: (i, j)),
        out_shape=jax.ShapeDtypeStruct((m, n), F32),
        scratch_shapes=[pltpu.VMEM((tm, d), BF16)],
        compiler_params=_params("parallel", "arbitrary"),
        name="norm_matmul",
    )(x, g, w)


def _s5_param_kernel(lr_ref, li_ref, ldt_ref, lrx_ref, lix_ref, ldtx_ref, br_ref, bi_ref,
                     pr_ref, pi_ref, bbr_ref, bbi_ref):
    def discretise(lr, li, ldt):
        dt = jnp.exp(ldt)
        mag = jnp.exp(lr * dt)
        return mag * jnp.cos(li * dt), mag * jnp.sin(li * dt)

    ar, ai = discretise(lr_ref[...], li_ref[...], ldt_ref[...])
    cr, ci = ar, ai
    for j in range(SUBLANES):
        pr_ref[j] = cr
        pi_ref[j] = ci
        cr, ci = cr * ar - ci * ai, cr * ai + ci * ar

    lr, li = lrx_ref[...], lix_ref[...]
    ar, ai = discretise(lr, li, ldtx_ref[...])
    den = lr * lr + li * li
    nr, ni = ar - 1.0, ai
    f_r = (nr * lr + ni * li) / den
    f_i = (ni * lr - nr * li) / den
    b_r, b_i = br_ref[...], bi_ref[...]
    bbr_ref[...] = f_r * b_r - f_i * b_i
    bbi_ref[...] = f_r * b_i + f_i * b_r


def s5_params(lam_re, lam_im, log_dt, b_re, b_im):
    depth = lam_re.shape[0]
    gc = S5_GROUPS * S5_CH
    rep = lambda a: jnp.repeat(a, S5_CH, axis=1)
    ldt = log_dt[..., None]
    bt = lambda b: jnp.transpose(b, (0, 1, 3, 2)).reshape(depth, gc, S5_STATE)
    small = pl.BlockSpec((None, S5_GROUPS, S5_STATE), lambda l: (l, 0, 0))
    small1 = pl.BlockSpec((None, S5_GROUPS, 1), lambda l: (l, 0, 0))
    big = pl.BlockSpec((None, gc, S5_STATE), lambda l: (l, 0, 0))
    big1 = pl.BlockSpec((None, gc, 1), lambda l: (l, 0, 0))
    powspec = pl.BlockSpec((None, SUBLANES, S5_GROUPS, S5_STATE), lambda l: (l, 0, 0, 0))
    return pl.pallas_call(
        _s5_param_kernel,
        grid=(depth,),
        in_specs=[small, small, small1, big, big, big1, big, big],
        out_specs=[powspec, powspec, big, big],
        out_shape=[
            jax.ShapeDtypeStruct((depth, SUBLANES, S5_GROUPS, S5_STATE), F32),
            jax.ShapeDtypeStruct((depth, SUBLANES, S5_GROUPS, S5_STATE), F32),
            jax.ShapeDtypeStruct((depth, gc, S5_STATE), F32),
            jax.ShapeDtypeStruct((depth, gc, S5_STATE), F32),
        ],
        compiler_params=_params("arbitrary"),
        name="s5_params",
    )(lam_re, lam_im, ldt, rep(lam_re), rep(lam_im), rep(ldt), bt(b_re), bt(b_im))


def s5_layouts(pow_r, pow_i, bb_r, bb_i, c_re, c_im):
    depth = pow_r.shape[0]
    eye = jnp.eye(S5_GROUPS, dtype=F32)[:, None, :, None]

    def bdiag_b(bb):
        t = bb.reshape(depth, S5_GROUPS, S5_CH, 1, S5_STATE) * eye[None]
        return t.reshape(depth, GROUP_W, S5_NSTATE)

    def bdiag_c(c):
        t = jnp.transpose(c, (0, 1, 3, 2)).reshape(depth, S5_GROUPS, S5_STATE, 1, S5_CH) * eye[None]
        return t.reshape(depth, S5_NSTATE, GROUP_W)

    bmat = jnp.concatenate([bdiag_b(bb_r), bdiag_b(bb_i)], axis=2).astype(BF16)
    cmat = jnp.concatenate([bdiag_c(c_re), -bdiag_c(c_im)], axis=1).astype(BF16)

    row = jnp.arange(SUBLANES)[None, :, None]

    def coef(p):
        p = p.reshape(depth, SUBLANES, S5_NSTATE)
        tiles = [jnp.where(row >= s, p[:, s - 1][:, None, :], 0.0) for s in (1, 2, 4)]
        return jnp.stack(tiles + [p], axis=1)

    return bmat, cmat, coef(pow_r), coef(pow_i)


def _s5_kernel(u_ref, bmat_ref, cmat_ref, cfr_ref, cfi_ref, d_ref, wglu_ref, o_ref, xs_ref, *, tl):
    n = S5_NSTATE
    l = pl.program_id(1)

    @pl.when(l == 0)
    def _():
        xs_ref[0:SUBLANES, :] = jnp.zeros((SUBLANES, 2 * n), F32)

    @pl.when(l > 0)
    def _():
        xs_ref[0:SUBLANES, :] = xs_ref[tl:tl + SUBLANES, :]

    u = u_ref[...]
    xs_ref[SUBLANES:, :] = jnp.dot(u.astype(BF16), bmat_ref[...], preferred_element_type=F32)

    def scan_lanes(lb0):
        cols = [(slice(lb * LANES, (lb + 1) * LANES), slice(n + lb * LANES, n + (lb + 1) * LANES))
                for lb in range(lb0, lb0 + S5_SCAN_LANE_BLOCKS)]

        def tile_scan(rt, prev):
            base = pl.multiple_of(SUBLANES + rt * SUBLANES, SUBLANES)
            new = []
            for q, (cr_, ci_) in enumerate(cols):
                zr = xs_ref[pl.ds(base, SUBLANES), cr_]
                zi = xs_ref[pl.ds(base, SUBLANES), ci_]
                for k, s in enumerate((1, 2, 4)):
                    sr = pltpu.roll(zr, s, 0)
                    si = pltpu.roll(zi, s, 0)
                    ar = cfr_ref[k, :, cr_]
                    ai = cfi_ref[k, :, cr_]
                    zr, zi = zr + (ar * sr - ai * si), zi + (ar * si + ai * sr)
                pr = jnp.broadcast_to(prev[2 * q][SUBLANES - 1:SUBLANES, :], (SUBLANES, LANES))
                pi = jnp.broadcast_to(prev[2 * q + 1][SUBLANES - 1:SUBLANES, :], (SUBLANES, LANES))
                ar = cfr_ref[3, :, cr_]
                ai = cfi_ref[3, :, cr_]
                xr = zr + (ar * pr - ai * pi)
                xi = zi + (ar * pi + ai * pr)
                xs_ref[pl.ds(base, SUBLANES), cr_] = xr
                xs_ref[pl.ds(base, SUBLANES), ci_] = xi
                new += [xr, xi]
            return tuple(new)

        init = []
        for cr_, ci_ in cols:
            init += [xs_ref[0:SUBLANES, cr_], xs_ref[0:SUBLANES, ci_]]
        lax.fori_loop(0, tl // SUBLANES, tile_scan, tuple(init))

    for lb0 in range(0, n // LANES, S5_SCAN_LANE_BLOCKS):
        scan_lanes(lb0)

    y = jnp.dot(xs_ref[SUBLANES:, :].astype(BF16), cmat_ref[...], preferred_element_type=F32)
    y = y + d_ref[...] * u
    g = jax.nn.gelu(y)
    gate = jnp.dot(g.astype(BF16), wglu_ref[...], preferred_element_type=F32)
    o_ref[...] = g * jax.nn.sigmoid(gate)


def s5_mixer(proj, bmat, cmat, coef_r, coef_i, d_skip, w_glu, layer, bsz, seq, tl=512):
    nl = seq // tl
    n2 = 2 * S5_NSTATE
    return pl.pallas_call(
        functools.partial(_s5_kernel, tl=tl),
        grid=(bsz, nl),
        in_specs=[
            pl.BlockSpec((tl, GROUP_W), lambda b, l: (b * nl + l, 0)),
            pl.BlockSpec((None, GROUP_W, n2), lambda b, l: (layer, 0, 0)),
            pl.BlockSpec((None, n2, GROUP_W), lambda b, l: (layer, 0, 0)),
            pl.BlockSpec((None, 4, SUBLANES, S5_NSTATE), lambda b, l: (layer, 0, 0, 0)),
            pl.BlockSpec((None, 4, SUBLANES, S5_NSTATE), lambda b, l: (layer, 0, 0, 0)),
            pl.BlockSpec((None, 1, GROUP_W), lambda b, l: (layer, 0, 0)),
            pl.BlockSpec((None, GROUP_W, GROUP_W), lambda b, l: (layer, 0, 0)),
        ],
        out_specs=pl.BlockSpec((tl, GROUP_W), lambda b, l: (b * nl + l, 0)),
        out_shape=jax.ShapeDtypeStruct((bsz * seq, GROUP_W), F32),
        scratch_shapes=[pltpu.VMEM((SUBLANES + tl, n2), F32)],
        compiler_params=_params("parallel", "arbitrary"),
        name="s5_mixer",
    )(proj, bmat, cmat, coef_r, coef_i, d_skip, w_glu)


POOL_HALO = 16


def _pool_kernel(u_ref, pw_ref, sc_ref, o_ref, ext_ref, *, tl):
    l = pl.program_id(1)

    @pl.when(l == 0)
    def _():
        ext_ref[0:POOL_HALO, :] = jnp.zeros((POOL_HALO, GROUP_W), F32)

    @pl.when(l > 0)
    def _():
        ext_ref[0:POOL_HALO, :] = ext_ref[tl:tl + POOL_HALO, :]

    ext_ref[POOL_HALO:, :] = u_ref[...]
    t = l * tl + lax.broadcasted_iota(jnp.int32, (tl, 1), 0)
    for gi, w in enumerate(POOL_WINDOWS):
        cols = slice(gi * POOL_CH, (gi + 1) * POOL_CH)
        acc = ext_ref[pl.ds(POOL_HALO, tl), cols]
        for k in range(1, w):
            acc = acc + ext_ref[pl.ds(POOL_HALO - k, tl), cols]
        cnt = jnp.minimum(t + 1, w).astype(F32)
        p = acc / cnt - u_ref[:, cols]
        y = jnp.dot(p.astype(BF16), pw_ref[gi], preferred_element_type=F32)
        o_ref[:, cols] = y * sc_ref[:, cols]


def pool_mixer(proj, pool_w, pool_scale, layer, bsz, seq, tl=512):
    nl = seq // tl
    ng = len(POOL_WINDOWS)
    return pl.pallas_call(
        functools.partial(_pool_kernel, tl=tl),
        grid=(bsz, nl),
        in_specs=[
            pl.BlockSpec((tl, GROUP_W), lambda b, l: (b * nl + l, 1)),
            pl.BlockSpec((None, ng, POOL_CH, POOL_CH), lambda b, l: (layer, 0, 0, 0)),
            pl.BlockSpec((None, 1, GROUP_W), lambda b, l: (layer, 0, 0)),
        ],
        out_specs=pl.BlockSpec((tl, GROUP_W), lambda b, l: (b * nl + l, 0)),
        out_shape=jax.ShapeDtypeStruct((bsz * seq, GROUP_W), F32),
        scratch_shapes=[pltpu.VMEM((POOL_HALO + tl, GROUP_W), F32)],
        compiler_params=_params("parallel", "arbitrary"),
        name="pool_mixer",
    )(proj, pool_w, pool_scale)


CONV_HALO = 32
CONV_ROWS = 32


def _conv_kernel(val_ref, gate_ref, wdw_ref, bdw_ref, lng_ref, lnb_ref, wpw_ref, o_ref,
                 ext_ref, cv_ref, *, tl):
    l = pl.program_id(1)

    @pl.when(l == 0)
    def _():
        ext_ref[0:CONV_HALO, :] = jnp.zeros((CONV_HALO, GROUP_W), F32)

    @pl.when(l > 0)
    def _():
        ext_ref[0:CONV_HALO, :] = ext_ref[tl:tl + CONV_HALO, :]

    ext_ref[CONV_HALO:, :] = val_ref[...] * jax.nn.sigmoid(gate_ref[...])

    off = CONV_HALO - (CONV_WIDTH - 1)

    for r0 in range(0, tl, CONV_ROWS):
        acc = jnp.broadcast_to(bdw_ref[...], (CONV_ROWS, GROUP_W))
        for k in range(CONV_WIDTH):
            acc = acc + wdw_ref[k:k + 1, :] * ext_ref[r0 + off + k:r0 + off + k + CONV_ROWS, :]
        cv_ref[r0:r0 + CONV_ROWS, :] = acc

    h = cv_ref[...]
    hc = h - jnp.mean(h, axis=-1, keepdims=True)
    y = hc * lax.rsqrt(jnp.mean(hc * hc, axis=-1, keepdims=True) + NORM_EPS)
    y = y * lng_ref[...] + lnb_ref[...]
    y = jax.nn.silu(y)
    o_ref[...] = jnp.dot(y.astype(BF16), wpw_ref[...], preferred_element_type=F32)


def conv_mixer(proj, w_dw, b_dw, ln_g, ln_b, w_pw, layer, bsz, seq, tl=256):
    nl = seq // tl
    vec = pl.BlockSpec((None, 1, GROUP_W), lambda b, l: (layer, 0, 0))
    return pl.pallas_call(
        functools.partial(_conv_kernel, tl=tl),
        grid=(bsz, nl),
        in_specs=[
            pl.BlockSpec((tl, GROUP_W), lambda b, l: (b * nl + l, 2)),
            pl.BlockSpec((tl, GROUP_W), lambda b, l: (b * nl + l, 3)),
            pl.BlockSpec((None, CONV_WIDTH, GROUP_W), lambda b, l: (layer, 0, 0)),
            vec, vec, vec,
            pl.BlockSpec((None, GROUP_W, GROUP_W), lambda b, l: (layer, 0, 0)),
        ],
        out_specs=pl.BlockSpec((tl, GROUP_W), lambda b, l: (b * nl + l, 0)),
        out_shape=jax.ShapeDtypeStruct((bsz * seq, GROUP_W), F32),
        scratch_shapes=[pltpu.VMEM((CONV_HALO + tl, GROUP_W), F32),
                        pltpu.VMEM((tl, GROUP_W), F32)],
        compiler_params=_params("parallel", "arbitrary"),
        name="conv_mixer",
    )(proj, proj, w_dw, b_dw, ln_g, ln_b, w_pw)


def _t5_bucket(dist):
    n = np.maximum(dist, 0)
    max_exact = REL_BUCKETS // 2
    large = max_exact + (np.log(np.maximum(n, 1) / max_exact) / np.log(REL_MAX_DIST / max_exact)
                         * (REL_BUCKETS - max_exact)).astype(np.int64)
    large = np.minimum(large, REL_BUCKETS - 1)
    return np.where(n < max_exact, n, large).astype(np.int32)


ATT_NEAR_BLOCKS = 5
ATT_NEAR = ATT_NEAR_BLOCKS * ATT_BLOCK
ATT_FAR_DIL = 16


def _attention_bias_layout(sub):
    a = np.arange(ATT_BLOCK)[:, None]
    d_near = (ATT_NEAR - ATT_BLOCK) + a - np.arange(ATT_NEAR)[None, :]
    mult = np.zeros(d_near.shape, np.int64)
    for window, dil in DILATED_PATTERNS:
        if dil != ATT_FAR_DIL:
            mult += (d_near >= 0) & (d_near % dil == 0) & (d_near // dil <= window // dil)
    d_far = ATT_FAR_DIL * (a - np.arange(sub)[None, :])
    mult_far = ((d_far >= 0) & (d_far // ATT_FAR_DIL <= ATT_BLOCK)).astype(np.int64)
    d = np.concatenate([d_near, d_far], axis=1)
    mult = np.concatenate([mult, mult_far], axis=1)
    addend = np.where(mult > 0, np.log(np.maximum(mult, 1)), NEG_INF).astype(np.float32)
    return _t5_bucket(d), addend


def _attn_bias_kernel(rb_ref, bucket_ref, add_ref, o_ref):
    h = pl.program_id(0)
    bucket = bucket_ref[...]
    t = jnp.zeros(bucket.shape, F32)
    for b in range(REL_BUCKETS):
        t = jnp.where(bucket == b, rb_ref[b, h], t)
    o_ref[...] = t + add_ref[...]


def attention_bias_table(rel_bias, seq):
    sub = seq // ATT_FAR_DIL
    bucket, addend = _attention_bias_layout(sub)
    width = ATT_NEAR + sub
    full = pl.BlockSpec((ATT_BLOCK, width), lambda h: (0, 0))
    return pl.pallas_call(
        _attn_bias_kernel,
        grid=(ATT_HEADS,),
        in_specs=[pl.BlockSpec(memory_space=pltpu.SMEM), full, full],
        out_specs=pl.BlockSpec((None, ATT_BLOCK, width), lambda h: (h, 0, 0)),
        out_shape=jax.ShapeDtypeStruct((ATT_HEADS, ATT_BLOCK, width), F32),
        compiler_params=_params("arbitrary"),
        name="attn_bias",
    )(rel_bias, jnp.asarray(bucket), jnp.asarray(addend))


ATT_HEADS_PER_BLOCK = LANES // ATT_HEAD_DIM


def _nt_dot(a, b):
    return lax.dot_general(a, b, (((1,), (1,)), ((), ())), preferred_element_type=F32)


def _attn_kernel(q_ref, k_ref, v_ref, bt_ref, o_ref, kb_ref, vb_ref, m3_ref, l3_ref, a3_ref, *, seq):
    sub = seq // ATT_FAR_DIL
    pad = ATT_NEAR - ATT_BLOCK
    scale = ATT_HEAD_DIM ** -0.5
    lane = lax.broadcasted_iota(jnp.int32, (1, LANES), 1)
    in_head = [(lane >= hh * ATT_HEAD_DIM) & (lane < (hh + 1) * ATT_HEAD_DIM)
               for hh in range(ATT_HEADS_PER_BLOCK)]

    def by_head(vals):
        out = vals[-1]
        for hh in range(ATT_HEADS_PER_BLOCK - 2, -1, -1):
            out = jnp.where(in_head[hh], vals[hh], out)
        return out

    def softmax_pv(qs, kb, vb, bias_of):
        ms, ls, accs = [], [], []
        for hh in range(ATT_HEADS_PER_BLOCK):
            qh = jnp.where(in_head[hh], qs, 0.0).astype(BF16)
            s = _nt_dot(qh, kb) + bias_of(hh)
            m = jnp.max(s, axis=-1, keepdims=True)
            p = jnp.exp(s - m)
            ms.append(m)
            ls.append(jnp.sum(p, axis=-1, keepdims=True))
            accs.append(jnp.dot(p.astype(BF16), vb, preferred_element_type=F32))
        full = (qs.shape[0], LANES)
        return (by_head([jnp.broadcast_to(m, full) for m in ms]),
                by_head([jnp.broadcast_to(l, full) for l in ls]), by_head(accs))

    kb_ref[...] = k_ref[...].astype(BF16)
    vb_ref[...] = v_ref[...].astype(BF16)

    for r in range(ATT_FAR_DIL):
        rows = pl.ds(r, sub, stride=ATT_FAR_DIL)
        qs = q_ref[rows, :] * scale
        kb = k_ref[rows, :].astype(BF16)
        vb = v_ref[rows, :].astype(BF16)
        m, l, acc = softmax_pv(qs, kb, vb, lambda hh: bt_ref[hh, 0:sub, ATT_NEAR:ATT_NEAR + sub])
        m3_ref[rows, :] = m
        l3_ref[rows, :] = l
        a3_ref[rows, :] = acc

    for i in range(seq // ATT_BLOCK):
        r0 = i * ATT_BLOCK
        lo = max(0, r0 - pad)
        c0 = ATT_NEAR - (r0 + ATT_BLOCK - lo)
        rows = slice(r0, r0 + ATT_BLOCK)
        qs = q_ref[rows, :] * scale
        kb = kb_ref[lo:r0 + ATT_BLOCK, :]
        vb = vb_ref[lo:r0 + ATT_BLOCK, :]
        m, l, acc = softmax_pv(qs, kb, vb, lambda hh: bt_ref[hh, :, c0:ATT_NEAR])
        m3 = m3_ref[rows, :]
        mm = jnp.maximum(m, m3)
        wn = jnp.exp(m - mm)
        wf = jnp.exp(m3 - mm)
        num = acc * wn + a3_ref[rows, :] * wf
        den = l * wn + l3_ref[rows, :] * wf
        o_ref[rows, :] = num / den


def dilated_attention(proj, bias_table, bsz, seq):
    assert seq % ATT_BLOCK == 0 and seq // ATT_FAR_DIL <= ATT_BLOCK
    nhp = ATT_HEADS // ATT_HEADS_PER_BLOCK
    qcol = 4 * GROUP_W // LANES
    kcol = 5 * GROUP_W // LANES
    vcol = 6 * GROUP_W // LANES
    width = bias_table.shape[-1]
    stat = pltpu.VMEM((seq, LANES), F32)
    half = pltpu.VMEM((seq, LANES), BF16)
    return pl.pallas_call(
        functools.partial(_attn_kernel, seq=seq),
        grid=(bsz, nhp),
        in_specs=[
            pl.BlockSpec((seq, LANES), lambda b, h: (b, qcol + h)),
            pl.BlockSpec((seq, LANES), lambda b, h: (b, kcol + h)),
            pl.BlockSpec((seq, LANES), lambda b, h: (b, vcol + h)),
            pl.BlockSpec((ATT_HEADS_PER_BLOCK, ATT_BLOCK, width), lambda b, h: (h, 0, 0)),
        ],
        out_specs=pl.BlockSpec((seq, LANES), lambda b, h: (b, h)),
        out_shape=jax.ShapeDtypeStruct((bsz * seq, GROUP_W), F32),
        scratch_shapes=[half, half, stat, stat, stat],
        compiler_params=_params("parallel", "parallel"),
        name="dilated_attention",
    )(proj, proj, proj, bias_table)


def _mix_out_kernel(ya_ref, yb_ref, yc_ref, yd_ref, g_ref, w_ref, h_ref, o_ref, yn_ref):
    @pl.when(pl.program_id(1) == 0)
    def _():
        for gi, y_ref in enumerate((ya_ref, yb_ref, yc_ref, yd_ref)):
            cols = slice(gi * GROUP_W, (gi + 1) * GROUP_W)
            yn_ref[:, cols] = _rms(y_ref[...], g_ref[:, cols]).astype(BF16)

    o_ref[...] = h_ref[...] + jnp.dot(yn_ref[...], w_ref[...], preferred_element_type=F32)


def mix_out(ya, yb, yc, yd, g, w, h, layer, tm=512, tn=1024):
    m, d = h.shape
    yspec = pl.BlockSpec((tm, GROUP_W), lambda i, j: (i, 0))
    return pl.pallas_call(
        _mix_out_kernel,
        grid=(m // tm, d // tn),
        in_specs=[
            yspec, yspec, yspec, yspec,
            pl.BlockSpec((None, 1, d), lambda i, j: (layer, 0, 0)),
            pl.BlockSpec((None, d, tn), lambda i, j: (layer, 0, j)),
            pl.BlockSpec((tm, tn), lambda i, j: (i, j)),
        ],
        out_specs=pl.BlockSpec((tm, tn), lambda i, j: (i, j)),
        out_shape=jax.ShapeDtypeStruct((m, d), F32),
        scratch_shapes=[pltpu.VMEM((tm, d), BF16)],
        compiler_params=_params("parallel", "arbitrary"),
        name="mix_out",
    )(ya, yb, yc, yd, g, w, h)


def _mem_kv_kernel(mem_ref, g_ref, wk_ref, wv_ref, k_ref, v_ref):
    mn = _rms(mem_ref[...], g_ref[...]).astype(BF16)
    k_ref[...] = jnp.dot(mn, wk_ref[...], preferred_element_type=F32).astype(BF16)
    v_ref[...] = jnp.dot(mn, wv_ref[...], preferred_element_type=F32).astype(BF16)


def mem_kv(mem, g, wk, wv):
    m, d = mem.shape
    depth = wk.shape[0]
    wspec = pl.BlockSpec((None, d, X_WIDTH), lambda l: (l, 0, 0))
    ospec = pl.BlockSpec((None, m, X_WIDTH), lambda l: (l, 0, 0))
    return pl.pallas_call(
        _mem_kv_kernel,
        grid=(depth,),
        in_specs=[pl.BlockSpec((m, d), lambda l: (0, 0)), pl.BlockSpec((1, d), lambda l: (0, 0)),
                  wspec, wspec],
        out_specs=[ospec, ospec],
        out_shape=[jax.ShapeDtypeStruct((depth, m, X_WIDTH), BF16)] * 2,
        compiler_params=_params("arbitrary"),
        name="mem_kv",
    )(mem, g, wk, wv)


def _xattn_kernel(h_ref, g_ref, wq_ref, k_ref, v_ref, wo_ref, o_ref):
    h = h_ref[...]
    hn = _rms(h, g_ref[...]).astype(BF16)
    q = jnp.dot(hn, wq_ref[...], preferred_element_type=F32) * (X_HEAD_DIM ** -0.5)
    outs = []
    for hd in range(X_HEADS):
        cols = slice(hd * X_HEAD_DIM, (hd + 1) * X_HEAD_DIM)
        s = lax.dot_general(q[:, cols].astype(BF16), k_ref[:, cols], (((1,), (1,)), ((), ())),
                            preferred_element_type=F32)
        p = jnp.exp(s - jnp.max(s, axis=-1, keepdims=True))
        p = p / jnp.sum(p, axis=-1, keepdims=True)
        outs.append(jnp.dot(p.astype(BF16), v_ref[:, cols], preferred_element_type=F32))
    o = jnp.concatenate(outs, axis=-1).astype(BF16)
    o_ref[...] = h + jnp.dot(o, wo_ref[...], preferred_element_type=F32)


def cross_attention(h, g, wq, k, v, wo, layer, bsz, seq, tm=512):
    d = h.shape[-1]
    nl = seq // tm
    mlen = k.shape[1] // bsz
    return pl.pallas_call(
        _xattn_kernel,
        grid=(bsz, nl),
        in_specs=[
            pl.BlockSpec((tm, d), lambda b, l: (b * nl + l, 0)),
            pl.BlockSpec((None, 1, d), lambda b, l: (layer, 0, 0)),
            pl.BlockSpec((None, d, X_WIDTH), lambda b, l: (layer, 0, 0)),
            pl.BlockSpec((None, mlen, X_WIDTH), lambda b, l: (layer, b, 0)),
            pl.BlockSpec((None, mlen, X_WIDTH), lambda b, l: (layer, b, 0)),
            pl.BlockSpec((None, X_WIDTH, d), lambda b, l: (layer, 0, 0)),
        ],
        out_specs=pl.BlockSpec((tm, d), lambda b, l: (b * nl + l, 0)),
        out_shape=jax.ShapeDtypeStruct(h.shape, F32),
        compiler_params=_params("parallel", "arbitrary"),
        name="cross_attention",
    )(h, g, wq, k, v, wo)


def _mlp_kernel(h_ref, g_ref, wu_ref, wd_ref, o_ref, hn_ref):
    @pl.when(pl.program_id(1) == 0)
    def _():
        h = h_ref[...]
        hn_ref[...] = _rms(h, g_ref[...]).astype(BF16)
        o_ref[...] = h

    a = jnp.dot(hn_ref[...], wu_ref[...], preferred_element_type=F32)
    a = jnp.square(jnp.maximum(a, 0.0)).astype(BF16)
    o_ref[...] += jnp.dot(a, wd_ref[...], preferred_element_type=F32)


def mlp(h, g, wu, wd, layer, tm=1024, tf=512):
    m, d = h.shape
    f = wu.shape[-1]
    return pl.pallas_call(
        _mlp_kernel,
        grid=(m // tm, f // tf),
        in_specs=[
            pl.BlockSpec((tm, d), lambda i, j: (i, 0)),
            pl.BlockSpec((None, 1, d), lambda i, j: (layer, 0, 0)),
            pl.BlockSpec((None, d, tf), lambda i, j: (layer, 0, j)),
            pl.BlockSpec((None, tf, d), lambda i, j: (layer, j, 0)),
        ],
        out_specs=pl.BlockSpec((tm, d), lambda i, j: (i, 0)),
        out_shape=jax.ShapeDtypeStruct((m, d), F32),
        scratch_shapes=[pltpu.VMEM((tm, d), BF16)],
        compiler_params=_params("parallel", "arbitrary"),
        name="mlp",
    )(h, g, wu, wd)


def _final_norm_kernel(h_ref, g_ref, o_ref):
    o_ref[...] = _rms(h_ref[...], g_ref[...])


def final_norm(h, g, tm=512):
    m, d = h.shape
    return pl.pallas_call(
        _final_norm_kernel,
        grid=(m // tm,),
        in_specs=[pl.BlockSpec((tm, d), lambda i: (i, 0)), pl.BlockSpec((1, d), lambda i: (0, 0))],
        out_specs=pl.BlockSpec((tm, d), lambda i: (i, 0)),
        out_shape=jax.ShapeDtypeStruct((m, d), F32),
        compiler_params=_params("parallel"),
        name="final_norm",
    )(h, g)


def kernel(x, mem, rel_bias, mem_norm_g, norm_mix_g, w_in, s5_lam_re, s5_lam_im, s5_log_dt, s5_b_re, s5_b_im, s5_c_re, s5_c_im, s5_d, s5_w_glu, pool_w, pool_scale, conv_w_dw, conv_b_dw, conv_ln_g, conv_ln_b, conv_w_pw, grp_norm_g, w_out, norm_x_g, w_xq, w_xk, w_xv, w_xo, norm_mlp_g, w_up, w_down, norm_final_g):
    bsz, seq, d = x.shape
    depth = w_in.shape[0]
    bf = lambda a: a.astype(BF16)
    vec = lambda a: a[:, None, :]

    k_mem, v_mem = mem_kv(mem.reshape(bsz * mem.shape[1], d), mem_norm_g[None, :], bf(w_xk), bf(w_xv))
    pow_r, pow_i, bb_r, bb_i = s5_params(s5_lam_re, s5_lam_im, s5_log_dt, s5_b_re, s5_b_im)
    bmat, cmat, coef_r, coef_i = s5_layouts(pow_r, pow_i, bb_r, bb_i, s5_c_re, s5_c_im)
    bias_table = attention_bias_table(rel_bias, seq)

    w_in_b, w_out_b, w_up_b, w_down_b = bf(w_in), bf(w_out), bf(w_up), bf(w_down)
    w_xq_b, w_xo_b = bf(w_xq), bf(w_xo)
    w_glu_b, pool_w_b, w_pw_b = bf(s5_w_glu), bf(pool_w), bf(conv_w_pw)
    norm_mix, s5_dv, pool_sc = vec(norm_mix_g), vec(s5_d), vec(pool_scale)
    b_dw, ln_g, ln_b = vec(conv_b_dw), vec(conv_ln_g), vec(conv_ln_b)
    grp_g, norm_x, norm_mlp = vec(grp_norm_g), vec(norm_x_g), vec(norm_mlp_g)

    h = x.reshape(bsz * seq, d)
    for l in range(depth):
        proj = norm_matmul(h, norm_mix, w_in_b, l)
        y_a = s5_mixer(proj, bmat, cmat, coef_r, coef_i, s5_dv, w_glu_b, l, bsz, seq)
        y_b = pool_mixer(proj, pool_w_b, pool_sc, l, bsz, seq)
        y_c = conv_mixer(proj, conv_w_dw, b_dw, ln_g, ln_b, w_pw_b, l, bsz, seq)
        y_d = dilated_attention(proj, bias_table, bsz, seq)
        h = mix_out(y_a, y_b, y_c, y_d, grp_g, w_out_b, h, l)
        h = cross_attention(h, norm_x, w_xq_b, k_mem, v_mem, w_xo_b, l, bsz, seq)
        h = mlp(h, norm_mlp, w_up_b, w_down_b, l)
    return final_norm(h, norm_final_g[None, :]).reshape(bsz, seq, d)
```

```python
import functools
import math

import numpy as np
import jax
import jax.numpy as jnp
from jax import lax
from jax.experimental import pallas as pl
from jax.experimental.pallas import tpu as pltpu

F32 = jnp.float32
BF16 = jnp.bfloat16

D_MODEL = 2048
DEPTH = 4
GROUP_W = 512
S5_GROUPS = 32
S5_CH = 16
S5_STATE = 64
S5_NSTATE = S5_GROUPS * S5_STATE
S5_SCAN_LANE_BLOCKS = 4
POOL_WINDOWS = (2, 4, 8, 16)
POOL_CH = 128
CONV_WIDTH = 31
ATT_HEADS = 8
ATT_HEAD_DIM = 64
DILATED_PATTERNS = ((128, 1), (512, 4), (2048, 16))
ATT_BLOCK = 128
REL_BUCKETS = 32
REL_MAX_DIST = 2048
MEM_LEN = 256
X_HEADS = 4
X_HEAD_DIM = 128
X_WIDTH = 512
D_FF = 4 * D_MODEL
NORM_EPS = 1e-6
NEG_INF = -1e30
IN_WIDTH = 7 * GROUP_W

SUBLANES = 8
LANES = 128
VMEM_LIMIT = 52 * 1024 * 1024


def _params(*sem):
    return pltpu.CompilerParams(dimension_semantics=sem, vmem_limit_bytes=VMEM_LIMIT)


def _rms(x, g):
    return x * lax.rsqrt(jnp.mean(x * x, axis=-1, keepdims=True) + NORM_EPS) * g


def _norm_matmul_kernel(x_ref, g_ref, w_ref, o_ref, xn_ref):
    @pl.when(pl.program_id(1) == 0)
    def _():
        xn_ref[...] = _rms(x_ref[...], g_ref[...]).astype(BF16)

    o_ref[...] = jnp.dot(xn_ref[...], w_ref[...], preferred_element_type=F32)


def norm_matmul(x, g, w, layer, tm=1024, tn=512):
    m, d = x.shape
    n = w.shape[-1]
    return pl.pallas_call(
        _norm_matmul_kernel,
        grid=(m // tm, n // tn),
        in_specs=[
            pl.BlockSpec((tm, d), lambda i, j: (i, 0)),
            pl.BlockSpec((None, 1, d), lambda i, j: (layer, 0, 0)),
            pl.BlockSpec((None, d, tn), lambda i, j: (layer, 0, j)),
        ],
        out_specs=pl.BlockSpec((tm, tn), lambda i, j: (i, j)),
        out_shape=jax.ShapeDtypeStruct((m, n), F32),
        scratch_shapes=[pltpu.VMEM((tm, d), BF16)],
        compiler_params=_params("parallel", "arbitrary"),
        name="norm_matmul",
    )(x, g, w)


def _s5_param_kernel(lr_ref, li_ref, ldt_ref, lrx_ref, lix_ref, ldtx_ref, br_ref, bi_ref,
                     pr_ref, pi_ref, bbr_ref, bbi_ref):
    def discretise(lr, li, ldt):
        dt = jnp.exp(ldt)
        mag = jnp.exp(lr * dt)
        return mag * jnp.cos(li * dt), mag * jnp.sin(li * dt)

    ar, ai = discretise(lr_ref[...], li_ref[...], ldt_ref[...])
    cr, ci = ar, ai
    for j in range(SUBLANES):
        pr_ref[j] = cr
        pi_ref[j] = ci
        cr, ci = cr * ar - ci * ai, cr * ai + ci * ar

    lr, li = lrx_ref[...], lix_ref[...]
    ar, ai = discretise(lr, li, ldtx_ref[...])
    den = lr * lr + li * li
    nr, ni = ar - 1.0, ai
    f_r = (nr * lr + ni * li) / den
    f_i = (ni * lr - nr * li) / den
    b_r, b_i = br_ref[...], bi_ref[...]
    bbr_ref[...] = f_r * b_r - f_i * b_i
    bbi_ref[...] = f_r * b_i + f_i * b_r


def s5_params(lam_re, lam_im, log_dt, b_re, b_im):
    depth = lam_re.shape[0]
    gc = S5_GROUPS * S5_CH
    rep = lambda a: jnp.repeat(a, S5_CH, axis=1)
    ldt = log_dt[..., None]
    bt = lambda b: jnp.transpose(b, (0, 1, 3, 2)).reshape(depth, gc, S5_STATE)
    small = pl.BlockSpec((None, S5_GROUPS, S5_STATE), lambda l: (l, 0, 0))
    small1 = pl.BlockSpec((None, S5_GROUPS, 1), lambda l: (l, 0, 0))
    big = pl.BlockSpec((None, gc, S5_STATE), lambda l: (l, 0, 0))
    big1 = pl.BlockSpec((None, gc, 1), lambda l: (l, 0, 0))
    powspec = pl.BlockSpec((None, SUBLANES, S5_GROUPS, S5_STATE), lambda l: (l, 0, 0, 0))
    return pl.pallas_call(
        _s5_param_kernel,
        grid=(depth,),
        in_specs=[small, small, small1, big, big, big1, big, big],
        out_specs=[powspec, powspec, big, big],
        out_shape=[
            jax.ShapeDtypeStruct((depth, SUBLANES, S5_GROUPS, S5_STATE), F32),
            jax.ShapeDtypeStruct((depth, SUBLANES, S5_GROUPS, S5_STATE), F32),
            jax.ShapeDtypeStruct((depth, gc, S5_STATE), F32),
            jax.ShapeDtypeStruct((depth, gc, S5_STATE), F32),
        ],
        compiler_params=_params("arbitrary"),
        name="s5_params",
    )(lam_re, lam_im, ldt, rep(lam_re), rep(lam_im), rep(ldt), bt(b_re), bt(b_im))


def s5_layouts(pow_r, pow_i, bb_r, bb_i, c_re, c_im):
    depth = pow_r.shape[0]
    eye = jnp.eye(S5_GROUPS, dtype=F32)[:, None, :, None]

    def bdiag_b(bb):
        t = bb.reshape(depth, S5_GROUPS, S5_CH, 1, S5_STATE) * eye[None]
        return t.reshape(depth, GROUP_W, S5_NSTATE)

    def bdiag_c(c):
        t = jnp.transpose(c, (0, 1, 3, 2)).reshape(depth, S5_GROUPS, S5_STATE, 1, S5_CH) * eye[None]
        return t.reshape(depth, S5_NSTATE, GROUP_W)

    bmat = jnp.concatenate([bdiag_b(bb_r), bdiag_b(bb_i)], axis=2).astype(BF16)
    cmat = jnp.concatenate([bdiag_c(c_re), -bdiag_c(c_im)], axis=1).astype(BF16)

    row = jnp.arange(SUBLANES)[None, :, None]

    def coef(p):
        p = p.reshape(depth, SUBLANES, S5_NSTATE)
        tiles = [jnp.where(row >= s, p[:, s - 1][:, None, :], 0.0) for s in (1, 2, 4)]
        return jnp.stack(tiles + [p], axis=1)

    return bmat, cmat, coef(pow_r), coef(pow_i)


def _s5_kernel(u_ref, bmat_ref, cmat_ref, cfr_ref, cfi_ref, d_ref, wglu_ref, o_ref, xs_ref, *, tl):
    n = S5_NSTATE
    l = pl.program_id(1)

    @pl.when(l == 0)
    def _():
        xs_ref[0:SUBLANES, :] = jnp.zeros((SUBLANES, 2 * n), F32)

    @pl.when(l > 0)
    def _():
        xs_ref[0:SUBLANES, :] = xs_ref[tl:tl + SUBLANES, :]

    u = u_ref[...]
    xs_ref[SUBLANES:, :] = jnp.dot(u.astype(BF16), bmat_ref[...], preferred_element_type=F32)

    def scan_lanes(lb0):
        cols = [(slice(lb * LANES, (lb + 1) * LANES), slice(n + lb * LANES, n + (lb + 1) * LANES))
                for lb in range(lb0, lb0 + S5_SCAN_LANE_BLOCKS)]

        def tile_scan(rt, prev):
            base = pl.multiple_of(SUBLANES + rt * SUBLANES, SUBLANES)
            new = []
            for q, (cr_, ci_) in enumerate(cols):
                zr = xs_ref[pl.ds(base, SUBLANES), cr_]
                zi = xs_ref[pl.ds(base, SUBLANES), ci_]
                for k, s in enumerate((1, 2, 4)):
                    sr = pltpu.roll(zr, s, 0)
                    si = pltpu.roll(zi, s, 0)
                    ar = cfr_ref[k, :, cr_]
                    ai = cfi_ref[k, :, cr_]
                    zr, zi = zr + (ar * sr - ai * si), zi + (ar * si + ai * sr)
                pr = jnp.broadcast_to(prev[2 * q][SUBLANES - 1:SUBLANES, :], (SUBLANES, LANES))
                pi = jnp.broadcast_to(prev[2 * q + 1][SUBLANES - 1:SUBLANES, :], (SUBLANES, LANES))
                ar = cfr_ref[3, :, cr_]
                ai = cfi_ref[3, :, cr_]
                xr = zr + (ar * pr - ai * pi)
                xi = zi + (ar * pi + ai * pr)
                xs_ref[pl.ds(base, SUBLANES), cr_] = xr
                xs_ref[pl.ds(base, SUBLANES), ci_] = xi
                new += [xr, xi]
            return tuple(new)

        init = []
        for cr_, ci_ in cols:
            init += [xs_ref[0:SUBLANES, cr_], xs_ref[0:SUBLANES, ci_]]
        lax.fori_loop(0, tl // SUBLANES, tile_scan, tuple(init))

    for lb0 in range(0, n // LANES, S5_SCAN_LANE_BLOCKS):
        scan_lanes(lb0)

    y = jnp.dot(xs_ref[SUBLANES:, :].astype(BF16), cmat_ref[...], preferred_element_type=F32)
    y = y + d_ref[...] * u
    g = jax.nn.gelu(y)
    gate = jnp.dot(g.astype(BF16), wglu_ref[...], preferred_element_type=F32)
    o_ref[...] = g * jax.nn.sigmoid(gate)


def s5_mixer(proj, bmat, cmat, coef_r, coef_i, d_skip, w_glu, layer, bsz, seq, tl=512):
    nl = seq // tl
    n2 = 2 * S5_NSTATE
    return pl.pallas_call(
        functools.partial(_s5_kernel, tl=tl),
        grid=(bsz, nl),
        in_specs=[
            pl.BlockSpec((tl, GROUP_W), lambda b, l: (b * nl + l, 0)),
            pl.BlockSpec((None, GROUP_W, n2), lambda b, l: (layer, 0, 0)),
            pl.BlockSpec((None, n2, GROUP_W), lambda b, l: (layer, 0, 0)),
            pl.BlockSpec((None, 4, SUBLANES, S5_NSTATE), lambda b, l: (layer, 0, 0, 0)),
            pl.BlockSpec((None, 4, SUBLANES, S5_NSTATE), lambda b, l: (layer, 0, 0, 0)),
            pl.BlockSpec((None, 1, GROUP_W), lambda b, l: (layer, 0, 0)),
            pl.BlockSpec((None, GROUP_W, GROUP_W), lambda b, l: (layer, 0, 0)),
        ],
        out_specs=pl.BlockSpec((tl, GROUP_W), lambda b, l: (b * nl + l, 0)),
        out_shape=jax.ShapeDtypeStruct((bsz * seq, GROUP_W), F32),
        scratch_shapes=[pltpu.VMEM((SUBLANES + tl, n2), F32)],
        compiler_params=_params("parallel", "arbitrary"),
        name="s5_mixer",
    )(proj, bmat, cmat, coef_r, coef_i, d_skip, w_glu)


POOL_HALO = 16


def _pool_kernel(u_ref, pw_ref, sc_ref, o_ref, ext_ref, *, tl):
    l = pl.program_id(1)

    @pl.when(l == 0)
    def _():
        ext_ref[0:POOL_HALO, :] = jnp.zeros((POOL_HALO, GROUP_W), F32)

    @pl.when(l > 0)
    def _():
        ext_ref[0:POOL_HALO, :] = ext_ref[tl:tl + POOL_HALO, :]

    ext_ref[POOL_HALO:, :] = u_ref[...]
    t = l * tl + lax.broadcasted_iota(jnp.int32, (tl, 1), 0)
    for gi, w in enumerate(POOL_WINDOWS):
        cols = slice(gi * POOL_CH, (gi + 1) * POOL_CH)
        acc = ext_ref[pl.ds(POOL_HALO, tl), cols]
        for k in range(1, w):
            acc = acc + ext_ref[pl.ds(POOL_HALO - k, tl), cols]
        cnt = jnp.minimum(t + 1, w).astype(F32)
        p = acc / cnt - u_ref[:, cols]
        y = jnp.dot(p.astype(BF16), pw_ref[gi], preferred_element_type=F32)
        o_ref[:, cols] = y * sc_ref[:, cols]


def pool_mixer(proj, pool_w, pool_scale, layer, bsz, seq, tl=512):
    nl = seq // tl
    ng = len(POOL_WINDOWS)
    return pl.pallas_call(
        functools.partial(_pool_kernel, tl=tl),
        grid=(bsz, nl),
        in_specs=[
            pl.BlockSpec((tl, GROUP_W), lambda b, l: (b * nl + l, 1)),
            pl.BlockSpec((None, ng, POOL_CH, POOL_CH), lambda b, l: (layer, 0, 0, 0)),
            pl.BlockSpec((None, 1, GROUP_W), lambda b, l: (layer, 0, 0)),
        ],
        out_specs=pl.BlockSpec((tl, GROUP_W), lambda b, l: (b * nl + l, 0)),
        out_shape=jax.ShapeDtypeStruct((bsz * seq, GROUP_W), F32),
        scratch_shapes=[pltpu.VMEM((POOL_HALO + tl, GROUP_W), F32)],
        compiler_params=_params("parallel", "arbitrary"),
        name="pool_mixer",
    )(proj, pool_w, pool_scale)


CONV_HALO = 32
CONV_ROWS = 32


def _conv_kernel(val_ref, gate_ref, wdw_ref, bdw_ref, lng_ref, lnb_ref, wpw_ref, o_ref,
                 ext_ref, cv_ref, *, tl):
    l = pl.program_id(1)

    @pl.when(l == 0)
    def _():
        ext_ref[0:CONV_HALO, :] = jnp.zeros((CONV_HALO, GROUP_W), F32)

    @pl.when(l > 0)
    def _():
        ext_ref[0:CONV_HALO, :] = ext_ref[tl:tl + CONV_HALO, :]

    ext_ref[CONV_HALO:, :] = val_ref[...] * jax.nn.sigmoid(gate_ref[...])

    off = CONV_HALO - (CONV_WIDTH - 1)

    for r0 in range(0, tl, CONV_ROWS):
        acc = jnp.broadcast_to(bdw_ref[...], (CONV_ROWS, GROUP_W))
        for k in range(CONV_WIDTH):
            acc = acc + wdw_ref[k:k + 1, :] * ext_ref[r0 + off + k:r0 + off + k + CONV_ROWS, :]
        cv_ref[r0:r0 + CONV_ROWS, :] = acc

    h = cv_ref[...]
    hc = h - jnp.mean(h, axis=-1, keepdims=True)
    y = hc * lax.rsqrt(jnp.mean(hc * hc, axis=-1, keepdims=True) + NORM_EPS)
    y = y * lng_ref[...] + lnb_ref[...]
    y = jax.nn.silu(y)
    o_ref[...] = jnp.dot(y.astype(BF16), wpw_ref[...], preferred_element_type=F32)


def conv_mixer(proj, w_dw, b_dw, ln_g, ln_b, w_pw, layer, bsz, seq, tl=256):
    nl = seq // tl
    vec = pl.BlockSpec((None, 1, GROUP_W), lambda b, l: (layer, 0, 0))
    return pl.pallas_call(
        functools.partial(_conv_kernel, tl=tl),
        grid=(bsz, nl),
        in_specs=[
            pl.BlockSpec((tl, GROUP_W), lambda b, l: (b * nl + l, 2)),
            pl.BlockSpec((tl, GROUP_W), lambda b, l: (b * nl + l, 3)),
            pl.BlockSpec((None, CONV_WIDTH, GROUP_W), lambda b, l: (layer, 0, 0)),
            vec, vec, vec,
            pl.BlockSpec((None, GROUP_W, GROUP_W), lambda b, l: (layer, 0, 0)),
        ],
        out_specs=pl.BlockSpec((tl, GROUP_W), lambda b, l: (b * nl + l, 0)),
        out_shape=jax.ShapeDtypeStruct((bsz * seq, GROUP_W), F32),
        scratch_shapes=[pltpu.VMEM((CONV_HALO + tl, GROUP_W), F32),
                        pltpu.VMEM((tl, GROUP_W), F32)],
        compiler_params=_params("parallel", "arbitrary"),
        name="conv_mixer",
    )(proj, proj, w_dw, b_dw, ln_g, ln_b, w_pw)


def _t5_bucket(dist):
    n = np.maximum(dist, 0)
    max_exact = REL_BUCKETS // 2
    large = max_exact + (np.log(np.maximum(n, 1) / max_exact) / np.log(REL_MAX_DIST / max_exact)
                         * (REL_BUCKETS - max_exact)).astype(np.int64)
    large = np.minimum(large, REL_BUCKETS - 1)
    return np.where(n < max_exact, n, large).astype(np.int32)


ATT_NEAR_BLOCKS = 5
ATT_NEAR = ATT_NEAR_BLOCKS * ATT_BLOCK
ATT_FAR_DIL = 16


def _attention_bias_layout(sub):
    a = np.arange(ATT_BLOCK)[:, None]
    d_near = (ATT_NEAR - ATT_BLOCK) + a - np.arange(ATT_NEAR)[None, :]
    mult = np.zeros(d_near.shape, np.int64)
    for window, dil in DILATED_PATTERNS:
        if dil != ATT_FAR_DIL:
            mult += (d_near >= 0) & (d_near % dil == 0) & (d_near // dil <= window // dil)
    d_far = ATT_FAR_DIL * (a - np.arange(sub)[None, :])
    mult_far = ((d_far >= 0) & (d_far // ATT_FAR_DIL <= ATT_BLOCK)).astype(np.int64)
    d = np.concatenate([d_near, d_far], axis=1)
    mult = np.concatenate([mult, mult_far], axis=1)
    addend = np.where(mult > 0, np.log(np.maximum(mult, 1)), NEG_INF).astype(np.float32)
    return _t5_bucket(d), addend


def _attn_bias_kernel(rb_ref, bucket_ref, add_ref, o_ref):
    h = pl.program_id(0)
    bucket = bucket_ref[...]
    t = jnp.zeros(bucket.shape, F32)
    for b in range(REL_BUCKETS):
        t = jnp.where(bucket == b, rb_ref[b, h], t)
    o_ref[...] = t + add_ref[...]


def attention_bias_table(rel_bias, seq):
    sub = seq // ATT_FAR_DIL
    bucket, addend = _attention_bias_layout(sub)
    width = ATT_NEAR + sub
    full = pl.BlockSpec((ATT_BLOCK, width), lambda h: (0, 0))
    return pl.pallas_call(
        _attn_bias_kernel,
        grid=(ATT_HEADS,),
        in_specs=[pl.BlockSpec(memory_space=pltpu.SMEM), full, full],
        out_specs=pl.BlockSpec((None, ATT_BLOCK, width), lambda h: (h, 0, 0)),
        out_shape=jax.ShapeDtypeStruct((ATT_HEADS, ATT_BLOCK, width), F32),
        compiler_params=_params("arbitrary"),
        name="attn_bias",
    )(rel_bias, jnp.asarray(bucket), jnp.asarray(addend))


ATT_HEADS_PER_BLOCK = LANES // ATT_HEAD_DIM
ATT_GROUP = 4


def _nt_dot(a, b):
    return lax.dot_general(a, b, (((1,), (1,)), ((), ())), preferred_element_type=F32)


def _attn_kernel(q_ref, k_ref, v_ref, bt_ref, o_ref, kb_ref, vb_ref, m3_ref, l3_ref, a3_ref, *, seq):
    sub = seq // ATT_FAR_DIL
    pad = ATT_NEAR - ATT_BLOCK
    scale = ATT_HEAD_DIM ** -0.5
    lane = lax.broadcasted_iota(jnp.int32, (1, LANES), 1)
    in_head = [(lane >= hh * ATT_HEAD_DIM) & (lane < (hh + 1) * ATT_HEAD_DIM)
               for hh in range(ATT_HEADS_PER_BLOCK)]

    def by_head(vals):
        out = vals[-1]
        for hh in range(ATT_HEADS_PER_BLOCK - 2, -1, -1):
            out = jnp.where(in_head[hh], vals[hh], out)
        return out

    def softmax_pv(tiles):
        heads = range(ATT_HEADS_PER_BLOCK)
        s = [[_nt_dot(jnp.where(in_head[hh], qs, 0.0).astype(BF16), kb) + bias_of(hh) for hh in heads]
             for qs, kb, _, bias_of in tiles]
        m = [[jnp.max(s_h, axis=-1, keepdims=True) for s_h in s_t] for s_t in s]
        p = [[jnp.exp(s_h - m_h) for s_h, m_h in zip(s_t, m_t)] for s_t, m_t in zip(s, m)]
        l = [[jnp.sum(p_h, axis=-1, keepdims=True) for p_h in p_t] for p_t in p]
        acc = [[jnp.dot(p_h.astype(BF16), tile[2], preferred_element_type=F32) for p_h in p_t]
               for p_t, tile in zip(p, tiles)]
        out = []
        for tile, m_t, l_t, acc_t in zip(tiles, m, l, acc):
            full = (tile[0].shape[0], LANES)
            out.append((by_head([jnp.broadcast_to(x, full) for x in m_t]),
                        by_head([jnp.broadcast_to(x, full) for x in l_t]), by_head(acc_t)))
        return out

    kb_ref[...] = k_ref[...].astype(BF16)
    vb_ref[...] = v_ref[...].astype(BF16)

    for g0 in range(0, ATT_FAR_DIL, ATT_GROUP):
        group = [pl.ds(r, sub, stride=ATT_FAR_DIL) for r in range(g0, g0 + ATT_GROUP)]
        far_bias = lambda hh: bt_ref[hh, 0:sub, ATT_NEAR:ATT_NEAR + sub]
        tiles = [(q_ref[rows, :] * scale, k_ref[rows, :].astype(BF16), v_ref[rows, :].astype(BF16), far_bias)
                 for rows in group]
        for rows, (m, l, acc) in zip(group, softmax_pv(tiles)):
            m3_ref[rows, :] = m
            l3_ref[rows, :] = l
            a3_ref[rows, :] = acc

    for g0 in range(0, seq // ATT_BLOCK, ATT_GROUP):
        group, tiles = [], []
        for i in range(g0, g0 + ATT_GROUP):
            r0 = i * ATT_BLOCK
            lo = max(0, r0 - pad)
            c0 = ATT_NEAR - (r0 + ATT_BLOCK - lo)
            group.append(slice(r0, r0 + ATT_BLOCK))
            tiles.append((q_ref[group[-1], :] * scale, kb_ref[lo:r0 + ATT_BLOCK, :], vb_ref[lo:r0 + ATT_BLOCK, :],
                          lambda hh, c0=c0: bt_ref[hh, :, c0:ATT_NEAR]))
        merged = []
        for rows, (m, l, acc) in zip(group, softmax_pv(tiles)):
            m3 = m3_ref[rows, :]
            mm = jnp.maximum(m, m3)
            wn = jnp.exp(m - mm)
            wf = jnp.exp(m3 - mm)
            num = acc * wn + a3_ref[rows, :] * wf
            den = l * wn + l3_ref[rows, :] * wf
            merged.append(num / den)
        for rows, o in zip(group, merged):
            o_ref[rows, :] = o


def dilated_attention(proj, bias_table, bsz, seq):
    assert seq % ATT_BLOCK == 0 and seq // ATT_FAR_DIL <= ATT_BLOCK
    nhp = ATT_HEADS // ATT_HEADS_PER_BLOCK
    qcol = 4 * GROUP_W // LANES
    kcol = 5 * GROUP_W // LANES
    vcol = 6 * GROUP_W // LANES
    width = bias_table.shape[-1]
    stat = pltpu.VMEM((seq, LANES), F32)
    half = pltpu.VMEM((seq, LANES), BF16)
    return pl.pallas_call(
        functools.partial(_attn_kernel, seq=seq),
        grid=(bsz, nhp),
        in_specs=[
            pl.BlockSpec((seq, LANES), lambda b, h: (b, qcol + h)),
            pl.BlockSpec((seq, LANES), lambda b, h: (b, kcol + h)),
            pl.BlockSpec((seq, LANES), lambda b, h: (b, vcol + h)),
            pl.BlockSpec((ATT_HEADS_PER_BLOCK, ATT_BLOCK, width), lambda b, h: (h, 0, 0)),
        ],
        out_specs=pl.BlockSpec((seq, LANES), lambda b, h: (b, h)),
        out_shape=jax.ShapeDtypeStruct((bsz * seq, GROUP_W), F32),
        scratch_shapes=[half, half, stat, stat, stat],
        compiler_params=_params("parallel", "parallel"),
        name="dilated_attention",
    )(proj, proj, proj, bias_table)


def _mix_out_kernel(ya_ref, yb_ref, yc_ref, yd_ref, g_ref, w_ref, h_ref, o_ref, yn_ref):
    @pl.when(pl.program_id(1) == 0)
    def _():
        for gi, y_ref in enumerate((ya_ref, yb_ref, yc_ref, yd_ref)):
            cols = slice(gi * GROUP_W, (gi + 1) * GROUP_W)
            yn_ref[:, cols] = _rms(y_ref[...], g_ref[:, cols]).astype(BF16)

    o_ref[...] = h_ref[...] + jnp.dot(yn_ref[...], w_ref[...], preferred_element_type=F32)


def mix_out(ya, yb, yc, yd, g, w, h, layer, tm=512, tn=1024):
    m, d = h.shape
    yspec = pl.BlockSpec((tm, GROUP_W), lambda i, j: (i, 0))
    return pl.pallas_call(
        _mix_out_kernel,
        grid=(m // tm, d // tn),
        in_specs=[
            yspec, yspec, yspec, yspec,
            pl.BlockSpec((None, 1, d), lambda i, j: (layer, 0, 0)),
            pl.BlockSpec((None, d, tn), lambda i, j: (layer, 0, j)),
            pl.BlockSpec((tm, tn), lambda i, j: (i, j)),
        ],
        out_specs=pl.BlockSpec((tm, tn), lambda i, j: (i, j)),
        out_shape=jax.ShapeDtypeStruct((m, d), F32),
        scratch_shapes=[pltpu.VMEM((tm, d), BF16)],
        compiler_params=_params("parallel", "arbitrary"),
        name="mix_out",
    )(ya, yb, yc, yd, g, w, h)


def _mem_kv_kernel(mem_ref, g_ref, wk_ref, wv_ref, k_ref, v_ref):
    mn = _rms(mem_ref[...], g_ref[...]).astype(BF16)
    k_ref[...] = jnp.dot(mn, wk_ref[...], preferred_element_type=F32).astype(BF16)
    v_ref[...] = jnp.dot(mn, wv_ref[...], preferred_element_type=F32).astype(BF16)


def mem_kv(mem, g, wk, wv):
    m, d = mem.shape
    depth = wk.shape[0]
    wspec = pl.BlockSpec((None, d, X_WIDTH), lambda l: (l, 0, 0))
    ospec = pl.BlockSpec((None, m, X_WIDTH), lambda l: (l, 0, 0))
    return pl.pallas_call(
        _mem_kv_kernel,
        grid=(depth,),
        in_specs=[pl.BlockSpec((m, d), lambda l: (0, 0)), pl.BlockSpec((1, d), lambda l: (0, 0)),
                  wspec, wspec],
        out_specs=[ospec, ospec],
        out_shape=[jax.ShapeDtypeStruct((depth, m, X_WIDTH), BF16)] * 2,
        compiler_params=_params("arbitrary"),
        name="mem_kv",
    )(mem, g, wk, wv)


def _xattn_kernel(h_ref, g_ref, wq_ref, k_ref, v_ref, wo_ref, o_ref):
    h = h_ref[...]
    hn = _rms(h, g_ref[...]).astype(BF16)
    q = jnp.dot(hn, wq_ref[...], preferred_element_type=F32) * (X_HEAD_DIM ** -0.5)
    outs = []
    for hd in range(X_HEADS):
        cols = slice(hd * X_HEAD_DIM, (hd + 1) * X_HEAD_DIM)
        s = lax.dot_general(q[:, cols].astype(BF16), k_ref[:, cols], (((1,), (1,)), ((), ())),
                            preferred_element_type=F32)
        p = jnp.exp(s - jnp.max(s, axis=-1, keepdims=True))
        p = p / jnp.sum(p, axis=-1, keepdims=True)
        outs.append(jnp.dot(p.astype(BF16), v_ref[:, cols], preferred_element_type=F32))
    o = jnp.concatenate(outs, axis=-1).astype(BF16)
    o_ref[...] = h + jnp.dot(o, wo_ref[...], preferred_element_type=F32)


def cross_attention(h, g, wq, k, v, wo, layer, bsz, seq, tm=512):
    d = h.shape[-1]
    nl = seq // tm
    mlen = k.shape[1] // bsz
    return pl.pallas_call(
        _xattn_kernel,
        grid=(bsz, nl),
        in_specs=[
            pl.BlockSpec((tm, d), lambda b, l: (b * nl + l, 0)),
            pl.BlockSpec((None, 1, d), lambda b, l: (layer, 0, 0)),
            pl.BlockSpec((None, d, X_WIDTH), lambda b, l: (layer, 0, 0)),
            pl.BlockSpec((None, mlen, X_WIDTH), lambda b, l: (layer, b, 0)),
            pl.BlockSpec((None, mlen, X_WIDTH), lambda b, l: (layer, b, 0)),
            pl.BlockSpec((None, X_WIDTH, d), lambda b, l: (layer, 0, 0)),
        ],
        out_specs=pl.BlockSpec((tm, d), lambda b, l: (b * nl + l, 0)),
        out_shape=jax.ShapeDtypeStruct(h.shape, F32),
        compiler_params=_params("parallel", "arbitrary"),
        name="cross_attention",
    )(h, g, wq, k, v, wo)


def _mlp_kernel(h_ref, g_ref, wu_ref, wd_ref, o_ref, hn_ref):
    @pl.when(pl.program_id(1) == 0)
    def _():
        h = h_ref[...]
        hn_ref[...] = _rms(h, g_ref[...]).astype(BF16)
        o_ref[...] = h

    a = jnp.dot(hn_ref[...], wu_ref[...], preferred_element_type=F32)
    a = jnp.square(jnp.maximum(a, 0.0)).astype(BF16)
    o_ref[...] += jnp.dot(a, wd_ref[...], preferred_element_type=F32)


def mlp(h, g, wu, wd, layer, tm=1024, tf=512):
    m, d = h.shape
    f = wu.shape[-1]
    return pl.pallas_call(
        _mlp_kernel,
        grid=(m // tm, f // tf),
        in_specs=[
            pl.BlockSpec((tm, d), lambda i, j: (i, 0)),
            pl.BlockSpec((None, 1, d), lambda i, j: (layer, 0, 0)),
            pl.BlockSpec((None, d, tf), lambda i, j: (layer, 0, j)),
            pl.BlockSpec((None, tf, d), lambda i, j: (layer, j, 0)),
        ],
        out_specs=pl.BlockSpec((tm, d), lambda i, j: (i, 0)),
        out_shape=jax.ShapeDtypeStruct((m, d), F32),
        scratch_shapes=[pltpu.VMEM((tm, d), BF16)],
        compiler_params=_params("parallel", "arbitrary"),
        name="mlp",
    )(h, g, wu, wd)


def _final_norm_kernel(h_ref, g_ref, o_ref):
    o_ref[...] = _rms(h_ref[...], g_ref[...])


def final_norm(h, g, tm=512):
    m, d = h.shape
    return pl.pallas_call(
        _final_norm_kernel,
        grid=(m // tm,),
        in_specs=[pl.BlockSpec((tm, d), lambda i: (i, 0)), pl.BlockSpec((1, d), lambda i: (0, 0))],
        out_specs=pl.BlockSpec((tm, d), lambda i: (i, 0)),
        out_shape=jax.ShapeDtypeStruct((m, d), F32),
        compiler_params=_params("parallel"),
        name="final_norm",
    )(h, g)


def kernel(x, mem, rel_bias, mem_norm_g, norm_mix_g, w_in, s5_lam_re, s5_lam_im, s5_log_dt, s5_b_re, s5_b_im, s5_c_re, s5_c_im, s5_d, s5_w_glu, pool_w, pool_scale, conv_w_dw, conv_b_dw, conv_ln_g, conv_ln_b, conv_w_pw, grp_norm_g, w_out, norm_x_g, w_xq, w_xk, w_xv, w_xo, norm_mlp_g, w_up, w_down, norm_final_g):
    bsz, seq, d = x.shape
    depth = w_in.shape[0]
    bf = lambda a: a.astype(BF16)
    vec = lambda a: a[:, None, :]

    k_mem, v_mem = mem_kv(mem.reshape(bsz * mem.shape[1], d), mem_norm_g[None, :], bf(w_xk), bf(w_xv))
    pow_r, pow_i, bb_r, bb_i = s5_params(s5_lam_re, s5_lam_im, s5_log_dt, s5_b_re, s5_b_im)
    bmat, cmat, coef_r, coef_i = s5_layouts(pow_r, pow_i, bb_r, bb_i, s5_c_re, s5_c_im)
    bias_table = attention_bias_table(rel_bias, seq)

    w_in_b, w_out_b, w_up_b, w_down_b = bf(w_in), bf(w_out), bf(w_up), bf(w_down)
    w_xq_b, w_xo_b = bf(w_xq), bf(w_xo)
    w_glu_b, pool_w_b, w_pw_b = bf(s5_w_glu), bf(pool_w), bf(conv_w_pw)
    norm_mix, s5_dv, pool_sc = vec(norm_mix_g), vec(s5_d), vec(pool_scale)
    b_dw, ln_g, ln_b = vec(conv_b_dw), vec(conv_ln_g), vec(conv_ln_b)
    grp_g, norm_x, norm_mlp = vec(grp_norm_g), vec(norm_x_g), vec(norm_mlp_g)

    h = x.reshape(bsz * seq, d)
    for l in range(depth):
        proj = norm_matmul(h, norm_mix, w_in_b, l)
        y_a = s5_mixer(proj, bmat, cmat, coef_r, coef_i, s5_dv, w_glu_b, l, bsz, seq)
        y_b = pool_mixer(proj, pool_w_b, pool_sc, l, bsz, seq)
        y_c = conv_mixer(proj, conv_w_dw, b_dw, ln_g, ln_b, w_pw_b, l, bsz, seq)
        y_d = dilated_attention(proj, bias_table, bsz, seq)
        h = mix_out(y_a, y_b, y_c, y_d, grp_g, w_out_b, h, l)
        h = cross_attention(h, norm_x, w_xq_b, k_mem, v_mem, w_xo_b, l, bsz, seq)
        h = mlp(h, norm_mlp, w_up_b, w_down_b, l)
    return final_norm(h, norm_final_g[None, :]).reshape(bsz, seq, d)
```

```python
import functools
import math

import numpy as np
import jax
import jax.numpy as jnp
from jax import lax
from jax.experimental import pallas as pl
from jax.experimental.pallas import tpu as pltpu

F32 = jnp.float32
BF16 = jnp.bfloat16

D_MODEL = 2048
DEPTH = 4
GROUP_W = 512
S5_GROUPS = 32
S5_CH = 16
S5_STATE = 64
S5_NSTATE = S5_GROUPS * S5_STATE
S5_SCAN_LANE_BLOCKS = 4
POOL_WINDOWS = (2, 4, 8, 16)
POOL_CH = 128
CONV_WIDTH = 31
ATT_HEADS = 8
ATT_HEAD_DIM = 64
DILATED_PATTERNS = ((128, 1), (512, 4), (2048, 16))
ATT_BLOCK = 128
REL_BUCKETS = 32
REL_MAX_DIST = 2048
MEM_LEN = 256
X_HEADS = 4
X_HEAD_DIM = 128
X_WIDTH = 512
D_FF = 4 * D_MODEL
NORM_EPS = 1e-6
NEG_INF = -1e30
IN_WIDTH = 7 * GROUP_W

SUBLANES = 8
LANES = 128
VMEM_LIMIT = 52 * 1024 * 1024


def _params(*sem):
    return pltpu.CompilerParams(dimension_semantics=sem, vmem_limit_bytes=VMEM_LIMIT)


def _rms(x, g):
    return x * lax.rsqrt(jnp.mean(x * x, axis=-1, keepdims=True) + NORM_EPS) * g


def _norm_matmul_kernel(x_ref, g_ref, w_ref, o_ref, xn_ref):
    @pl.when(pl.program_id(1) == 0)
    def _():
        xn_ref[...] = _rms(x_ref[...], g_ref[...]).astype(BF16)

    o_ref[...] = jnp.dot(xn_ref[...], w_ref[...], preferred_element_type=F32)


def norm_matmul(x, g, w, layer, tm=1024, tn=1792):
    m, d = x.shape
    n = w.shape[-1]
    return pl.pallas_call(
        _norm_matmul_kernel,
        grid=(m // tm, n // tn),
        in_specs=[
            pl.BlockSpec((tm, d), lambda i, j: (i, 0)),
            pl.BlockSpec((None, 1, d), lambda i, j: (layer, 0, 0)),
            pl.BlockSpec((None, d, tn), lambda i, j: (layer, 0, j)),
        ],
        out_specs=pl.BlockSpec((tm, tn), lambda i, j: (i, j)),
        out_shape=jax.ShapeDtypeStruct((m, n), F32),
        scratch_shapes=[pltpu.VMEM((tm, d), BF16)],
        compiler_params=_params("parallel", "arbitrary"),
        name="norm_matmul",
    )(x, g, w)


def _s5_param_kernel(lr_ref, li_ref, ldt_ref, lrx_ref, lix_ref, ldtx_ref, br_ref, bi_ref, cr_ref, ci_ref,
                     pr_ref, pi_ref, bmat_ref, cmat_ref):
    def discretise(lr, li, ldt):
        dt = jnp.exp(ldt)
        mag = jnp.exp(lr * dt)
        return mag * jnp.cos(li * dt), mag * jnp.sin(li * dt)

    ar, ai = discretise(lr_ref[...], li_ref[...], ldt_ref[...])
    cr, ci = ar, ai
    for j in range(SUBLANES):
        pr_ref[j] = cr
        pi_ref[j] = ci
        cr, ci = cr * ar - ci * ai, cr * ai + ci * ar

    lr, li = lrx_ref[...], lix_ref[...]
    ar, ai = discretise(lr, li, ldtx_ref[...])
    den = lr * lr + li * li
    nr, ni = ar - 1.0, ai
    f_r = (nr * lr + ni * li) / den
    f_i = (ni * lr - nr * li) / den
    b_r, b_i = br_ref[...], bi_ref[...]
    bb_r = f_r * b_r - f_i * b_i
    bb_i = f_r * b_i + f_i * b_r

    def block_diagonal(x, row_shift, col_shift):
        rows, w = x.shape
        cols = (rows >> row_shift) << col_shift
        sel = ((lax.broadcasted_iota(jnp.int32, (w, cols), 1) & (w - 1))
               == lax.broadcasted_iota(jnp.int32, (w, cols), 0))
        tiled = jnp.dot(x.astype(BF16), jnp.where(sel, 1.0, 0.0).astype(BF16), preferred_element_type=F32)
        same = ((lax.broadcasted_iota(jnp.int32, (rows, cols), 0) >> row_shift)
                == (lax.broadcasted_iota(jnp.int32, (rows, cols), 1) >> col_shift))
        return jnp.where(same, tiled, 0.0).astype(BF16)

    ch_shift, st_shift = S5_CH.bit_length() - 1, S5_STATE.bit_length() - 1
    bmat_ref[:, 0:S5_NSTATE] = block_diagonal(bb_r, ch_shift, st_shift)
    bmat_ref[:, S5_NSTATE:] = block_diagonal(bb_i, ch_shift, st_shift)
    cmat_ref[0:S5_NSTATE, :] = block_diagonal(cr_ref[...], st_shift, ch_shift)
    cmat_ref[S5_NSTATE:, :] = block_diagonal(-ci_ref[...], st_shift, ch_shift)


def s5_params(lam_re, lam_im, log_dt, b_re, b_im, c_re, c_im):
    depth = lam_re.shape[0]
    gc = S5_GROUPS * S5_CH
    rep = lambda a: jnp.repeat(a, S5_CH, axis=1)
    ldt = log_dt[..., None]
    bt = lambda b: jnp.transpose(b, (0, 1, 3, 2)).reshape(depth, gc, S5_STATE)
    ct = lambda c: jnp.transpose(c, (0, 1, 3, 2)).reshape(depth, S5_NSTATE, S5_CH)
    small = pl.BlockSpec((None, S5_GROUPS, S5_STATE), lambda l: (l, 0, 0))
    small1 = pl.BlockSpec((None, S5_GROUPS, 1), lambda l: (l, 0, 0))
    big = pl.BlockSpec((None, gc, S5_STATE), lambda l: (l, 0, 0))
    big1 = pl.BlockSpec((None, gc, 1), lambda l: (l, 0, 0))
    cspec = pl.BlockSpec((None, S5_NSTATE, S5_CH), lambda l: (l, 0, 0))
    powspec = pl.BlockSpec((None, SUBLANES, S5_GROUPS, S5_STATE), lambda l: (l, 0, 0, 0))
    return pl.pallas_call(
        _s5_param_kernel,
        grid=(depth,),
        in_specs=[small, small, small1, big, big, big1, big, big, cspec, cspec],
        out_specs=[powspec, powspec,
                   pl.BlockSpec((None, gc, 2 * S5_NSTATE), lambda l: (l, 0, 0)),
                   pl.BlockSpec((None, 2 * S5_NSTATE, gc), lambda l: (l, 0, 0))],
        out_shape=[
            jax.ShapeDtypeStruct((depth, SUBLANES, S5_GROUPS, S5_STATE), F32),
            jax.ShapeDtypeStruct((depth, SUBLANES, S5_GROUPS, S5_STATE), F32),
            jax.ShapeDtypeStruct((depth, gc, 2 * S5_NSTATE), BF16),
            jax.ShapeDtypeStruct((depth, 2 * S5_NSTATE, gc), BF16),
        ],
        compiler_params=_params("arbitrary"),
        name="s5_params",
    )(lam_re, lam_im, ldt, rep(lam_re), rep(lam_im), rep(ldt), bt(b_re), bt(b_im), ct(c_re), ct(c_im))


def s5_scan_coefficients(pow_r, pow_i):
    depth = pow_r.shape[0]
    row = jnp.arange(SUBLANES)[None, :, None]

    def coef(p):
        p = p.reshape(depth, SUBLANES, S5_NSTATE)
        tiles = [jnp.where(row >= s, p[:, s - 1][:, None, :], 0.0) for s in (1, 2, 4)]
        return jnp.stack(tiles + [p], axis=1)

    return coef(pow_r), coef(pow_i)


def _s5_kernel(u_ref, bmat_ref, cmat_ref, cfr_ref, cfi_ref, d_ref, wglu_ref, o_ref, xs_ref, *, tl):
    n = S5_NSTATE
    l = pl.program_id(1)

    @pl.when(l == 0)
    def _():
        xs_ref[0:SUBLANES, :] = jnp.zeros((SUBLANES, 2 * n), F32)

    @pl.when(l > 0)
    def _():
        xs_ref[0:SUBLANES, :] = xs_ref[tl:tl + SUBLANES, :]

    u = u_ref[...]
    xs_ref[SUBLANES:, :] = jnp.dot(u.astype(BF16), bmat_ref[...], preferred_element_type=F32)

    def scan_lanes(lb0):
        cols = [(slice(lb * LANES, (lb + 1) * LANES), slice(n + lb * LANES, n + (lb + 1) * LANES))
                for lb in range(lb0, lb0 + S5_SCAN_LANE_BLOCKS)]

        def tile_scan(rt, prev):
            base = pl.multiple_of(SUBLANES + rt * SUBLANES, SUBLANES)
            new = []
            for q, (cr_, ci_) in enumerate(cols):
                zr = xs_ref[pl.ds(base, SUBLANES), cr_]
                zi = xs_ref[pl.ds(base, SUBLANES), ci_]
                for k, s in enumerate((1, 2, 4)):
                    sr = pltpu.roll(zr, s, 0)
                    si = pltpu.roll(zi, s, 0)
                    ar = cfr_ref[k, :, cr_]
                    ai = cfi_ref[k, :, cr_]
                    zr, zi = zr + (ar * sr - ai * si), zi + (ar * si + ai * sr)
                pr = jnp.broadcast_to(prev[2 * q][SUBLANES - 1:SUBLANES, :], (SUBLANES, LANES))
                pi = jnp.broadcast_to(prev[2 * q + 1][SUBLANES - 1:SUBLANES, :], (SUBLANES, LANES))
                ar = cfr_ref[3, :, cr_]
                ai = cfi_ref[3, :, cr_]
                xr = zr + (ar * pr - ai * pi)
                xi = zi + (ar * pi + ai * pr)
                xs_ref[pl.ds(base, SUBLANES), cr_] = xr
                xs_ref[pl.ds(base, SUBLANES), ci_] = xi
                new += [xr, xi]
            return tuple(new)

        init = []
        for cr_, ci_ in cols:
            init += [xs_ref[0:SUBLANES, cr_], xs_ref[0:SUBLANES, ci_]]
        lax.fori_loop(0, tl // SUBLANES, tile_scan, tuple(init))

    for lb0 in range(0, n // LANES, S5_SCAN_LANE_BLOCKS):
        scan_lanes(lb0)

    y = jnp.dot(xs_ref[SUBLANES:, :].astype(BF16), cmat_ref[...], preferred_element_type=F32)
    y = y + d_ref[...] * u
    g = jax.nn.gelu(y)
    gate = jnp.dot(g.astype(BF16), wglu_ref[...], preferred_element_type=F32)
    o_ref[...] = g * jax.nn.sigmoid(gate)


def s5_mixer(proj, bmat, cmat, coef_r, coef_i, d_skip, w_glu, layer, bsz, seq, tl=512):
    nl = seq // tl
    n2 = 2 * S5_NSTATE
    return pl.pallas_call(
        functools.partial(_s5_kernel, tl=tl),
        grid=(bsz, nl),
        in_specs=[
            pl.BlockSpec((tl, GROUP_W), lambda b, l: (b * nl + l, 0)),
            pl.BlockSpec((None, GROUP_W, n2), lambda b, l: (layer, 0, 0)),
            pl.BlockSpec((None, n2, GROUP_W), lambda b, l: (layer, 0, 0)),
            pl.BlockSpec((None, 4, SUBLANES, S5_NSTATE), lambda b, l: (layer, 0, 0, 0)),
            pl.BlockSpec((None, 4, SUBLANES, S5_NSTATE), lambda b, l: (layer, 0, 0, 0)),
            pl.BlockSpec((None, 1, GROUP_W), lambda b, l: (layer, 0, 0)),
            pl.BlockSpec((None, GROUP_W, GROUP_W), lambda b, l: (layer, 0, 0)),
        ],
        out_specs=pl.BlockSpec((tl, GROUP_W), lambda b, l: (b * nl + l, 0)),
        out_shape=jax.ShapeDtypeStruct((bsz * seq, GROUP_W), F32),
        scratch_shapes=[pltpu.VMEM((SUBLANES + tl, n2), F32)],
        compiler_params=_params("parallel", "arbitrary"),
        name="s5_mixer",
    )(proj, bmat, cmat, coef_r, coef_i, d_skip, w_glu)


POOL_HALO = 16


def _pool_kernel(u_ref, pw_ref, sc_ref, o_ref, ext_ref, *, tl):
    l = pl.program_id(1)

    @pl.when(l == 0)
    def _():
        ext_ref[0:POOL_HALO, :] = jnp.zeros((POOL_HALO, GROUP_W), F32)

    @pl.when(l > 0)
    def _():
        ext_ref[0:POOL_HALO, :] = ext_ref[tl:tl + POOL_HALO, :]

    ext_ref[POOL_HALO:, :] = u_ref[...]
    t = l * tl + lax.broadcasted_iota(jnp.int32, (tl, 1), 0)
    for gi, w in enumerate(POOL_WINDOWS):
        cols = slice(gi * POOL_CH, (gi + 1) * POOL_CH)
        acc = ext_ref[pl.ds(POOL_HALO, tl), cols]
        for k in range(1, w):
            acc = acc + ext_ref[pl.ds(POOL_HALO - k, tl), cols]
        cnt = jnp.minimum(t + 1, w).astype(F32)
        p = acc / cnt - u_ref[:, cols]
        y = jnp.dot(p.astype(BF16), pw_ref[gi], preferred_element_type=F32)
        o_ref[:, cols] = y * sc_ref[:, cols]


def pool_mixer(proj, pool_w, pool_scale, layer, bsz, seq, tl=512):
    nl = seq // tl
    ng = len(POOL_WINDOWS)
    return pl.pallas_call(
        functools.partial(_pool_kernel, tl=tl),
        grid=(bsz, nl),
        in_specs=[
            pl.BlockSpec((tl, GROUP_W), lambda b, l: (b * nl + l, 1)),
            pl.BlockSpec((None, ng, POOL_CH, POOL_CH), lambda b, l: (layer, 0, 0, 0)),
            pl.BlockSpec((None, 1, GROUP_W), lambda b, l: (layer, 0, 0)),
        ],
        out_specs=pl.BlockSpec((tl, GROUP_W), lambda b, l: (b * nl + l, 0)),
        out_shape=jax.ShapeDtypeStruct((bsz * seq, GROUP_W), F32),
        scratch_shapes=[pltpu.VMEM((POOL_HALO + tl, GROUP_W), F32)],
        compiler_params=_params("parallel", "arbitrary"),
        name="pool_mixer",
    )(proj, pool_w, pool_scale)


CONV_HALO = 32
CONV_ROWS = 32


def _conv_kernel(val_ref, gate_ref, wdw_ref, bdw_ref, lng_ref, lnb_ref, wpw_ref, o_ref,
                 sh_ref, cv_ref, *, tl):
    l = pl.program_id(1)
    n = tl + CONV_HALO

    @pl.when(l == 0)
    def _():
        sh_ref[0, 0:CONV_HALO, :] = jnp.zeros((CONV_HALO, GROUP_W), F32)

    @pl.when(l > 0)
    def _():
        sh_ref[0, 0:CONV_HALO, :] = sh_ref[0, tl:n, :]

    sh_ref[0, CONV_HALO:n, :] = val_ref[...] * jax.nn.sigmoid(gate_ref[...])
    sh_ref[0, n:, :] = jnp.zeros((SUBLANES, GROUP_W), F32)
    for s in range(1, SUBLANES):
        sh_ref[s, 0:n, :] = sh_ref[0, s:s + n, :]

    off = CONV_HALO - (CONV_WIDTH - 1)

    def rows(c, carry):
        r0 = pl.multiple_of(c * CONV_ROWS, CONV_ROWS)
        acc = jnp.broadcast_to(bdw_ref[...], (CONV_ROWS, GROUP_W))
        for k in range(CONV_WIDTH):
            q, s = divmod(off + k, SUBLANES)
            acc = acc + wdw_ref[k:k + 1, :] * sh_ref[s, pl.ds(r0 + q * SUBLANES, CONV_ROWS), :]
        cv_ref[pl.ds(r0, CONV_ROWS), :] = acc
        return carry

    lax.fori_loop(0, tl // CONV_ROWS, rows, 0)

    h = cv_ref[...]
    hc = h - jnp.mean(h, axis=-1, keepdims=True)
    y = hc * lax.rsqrt(jnp.mean(hc * hc, axis=-1, keepdims=True) + NORM_EPS)
    y = y * lng_ref[...] + lnb_ref[...]
    y = jax.nn.silu(y)
    o_ref[...] = jnp.dot(y.astype(BF16), wpw_ref[...], preferred_element_type=F32)


def conv_mixer(proj, w_dw, b_dw, ln_g, ln_b, w_pw, layer, bsz, seq, tl=512):
    nl = seq // tl
    vec = pl.BlockSpec((None, 1, GROUP_W), lambda b, l: (layer, 0, 0))
    return pl.pallas_call(
        functools.partial(_conv_kernel, tl=tl),
        grid=(bsz, nl),
        in_specs=[
            pl.BlockSpec((tl, GROUP_W), lambda b, l: (b * nl + l, 2)),
            pl.BlockSpec((tl, GROUP_W), lambda b, l: (b * nl + l, 3)),
            pl.BlockSpec((None, CONV_WIDTH, GROUP_W), lambda b, l: (layer, 0, 0)),
            vec, vec, vec,
            pl.BlockSpec((None, GROUP_W, GROUP_W), lambda b, l: (layer, 0, 0)),
        ],
        out_specs=pl.BlockSpec((tl, GROUP_W), lambda b, l: (b * nl + l, 0)),
        out_shape=jax.ShapeDtypeStruct((bsz * seq, GROUP_W), F32),
        scratch_shapes=[pltpu.VMEM((SUBLANES, CONV_HALO + tl + SUBLANES, GROUP_W), F32),
                        pltpu.VMEM((tl, GROUP_W), F32)],
        compiler_params=_params("parallel", "arbitrary"),
        name="conv_mixer",
    )(proj, proj, w_dw, b_dw, ln_g, ln_b, w_pw)


def _t5_bucket(dist):
    n = np.maximum(dist, 0)
    max_exact = REL_BUCKETS // 2
    large = max_exact + (np.log(np.maximum(n, 1) / max_exact) / np.log(REL_MAX_DIST / max_exact)
                         * (REL_BUCKETS - max_exact)).astype(np.int64)
    large = np.minimum(large, REL_BUCKETS - 1)
    return np.where(n < max_exact, n, large).astype(np.int32)


ATT_NEAR_BLOCKS = 5
ATT_NEAR = ATT_NEAR_BLOCKS * ATT_BLOCK
ATT_FAR_DIL = 16


def _attention_bias_layout(sub):
    a = np.arange(ATT_BLOCK)[:, None]
    d_near = (ATT_NEAR - ATT_BLOCK) + a - np.arange(ATT_NEAR)[None, :]
    mult = np.zeros(d_near.shape, np.int64)
    for window, dil in DILATED_PATTERNS:
        if dil != ATT_FAR_DIL:
            mult += (d_near >= 0) & (d_near % dil == 0) & (d_near // dil <= window // dil)
    d_far = ATT_FAR_DIL * (a - np.arange(sub)[None, :])
    mult_far = ((d_far >= 0) & (d_far // ATT_FAR_DIL <= ATT_BLOCK)).astype(np.int64)
    d = np.concatenate([d_near, d_far], axis=1)
    mult = np.concatenate([mult, mult_far], axis=1)
    addend = np.where(mult > 0, np.log(np.maximum(mult, 1)), NEG_INF).astype(np.float32)
    return _t5_bucket(d), addend


def _attn_bias_kernel(rb_ref, bucket_ref, add_ref, o_ref):
    h = pl.program_id(0)
    bucket = bucket_ref[...]
    t = jnp.zeros(bucket.shape, F32)
    for b in range(REL_BUCKETS):
        t = jnp.where(bucket == b, rb_ref[b, h], t)
    o_ref[...] = t + add_ref[...]


def attention_bias_table(rel_bias, seq):
    sub = seq // ATT_FAR_DIL
    bucket, addend = _attention_bias_layout(sub)
    width = ATT_NEAR + sub
    full = pl.BlockSpec((ATT_BLOCK, width), lambda h: (0, 0))
    return pl.pallas_call(
        _attn_bias_kernel,
        grid=(ATT_HEADS,),
        in_specs=[pl.BlockSpec(memory_space=pltpu.SMEM), full, full],
        out_specs=pl.BlockSpec((None, ATT_BLOCK, width), lambda h: (h, 0, 0)),
        out_shape=jax.ShapeDtypeStruct((ATT_HEADS, ATT_BLOCK, width), F32),
        compiler_params=_params("arbitrary"),
        name="attn_bias",
    )(rel_bias, jnp.asarray(bucket), jnp.asarray(addend))


ATT_HEADS_PER_BLOCK = LANES // ATT_HEAD_DIM
ATT_GROUP = 4


def _nt_dot(a, b):
    return lax.dot_general(a, b, (((1,), (1,)), ((), ())), preferred_element_type=F32)


def _attn_kernel(q_ref, k_ref, v_ref, bt_ref, o_ref, kb_ref, vb_ref, m3_ref, l3_ref, a3_ref, *, seq):
    sub = seq // ATT_FAR_DIL
    pad = ATT_NEAR - ATT_BLOCK
    scale = ATT_HEAD_DIM ** -0.5
    lane = lax.broadcasted_iota(jnp.int32, (1, LANES), 1)
    in_head = [(lane >= hh * ATT_HEAD_DIM) & (lane < (hh + 1) * ATT_HEAD_DIM)
               for hh in range(ATT_HEADS_PER_BLOCK)]

    def by_head(vals):
        out = vals[-1]
        for hh in range(ATT_HEADS_PER_BLOCK - 2, -1, -1):
            out = jnp.where(in_head[hh], vals[hh], out)
        return out

    def softmax_pv(tiles):
        heads = range(ATT_HEADS_PER_BLOCK)
        s = [[_nt_dot(jnp.where(in_head[hh], qs, 0.0).astype(BF16), kb) + bias_of(hh) for hh in heads]
             for qs, kb, _, bias_of in tiles]
        m = [[jnp.max(s_h, axis=-1, keepdims=True) for s_h in s_t] for s_t in s]
        p = [[jnp.exp(s_h - m_h) for s_h, m_h in zip(s_t, m_t)] for s_t, m_t in zip(s, m)]
        l = [[jnp.sum(p_h, axis=-1, keepdims=True) for p_h in p_t] for p_t in p]
        acc = [[jnp.dot(p_h.astype(BF16), tile[2], preferred_element_type=F32) for p_h in p_t]
               for p_t, tile in zip(p, tiles)]
        out = []
        for tile, m_t, l_t, acc_t in zip(tiles, m, l, acc):
            full = (tile[0].shape[0], LANES)
            out.append((by_head([jnp.broadcast_to(x, full) for x in m_t]),
                        by_head([jnp.broadcast_to(x, full) for x in l_t]), by_head(acc_t)))
        return out

    kb_ref[...] = k_ref[...].astype(BF16)
    vb_ref[...] = v_ref[...].astype(BF16)

    for g0 in range(0, ATT_FAR_DIL, ATT_GROUP):
        group = [pl.ds(r, sub, stride=ATT_FAR_DIL) for r in range(g0, g0 + ATT_GROUP)]
        far_bias = lambda hh: bt_ref[hh, 0:sub, ATT_NEAR:ATT_NEAR + sub]
        tiles = [(q_ref[rows, :] * scale, k_ref[rows, :].astype(BF16), v_ref[rows, :].astype(BF16), far_bias)
                 for rows in group]
        for rows, (m, l, acc) in zip(group, softmax_pv(tiles)):
            m3_ref[rows, :] = m
            l3_ref[rows, :] = l
            a3_ref[rows, :] = acc

    for g0 in range(0, seq // ATT_BLOCK, ATT_GROUP):
        group, tiles = [], []
        for i in range(g0, g0 + ATT_GROUP):
            r0 = i * ATT_BLOCK
            lo = max(0, r0 - pad)
            c0 = ATT_NEAR - (r0 + ATT_BLOCK - lo)
            group.append(slice(r0, r0 + ATT_BLOCK))
            tiles.append((q_ref[group[-1], :] * scale, kb_ref[lo:r0 + ATT_BLOCK, :], vb_ref[lo:r0 + ATT_BLOCK, :],
                          lambda hh, c0=c0: bt_ref[hh, :, c0:ATT_NEAR]))
        merged = []
        for rows, (m, l, acc) in zip(group, softmax_pv(tiles)):
            m3 = m3_ref[rows, :]
            mm = jnp.maximum(m, m3)
            wn = jnp.exp(m - mm)
            wf = jnp.exp(m3 - mm)
            num = acc * wn + a3_ref[rows, :] * wf
            den = l * wn + l3_ref[rows, :] * wf
            merged.append(num / den)
        for rows, o in zip(group, merged):
            o_ref[rows, :] = o


def dilated_attention(proj, bias_table, bsz, seq):
    assert seq % ATT_BLOCK == 0 and seq // ATT_FAR_DIL <= ATT_BLOCK
    nhp = ATT_HEADS // ATT_HEADS_PER_BLOCK
    qcol = 4 * GROUP_W // LANES
    kcol = 5 * GROUP_W // LANES
    vcol = 6 * GROUP_W // LANES
    width = bias_table.shape[-1]
    stat = pltpu.VMEM((seq, LANES), F32)
    half = pltpu.VMEM((seq, LANES), BF16)
    return pl.pallas_call(
        functools.partial(_attn_kernel, seq=seq),
        grid=(bsz, nhp),
        in_specs=[
            pl.BlockSpec((seq, LANES), lambda b, h: (b, qcol + h)),
            pl.BlockSpec((seq, LANES), lambda b, h: (b, kcol + h)),
            pl.BlockSpec((seq, LANES), lambda b, h: (b, vcol + h)),
            pl.BlockSpec((ATT_HEADS_PER_BLOCK, ATT_BLOCK, width), lambda b, h: (h, 0, 0)),
        ],
        out_specs=pl.BlockSpec((seq, LANES), lambda b, h: (b, h)),
        out_shape=jax.ShapeDtypeStruct((bsz * seq, GROUP_W), F32),
        scratch_shapes=[half, half, stat, stat, stat],
        compiler_params=_params("parallel", "parallel"),
        name="dilated_attention",
    )(proj, proj, proj, bias_table)


def _mix_out_kernel(ya_ref, yb_ref, yc_ref, yd_ref, g_ref, w_ref, h_ref, o_ref):
    yn = [_rms(y_ref[...], g_ref[:, gi * GROUP_W:(gi + 1) * GROUP_W]).astype(BF16)
          for gi, y_ref in enumerate((ya_ref, yb_ref, yc_ref, yd_ref))]
    o_ref[...] = h_ref[...] + jnp.dot(jnp.concatenate(yn, axis=-1), w_ref[...], preferred_element_type=F32)


def mix_out(ya, yb, yc, yd, g, w, h, layer, tm=512):
    m, d = h.shape
    yspec = pl.BlockSpec((tm, GROUP_W), lambda i: (i, 0))
    return pl.pallas_call(
        _mix_out_kernel,
        grid=(m // tm,),
        in_specs=[
            yspec, yspec, yspec, yspec,
            pl.BlockSpec((None, 1, d), lambda i: (layer, 0, 0)),
            pl.BlockSpec((None, d, d), lambda i: (layer, 0, 0)),
            pl.BlockSpec((tm, d), lambda i: (i, 0)),
        ],
        out_specs=pl.BlockSpec((tm, d), lambda i: (i, 0)),
        out_shape=jax.ShapeDtypeStruct((m, d), F32),
        compiler_params=_params("parallel"),
        name="mix_out",
    )(ya, yb, yc, yd, g, w, h)


def _mem_kv_kernel(mem_ref, g_ref, wk_ref, wv_ref, k_ref, v_ref):
    mn = _rms(mem_ref[...], g_ref[...]).astype(BF16)
    k_ref[...] = jnp.dot(mn, wk_ref[...], preferred_element_type=F32).astype(BF16)
    v_ref[...] = jnp.dot(mn, wv_ref[...], preferred_element_type=F32).astype(BF16)


def mem_kv(mem, g, wk, wv):
    m, d = mem.shape
    depth = wk.shape[0]
    wspec = pl.BlockSpec((None, d, X_WIDTH), lambda l: (l, 0, 0))
    ospec = pl.BlockSpec((None, m, X_WIDTH), lambda l: (l, 0, 0))
    return pl.pallas_call(
        _mem_kv_kernel,
        grid=(depth,),
        in_specs=[pl.BlockSpec((m, d), lambda l: (0, 0)), pl.BlockSpec((1, d), lambda l: (0, 0)),
                  wspec, wspec],
        out_specs=[ospec, ospec],
        out_shape=[jax.ShapeDtypeStruct((depth, m, X_WIDTH), BF16)] * 2,
        compiler_params=_params("arbitrary"),
        name="mem_kv",
    )(mem, g, wk, wv)


def _xattn_kernel(h_ref, g_ref, wq_ref, k_ref, v_ref, wo_ref, o_ref):
    h = h_ref[...]
    hn = _rms(h, g_ref[...]).astype(BF16)
    q = jnp.dot(hn, wq_ref[...], preferred_element_type=F32) * (X_HEAD_DIM ** -0.5)
    outs = []
    for hd in range(X_HEADS):
        cols = slice(hd * X_HEAD_DIM, (hd + 1) * X_HEAD_DIM)
        s = lax.dot_general(q[:, cols].astype(BF16), k_ref[:, cols], (((1,), (1,)), ((), ())),
                            preferred_element_type=F32)
        p = jnp.exp(s - jnp.max(s, axis=-1, keepdims=True))
        p = p / jnp.sum(p, axis=-1, keepdims=True)
        outs.append(jnp.dot(p.astype(BF16), v_ref[:, cols], preferred_element_type=F32))
    o = jnp.concatenate(outs, axis=-1).astype(BF16)
    o_ref[...] = h + jnp.dot(o, wo_ref[...], preferred_element_type=F32)


def cross_attention(h, g, wq, k, v, wo, layer, bsz, seq, tm=512):
    d = h.shape[-1]
    nl = seq // tm
    mlen = k.shape[1] // bsz
    return pl.pallas_call(
        _xattn_kernel,
        grid=(bsz, nl),
        in_specs=[
            pl.BlockSpec((tm, d), lambda b, l: (b * nl + l, 0)),
            pl.BlockSpec((None, 1, d), lambda b, l: (layer, 0, 0)),
            pl.BlockSpec((None, d, X_WIDTH), lambda b, l: (layer, 0, 0)),
            pl.BlockSpec((None, mlen, X_WIDTH), lambda b, l: (layer, b, 0)),
            pl.BlockSpec((None, mlen, X_WIDTH), lambda b, l: (layer, b, 0)),
            pl.BlockSpec((None, X_WIDTH, d), lambda b, l: (layer, 0, 0)),
        ],
        out_specs=pl.BlockSpec((tm, d), lambda b, l: (b * nl + l, 0)),
        out_shape=jax.ShapeDtypeStruct(h.shape, F32),
        compiler_params=_params("parallel", "arbitrary"),
        name="cross_attention",
    )(h, g, wq, k, v, wo)


def _mlp_kernel(h_ref, g_ref, wu_ref, wd_ref, *rest, out_norm):
    o_ref, hn_ref = rest[-2:]
    j = pl.program_id(1)

    @pl.when(j == 0)
    def _():
        h = h_ref[...]
        hn_ref[...] = _rms(h, g_ref[...]).astype(BF16)
        o_ref[...] = h

    a = jnp.dot(hn_ref[...], wu_ref[...], preferred_element_type=F32)
    a = jnp.square(jnp.maximum(a, 0.0)).astype(BF16)
    o_ref[...] += jnp.dot(a, wd_ref[...], preferred_element_type=F32)

    if out_norm:
        @pl.when(j == pl.num_programs(1) - 1)
        def _():
            o_ref[...] = _rms(o_ref[...], rest[0][...])


def mlp(h, g, wu, wd, layer, out_norm_g=None, tm=1024, tf=512):
    m, d = h.shape
    f = wu.shape[-1]
    out_norm = out_norm_g is not None
    extra_specs = [pl.BlockSpec((1, d), lambda i, j: (0, 0))] if out_norm else []
    extra_args = (out_norm_g,) if out_norm else ()
    return pl.pallas_call(
        functools.partial(_mlp_kernel, out_norm=out_norm),
        grid=(m // tm, f // tf),
        in_specs=[
            pl.BlockSpec((tm, d), lambda i, j: (i, 0)),
            pl.BlockSpec((None, 1, d), lambda i, j: (layer, 0, 0)),
            pl.BlockSpec((None, d, tf), lambda i, j: (layer, 0, j)),
            pl.BlockSpec((None, tf, d), lambda i, j: (layer, j, 0)),
        ] + extra_specs,
        out_specs=pl.BlockSpec((tm, d), lambda i, j: (i, 0)),
        out_shape=jax.ShapeDtypeStruct((m, d), F32),
        scratch_shapes=[pltpu.VMEM((tm, d), BF16)],
        compiler_params=_params("parallel", "arbitrary"),
        name="mlp",
    )(h, g, wu, wd, *extra_args)


def kernel(x, mem, rel_bias, mem_norm_g, norm_mix_g, w_in, s5_lam_re, s5_lam_im, s5_log_dt, s5_b_re, s5_b_im, s5_c_re, s5_c_im, s5_d, s5_w_glu, pool_w, pool_scale, conv_w_dw, conv_b_dw, conv_ln_g, conv_ln_b, conv_w_pw, grp_norm_g, w_out, norm_x_g, w_xq, w_xk, w_xv, w_xo, norm_mlp_g, w_up, w_down, norm_final_g):
    bsz, seq, d = x.shape
    depth = w_in.shape[0]
    bf = lambda a: a.astype(BF16)
    vec = lambda a: a[:, None, :]

    k_mem, v_mem = mem_kv(mem.reshape(bsz * mem.shape[1], d), mem_norm_g[None, :], bf(w_xk), bf(w_xv))
    pow_r, pow_i, bmat, cmat = s5_params(s5_lam_re, s5_lam_im, s5_log_dt, s5_b_re, s5_b_im, s5_c_re, s5_c_im)
    coef_r, coef_i = s5_scan_coefficients(pow_r, pow_i)
    bias_table = attention_bias_table(rel_bias, seq)

    w_in_b, w_out_b, w_up_b, w_down_b = bf(w_in), bf(w_out), bf(w_up), bf(w_down)
    w_xq_b, w_xo_b = bf(w_xq), bf(w_xo)
    w_glu_b, pool_w_b, w_pw_b = bf(s5_w_glu), bf(pool_w), bf(conv_w_pw)
    norm_mix, s5_dv, pool_sc = vec(norm_mix_g), vec(s5_d), vec(pool_scale)
    b_dw, ln_g, ln_b = vec(conv_b_dw), vec(conv_ln_g), vec(conv_ln_b)
    grp_g, norm_x, norm_mlp = vec(grp_norm_g), vec(norm_x_g), vec(norm_mlp_g)

    h = x.reshape(bsz * seq, d)
    for l in range(depth):
        proj = norm_matmul(h, norm_mix, w_in_b, l)
        y_a = s5_mixer(proj, bmat, cmat, coef_r, coef_i, s5_dv, w_glu_b, l, bsz, seq)
        y_b = pool_mixer(proj, pool_w_b, pool_sc, l, bsz, seq)
        y_c = conv_mixer(proj, conv_w_dw, b_dw, ln_g, ln_b, w_pw_b, l, bsz, seq)
        y_d = dilated_attention(proj, bias_table, bsz, seq)
        h = mix_out(y_a, y_b, y_c, y_d, grp_g, w_out_b, h, l)
        h = cross_attention(h, norm_x, w_xq_b, k_mem, v_mem, w_xo_b, l, bsz, seq)
        h = mlp(h, norm_mlp, w_up_b, w_down_b, l,
                out_norm_g=norm_final_g[None, :] if l == depth - 1 else None)
    return h.reshape(bsz, seq, d)
```

```python
import functools
import math

import numpy as np
import jax
import jax.numpy as jnp
from jax import lax
from jax.experimental import pallas as pl
from jax.experimental.pallas import tpu as pltpu

F32 = jnp.float32
BF16 = jnp.bfloat16

D_MODEL = 2048
DEPTH = 4
GROUP_W = 512
S5_GROUPS = 32
S5_CH = 16
S5_STATE = 64
S5_NSTATE = S5_GROUPS * S5_STATE
S5_SCAN_LANE_BLOCKS = 8
S5_BLOCK = 512
POOL_WINDOWS = (2, 4, 8, 16)
POOL_CH = 128
CONV_WIDTH = 31
ATT_HEADS = 8
ATT_HEAD_DIM = 64
DILATED_PATTERNS = ((128, 1), (512, 4), (2048, 16))
ATT_BLOCK = 128
REL_BUCKETS = 32
REL_MAX_DIST = 2048
MEM_LEN = 256
X_HEADS = 4
X_HEAD_DIM = 128
X_WIDTH = 512
D_FF = 4 * D_MODEL
NORM_EPS = 1e-6
NEG_INF = -1e30
IN_WIDTH = 7 * GROUP_W

SUBLANES = 8
LANES = 128
VMEM_LIMIT = 52 * 1024 * 1024


def _params(*sem):
    return pltpu.CompilerParams(dimension_semantics=sem, vmem_limit_bytes=VMEM_LIMIT)


def _rms(x, g):
    return x * lax.rsqrt(jnp.mean(x * x, axis=-1, keepdims=True) + NORM_EPS) * g


def _norm_matmul_kernel(x_ref, g_ref, w_ref, o_ref, xn_ref):
    @pl.when(pl.program_id(1) == 0)
    def _():
        xn_ref[...] = _rms(x_ref[...], g_ref[...]).astype(BF16)

    o_ref[...] = jnp.dot(xn_ref[...], w_ref[...], preferred_element_type=F32)


def norm_matmul(x, g, w, layer, tm=1024, tn=1792):
    m, d = x.shape
    n = w.shape[-1]
    return pl.pallas_call(
        _norm_matmul_kernel,
        grid=(m // tm, n // tn),
        in_specs=[
            pl.BlockSpec((tm, d), lambda i, j: (i, 0)),
            pl.BlockSpec((None, 1, d), lambda i, j: (layer, 0, 0)),
            pl.BlockSpec((None, d, tn), lambda i, j: (layer, 0, j)),
        ],
        out_specs=pl.BlockSpec((tm, tn), lambda i, j: (i, j)),
        out_shape=jax.ShapeDtypeStruct((m, n), F32),
        scratch_shapes=[pltpu.VMEM((tm, d), BF16)],
        compiler_params=_params("parallel", "arbitrary"),
        name="norm_matmul",
    )(x, g, w)


def _s5_param_kernel(lr_ref, li_ref, ldt_ref, lrx_ref, lix_ref, ldtx_ref, br_ref, bi_ref, cr_ref, ci_ref,
                     pr_ref, pi_ref, bmat_ref, cmat_ref, *, chunk):
    def discretise(lr, li, ldt):
        dt = jnp.exp(ldt)
        mag = jnp.exp(lr * dt)
        return mag * jnp.cos(li * dt), mag * jnp.sin(li * dt)

    cr, ci = discretise(lr_ref[...], li_ref[...], ldt_ref[...])
    slot = {1: 0, chunk: 1, 2 * chunk: 2, 4 * chunk: 3}
    e = 1
    while True:
        if e in slot:
            pr_ref[slot[e]] = cr
            pi_ref[slot[e]] = ci
        if e == 4 * chunk:
            break
        cr, ci = cr * cr - ci * ci, 2.0 * (cr * ci)
        e *= 2

    lr, li = lrx_ref[...], lix_ref[...]
    ar, ai = discretise(lr, li, ldtx_ref[...])
    den = lr * lr + li * li
    nr, ni = ar - 1.0, ai
    f_r = (nr * lr + ni * li) / den
    f_i = (ni * lr - nr * li) / den
    b_r, b_i = br_ref[...], bi_ref[...]
    bb_r = f_r * b_r - f_i * b_i
    bb_i = f_r * b_i + f_i * b_r

    def block_diagonal(x, row_shift, col_shift):
        rows, w = x.shape
        cols = (rows >> row_shift) << col_shift
        sel = ((lax.broadcasted_iota(jnp.int32, (w, cols), 1) & (w - 1))
               == lax.broadcasted_iota(jnp.int32, (w, cols), 0))
        tiled = jnp.dot(x.astype(BF16), jnp.where(sel, 1.0, 0.0).astype(BF16), preferred_element_type=F32)
        same = ((lax.broadcasted_iota(jnp.int32, (rows, cols), 0) >> row_shift)
                == (lax.broadcasted_iota(jnp.int32, (rows, cols), 1) >> col_shift))
        return jnp.where(same, tiled, 0.0).astype(BF16)

    ch_shift, st_shift = S5_CH.bit_length() - 1, S5_STATE.bit_length() - 1
    bmat_ref[:, 0:S5_NSTATE] = block_diagonal(bb_r, ch_shift, st_shift)
    bmat_ref[:, S5_NSTATE:] = block_diagonal(bb_i, ch_shift, st_shift)
    cmat_ref[0:S5_NSTATE, :] = block_diagonal(cr_ref[...], st_shift, ch_shift)
    cmat_ref[S5_NSTATE:, :] = block_diagonal(-ci_ref[...], st_shift, ch_shift)


S5_POWERS = 4


def s5_params(lam_re, lam_im, log_dt, b_re, b_im, c_re, c_im, chunk):
    depth = lam_re.shape[0]
    gc = S5_GROUPS * S5_CH
    rep = lambda a: jnp.repeat(a, S5_CH, axis=1)
    ldt = log_dt[..., None]
    bt = lambda b: jnp.transpose(b, (0, 1, 3, 2)).reshape(depth, gc, S5_STATE)
    ct = lambda c: jnp.transpose(c, (0, 1, 3, 2)).reshape(depth, S5_NSTATE, S5_CH)
    small = pl.BlockSpec((None, S5_GROUPS, S5_STATE), lambda l: (l, 0, 0))
    small1 = pl.BlockSpec((None, S5_GROUPS, 1), lambda l: (l, 0, 0))
    big = pl.BlockSpec((None, gc, S5_STATE), lambda l: (l, 0, 0))
    big1 = pl.BlockSpec((None, gc, 1), lambda l: (l, 0, 0))
    cspec = pl.BlockSpec((None, S5_NSTATE, S5_CH), lambda l: (l, 0, 0))
    powspec = pl.BlockSpec((None, S5_POWERS, S5_GROUPS, S5_STATE), lambda l: (l, 0, 0, 0))
    return pl.pallas_call(
        functools.partial(_s5_param_kernel, chunk=chunk),
        grid=(depth,),
        in_specs=[small, small, small1, big, big, big1, big, big, cspec, cspec],
        out_specs=[powspec, powspec,
                   pl.BlockSpec((None, gc, 2 * S5_NSTATE), lambda l: (l, 0, 0)),
                   pl.BlockSpec((None, 2 * S5_NSTATE, gc), lambda l: (l, 0, 0))],
        out_shape=[
            jax.ShapeDtypeStruct((depth, S5_POWERS, S5_GROUPS, S5_STATE), F32),
            jax.ShapeDtypeStruct((depth, S5_POWERS, S5_GROUPS, S5_STATE), F32),
            jax.ShapeDtypeStruct((depth, gc, 2 * S5_NSTATE), BF16),
            jax.ShapeDtypeStruct((depth, 2 * S5_NSTATE, gc), BF16),
        ],
        compiler_params=_params("arbitrary"),
        name="s5_params",
    )(lam_re, lam_im, ldt, rep(lam_re), rep(lam_im), rep(ldt), bt(b_re), bt(b_im), ct(c_re), ct(c_im))


def s5_scan_coefficients(pow_r, pow_i):
    depth = pow_r.shape[0]
    row = jnp.arange(SUBLANES)[None, :, None]

    def coef(p):
        p = p.reshape(depth, S5_POWERS, 1, S5_NSTATE)
        full = [jnp.broadcast_to(p[:, e], (depth, SUBLANES, S5_NSTATE)) for e in (0, 1)]
        masked = [jnp.where(row >= s, p[:, e], 0.0) for e, s in ((1, 1), (2, 2), (3, 4))]
        return jnp.stack(full + masked, axis=1)

    return coef(pow_r), coef(pow_i)


MXU_COLS = 256
S5_COEF_TILES = 5


def _s5_kernel(u0_ref, u1_ref, u2_ref, u3_ref, bmat_ref, cmat_ref, cfr_ref, cfi_ref, d_ref, wglu_ref, o_ref,
               up_ref, uperm_ref, xs_ref, cin_ref, yp_ref, *, tl):
    n = S5_NSTATE
    tc = tl // SUBLANES
    pitch = tc + SUBLANES
    u_refs = (u0_ref, u1_ref, u2_ref, u3_ref)
    l = pl.program_id(1)

    @pl.when(l == 0)
    def _():
        cin_ref[...] = jnp.zeros(cin_ref.shape, F32)

    for j, u_ref in enumerate(u_refs):
        for c in range(SUBLANES):
            up_ref[j, c * pitch:c * pitch + tc, :] = u_ref[c * tc:(c + 1) * tc, :]

    def gather(t, carry):
        r = pl.multiple_of(t * SUBLANES, SUBLANES)
        for j in range(len(u_refs)):
            uperm_ref[pl.ds(r, SUBLANES), j * LANES:(j + 1) * LANES] = up_ref[j, pl.ds(t, SUBLANES, stride=pitch), :]
        return carry

    lax.fori_loop(0, tc, gather, 0)

    ub = uperm_ref[...].astype(BF16)
    blocks_per_part = n // MXU_COLS
    for jb in range(2 * blocks_per_part):
        k = (jb % blocks_per_part) * MXU_COLS // S5_STATE * S5_CH // LANES
        xs_ref[:, jb * MXU_COLS:(jb + 1) * MXU_COLS] = jnp.dot(
            ub[:, k * LANES:(k + 1) * LANES], bmat_ref[k * LANES:(k + 1) * LANES, jb * MXU_COLS:(jb + 1) * MXU_COLS],
            preferred_element_type=F32)

    row = lax.broadcasted_iota(jnp.int32, (SUBLANES, LANES), 0)

    def cmul_add(ar, ai, xr, xi, br, bi):
        return (ar * xr - ai * xi) + br, (ar * xi + ai * xr) + bi

    def scan_lanes(lb0):
        cols = [(slice(lb * LANES, (lb + 1) * LANES), slice(n + lb * LANES, n + (lb + 1) * LANES))
                for lb in range(lb0, lb0 + S5_SCAN_LANE_BLOCKS)]
        a = [(cfr_ref[0, :, cr_], cfi_ref[0, :, cr_]) for cr_, _ in cols]

        def recurrence(init, store):
            def step(t, x):
                base = pl.multiple_of(t * SUBLANES, SUBLANES)
                new = []
                for q, (cr_, ci_) in enumerate(cols):
                    nr, ni = cmul_add(a[q][0], a[q][1], x[2 * q], x[2 * q + 1],
                                      xs_ref[pl.ds(base, SUBLANES), cr_], xs_ref[pl.ds(base, SUBLANES), ci_])
                    if store:
                        xs_ref[pl.ds(base, SUBLANES), cr_] = nr
                        xs_ref[pl.ds(base, SUBLANES), ci_] = ni
                    new += [nr, ni]
                return tuple(new)
            return lax.fori_loop(0, tc, step, tuple(init))

        zero = jnp.zeros((SUBLANES, LANES), F32)
        ends = recurrence([zero] * (2 * len(cols)), store=False)

        starts = []
        for q, (cr_, ci_) in enumerate(cols):
            er, ei = ends[2 * q], ends[2 * q + 1]
            wr = jnp.where(row == 0, cin_ref[:, cr_], pltpu.roll(er, 1, 0))
            wi = jnp.where(row == 0, cin_ref[:, ci_], pltpu.roll(ei, 1, 0))
            for k, s in ((2, 1), (3, 2), (4, 4)):
                wr, wi = cmul_add(cfr_ref[k, :, cr_], cfi_ref[k, :, cr_],
                                  pltpu.roll(wr, s, 0), pltpu.roll(wi, s, 0), wr, wi)
            tr, ti = cmul_add(cfr_ref[1, :, cr_], cfi_ref[1, :, cr_], wr, wi, er, ei)
            cin_ref[:, cr_] = pltpu.roll(tr, 1, 0)
            cin_ref[:, ci_] = pltpu.roll(ti, 1, 0)
            starts += [wr, wi]
        recurrence(starts, store=True)

    for lb0 in range(0, n // LANES, S5_SCAN_LANE_BLOCKS):
        scan_lanes(lb0)

    ys = []
    for m in range(GROUP_W // MXU_COLS):
        k0 = m * MXU_COLS // S5_CH * S5_STATE
        k1 = (m + 1) * MXU_COLS // S5_CH * S5_STATE
        cols = slice(m * MXU_COLS, (m + 1) * MXU_COLS)
        ys.append(jnp.dot(xs_ref[:, k0:k1].astype(BF16), cmat_ref[k0:k1, cols], preferred_element_type=F32)
                  + jnp.dot(xs_ref[:, n + k0:n + k1].astype(BF16), cmat_ref[n + k0:n + k1, cols],
                            preferred_element_type=F32))
    y = jnp.concatenate(ys, axis=-1) + d_ref[...] * uperm_ref[...]
    g = jax.nn.gelu(y)
    gate = jnp.dot(g.astype(BF16), wglu_ref[...], preferred_element_type=F32)
    out = g * jax.nn.sigmoid(gate)
    for j in range(len(u_refs)):
        yp_ref[j] = out[:, j * LANES:(j + 1) * LANES]

    tiles_per_chunk = tc // SUBLANES
    shift = tiles_per_chunk.bit_length() - 1

    def scatter(idx, carry):
        c = idx >> shift
        m = idx & (tiles_per_chunk - 1)
        r = pl.multiple_of(idx * SUBLANES, SUBLANES)
        src = m * (SUBLANES * SUBLANES) + c
        for j in range(len(u_refs)):
            o_ref[pl.ds(r, SUBLANES), j * LANES:(j + 1) * LANES] = yp_ref[j, pl.ds(src, SUBLANES, stride=SUBLANES), :]
        return carry

    lax.fori_loop(0, SUBLANES * tiles_per_chunk, scatter, 0)


def s5_mixer(proj, bmat, cmat, coef_r, coef_i, d_skip, w_glu, layer, bsz, seq, tl=S5_BLOCK):
    nl = seq // tl
    n2 = 2 * S5_NSTATE
    tc = tl // SUBLANES
    assert tc % SUBLANES == 0 and tc & (tc - 1) == 0
    nu = GROUP_W // LANES
    uspecs = [pl.BlockSpec((tl, LANES), lambda b, l, j=j: (b * nl + l, j)) for j in range(nu)]
    coef = pl.BlockSpec((None, S5_COEF_TILES, SUBLANES, S5_NSTATE), lambda b, l: (layer, 0, 0, 0))
    return pl.pallas_call(
        functools.partial(_s5_kernel, tl=tl),
        grid=(bsz, nl),
        in_specs=uspecs + [
            pl.BlockSpec((None, GROUP_W, n2), lambda b, l: (layer, 0, 0)),
            pl.BlockSpec((None, n2, GROUP_W), lambda b, l: (layer, 0, 0)),
            coef, coef,
            pl.BlockSpec((None, 1, GROUP_W), lambda b, l: (layer, 0, 0)),
            pl.BlockSpec((None, GROUP_W, GROUP_W), lambda b, l: (layer, 0, 0)),
        ],
        out_specs=pl.BlockSpec((tl, GROUP_W), lambda b, l: (b * nl + l, 0)),
        out_shape=jax.ShapeDtypeStruct((bsz * seq, GROUP_W), F32),
        scratch_shapes=[
            pltpu.VMEM((nu, SUBLANES * (tc + SUBLANES), LANES), F32),
            pltpu.VMEM((tl, GROUP_W), F32),
            pltpu.VMEM((tl, n2), F32),
            pltpu.VMEM((SUBLANES, n2), F32),
            pltpu.VMEM((nu, tl, LANES), F32),
        ],
        compiler_params=_params("parallel", "arbitrary"),
        name="s5_mixer",
    )(*([proj] * nu), bmat, cmat, coef_r, coef_i, d_skip, w_glu)


POOL_HALO = 16


def _pool_kernel(u_ref, pw_ref, sc_ref, o_ref, ext_ref, *, tl):
    l = pl.program_id(1)

    @pl.when(l == 0)
    def _():
        ext_ref[0:POOL_HALO, :] = jnp.zeros((POOL_HALO, GROUP_W), F32)

    @pl.when(l > 0)
    def _():
        ext_ref[0:POOL_HALO, :] = ext_ref[tl:tl + POOL_HALO, :]

    ext_ref[POOL_HALO:, :] = u_ref[...]
    t = l * tl + lax.broadcasted_iota(jnp.int32, (tl, 1), 0)
    for gi, w in enumerate(POOL_WINDOWS):
        cols = slice(gi * POOL_CH, (gi + 1) * POOL_CH)
        acc = ext_ref[pl.ds(POOL_HALO, tl), cols]
        for k in range(1, w):
            acc = acc + ext_ref[pl.ds(POOL_HALO - k, tl), cols]
        cnt = jnp.minimum(t + 1, w).astype(F32)
        p = acc / cnt - u_ref[:, cols]
        y = jnp.dot(p.astype(BF16), pw_ref[gi], preferred_element_type=F32)
        o_ref[:, cols] = y * sc_ref[:, cols]


def pool_mixer(proj, pool_w, pool_scale, layer, bsz, seq, tl=512):
    nl = seq // tl
    ng = len(POOL_WINDOWS)
    return pl.pallas_call(
        functools.partial(_pool_kernel, tl=tl),
        grid=(bsz, nl),
        in_specs=[
            pl.BlockSpec((tl, GROUP_W), lambda b, l: (b * nl + l, 1)),
            pl.BlockSpec((None, ng, POOL_CH, POOL_CH), lambda b, l: (layer, 0, 0, 0)),
            pl.BlockSpec((None, 1, GROUP_W), lambda b, l: (layer, 0, 0)),
        ],
        out_specs=pl.BlockSpec((tl, GROUP_W), lambda b, l: (b * nl + l, 0)),
        out_shape=jax.ShapeDtypeStruct((bsz * seq, GROUP_W), F32),
        scratch_shapes=[pltpu.VMEM((POOL_HALO + tl, GROUP_W), F32)],
        compiler_params=_params("parallel", "arbitrary"),
        name="pool_mixer",
    )(proj, pool_w, pool_scale)


CONV_HALO = 32
CONV_ROWS = 32


def _conv_kernel(val_ref, gate_ref, wdw_ref, bdw_ref, lng_ref, lnb_ref, wpw_ref, o_ref,
                 sh_ref, cv_ref, *, tl):
    l = pl.program_id(1)
    n = tl + CONV_HALO

    @pl.when(l == 0)
    def _():
        sh_ref[0, 0:CONV_HALO, :] = jnp.zeros((CONV_HALO, GROUP_W), F32)

    @pl.when(l > 0)
    def _():
        sh_ref[0, 0:CONV_HALO, :] = sh_ref[0, tl:n, :]

    sh_ref[0, CONV_HALO:n, :] = val_ref[...] * jax.nn.sigmoid(gate_ref[...])
    sh_ref[0, n:, :] = jnp.zeros((SUBLANES, GROUP_W), F32)
    for s in range(1, SUBLANES):
        sh_ref[s, 0:n, :] = sh_ref[0, s:s + n, :]

    off = CONV_HALO - (CONV_WIDTH - 1)

    def rows(c, carry):
        r0 = pl.multiple_of(c * CONV_ROWS, CONV_ROWS)
        acc = jnp.broadcast_to(bdw_ref[...], (CONV_ROWS, GROUP_W))
        for k in range(CONV_WIDTH):
            q, s = divmod(off + k, SUBLANES)
            acc = acc + wdw_ref[k:k + 1, :] * sh_ref[s, pl.ds(r0 + q * SUBLANES, CONV_ROWS), :]
        cv_ref[pl.ds(r0, CONV_ROWS), :] = acc
        return carry

    lax.fori_loop(0, tl // CONV_ROWS, rows, 0)

    h = cv_ref[...]
    hc = h - jnp.mean(h, axis=-1, keepdims=True)
    y = hc * lax.rsqrt(jnp.mean(hc * hc, axis=-1, keepdims=True) + NORM_EPS)
    y = y * lng_ref[...] + lnb_ref[...]
    y = jax.nn.silu(y)
    o_ref[...] = jnp.dot(y.astype(BF16), wpw_ref[...], preferred_element_type=F32)


def conv_mixer(proj, w_dw, b_dw, ln_g, ln_b, w_pw, layer, bsz, seq, tl=512):
    nl = seq // tl
    vec = pl.BlockSpec((None, 1, GROUP_W), lambda b, l: (layer, 0, 0))
    return pl.pallas_call(
        functools.partial(_conv_kernel, tl=tl),
        grid=(bsz, nl),
        in_specs=[
            pl.BlockSpec((tl, GROUP_W), lambda b, l: (b * nl + l, 2)),
            pl.BlockSpec((tl, GROUP_W), lambda b, l: (b * nl + l, 3)),
            pl.BlockSpec((None, CONV_WIDTH, GROUP_W), lambda b, l: (layer, 0, 0)),
            vec, vec, vec,
            pl.BlockSpec((None, GROUP_W, GROUP_W), lambda b, l: (layer, 0, 0)),
        ],
        out_specs=pl.BlockSpec((tl, GROUP_W), lambda b, l: (b * nl + l, 0)),
        out_shape=jax.ShapeDtypeStruct((bsz * seq, GROUP_W), F32),
        scratch_shapes=[pltpu.VMEM((SUBLANES, CONV_HALO + tl + SUBLANES, GROUP_W), F32),
                        pltpu.VMEM((tl, GROUP_W), F32)],
        compiler_params=_params("parallel", "arbitrary"),
        name="conv_mixer",
    )(proj, proj, w_dw, b_dw, ln_g, ln_b, w_pw)


def _t5_bucket(dist):
    n = np.maximum(dist, 0)
    max_exact = REL_BUCKETS // 2
    large = max_exact + (np.log(np.maximum(n, 1) / max_exact) / np.log(REL_MAX_DIST / max_exact)
                         * (REL_BUCKETS - max_exact)).astype(np.int64)
    large = np.minimum(large, REL_BUCKETS - 1)
    return np.where(n < max_exact, n, large).astype(np.int32)


ATT_NEAR_BLOCKS = 5
ATT_NEAR = ATT_NEAR_BLOCKS * ATT_BLOCK
ATT_FAR_DIL = 16


def _attention_bias_layout(sub):
    a = np.arange(ATT_BLOCK)[:, None]
    d_near = (ATT_NEAR - ATT_BLOCK) + a - np.arange(ATT_NEAR)[None, :]
    mult = np.zeros(d_near.shape, np.int64)
    for window, dil in DILATED_PATTERNS:
        if dil != ATT_FAR_DIL:
            mult += (d_near >= 0) & (d_near % dil == 0) & (d_near // dil <= window // dil)
    d_far = ATT_FAR_DIL * (a - np.arange(sub)[None, :])
    mult_far = ((d_far >= 0) & (d_far // ATT_FAR_DIL <= ATT_BLOCK)).astype(np.int64)
    d = np.concatenate([d_near, d_far], axis=1)
    mult = np.concatenate([mult, mult_far], axis=1)
    addend = np.where(mult > 0, np.log(np.maximum(mult, 1)), NEG_INF).astype(np.float32)
    return _t5_bucket(d), addend


def _attn_bias_kernel(rb_ref, bucket_ref, add_ref, o_ref):
    h = pl.program_id(0)
    bucket = bucket_ref[...]
    t = jnp.zeros(bucket.shape, F32)
    for b in range(REL_BUCKETS):
        t = jnp.where(bucket == b, rb_ref[b, h], t)
    o_ref[...] = t + add_ref[...]


def attention_bias_table(rel_bias, seq):
    sub = seq // ATT_FAR_DIL
    bucket, addend = _attention_bias_layout(sub)
    width = ATT_NEAR + sub
    full = pl.BlockSpec((ATT_BLOCK, width), lambda h: (0, 0))
    return pl.pallas_call(
        _attn_bias_kernel,
        grid=(ATT_HEADS,),
        in_specs=[pl.BlockSpec(memory_space=pltpu.SMEM), full, full],
        out_specs=pl.BlockSpec((None, ATT_BLOCK, width), lambda h: (h, 0, 0)),
        out_shape=jax.ShapeDtypeStruct((ATT_HEADS, ATT_BLOCK, width), F32),
        compiler_params=_params("arbitrary"),
        name="attn_bias",
    )(rel_bias, jnp.asarray(bucket), jnp.asarray(addend))


ATT_HEADS_PER_BLOCK = LANES // ATT_HEAD_DIM
ATT_GROUP = 4


def _nt_dot(a, b):
    return lax.dot_general(a, b, (((1,), (1,)), ((), ())), preferred_element_type=F32)


def _attn_kernel(q_ref, k_ref, v_ref, bt_ref, o_ref, kb_ref, vb_ref, m3_ref, l3_ref, a3_ref, *, seq):
    sub = seq // ATT_FAR_DIL
    pad = ATT_NEAR - ATT_BLOCK
    scale = ATT_HEAD_DIM ** -0.5
    lane = lax.broadcasted_iota(jnp.int32, (1, LANES), 1)
    in_head = [(lane >= hh * ATT_HEAD_DIM) & (lane < (hh + 1) * ATT_HEAD_DIM)
               for hh in range(ATT_HEADS_PER_BLOCK)]

    def by_head(vals):
        out = vals[-1]
        for hh in range(ATT_HEADS_PER_BLOCK - 2, -1, -1):
            out = jnp.where(in_head[hh], vals[hh], out)
        return out

    def softmax_pv(tiles):
        heads = range(ATT_HEADS_PER_BLOCK)
        s = [[_nt_dot(jnp.where(in_head[hh], qs, 0.0).astype(BF16), kb) + bias_of(hh) for hh in heads]
             for qs, kb, _, bias_of in tiles]
        m = [[jnp.max(s_h, axis=-1, keepdims=True) for s_h in s_t] for s_t in s]
        p = [[jnp.exp(s_h - m_h) for s_h, m_h in zip(s_t, m_t)] for s_t, m_t in zip(s, m)]
        l = [[jnp.sum(p_h, axis=-1, keepdims=True) for p_h in p_t] for p_t in p]
        acc = [[jnp.dot(p_h.astype(BF16), tile[2], preferred_element_type=F32) for p_h in p_t]
               for p_t, tile in zip(p, tiles)]
        out = []
        for tile, m_t, l_t, acc_t in zip(tiles, m, l, acc):
            full = (tile[0].shape[0], LANES)
            out.append((by_head([jnp.broadcast_to(x, full) for x in m_t]),
                        by_head([jnp.broadcast_to(x, full) for x in l_t]), by_head(acc_t)))
        return out

    kb_ref[...] = k_ref[...].astype(BF16)
    vb_ref[...] = v_ref[...].astype(BF16)

    for g0 in range(0, ATT_FAR_DIL, ATT_GROUP):
        group = [pl.ds(r, sub, stride=ATT_FAR_DIL) for r in range(g0, g0 + ATT_GROUP)]
        far_bias = lambda hh: bt_ref[hh, 0:sub, ATT_NEAR:ATT_NEAR + sub]
        tiles = [(q_ref[rows, :] * scale, k_ref[rows, :].astype(BF16), v_ref[rows, :].astype(BF16), far_bias)
                 for rows in group]
        for rows, (m, l, acc) in zip(group, softmax_pv(tiles)):
            m3_ref[rows, :] = m
            l3_ref[rows, :] = l
            a3_ref[rows, :] = acc

    for g0 in range(0, seq // ATT_BLOCK, ATT_GROUP):
        group, tiles = [], []
        for i in range(g0, g0 + ATT_GROUP):
            r0 = i * ATT_BLOCK
            lo = max(0, r0 - pad)
            c0 = ATT_NEAR - (r0 + ATT_BLOCK - lo)
            group.append(slice(r0, r0 + ATT_BLOCK))
            tiles.append((q_ref[group[-1], :] * scale, kb_ref[lo:r0 + ATT_BLOCK, :], vb_ref[lo:r0 + ATT_BLOCK, :],
                          lambda hh, c0=c0: bt_ref[hh, :, c0:ATT_NEAR]))
        merged = []
        for rows, (m, l, acc) in zip(group, softmax_pv(tiles)):
            m3 = m3_ref[rows, :]
            mm = jnp.maximum(m, m3)
            wn = jnp.exp(m - mm)
            wf = jnp.exp(m3 - mm)
            num = acc * wn + a3_ref[rows, :] * wf
            den = l * wn + l3_ref[rows, :] * wf
            merged.append(num / den)
        for rows, o in zip(group, merged):
            o_ref[rows, :] = o


def dilated_attention(proj, bias_table, bsz, seq):
    assert seq % ATT_BLOCK == 0 and seq // ATT_FAR_DIL <= ATT_BLOCK
    nhp = ATT_HEADS // ATT_HEADS_PER_BLOCK
    qcol = 4 * GROUP_W // LANES
    kcol = 5 * GROUP_W // LANES
    vcol = 6 * GROUP_W // LANES
    width = bias_table.shape[-1]
    stat = pltpu.VMEM((seq, LANES), F32)
    half = pltpu.VMEM((seq, LANES), BF16)
    return pl.pallas_call(
        functools.partial(_attn_kernel, seq=seq),
        grid=(bsz, nhp),
        in_specs=[
            pl.BlockSpec((seq, LANES), lambda b, h: (b, qcol + h)),
            pl.BlockSpec((seq, LANES), lambda b, h: (b, kcol + h)),
            pl.BlockSpec((seq, LANES), lambda b, h: (b, vcol + h)),
            pl.BlockSpec((ATT_HEADS_PER_BLOCK, ATT_BLOCK, width), lambda b, h: (h, 0, 0)),
        ],
        out_specs=pl.BlockSpec((seq, LANES), lambda b, h: (b, h)),
        out_shape=jax.ShapeDtypeStruct((bsz * seq, GROUP_W), F32),
        scratch_shapes=[half, half, stat, stat, stat],
        compiler_params=_params("parallel", "parallel"),
        name="dilated_attention",
    )(proj, proj, proj, bias_table)


def _mix_out_kernel(ya_ref, yb_ref, yc_ref, yd_ref, g_ref, w_ref, h_ref, o_ref):
    yn = [_rms(y_ref[...], g_ref[:, gi * GROUP_W:(gi + 1) * GROUP_W]).astype(BF16)
          for gi, y_ref in enumerate((ya_ref, yb_ref, yc_ref, yd_ref))]
    o_ref[...] = h_ref[...] + jnp.dot(jnp.concatenate(yn, axis=-1), w_ref[...], preferred_element_type=F32)


def mix_out(ya, yb, yc, yd, g, w, h, layer, tm=512):
    m, d = h.shape
    yspec = pl.BlockSpec((tm, GROUP_W), lambda i: (i, 0))
    return pl.pallas_call(
        _mix_out_kernel,
        grid=(m // tm,),
        in_specs=[
            yspec, yspec, yspec, yspec,
            pl.BlockSpec((None, 1, d), lambda i: (layer, 0, 0)),
            pl.BlockSpec((None, d, d), lambda i: (layer, 0, 0)),
            pl.BlockSpec((tm, d), lambda i: (i, 0)),
        ],
        out_specs=pl.BlockSpec((tm, d), lambda i: (i, 0)),
        out_shape=jax.ShapeDtypeStruct((m, d), F32),
        compiler_params=_params("parallel"),
        name="mix_out",
    )(ya, yb, yc, yd, g, w, h)


def _mem_kv_kernel(mem_ref, g_ref, wk_ref, wv_ref, k_ref, v_ref):
    mn = _rms(mem_ref[...], g_ref[...]).astype(BF16)
    k_ref[...] = jnp.dot(mn, wk_ref[...], preferred_element_type=F32).astype(BF16)
    v_ref[...] = jnp.dot(mn, wv_ref[...], preferred_element_type=F32).astype(BF16)


def mem_kv(mem, g, wk, wv):
    m, d = mem.shape
    depth = wk.shape[0]
    wspec = pl.BlockSpec((None, d, X_WIDTH), lambda l: (l, 0, 0))
    ospec = pl.BlockSpec((None, m, X_WIDTH), lambda l: (l, 0, 0))
    return pl.pallas_call(
        _mem_kv_kernel,
        grid=(depth,),
        in_specs=[pl.BlockSpec((m, d), lambda l: (0, 0)), pl.BlockSpec((1, d), lambda l: (0, 0)),
                  wspec, wspec],
        out_specs=[ospec, ospec],
        out_shape=[jax.ShapeDtypeStruct((depth, m, X_WIDTH), BF16)] * 2,
        compiler_params=_params("arbitrary"),
        name="mem_kv",
    )(mem, g, wk, wv)


def _xattn_kernel(h_ref, g_ref, wq_ref, k_ref, v_ref, wo_ref, o_ref):
    h = h_ref[...]
    hn = _rms(h, g_ref[...]).astype(BF16)
    q = jnp.dot(hn, wq_ref[...], preferred_element_type=F32) * (X_HEAD_DIM ** -0.5)
    outs = []
    for hd in range(X_HEADS):
        cols = slice(hd * X_HEAD_DIM, (hd + 1) * X_HEAD_DIM)
        s = lax.dot_general(q[:, cols].astype(BF16), k_ref[:, cols], (((1,), (1,)), ((), ())),
                            preferred_element_type=F32)
        p = jnp.exp(s - jnp.max(s, axis=-1, keepdims=True))
        p = p / jnp.sum(p, axis=-1, keepdims=True)
        outs.append(jnp.dot(p.astype(BF16), v_ref[:, cols], preferred_element_type=F32))
    o = jnp.concatenate(outs, axis=-1).astype(BF16)
    o_ref[...] = h + jnp.dot(o, wo_ref[...], preferred_element_type=F32)


def cross_attention(h, g, wq, k, v, wo, layer, bsz, seq, tm=512):
    d = h.shape[-1]
    nl = seq // tm
    mlen = k.shape[1] // bsz
    return pl.pallas_call(
        _xattn_kernel,
        grid=(bsz, nl),
        in_specs=[
            pl.BlockSpec((tm, d), lambda b, l: (b * nl + l, 0)),
            pl.BlockSpec((None, 1, d), lambda b, l: (layer, 0, 0)),
            pl.BlockSpec((None, d, X_WIDTH), lambda b, l: (layer, 0, 0)),
            pl.BlockSpec((None, mlen, X_WIDTH), lambda b, l: (layer, b, 0)),
            pl.BlockSpec((None, mlen, X_WIDTH), lambda b, l: (layer, b, 0)),
            pl.BlockSpec((None, X_WIDTH, d), lambda b, l: (layer, 0, 0)),
        ],
        out_specs=pl.BlockSpec((tm, d), lambda b, l: (b * nl + l, 0)),
        out_shape=jax.ShapeDtypeStruct(h.shape, F32),
        compiler_params=_params("parallel", "arbitrary"),
        name="cross_attention",
    )(h, g, wq, k, v, wo)


def _mlp_kernel(h_ref, g_ref, wu_ref, wd_ref, *rest, out_norm):
    o_ref, hn_ref = rest[-2:]
    j = pl.program_id(1)

    @pl.when(j == 0)
    def _():
        h = h_ref[...]
        hn_ref[...] = _rms(h, g_ref[...]).astype(BF16)
        o_ref[...] = h

    a = jnp.dot(hn_ref[...], wu_ref[...], preferred_element_type=F32)
    a = jnp.square(jnp.maximum(a, 0.0)).astype(BF16)
    o_ref[...] += jnp.dot(a, wd_ref[...], preferred_element_type=F32)

    if out_norm:
        @pl.when(j == pl.num_programs(1) - 1)
        def _():
            o_ref[...] = _rms(o_ref[...], rest[0][...])


def mlp(h, g, wu, wd, layer, out_norm_g=None, tm=1024, tf=512):
    m, d = h.shape
    f = wu.shape[-1]
    out_norm = out_norm_g is not None
    extra_specs = [pl.BlockSpec((1, d), lambda i, j: (0, 0))] if out_norm else []
    extra_args = (out_norm_g,) if out_norm else ()
    return pl.pallas_call(
        functools.partial(_mlp_kernel, out_norm=out_norm),
        grid=(m // tm, f // tf),
        in_specs=[
            pl.BlockSpec((tm, d), lambda i, j: (i, 0)),
            pl.BlockSpec((None, 1, d), lambda i, j: (layer, 0, 0)),
            pl.BlockSpec((None, d, tf), lambda i, j: (layer, 0, j)),
            pl.BlockSpec((None, tf, d), lambda i, j: (layer, j, 0)),
        ] + extra_specs,
        out_specs=pl.BlockSpec((tm, d), lambda i, j: (i, 0)),
        out_shape=jax.ShapeDtypeStruct((m, d), F32),
        scratch_shapes=[pltpu.VMEM((tm, d), BF16)],
        compiler_params=_params("parallel", "arbitrary"),
        name="mlp",
    )(h, g, wu, wd, *extra_args)


def kernel(x, mem, rel_bias, mem_norm_g, norm_mix_g, w_in, s5_lam_re, s5_lam_im, s5_log_dt, s5_b_re, s5_b_im, s5_c_re, s5_c_im, s5_d, s5_w_glu, pool_w, pool_scale, conv_w_dw, conv_b_dw, conv_ln_g, conv_ln_b, conv_w_pw, grp_norm_g, w_out, norm_x_g, w_xq, w_xk, w_xv, w_xo, norm_mlp_g, w_up, w_down, norm_final_g):
    bsz, seq, d = x.shape
    depth = w_in.shape[0]
    bf = lambda a: a.astype(BF16)
    vec = lambda a: a[:, None, :]

    k_mem, v_mem = mem_kv(mem.reshape(bsz * mem.shape[1], d), mem_norm_g[None, :], bf(w_xk), bf(w_xv))
    pow_r, pow_i, bmat, cmat = s5_params(s5_lam_re, s5_lam_im, s5_log_dt, s5_b_re, s5_b_im, s5_c_re, s5_c_im,
                                         chunk=S5_BLOCK // SUBLANES)
    coef_r, coef_i = s5_scan_coefficients(pow_r, pow_i)
    bias_table = attention_bias_table(rel_bias, seq)

    w_in_b, w_out_b, w_up_b, w_down_b = bf(w_in), bf(w_out), bf(w_up), bf(w_down)
    w_xq_b, w_xo_b = bf(w_xq), bf(w_xo)
    w_glu_b, pool_w_b, w_pw_b = bf(s5_w_glu), bf(pool_w), bf(conv_w_pw)
    norm_mix, s5_dv, pool_sc = vec(norm_mix_g), vec(s5_d), vec(pool_scale)
    b_dw, ln_g, ln_b = vec(conv_b_dw), vec(conv_ln_g), vec(conv_ln_b)
    grp_g, norm_x, norm_mlp = vec(grp_norm_g), vec(norm_x_g), vec(norm_mlp_g)

    h = x.reshape(bsz * seq, d)
    for l in range(depth):
        proj = norm_matmul(h, norm_mix, w_in_b, l)
        y_a = s5_mixer(proj, bmat, cmat, coef_r, coef_i, s5_dv, w_glu_b, l, bsz, seq)
        y_b = pool_mixer(proj, pool_w_b, pool_sc, l, bsz, seq)
        y_c = conv_mixer(proj, conv_w_dw, b_dw, ln_g, ln_b, w_pw_b, l, bsz, seq)
        y_d = dilated_attention(proj, bias_table, bsz, seq)
        h = mix_out(y_a, y_b, y_c, y_d, grp_g, w_out_b, h, l)
        h = cross_attention(h, norm_x, w_xq_b, k_mem, v_mem, w_xo_b, l, bsz, seq)
        h = mlp(h, norm_mlp, w_up_b, w_down_b, l,
                out_norm_g=norm_final_g[None, :] if l == depth - 1 else None)
    return h.reshape(bsz, seq, d)
```

```python
import functools
import math

import numpy as np
import jax
import jax.numpy as jnp
from jax import lax
from jax.experimental import pallas as pl
from jax.experimental.pallas import tpu as pltpu

F32 = jnp.float32
BF16 = jnp.bfloat16

D_MODEL = 2048
DEPTH = 4
GROUP_W = 512
S5_GROUPS = 32
S5_CH = 16
S5_STATE = 64
S5_NSTATE = S5_GROUPS * S5_STATE
S5_SCAN_LANE_BLOCKS = 8
S5_BLOCK = 512
POOL_WINDOWS = (2, 4, 8, 16)
POOL_CH = 128
CONV_WIDTH = 31
ATT_HEADS = 8
ATT_HEAD_DIM = 64
DILATED_PATTERNS = ((128, 1), (512, 4), (2048, 16))
ATT_BLOCK = 128
REL_BUCKETS = 32
REL_MAX_DIST = 2048
MEM_LEN = 256
X_HEADS = 4
X_HEAD_DIM = 128
X_WIDTH = 512
D_FF = 4 * D_MODEL
NORM_EPS = 1e-6
NEG_INF = -1e30
IN_WIDTH = 7 * GROUP_W

SUBLANES = 8
LANES = 128
VMEM_LIMIT = 52 * 1024 * 1024


def _params(*sem):
    return pltpu.CompilerParams(dimension_semantics=sem, vmem_limit_bytes=VMEM_LIMIT)


def _rms(x, g):
    return x * lax.rsqrt(jnp.mean(x * x, axis=-1, keepdims=True) + NORM_EPS) * g


def _norm_matmul_kernel(x_ref, g_ref, w_ref, o_ref, xn_ref):
    @pl.when(pl.program_id(1) == 0)
    def _():
        xn_ref[...] = _rms(x_ref[...], g_ref[...]).astype(BF16)

    o_ref[...] = jnp.dot(xn_ref[...], w_ref[...].astype(BF16), preferred_element_type=F32)


def norm_matmul(x, g, w, layer, tm=1024, tn=896):
    m, d = x.shape
    n = w.shape[-1]
    return pl.pallas_call(
        _norm_matmul_kernel,
        grid=(m // tm, n // tn),
        in_specs=[
            pl.BlockSpec((tm, d), lambda i, j: (i, 0)),
            pl.BlockSpec((None, 1, d), lambda i, j: (layer, 0, 0)),
            pl.BlockSpec((None, d, tn), lambda i, j: (layer, 0, j)),
        ],
        out_specs=pl.BlockSpec((tm, tn), lambda i, j: (i, j)),
        out_shape=jax.ShapeDtypeStruct((m, n), F32),
        scratch_shapes=[pltpu.VMEM((tm, d), BF16)],
        compiler_params=_params("parallel", "arbitrary"),
        name="norm_matmul",
    )(x, g, w)


def _s5_param_kernel(lr_ref, li_ref, ldt_ref, lrx_ref, lix_ref, ldtx_ref, br_ref, bi_ref, cr_ref, ci_ref,
                     pr_ref, pi_ref, bmat_ref, cmat_ref, *, chunk):
    def discretise(lr, li, ldt):
        dt = jnp.exp(ldt)
        mag = jnp.exp(lr * dt)
        return mag * jnp.cos(li * dt), mag * jnp.sin(li * dt)

    cr, ci = discretise(lr_ref[...], li_ref[...], ldt_ref[...])
    slot = {1: 0, chunk: 1, 2 * chunk: 2, 4 * chunk: 3}
    e = 1
    while True:
        if e in slot:
            pr_ref[slot[e]] = cr
            pi_ref[slot[e]] = ci
        if e == 4 * chunk:
            break
        cr, ci = cr * cr - ci * ci, 2.0 * (cr * ci)
        e *= 2

    lr, li = lrx_ref[...], lix_ref[...]
    ar, ai = discretise(lr, li, ldtx_ref[...])
    den = lr * lr + li * li
    nr, ni = ar - 1.0, ai
    f_r = (nr * lr + ni * li) / den
    f_i = (ni * lr - nr * li) / den
    b_r, b_i = br_ref[...], bi_ref[...]
    bb_r = f_r * b_r - f_i * b_i
    bb_i = f_r * b_i + f_i * b_r

    def block_diagonal(x, row_shift, col_shift):
        rows, w = x.shape
        cols = (rows >> row_shift) << col_shift
        sel = ((lax.broadcasted_iota(jnp.int32, (w, cols), 1) & (w - 1))
               == lax.broadcasted_iota(jnp.int32, (w, cols), 0))
        tiled = jnp.dot(x.astype(BF16), jnp.where(sel, 1.0, 0.0).astype(BF16), preferred_element_type=F32)
        same = ((lax.broadcasted_iota(jnp.int32, (rows, cols), 0) >> row_shift)
                == (lax.broadcasted_iota(jnp.int32, (rows, cols), 1) >> col_shift))
        return jnp.where(same, tiled, 0.0).astype(BF16)

    ch_shift, st_shift = S5_CH.bit_length() - 1, S5_STATE.bit_length() - 1
    bmat_ref[:, 0:S5_NSTATE] = block_diagonal(bb_r, ch_shift, st_shift)
    bmat_ref[:, S5_NSTATE:] = block_diagonal(bb_i, ch_shift, st_shift)
    cmat_ref[0:S5_NSTATE, :] = block_diagonal(cr_ref[...], st_shift, ch_shift)
    cmat_ref[S5_NSTATE:, :] = block_diagonal(-ci_ref[...], st_shift, ch_shift)


S5_POWERS = 4


def s5_params(lam_re, lam_im, log_dt, b_re, b_im, c_re, c_im, chunk):
    depth = lam_re.shape[0]
    gc = S5_GROUPS * S5_CH
    rep = lambda a: jnp.repeat(a, S5_CH, axis=1)
    ldt = log_dt[..., None]
    bt = lambda b: jnp.transpose(b, (0, 1, 3, 2)).reshape(depth, gc, S5_STATE)
    ct = lambda c: jnp.transpose(c, (0, 1, 3, 2)).reshape(depth, S5_NSTATE, S5_CH)
    small = pl.BlockSpec((None, S5_GROUPS, S5_STATE), lambda l: (l, 0, 0))
    small1 = pl.BlockSpec((None, S5_GROUPS, 1), lambda l: (l, 0, 0))
    big = pl.BlockSpec((None, gc, S5_STATE), lambda l: (l, 0, 0))
    big1 = pl.BlockSpec((None, gc, 1), lambda l: (l, 0, 0))
    cspec = pl.BlockSpec((None, S5_NSTATE, S5_CH), lambda l: (l, 0, 0))
    powspec = pl.BlockSpec((None, S5_POWERS, S5_GROUPS, S5_STATE), lambda l: (l, 0, 0, 0))
    return pl.pallas_call(
        functools.partial(_s5_param_kernel, chunk=chunk),
        grid=(depth,),
        in_specs=[small, small, small1, big, big, big1, big, big, cspec, cspec],
        out_specs=[powspec, powspec,
                   pl.BlockSpec((None, gc, 2 * S5_NSTATE), lambda l: (l, 0, 0)),
                   pl.BlockSpec((None, 2 * S5_NSTATE, gc), lambda l: (l, 0, 0))],
        out_shape=[
            jax.ShapeDtypeStruct((depth, S5_POWERS, S5_GROUPS, S5_STATE), F32),
            jax.ShapeDtypeStruct((depth, S5_POWERS, S5_GROUPS, S5_STATE), F32),
            jax.ShapeDtypeStruct((depth, gc, 2 * S5_NSTATE), BF16),
            jax.ShapeDtypeStruct((depth, 2 * S5_NSTATE, gc), BF16),
        ],
        compiler_params=_params("arbitrary"),
        name="s5_params",
    )(lam_re, lam_im, ldt, rep(lam_re), rep(lam_im), rep(ldt), bt(b_re), bt(b_im), ct(c_re), ct(c_im))


def s5_scan_coefficients(pow_r, pow_i):
    depth = pow_r.shape[0]
    row = jnp.arange(SUBLANES)[None, :, None]

    def coef(p):
        p = p.reshape(depth, S5_POWERS, 1, S5_NSTATE)
        full = [jnp.broadcast_to(p[:, e], (depth, SUBLANES, S5_NSTATE)) for e in (0, 1)]
        masked = [jnp.where(row >= s, p[:, e], 0.0) for e, s in ((1, 1), (2, 2), (3, 4))]
        return jnp.stack(full + masked, axis=1)

    return coef(pow_r), coef(pow_i)


MXU_COLS = 256
S5_COEF_TILES = 5


def _s5_kernel(u0_ref, u1_ref, u2_ref, u3_ref, bmat_ref, cmat_ref, cfr_ref, cfi_ref, d_ref, wglu_ref, o_ref,
               up_ref, uperm_ref, xs_ref, cin_ref, yp_ref, *, tl):
    n = S5_NSTATE
    tc = tl // SUBLANES
    pitch = tc + SUBLANES
    u_refs = (u0_ref, u1_ref, u2_ref, u3_ref)
    l = pl.program_id(1)

    @pl.when(l == 0)
    def _():
        cin_ref[...] = jnp.zeros(cin_ref.shape, F32)

    for j, u_ref in enumerate(u_refs):
        for c in range(SUBLANES):
            up_ref[j, c * pitch:c * pitch + tc, :] = u_ref[c * tc:(c + 1) * tc, :]

    def gather(t, carry):
        r = pl.multiple_of(t * SUBLANES, SUBLANES)
        for j in range(len(u_refs)):
            uperm_ref[pl.ds(r, SUBLANES), j * LANES:(j + 1) * LANES] = up_ref[j, pl.ds(t, SUBLANES, stride=pitch), :]
        return carry

    lax.fori_loop(0, tc, gather, 0)

    ub = uperm_ref[...].astype(BF16)
    blocks_per_part = n // MXU_COLS
    for jb in range(2 * blocks_per_part):
        k = (jb % blocks_per_part) * MXU_COLS // S5_STATE * S5_CH // LANES
        xs_ref[:, jb * MXU_COLS:(jb + 1) * MXU_COLS] = jnp.dot(
            ub[:, k * LANES:(k + 1) * LANES], bmat_ref[k * LANES:(k + 1) * LANES, jb * MXU_COLS:(jb + 1) * MXU_COLS],
            preferred_element_type=F32)

    row = lax.broadcasted_iota(jnp.int32, (SUBLANES, LANES), 0)

    def cmul_add(ar, ai, xr, xi, br, bi):
        return (ar * xr - ai * xi) + br, (ar * xi + ai * xr) + bi

    def scan_lanes(lb0):
        cols = [(slice(lb * LANES, (lb + 1) * LANES), slice(n + lb * LANES, n + (lb + 1) * LANES))
                for lb in range(lb0, lb0 + S5_SCAN_LANE_BLOCKS)]
        a = [(cfr_ref[0, :, cr_], cfi_ref[0, :, cr_]) for cr_, _ in cols]

        def recurrence(init, store):
            def step(t, x):
                base = pl.multiple_of(t * SUBLANES, SUBLANES)
                new = []
                for q, (cr_, ci_) in enumerate(cols):
                    nr, ni = cmul_add(a[q][0], a[q][1], x[2 * q], x[2 * q + 1],
                                      xs_ref[pl.ds(base, SUBLANES), cr_], xs_ref[pl.ds(base, SUBLANES), ci_])
                    if store:
                        xs_ref[pl.ds(base, SUBLANES), cr_] = nr
                        xs_ref[pl.ds(base, SUBLANES), ci_] = ni
                    new += [nr, ni]
                return tuple(new)
            return lax.fori_loop(0, tc, step, tuple(init))

        zero = jnp.zeros((SUBLANES, LANES), F32)
        ends = recurrence([zero] * (2 * len(cols)), store=False)

        starts = []
        for q, (cr_, ci_) in enumerate(cols):
            er, ei = ends[2 * q], ends[2 * q + 1]
            wr = jnp.where(row == 0, cin_ref[:, cr_], pltpu.roll(er, 1, 0))
            wi = jnp.where(row == 0, cin_ref[:, ci_], pltpu.roll(ei, 1, 0))
            for k, s in ((2, 1), (3, 2), (4, 4)):
                wr, wi = cmul_add(cfr_ref[k, :, cr_], cfi_ref[k, :, cr_],
                                  pltpu.roll(wr, s, 0), pltpu.roll(wi, s, 0), wr, wi)
            tr, ti = cmul_add(cfr_ref[1, :, cr_], cfi_ref[1, :, cr_], wr, wi, er, ei)
            cin_ref[:, cr_] = pltpu.roll(tr, 1, 0)
            cin_ref[:, ci_] = pltpu.roll(ti, 1, 0)
            starts += [wr, wi]
        recurrence(starts, store=True)

    for lb0 in range(0, n // LANES, S5_SCAN_LANE_BLOCKS):
        scan_lanes(lb0)

    ys = []
    for m in range(GROUP_W // MXU_COLS):
        k0 = m * MXU_COLS // S5_CH * S5_STATE
        k1 = (m + 1) * MXU_COLS // S5_CH * S5_STATE
        cols = slice(m * MXU_COLS, (m + 1) * MXU_COLS)
        ys.append(jnp.dot(xs_ref[:, k0:k1].astype(BF16), cmat_ref[k0:k1, cols], preferred_element_type=F32)
                  + jnp.dot(xs_ref[:, n + k0:n + k1].astype(BF16), cmat_ref[n + k0:n + k1, cols],
                            preferred_element_type=F32))
    y = jnp.concatenate(ys, axis=-1) + d_ref[...] * uperm_ref[...]
    g = jax.nn.gelu(y)
    gate = jnp.dot(g.astype(BF16), wglu_ref[...].astype(BF16), preferred_element_type=F32)
    out = g * jax.nn.sigmoid(gate)
    for j in range(len(u_refs)):
        yp_ref[j] = out[:, j * LANES:(j + 1) * LANES]

    tiles_per_chunk = tc // SUBLANES
    shift = tiles_per_chunk.bit_length() - 1

    def scatter(idx, carry):
        c = idx >> shift
        m = idx & (tiles_per_chunk - 1)
        r = pl.multiple_of(idx * SUBLANES, SUBLANES)
        src = m * (SUBLANES * SUBLANES) + c
        for j in range(len(u_refs)):
            o_ref[pl.ds(r, SUBLANES), j * LANES:(j + 1) * LANES] = yp_ref[j, pl.ds(src, SUBLANES, stride=SUBLANES), :]
        return carry

    lax.fori_loop(0, SUBLANES * tiles_per_chunk, scatter, 0)


def s5_mixer(proj, bmat, cmat, coef_r, coef_i, d_skip, w_glu, layer, bsz, seq, tl=S5_BLOCK):
    nl = seq // tl
    n2 = 2 * S5_NSTATE
    tc = tl // SUBLANES
    assert tc % SUBLANES == 0 and tc & (tc - 1) == 0
    nu = GROUP_W // LANES
    uspecs = [pl.BlockSpec((tl, LANES), lambda b, l, j=j: (b * nl + l, j)) for j in range(nu)]
    coef = pl.BlockSpec((None, S5_COEF_TILES, SUBLANES, S5_NSTATE), lambda b, l: (layer, 0, 0, 0))
    return pl.pallas_call(
        functools.partial(_s5_kernel, tl=tl),
        grid=(bsz, nl),
        in_specs=uspecs + [
            pl.BlockSpec((None, GROUP_W, n2), lambda b, l: (layer, 0, 0)),
            pl.BlockSpec((None, n2, GROUP_W), lambda b, l: (layer, 0, 0)),
            coef, coef,
            pl.BlockSpec((None, 1, GROUP_W), lambda b, l: (layer, 0, 0)),
            pl.BlockSpec((None, GROUP_W, GROUP_W), lambda b, l: (layer, 0, 0)),
        ],
        out_specs=pl.BlockSpec((tl, GROUP_W), lambda b, l: (b * nl + l, 0)),
        out_shape=jax.ShapeDtypeStruct((bsz * seq, GROUP_W), F32),
        scratch_shapes=[
            pltpu.VMEM((nu, SUBLANES * (tc + SUBLANES), LANES), F32),
            pltpu.VMEM((tl, GROUP_W), F32),
            pltpu.VMEM((tl, n2), F32),
            pltpu.VMEM((SUBLANES, n2), F32),
            pltpu.VMEM((nu, tl, LANES), F32),
        ],
        compiler_params=_params("parallel", "arbitrary"),
        name="s5_mixer",
    )(*([proj] * nu), bmat, cmat, coef_r, coef_i, d_skip, w_glu)


POOL_HALO = 16


def _pool_kernel(u_ref, pw_ref, sc_ref, o_ref, ext_ref, *, tl):
    l = pl.program_id(1)

    @pl.when(l == 0)
    def _():
        ext_ref[0:POOL_HALO, :] = jnp.zeros((POOL_HALO, GROUP_W), F32)

    @pl.when(l > 0)
    def _():
        ext_ref[0:POOL_HALO, :] = ext_ref[tl:tl + POOL_HALO, :]

    ext_ref[POOL_HALO:, :] = u_ref[...]
    t = l * tl + lax.broadcasted_iota(jnp.int32, (tl, 1), 0)
    for gi, w in enumerate(POOL_WINDOWS):
        cols = slice(gi * POOL_CH, (gi + 1) * POOL_CH)
        acc = ext_ref[pl.ds(POOL_HALO, tl), cols]
        for k in range(1, w):
            acc = acc + ext_ref[pl.ds(POOL_HALO - k, tl), cols]
        cnt = jnp.minimum(t + 1, w).astype(F32)
        p = acc / cnt - u_ref[:, cols]
        y = jnp.dot(p.astype(BF16), pw_ref[gi].astype(BF16), preferred_element_type=F32)
        o_ref[:, cols] = y * sc_ref[:, cols]


def pool_mixer(proj, pool_w, pool_scale, layer, bsz, seq, tl=512):
    nl = seq // tl
    ng = len(POOL_WINDOWS)
    return pl.pallas_call(
        functools.partial(_pool_kernel, tl=tl),
        grid=(bsz, nl),
        in_specs=[
            pl.BlockSpec((tl, GROUP_W), lambda b, l: (b * nl + l, 1)),
            pl.BlockSpec((None, ng, POOL_CH, POOL_CH), lambda b, l: (layer, 0, 0, 0)),
            pl.BlockSpec((None, 1, GROUP_W), lambda b, l: (layer, 0, 0)),
        ],
        out_specs=pl.BlockSpec((tl, GROUP_W), lambda b, l: (b * nl + l, 0)),
        out_shape=jax.ShapeDtypeStruct((bsz * seq, GROUP_W), F32),
        scratch_shapes=[pltpu.VMEM((POOL_HALO + tl, GROUP_W), F32)],
        compiler_params=_params("parallel", "arbitrary"),
        name="pool_mixer",
    )(proj, pool_w, pool_scale)


CONV_HALO = 32
CONV_ROWS = 32


def _conv_kernel(val_ref, gate_ref, wdw_ref, bdw_ref, lng_ref, lnb_ref, wpw_ref, o_ref,
                 sh_ref, cv_ref, *, tl):
    l = pl.program_id(1)
    n = tl + CONV_HALO

    @pl.when(l == 0)
    def _():
        sh_ref[0, 0:CONV_HALO, :] = jnp.zeros((CONV_HALO, GROUP_W), F32)

    @pl.when(l > 0)
    def _():
        sh_ref[0, 0:CONV_HALO, :] = sh_ref[0, tl:n, :]

    sh_ref[0, CONV_HALO:n, :] = val_ref[...] * jax.nn.sigmoid(gate_ref[...])
    sh_ref[0, n:, :] = jnp.zeros((SUBLANES, GROUP_W), F32)
    for s in range(1, SUBLANES):
        sh_ref[s, 0:n, :] = sh_ref[0, s:s + n, :]

    off = CONV_HALO - (CONV_WIDTH - 1)

    def rows(c, carry):
        r0 = pl.multiple_of(c * CONV_ROWS, CONV_ROWS)
        acc = jnp.broadcast_to(bdw_ref[...], (CONV_ROWS, GROUP_W))
        for k in range(CONV_WIDTH):
            q, s = divmod(off + k, SUBLANES)
            acc = acc + wdw_ref[k:k + 1, :] * sh_ref[s, pl.ds(r0 + q * SUBLANES, CONV_ROWS), :]
        cv_ref[pl.ds(r0, CONV_ROWS), :] = acc
        return carry

    lax.fori_loop(0, tl // CONV_ROWS, rows, 0)

    h = cv_ref[...]
    hc = h - jnp.mean(h, axis=-1, keepdims=True)
    y = hc * lax.rsqrt(jnp.mean(hc * hc, axis=-1, keepdims=True) + NORM_EPS)
    y = y * lng_ref[...] + lnb_ref[...]
    y = jax.nn.silu(y)
    o_ref[...] = jnp.dot(y.astype(BF16), wpw_ref[...].astype(BF16), preferred_element_type=F32)


def conv_mixer(proj, w_dw, b_dw, ln_g, ln_b, w_pw, layer, bsz, seq, tl=512):
    nl = seq // tl
    vec = pl.BlockSpec((None, 1, GROUP_W), lambda b, l: (layer, 0, 0))
    return pl.pallas_call(
        functools.partial(_conv_kernel, tl=tl),
        grid=(bsz, nl),
        in_specs=[
            pl.BlockSpec((tl, GROUP_W), lambda b, l: (b * nl + l, 2)),
            pl.BlockSpec((tl, GROUP_W), lambda b, l: (b * nl + l, 3)),
            pl.BlockSpec((None, CONV_WIDTH, GROUP_W), lambda b, l: (layer, 0, 0)),
            vec, vec, vec,
            pl.BlockSpec((None, GROUP_W, GROUP_W), lambda b, l: (layer, 0, 0)),
        ],
        out_specs=pl.BlockSpec((tl, GROUP_W), lambda b, l: (b * nl + l, 0)),
        out_shape=jax.ShapeDtypeStruct((bsz * seq, GROUP_W), F32),
        scratch_shapes=[pltpu.VMEM((SUBLANES, CONV_HALO + tl + SUBLANES, GROUP_W), F32),
                        pltpu.VMEM((tl, GROUP_W), F32)],
        compiler_params=_params("parallel", "arbitrary"),
        name="conv_mixer",
    )(proj, proj, w_dw, b_dw, ln_g, ln_b, w_pw)


def _t5_bucket(dist):
    n = np.maximum(dist, 0)
    max_exact = REL_BUCKETS // 2
    large = max_exact + (np.log(np.maximum(n, 1) / max_exact) / np.log(REL_MAX_DIST / max_exact)
                         * (REL_BUCKETS - max_exact)).astype(np.int64)
    large = np.minimum(large, REL_BUCKETS - 1)
    return np.where(n < max_exact, n, large).astype(np.int32)


ATT_NEAR_BLOCKS = 5
ATT_NEAR = ATT_NEAR_BLOCKS * ATT_BLOCK
ATT_FAR_DIL = 16


def _attention_bias_layout(sub):
    a = np.arange(ATT_BLOCK)[:, None]
    d_near = (ATT_NEAR - ATT_BLOCK) + a - np.arange(ATT_NEAR)[None, :]
    mult = np.zeros(d_near.shape, np.int64)
    for window, dil in DILATED_PATTERNS:
        if dil != ATT_FAR_DIL:
            mult += (d_near >= 0) & (d_near % dil == 0) & (d_near // dil <= window // dil)
    d_far = ATT_FAR_DIL * (a - np.arange(sub)[None, :])
    mult_far = ((d_far >= 0) & (d_far // ATT_FAR_DIL <= ATT_BLOCK)).astype(np.int64)
    d = np.concatenate([d_near, d_far], axis=1)
    mult = np.concatenate([mult, mult_far], axis=1)
    addend = np.where(mult > 0, np.log(np.maximum(mult, 1)), NEG_INF).astype(np.float32)
    return _t5_bucket(d), addend


def _attn_bias_kernel(rb_ref, bucket_ref, add_ref, o_ref):
    h = pl.program_id(0)
    bucket = bucket_ref[...]
    t = jnp.zeros(bucket.shape, F32)
    for b in range(REL_BUCKETS):
        t = jnp.where(bucket == b, rb_ref[b, h], t)
    o_ref[...] = t + add_ref[...]


def attention_bias_table(rel_bias, seq):
    sub = seq // ATT_FAR_DIL
    bucket, addend = _attention_bias_layout(sub)
    width = ATT_NEAR + sub
    full = pl.BlockSpec((ATT_BLOCK, width), lambda h: (0, 0))
    return pl.pallas_call(
        _attn_bias_kernel,
        grid=(ATT_HEADS,),
        in_specs=[pl.BlockSpec(memory_space=pltpu.SMEM), full, full],
        out_specs=pl.BlockSpec((None, ATT_BLOCK, width), lambda h: (h, 0, 0)),
        out_shape=jax.ShapeDtypeStruct((ATT_HEADS, ATT_BLOCK, width), F32),
        compiler_params=_params("arbitrary"),
        name="attn_bias",
    )(rel_bias, jnp.asarray(bucket), jnp.asarray(addend))


ATT_HEADS_PER_BLOCK = LANES // ATT_HEAD_DIM
ATT_GROUP = 4


def _nt_dot(a, b):
    return lax.dot_general(a, b, (((1,), (1,)), ((), ())), preferred_element_type=F32)


def _attn_kernel(q_ref, k_ref, v_ref, bt_ref, wu_ref, wd_ref, o_ref, wub_ref, wdb_ref,
                 kb_ref, vb_ref, m3_ref, l3_ref, a3_ref, *, seq):
    sub = seq // ATT_FAR_DIL
    pad = ATT_NEAR - ATT_BLOCK
    scale = ATT_HEAD_DIM ** -0.5
    lane = lax.broadcasted_iota(jnp.int32, (1, LANES), 1)
    in_head = [(lane >= hh * ATT_HEAD_DIM) & (lane < (hh + 1) * ATT_HEAD_DIM)
               for hh in range(ATT_HEADS_PER_BLOCK)]

    def by_head(vals):
        out = vals[-1]
        for hh in range(ATT_HEADS_PER_BLOCK - 2, -1, -1):
            out = jnp.where(in_head[hh], vals[hh], out)
        return out

    def softmax_pv(tiles):
        heads = range(ATT_HEADS_PER_BLOCK)
        s = [[_nt_dot(jnp.where(in_head[hh], qs, 0.0).astype(BF16), kb) + bias_of(hh) for hh in heads]
             for qs, kb, _, bias_of in tiles]
        m = [[jnp.max(s_h, axis=-1, keepdims=True) for s_h in s_t] for s_t in s]
        p = [[jnp.exp(s_h - m_h) for s_h, m_h in zip(s_t, m_t)] for s_t, m_t in zip(s, m)]
        l = [[jnp.sum(p_h, axis=-1, keepdims=True) for p_h in p_t] for p_t in p]
        acc = [[jnp.dot(p_h.astype(BF16), tile[2], preferred_element_type=F32) for p_h in p_t]
               for p_t, tile in zip(p, tiles)]
        out = []
        for tile, m_t, l_t, acc_t in zip(tiles, m, l, acc):
            full = (tile[0].shape[0], LANES)
            out.append((by_head([jnp.broadcast_to(x, full) for x in m_t]),
                        by_head([jnp.broadcast_to(x, full) for x in l_t]), by_head(acc_t)))
        return out

    wub_ref[...] = wu_ref[...].astype(BF16)
    wdb_ref[...] = wd_ref[...].astype(BF16)

    kb_ref[...] = k_ref[...].astype(BF16)
    vb_ref[...] = v_ref[...].astype(BF16)

    for g0 in range(0, ATT_FAR_DIL, ATT_GROUP):
        group = [pl.ds(r, sub, stride=ATT_FAR_DIL) for r in range(g0, g0 + ATT_GROUP)]
        far_bias = lambda hh: bt_ref[hh, 0:sub, ATT_NEAR:ATT_NEAR + sub]
        tiles = [(q_ref[rows, :] * scale, k_ref[rows, :].astype(BF16), v_ref[rows, :].astype(BF16), far_bias)
                 for rows in group]
        for rows, (m, l, acc) in zip(group, softmax_pv(tiles)):
            m3_ref[rows, :] = m
            l3_ref[rows, :] = l
            a3_ref[rows, :] = acc

    for g0 in range(0, seq // ATT_BLOCK, ATT_GROUP):
        group, tiles = [], []
        for i in range(g0, g0 + ATT_GROUP):
            r0 = i * ATT_BLOCK
            lo = max(0, r0 - pad)
            c0 = ATT_NEAR - (r0 + ATT_BLOCK - lo)
            group.append(slice(r0, r0 + ATT_BLOCK))
            tiles.append((q_ref[group[-1], :] * scale, kb_ref[lo:r0 + ATT_BLOCK, :], vb_ref[lo:r0 + ATT_BLOCK, :],
                          lambda hh, c0=c0: bt_ref[hh, :, c0:ATT_NEAR]))
        merged = []
        for rows, (m, l, acc) in zip(group, softmax_pv(tiles)):
            m3 = m3_ref[rows, :]
            mm = jnp.maximum(m, m3)
            wn = jnp.exp(m - mm)
            wf = jnp.exp(m3 - mm)
            num = acc * wn + a3_ref[rows, :] * wf
            den = l * wn + l3_ref[rows, :] * wf
            merged.append(num / den)
        for rows, o in zip(group, merged):
            o_ref[rows, :] = o


def dilated_attention(proj, bias_table, w_up, w_down, layer, bsz, seq):
    assert seq % ATT_BLOCK == 0 and seq // ATT_FAR_DIL <= ATT_BLOCK
    nhp = ATT_HEADS // ATT_HEADS_PER_BLOCK
    steps = bsz * nhp
    d, f = w_up.shape[1:]
    assert d % steps == 0 and f % steps == 0
    qcol = 4 * GROUP_W // LANES
    kcol = 5 * GROUP_W // LANES
    vcol = 6 * GROUP_W // LANES
    width = bias_table.shape[-1]
    stat = pltpu.VMEM((seq, LANES), F32)
    half = pltpu.VMEM((seq, LANES), BF16)
    return pl.pallas_call(
        functools.partial(_attn_kernel, seq=seq),
        grid=(bsz, nhp),
        in_specs=[
            pl.BlockSpec((seq, LANES), lambda b, h: (b, qcol + h)),
            pl.BlockSpec((seq, LANES), lambda b, h: (b, kcol + h)),
            pl.BlockSpec((seq, LANES), lambda b, h: (b, vcol + h)),
            pl.BlockSpec((ATT_HEADS_PER_BLOCK, ATT_BLOCK, width), lambda b, h: (h, 0, 0)),
            pl.BlockSpec((None, d // steps, f), lambda b, h: (layer, b * nhp + h, 0)),
            pl.BlockSpec((None, f // steps, d), lambda b, h: (layer, b * nhp + h, 0)),
        ],
        out_specs=[
            pl.BlockSpec((seq, LANES), lambda b, h: (b, h)),
            pl.BlockSpec((d // steps, f), lambda b, h: (b * nhp + h, 0)),
            pl.BlockSpec((f // steps, d), lambda b, h: (b * nhp + h, 0)),
        ],
        out_shape=[
            jax.ShapeDtypeStruct((bsz * seq, GROUP_W), F32),
            jax.ShapeDtypeStruct((d, f), BF16),
            jax.ShapeDtypeStruct((f, d), BF16),
        ],
        scratch_shapes=[half, half, stat, stat, stat],
        compiler_params=_params("parallel", "parallel"),
        name="dilated_attention",
    )(proj, proj, proj, bias_table, w_up, w_down)


def _mix_out_kernel(ya_ref, yb_ref, yc_ref, yd_ref, g_ref, w_ref, h_ref, o_ref, wb_ref):
    @pl.when(pl.program_id(0) == 0)
    def _():
        wb_ref[...] = w_ref[...].astype(BF16)

    yn = [_rms(y_ref[...], g_ref[:, gi * GROUP_W:(gi + 1) * GROUP_W]).astype(BF16)
          for gi, y_ref in enumerate((ya_ref, yb_ref, yc_ref, yd_ref))]
    o_ref[...] = h_ref[...] + jnp.dot(jnp.concatenate(yn, axis=-1), wb_ref[...], preferred_element_type=F32)


def mix_out(ya, yb, yc, yd, g, w, h, layer, tm=512):
    m, d = h.shape
    yspec = pl.BlockSpec((tm, GROUP_W), lambda i: (i, 0))
    return pl.pallas_call(
        _mix_out_kernel,
        grid=(m // tm,),
        in_specs=[
            yspec, yspec, yspec, yspec,
            pl.BlockSpec((None, 1, d), lambda i: (layer, 0, 0)),
            pl.BlockSpec((None, d, d), lambda i: (layer, 0, 0), pipeline_mode=pl.Buffered(1)),
            pl.BlockSpec((tm, d), lambda i: (i, 0)),
        ],
        out_specs=pl.BlockSpec((tm, d), lambda i: (i, 0)),
        out_shape=jax.ShapeDtypeStruct((m, d), F32),
        scratch_shapes=[pltpu.VMEM((d, d), BF16)],
        compiler_params=_params("arbitrary"),
        name="mix_out",
    )(ya, yb, yc, yd, g, w, h)


def _mem_kv_kernel(mem_ref, g_ref, wk_ref, wv_ref, k_ref, v_ref):
    mn = _rms(mem_ref[...], g_ref[...]).astype(BF16)
    k_ref[...] = jnp.dot(mn, wk_ref[...].astype(BF16), preferred_element_type=F32).astype(BF16)
    v_ref[...] = jnp.dot(mn, wv_ref[...].astype(BF16), preferred_element_type=F32).astype(BF16)


def mem_kv(mem, g, wk, wv):
    m, d = mem.shape
    depth = wk.shape[0]
    wspec = pl.BlockSpec((None, d, X_WIDTH), lambda l: (l, 0, 0))
    ospec = pl.BlockSpec((None, m, X_WIDTH), lambda l: (l, 0, 0))
    return pl.pallas_call(
        _mem_kv_kernel,
        grid=(depth,),
        in_specs=[pl.BlockSpec((m, d), lambda l: (0, 0)), pl.BlockSpec((1, d), lambda l: (0, 0)),
                  wspec, wspec],
        out_specs=[ospec, ospec],
        out_shape=[jax.ShapeDtypeStruct((depth, m, X_WIDTH), BF16)] * 2,
        compiler_params=_params("arbitrary"),
        name="mem_kv",
    )(mem, g, wk, wv)


def _xattn_kernel(h_ref, g_ref, wq_ref, k_ref, v_ref, wo_ref, o_ref, wqb_ref, wob_ref):
    @pl.when((pl.program_id(0) == 0) & (pl.program_id(1) == 0))
    def _():
        wqb_ref[...] = wq_ref[...].astype(BF16)
        wob_ref[...] = wo_ref[...].astype(BF16)

    h = h_ref[...]
    hn = _rms(h, g_ref[...]).astype(BF16)
    q = jnp.dot(hn, wqb_ref[...], preferred_element_type=F32) * (X_HEAD_DIM ** -0.5)
    outs = []
    for hd in range(X_HEADS):
        cols = slice(hd * X_HEAD_DIM, (hd + 1) * X_HEAD_DIM)
        s = lax.dot_general(q[:, cols].astype(BF16), k_ref[:, cols], (((1,), (1,)), ((), ())),
                            preferred_element_type=F32)
        p = jnp.exp(s - jnp.max(s, axis=-1, keepdims=True))
        p = p / jnp.sum(p, axis=-1, keepdims=True)
        outs.append(jnp.dot(p.astype(BF16), v_ref[:, cols], preferred_element_type=F32))
    o = jnp.concatenate(outs, axis=-1).astype(BF16)
    o_ref[...] = h + jnp.dot(o, wob_ref[...], preferred_element_type=F32)


def cross_attention(h, g, wq, k, v, wo, layer, bsz, seq, tm=512):
    d = h.shape[-1]
    nl = seq // tm
    mlen = k.shape[1] // bsz
    return pl.pallas_call(
        _xattn_kernel,
        grid=(bsz, nl),
        in_specs=[
            pl.BlockSpec((tm, d), lambda b, l: (b * nl + l, 0)),
            pl.BlockSpec((None, 1, d), lambda b, l: (layer, 0, 0)),
            pl.BlockSpec((None, d, X_WIDTH), lambda b, l: (layer, 0, 0)),
            pl.BlockSpec((None, mlen, X_WIDTH), lambda b, l: (layer, b, 0)),
            pl.BlockSpec((None, mlen, X_WIDTH), lambda b, l: (layer, b, 0)),
            pl.BlockSpec((None, X_WIDTH, d), lambda b, l: (layer, 0, 0)),
        ],
        out_specs=pl.BlockSpec((tm, d), lambda b, l: (b * nl + l, 0)),
        out_shape=jax.ShapeDtypeStruct(h.shape, F32),
        scratch_shapes=[pltpu.VMEM((d, X_WIDTH), BF16), pltpu.VMEM((X_WIDTH, d), BF16)],
        compiler_params=_params("arbitrary", "arbitrary"),
        name="cross_attention",
    )(h, g, wq, k, v, wo)


def _mlp_kernel(h_ref, g_ref, wu_ref, wd_ref, *rest, out_norm):
    o_ref, hn_ref = rest[-2:]
    j = pl.program_id(1)

    @pl.when(j == 0)
    def _():
        h = h_ref[...]
        hn_ref[...] = _rms(h, g_ref[...]).astype(BF16)
        o_ref[...] = h

    a = jnp.dot(hn_ref[...], wu_ref[...], preferred_element_type=F32)
    a = jnp.square(jnp.maximum(a, 0.0)).astype(BF16)
    o_ref[...] += jnp.dot(a, wd_ref[...], preferred_element_type=F32)

    if out_norm:
        @pl.when(j == pl.num_programs(1) - 1)
        def _():
            o_ref[...] = _rms(o_ref[...], rest[0][...])


def mlp(h, g, wu, wd, layer, out_norm_g=None, tm=1024, tf=512):
    m, d = h.shape
    f = wu.shape[-1]
    out_norm = out_norm_g is not None
    extra_specs = [pl.BlockSpec((1, d), lambda i, j: (0, 0))] if out_norm else []
    extra_args = (out_norm_g,) if out_norm else ()
    return pl.pallas_call(
        functools.partial(_mlp_kernel, out_norm=out_norm),
        grid=(m // tm, f // tf),
        in_specs=[
            pl.BlockSpec((tm, d), lambda i, j: (i, 0)),
            pl.BlockSpec((None, 1, d), lambda i, j: (layer, 0, 0)),
            pl.BlockSpec((d, tf), lambda i, j: (0, j)),
            pl.BlockSpec((tf, d), lambda i, j: (j, 0)),
        ] + extra_specs,
        out_specs=pl.BlockSpec((tm, d), lambda i, j: (i, 0)),
        out_shape=jax.ShapeDtypeStruct((m, d), F32),
        scratch_shapes=[pltpu.VMEM((tm, d), BF16)],
        compiler_params=_params("parallel", "arbitrary"),
        name="mlp",
    )(h, g, wu, wd, *extra_args)


def kernel(x, mem, rel_bias, mem_norm_g, norm_mix_g, w_in, s5_lam_re, s5_lam_im, s5_log_dt, s5_b_re, s5_b_im, s5_c_re, s5_c_im, s5_d, s5_w_glu, pool_w, pool_scale, conv_w_dw, conv_b_dw, conv_ln_g, conv_ln_b, conv_w_pw, grp_norm_g, w_out, norm_x_g, w_xq, w_xk, w_xv, w_xo, norm_mlp_g, w_up, w_down, norm_final_g):
    bsz, seq, d = x.shape
    depth = w_in.shape[0]
    vec = lambda a: a[:, None, :]

    k_mem, v_mem = mem_kv(mem.reshape(bsz * mem.shape[1], d), mem_norm_g[None, :], w_xk, w_xv)
    pow_r, pow_i, bmat, cmat = s5_params(s5_lam_re, s5_lam_im, s5_log_dt, s5_b_re, s5_b_im, s5_c_re, s5_c_im,
                                         chunk=S5_BLOCK // SUBLANES)
    coef_r, coef_i = s5_scan_coefficients(pow_r, pow_i)
    bias_table = attention_bias_table(rel_bias, seq)

    norm_mix, s5_dv, pool_sc = vec(norm_mix_g), vec(s5_d), vec(pool_scale)
    b_dw, ln_g, ln_b = vec(conv_b_dw), vec(conv_ln_g), vec(conv_ln_b)
    grp_g, norm_x, norm_mlp = vec(grp_norm_g), vec(norm_x_g), vec(norm_mlp_g)

    h = x.reshape(bsz * seq, d)
    for l in range(depth):
        proj = norm_matmul(h, norm_mix, w_in, l)
        y_a = s5_mixer(proj, bmat, cmat, coef_r, coef_i, s5_dv, s5_w_glu, l, bsz, seq)
        y_b = pool_mixer(proj, pool_w, pool_sc, l, bsz, seq)
        y_c = conv_mixer(proj, conv_w_dw, b_dw, ln_g, ln_b, conv_w_pw, l, bsz, seq)
        y_d, w_up_b, w_down_b = dilated_attention(proj, bias_table, w_up, w_down, l, bsz, seq)
        h = mix_out(y_a, y_b, y_c, y_d, grp_g, w_out, h, l)
        h = cross_attention(h, norm_x, w_xq, k_mem, v_mem, w_xo, l, bsz, seq)
        h = mlp(h, norm_mlp, w_up_b, w_down_b, l,
                out_norm_g=norm_final_g[None, :] if l == depth - 1 else None)
    return h.reshape(bsz, seq, d)
```

```python
import functools
import math

import numpy as np
import jax
import jax.numpy as jnp
from jax import lax
from jax.experimental import pallas as pl
from jax.experimental.pallas import tpu as pltpu

F32 = jnp.float32
BF16 = jnp.bfloat16

D_MODEL = 2048
DEPTH = 4
GROUP_W = 512
S5_GROUPS = 32
S5_CH = 16
S5_STATE = 64
S5_NSTATE = S5_GROUPS * S5_STATE
S5_SCAN_LANE_BLOCKS = 8
S5_BLOCK = 512
POOL_WINDOWS = (2, 4, 8, 16)
POOL_CH = 128
CONV_WIDTH = 31
ATT_HEADS = 8
ATT_HEAD_DIM = 64
DILATED_PATTERNS = ((128, 1), (512, 4), (2048, 16))
ATT_BLOCK = 128
REL_BUCKETS = 32
REL_MAX_DIST = 2048
MEM_LEN = 256
X_HEADS = 4
X_HEAD_DIM = 128
X_WIDTH = 512
D_FF = 4 * D_MODEL
NORM_EPS = 1e-6
NEG_INF = -1e30
LOG2E = math.log2(math.e)
IN_WIDTH = 7 * GROUP_W

SUBLANES = 8
LANES = 128
VMEM_LIMIT = 52 * 1024 * 1024


def _params(*sem):
    return pltpu.CompilerParams(dimension_semantics=sem, vmem_limit_bytes=VMEM_LIMIT)


def _rms(x, g):
    return x * lax.rsqrt(jnp.mean(x * x, axis=-1, keepdims=True) + NORM_EPS) * g


def _norm_matmul_kernel(x_ref, g_ref, w_ref, o_ref, xn_ref):
    @pl.when(pl.program_id(1) == 0)
    def _():
        xn_ref[...] = _rms(x_ref[...], g_ref[...]).astype(BF16)

    o_ref[...] = jnp.dot(xn_ref[...], w_ref[...], preferred_element_type=F32)


def norm_matmul(x, g, w, layer, tm=1024, tn=1792):
    m, d = x.shape
    n = w.shape[-1]
    return pl.pallas_call(
        _norm_matmul_kernel,
        grid=(m // tm, n // tn),
        in_specs=[
            pl.BlockSpec((tm, d), lambda i, j: (i, 0)),
            pl.BlockSpec((None, 1, d), lambda i, j: (layer, 0, 0)),
            pl.BlockSpec((d, tn), lambda i, j: (0, j)),
        ],
        out_specs=pl.BlockSpec((tm, tn), lambda i, j: (i, j)),
        out_shape=jax.ShapeDtypeStruct((m, n), F32),
        scratch_shapes=[pltpu.VMEM((tm, d), BF16)],
        compiler_params=_params("parallel", "arbitrary"),
        name="norm_matmul",
    )(x, g, w)


def _s5_param_kernel(lr_ref, li_ref, ldt_ref, lrx_ref, lix_ref, ldtx_ref, br_ref, bi_ref, cr_ref, ci_ref,
                     pr_ref, pi_ref, bmat_ref, cmat_ref, *, chunk):
    def discretise(lr, li, ldt):
        dt = jnp.exp(ldt)
        mag = jnp.exp(lr * dt)
        return mag * jnp.cos(li * dt), mag * jnp.sin(li * dt)

    cr, ci = discretise(lr_ref[...], li_ref[...], ldt_ref[...])
    slot = {1: 0, chunk: 1, 2 * chunk: 2, 4 * chunk: 3}
    e = 1
    while True:
        if e in slot:
            pr_ref[slot[e]] = cr
            pi_ref[slot[e]] = ci
        if e == 4 * chunk:
            break
        cr, ci = cr * cr - ci * ci, 2.0 * (cr * ci)
        e *= 2

    lr, li = lrx_ref[...], lix_ref[...]
    ar, ai = discretise(lr, li, ldtx_ref[...])
    den = lr * lr + li * li
    nr, ni = ar - 1.0, ai
    f_r = (nr * lr + ni * li) / den
    f_i = (ni * lr - nr * li) / den
    b_r, b_i = br_ref[...], bi_ref[...]
    bb_r = f_r * b_r - f_i * b_i
    bb_i = f_r * b_i + f_i * b_r

    def block_diagonal(x, row_shift, col_shift):
        rows, w = x.shape
        cols = (rows >> row_shift) << col_shift
        sel = ((lax.broadcasted_iota(jnp.int32, (w, cols), 1) & (w - 1))
               == lax.broadcasted_iota(jnp.int32, (w, cols), 0))
        tiled = jnp.dot(x.astype(BF16), jnp.where(sel, 1.0, 0.0).astype(BF16), preferred_element_type=F32)
        same = ((lax.broadcasted_iota(jnp.int32, (rows, cols), 0) >> row_shift)
                == (lax.broadcasted_iota(jnp.int32, (rows, cols), 1) >> col_shift))
        return jnp.where(same, tiled, 0.0).astype(BF16)

    ch_shift, st_shift = S5_CH.bit_length() - 1, S5_STATE.bit_length() - 1
    bmat_ref[:, 0:S5_NSTATE] = block_diagonal(bb_r, ch_shift, st_shift)
    bmat_ref[:, S5_NSTATE:] = block_diagonal(bb_i, ch_shift, st_shift)
    cmat_ref[0:S5_NSTATE, :] = block_diagonal(cr_ref[...], st_shift, ch_shift)
    cmat_ref[S5_NSTATE:, :] = block_diagonal(-ci_ref[...], st_shift, ch_shift)


S5_POWERS = 4


def s5_params(lam_re, lam_im, log_dt, b_re, b_im, c_re, c_im, chunk):
    depth = lam_re.shape[0]
    gc = S5_GROUPS * S5_CH
    rep = lambda a: jnp.repeat(a, S5_CH, axis=1)
    ldt = log_dt[..., None]
    bt = lambda b: jnp.transpose(b, (0, 1, 3, 2)).reshape(depth, gc, S5_STATE)
    ct = lambda c: jnp.transpose(c, (0, 1, 3, 2)).reshape(depth, S5_NSTATE, S5_CH)
    small = pl.BlockSpec((None, S5_GROUPS, S5_STATE), lambda l: (l, 0, 0))
    small1 = pl.BlockSpec((None, S5_GROUPS, 1), lambda l: (l, 0, 0))
    big = pl.BlockSpec((None, gc, S5_STATE), lambda l: (l, 0, 0))
    big1 = pl.BlockSpec((None, gc, 1), lambda l: (l, 0, 0))
    cspec = pl.BlockSpec((None, S5_NSTATE, S5_CH), lambda l: (l, 0, 0))
    powspec = pl.BlockSpec((None, S5_POWERS, S5_GROUPS, S5_STATE), lambda l: (l, 0, 0, 0))
    return pl.pallas_call(
        functools.partial(_s5_param_kernel, chunk=chunk),
        grid=(depth,),
        in_specs=[small, small, small1, big, big, big1, big, big, cspec, cspec],
        out_specs=[powspec, powspec,
                   pl.BlockSpec((None, gc, 2 * S5_NSTATE), lambda l: (l, 0, 0)),
                   pl.BlockSpec((None, 2 * S5_NSTATE, gc), lambda l: (l, 0, 0))],
        out_shape=[
            jax.ShapeDtypeStruct((depth, S5_POWERS, S5_GROUPS, S5_STATE), F32),
            jax.ShapeDtypeStruct((depth, S5_POWERS, S5_GROUPS, S5_STATE), F32),
            jax.ShapeDtypeStruct((depth, gc, 2 * S5_NSTATE), BF16),
            jax.ShapeDtypeStruct((depth, 2 * S5_NSTATE, gc), BF16),
        ],
        compiler_params=_params("arbitrary"),
        name="s5_params",
    )(lam_re, lam_im, ldt, rep(lam_re), rep(lam_im), rep(ldt), bt(b_re), bt(b_im), ct(c_re), ct(c_im))


def s5_scan_coefficients(pow_r, pow_i):
    depth = pow_r.shape[0]
    row = jnp.arange(SUBLANES)[None, :, None]

    def coef(p):
        p = p.reshape(depth, S5_POWERS, 1, S5_NSTATE)
        full = [jnp.broadcast_to(p[:, e], (depth, SUBLANES, S5_NSTATE)) for e in (0, 1)]
        masked = [jnp.where(row >= s, p[:, e], 0.0) for e, s in ((1, 1), (2, 2), (3, 4))]
        return jnp.stack(full + masked, axis=1)

    return coef(pow_r), coef(pow_i)


MXU_COLS = 256
S5_COEF_TILES = 5


def _s5_kernel(u0_ref, u1_ref, u2_ref, u3_ref, bmat_ref, cmat_ref, cfr_ref, cfi_ref, d_ref, wglu_ref, o_ref,
               up_ref, uperm_ref, xs_ref, cin_ref, yp_ref, *, tl):
    n = S5_NSTATE
    tc = tl // SUBLANES
    pitch = tc + SUBLANES
    u_refs = (u0_ref, u1_ref, u2_ref, u3_ref)
    l = pl.program_id(1)

    @pl.when(l == 0)
    def _():
        cin_ref[...] = jnp.zeros(cin_ref.shape, F32)

    for j, u_ref in enumerate(u_refs):
        for c in range(SUBLANES):
            up_ref[j, c * pitch:c * pitch + tc, :] = u_ref[c * tc:(c + 1) * tc, :]

    def gather(t, carry):
        r = pl.multiple_of(t * SUBLANES, SUBLANES)
        for j in range(len(u_refs)):
            uperm_ref[pl.ds(r, SUBLANES), j * LANES:(j + 1) * LANES] = up_ref[j, pl.ds(t, SUBLANES, stride=pitch), :]
        return carry

    lax.fori_loop(0, tc, gather, 0)

    ub = uperm_ref[...].astype(BF16)
    blocks_per_part = n // MXU_COLS
    for jb in range(2 * blocks_per_part):
        k = (jb % blocks_per_part) * MXU_COLS // S5_STATE * S5_CH // LANES
        xs_ref[:, jb * MXU_COLS:(jb + 1) * MXU_COLS] = jnp.dot(
            ub[:, k * LANES:(k + 1) * LANES], bmat_ref[k * LANES:(k + 1) * LANES, jb * MXU_COLS:(jb + 1) * MXU_COLS],
            preferred_element_type=F32)

    row = lax.broadcasted_iota(jnp.int32, (SUBLANES, LANES), 0)

    def cmul_add(ar, ai, xr, xi, br, bi):
        return (ar * xr - ai * xi) + br, (ar * xi + ai * xr) + bi

    def scan_lanes(lb0):
        cols = [(slice(lb * LANES, (lb + 1) * LANES), slice(n + lb * LANES, n + (lb + 1) * LANES))
                for lb in range(lb0, lb0 + S5_SCAN_LANE_BLOCKS)]
        a = [(cfr_ref[0, :, cr_], cfi_ref[0, :, cr_]) for cr_, _ in cols]

        def recurrence(init, store):
            def step(t, x):
                base = pl.multiple_of(t * SUBLANES, SUBLANES)
                new = []
                for q, (cr_, ci_) in enumerate(cols):
                    nr, ni = cmul_add(a[q][0], a[q][1], x[2 * q], x[2 * q + 1],
                                      xs_ref[pl.ds(base, SUBLANES), cr_], xs_ref[pl.ds(base, SUBLANES), ci_])
                    if store:
                        xs_ref[pl.ds(base, SUBLANES), cr_] = nr
                        xs_ref[pl.ds(base, SUBLANES), ci_] = ni
                    new += [nr, ni]
                return tuple(new)
            return lax.fori_loop(0, tc, step, tuple(init))

        zero = jnp.zeros((SUBLANES, LANES), F32)
        ends = recurrence([zero] * (2 * len(cols)), store=False)

        starts = []
        for q, (cr_, ci_) in enumerate(cols):
            er, ei = ends[2 * q], ends[2 * q + 1]
            wr = jnp.where(row == 0, cin_ref[:, cr_], pltpu.roll(er, 1, 0))
            wi = jnp.where(row == 0, cin_ref[:, ci_], pltpu.roll(ei, 1, 0))
            for k, s in ((2, 1), (3, 2), (4, 4)):
                wr, wi = cmul_add(cfr_ref[k, :, cr_], cfi_ref[k, :, cr_],
                                  pltpu.roll(wr, s, 0), pltpu.roll(wi, s, 0), wr, wi)
            tr, ti = cmul_add(cfr_ref[1, :, cr_], cfi_ref[1, :, cr_], wr, wi, er, ei)
            cin_ref[:, cr_] = pltpu.roll(tr, 1, 0)
            cin_ref[:, ci_] = pltpu.roll(ti, 1, 0)
            starts += [wr, wi]
        recurrence(starts, store=True)

    for lb0 in range(0, n // LANES, S5_SCAN_LANE_BLOCKS):
        scan_lanes(lb0)

    ys = []
    for m in range(GROUP_W // MXU_COLS):
        k0 = m * MXU_COLS // S5_CH * S5_STATE
        k1 = (m + 1) * MXU_COLS // S5_CH * S5_STATE
        cols = slice(m * MXU_COLS, (m + 1) * MXU_COLS)
        ys.append(jnp.dot(xs_ref[:, k0:k1].astype(BF16), cmat_ref[k0:k1, cols], preferred_element_type=F32)
                  + jnp.dot(xs_ref[:, n + k0:n + k1].astype(BF16), cmat_ref[n + k0:n + k1, cols],
                            preferred_element_type=F32))
    y = jnp.concatenate(ys, axis=-1) + d_ref[...] * uperm_ref[...]
    g = jax.nn.gelu(y)
    gate = jnp.dot(g.astype(BF16), wglu_ref[...].astype(BF16), preferred_element_type=F32)
    out = g * jax.nn.sigmoid(gate)
    for j in range(len(u_refs)):
        yp_ref[j] = out[:, j * LANES:(j + 1) * LANES]

    tiles_per_chunk = tc // SUBLANES
    shift = tiles_per_chunk.bit_length() - 1

    def scatter(idx, carry):
        c = idx >> shift
        m = idx & (tiles_per_chunk - 1)
        r = pl.multiple_of(idx * SUBLANES, SUBLANES)
        src = m * (SUBLANES * SUBLANES) + c
        for j in range(len(u_refs)):
            o_ref[pl.ds(r, SUBLANES), j * LANES:(j + 1) * LANES] = yp_ref[j, pl.ds(src, SUBLANES, stride=SUBLANES), :]
        return carry

    lax.fori_loop(0, SUBLANES * tiles_per_chunk, scatter, 0)


def s5_mixer(proj, bmat, cmat, coef_r, coef_i, d_skip, w_glu, layer, bsz, seq, tl=S5_BLOCK):
    nl = seq // tl
    n2 = 2 * S5_NSTATE
    tc = tl // SUBLANES
    assert tc % SUBLANES == 0 and tc & (tc - 1) == 0
    nu = GROUP_W // LANES
    uspecs = [pl.BlockSpec((tl, LANES), lambda b, l, j=j: (b * nl + l, j)) for j in range(nu)]
    coef = pl.BlockSpec((None, S5_COEF_TILES, SUBLANES, S5_NSTATE), lambda b, l: (layer, 0, 0, 0))
    return pl.pallas_call(
        functools.partial(_s5_kernel, tl=tl),
        grid=(bsz, nl),
        in_specs=uspecs + [
            pl.BlockSpec((None, GROUP_W, n2), lambda b, l: (layer, 0, 0)),
            pl.BlockSpec((None, n2, GROUP_W), lambda b, l: (layer, 0, 0)),
            coef, coef,
            pl.BlockSpec((None, 1, GROUP_W), lambda b, l: (layer, 0, 0)),
            pl.BlockSpec((None, GROUP_W, GROUP_W), lambda b, l: (layer, 0, 0)),
        ],
        out_specs=pl.BlockSpec((tl, GROUP_W), lambda b, l: (b * nl + l, 0)),
        out_shape=jax.ShapeDtypeStruct((bsz * seq, GROUP_W), F32),
        scratch_shapes=[
            pltpu.VMEM((nu, SUBLANES * (tc + SUBLANES), LANES), F32),
            pltpu.VMEM((tl, GROUP_W), F32),
            pltpu.VMEM((tl, n2), F32),
            pltpu.VMEM((SUBLANES, n2), F32),
            pltpu.VMEM((nu, tl, LANES), F32),
        ],
        compiler_params=_params("parallel", "arbitrary"),
        name="s5_mixer",
    )(*([proj] * nu), bmat, cmat, coef_r, coef_i, d_skip, w_glu)


POOL_HALO = 16


def _pool_kernel(u_ref, pw_ref, sc_ref, o_ref, ext_ref, *, tl):
    l = pl.program_id(1)

    @pl.when(l == 0)
    def _():
        ext_ref[0:POOL_HALO, :] = jnp.zeros((POOL_HALO, GROUP_W), F32)

    @pl.when(l > 0)
    def _():
        ext_ref[0:POOL_HALO, :] = ext_ref[tl:tl + POOL_HALO, :]

    ext_ref[POOL_HALO:, :] = u_ref[...]
    t = l * tl + lax.broadcasted_iota(jnp.int32, (tl, 1), 0)
    for gi, w in enumerate(POOL_WINDOWS):
        cols = slice(gi * POOL_CH, (gi + 1) * POOL_CH)
        acc = ext_ref[pl.ds(POOL_HALO, tl), cols]
        for k in range(1, w):
            acc = acc + ext_ref[pl.ds(POOL_HALO - k, tl), cols]
        cnt = jnp.minimum(t + 1, w).astype(F32)
        p = acc / cnt - u_ref[:, cols]
        y = jnp.dot(p.astype(BF16), pw_ref[gi].astype(BF16), preferred_element_type=F32)
        o_ref[:, cols] = y * sc_ref[:, cols]


def pool_mixer(proj, pool_w, pool_scale, layer, bsz, seq, tl=512):
    nl = seq // tl
    ng = len(POOL_WINDOWS)
    return pl.pallas_call(
        functools.partial(_pool_kernel, tl=tl),
        grid=(bsz, nl),
        in_specs=[
            pl.BlockSpec((tl, GROUP_W), lambda b, l: (b * nl + l, 1)),
            pl.BlockSpec((None, ng, POOL_CH, POOL_CH), lambda b, l: (layer, 0, 0, 0)),
            pl.BlockSpec((None, 1, GROUP_W), lambda b, l: (layer, 0, 0)),
        ],
        out_specs=pl.BlockSpec((tl, GROUP_W), lambda b, l: (b * nl + l, 0)),
        out_shape=jax.ShapeDtypeStruct((bsz * seq, GROUP_W), F32),
        scratch_shapes=[pltpu.VMEM((POOL_HALO + tl, GROUP_W), F32)],
        compiler_params=_params("parallel", "arbitrary"),
        name="pool_mixer",
    )(proj, pool_w, pool_scale)


CONV_HALO = 32
CONV_ROWS = 32


def _conv_kernel(val_ref, gate_ref, wdw_ref, bdw_ref, lng_ref, lnb_ref, wpw_ref, o_ref,
                 sh_ref, cv_ref, wb_ref, *, tl):
    l = pl.program_id(1)
    n = tl + CONV_HALO

    @pl.when(l == 0)
    def _():
        sh_ref[0, 0:CONV_HALO, :] = jnp.zeros((CONV_HALO, GROUP_W), F32)

    @pl.when(l > 0)
    def _():
        sh_ref[0, 0:CONV_HALO, :] = sh_ref[0, tl:n, :]

    sh_ref[0, CONV_HALO:n, :] = val_ref[...] * jax.nn.sigmoid(gate_ref[...])
    sh_ref[0, n:, :] = jnp.zeros((SUBLANES, GROUP_W), F32)
    for s in range(1, SUBLANES):
        sh_ref[s, 0:n, :] = sh_ref[0, s:s + n, :]

    off = CONV_HALO - (CONV_WIDTH - 1)

    for k in range(CONV_WIDTH):
        wb_ref[k] = jnp.broadcast_to(wdw_ref[k:k + 1, :], (SUBLANES, GROUP_W))
    wb_ref[CONV_WIDTH] = jnp.broadcast_to(bdw_ref[...], (SUBLANES, GROUP_W))
    tiles = CONV_ROWS // SUBLANES

    def rows(c, carry):
        r0 = pl.multiple_of(c * CONV_ROWS, CONV_ROWS)
        acc = jnp.broadcast_to(wb_ref[CONV_WIDTH][None], (tiles, SUBLANES, GROUP_W))
        for k in range(CONV_WIDTH):
            q, s = divmod(off + k, SUBLANES)
            x = sh_ref[s, pl.ds(r0 + q * SUBLANES, CONV_ROWS), :].reshape(tiles, SUBLANES, GROUP_W)
            acc = acc + wb_ref[k][None] * x
        cv_ref[pl.ds(r0, CONV_ROWS), :] = acc.reshape(CONV_ROWS, GROUP_W)
        return carry

    lax.fori_loop(0, tl // CONV_ROWS, rows, 0)

    h = cv_ref[...]
    hc = h - jnp.mean(h, axis=-1, keepdims=True)
    y = hc * lax.rsqrt(jnp.mean(hc * hc, axis=-1, keepdims=True) + NORM_EPS)
    y = y * lng_ref[...] + lnb_ref[...]
    y = jax.nn.silu(y)
    o_ref[...] = jnp.dot(y.astype(BF16), wpw_ref[...].astype(BF16), preferred_element_type=F32)


def conv_mixer(proj, w_dw, b_dw, ln_g, ln_b, w_pw, layer, bsz, seq, tl=512):
    nl = seq // tl
    vec = pl.BlockSpec((None, 1, GROUP_W), lambda b, l: (layer, 0, 0))
    return pl.pallas_call(
        functools.partial(_conv_kernel, tl=tl),
        grid=(bsz, nl),
        in_specs=[
            pl.BlockSpec((tl, GROUP_W), lambda b, l: (b * nl + l, 2)),
            pl.BlockSpec((tl, GROUP_W), lambda b, l: (b * nl + l, 3)),
            pl.BlockSpec((None, CONV_WIDTH, GROUP_W), lambda b, l: (layer, 0, 0)),
            vec, vec, vec,
            pl.BlockSpec((None, GROUP_W, GROUP_W), lambda b, l: (layer, 0, 0)),
        ],
        out_specs=pl.BlockSpec((tl, GROUP_W), lambda b, l: (b * nl + l, 0)),
        out_shape=jax.ShapeDtypeStruct((bsz * seq, GROUP_W), F32),
        scratch_shapes=[pltpu.VMEM((SUBLANES, CONV_HALO + tl + SUBLANES, GROUP_W), F32),
                        pltpu.VMEM((tl, GROUP_W), F32),
                        pltpu.VMEM((CONV_WIDTH + 1, SUBLANES, GROUP_W), F32)],
        compiler_params=_params("parallel", "arbitrary"),
        name="conv_mixer",
    )(proj, proj, w_dw, b_dw, ln_g, ln_b, w_pw)


def _t5_bucket(dist):
    n = np.maximum(dist, 0)
    max_exact = REL_BUCKETS // 2
    large = max_exact + (np.log(np.maximum(n, 1) / max_exact) / np.log(REL_MAX_DIST / max_exact)
                         * (REL_BUCKETS - max_exact)).astype(np.int64)
    large = np.minimum(large, REL_BUCKETS - 1)
    return np.where(n < max_exact, n, large).astype(np.int32)


ATT_NEAR_BLOCKS = 5
ATT_NEAR = ATT_NEAR_BLOCKS * ATT_BLOCK
ATT_FAR_DIL = 16


def _attention_bias_layout(sub):
    a = np.arange(ATT_BLOCK)[:, None]
    d_near = (ATT_NEAR - ATT_BLOCK) + a - np.arange(ATT_NEAR)[None, :]
    mult = np.zeros(d_near.shape, np.int64)
    for window, dil in DILATED_PATTERNS:
        if dil != ATT_FAR_DIL:
            mult += (d_near >= 0) & (d_near % dil == 0) & (d_near // dil <= window // dil)
    d_far = ATT_FAR_DIL * (a - np.arange(sub)[None, :])
    mult_far = ((d_far >= 0) & (d_far // ATT_FAR_DIL <= ATT_BLOCK)).astype(np.int64)
    d = np.concatenate([d_near, d_far], axis=1)
    mult = np.concatenate([mult, mult_far], axis=1)
    addend = np.where(mult > 0, np.log(np.maximum(mult, 1)), NEG_INF).astype(np.float32)
    return _t5_bucket(d), addend


def _attn_bias_kernel(rb_ref, bucket_ref, add_ref, o_ref):
    h = pl.program_id(0)
    bucket = bucket_ref[...]
    t = jnp.zeros(bucket.shape, F32)
    for b in range(REL_BUCKETS):
        t = jnp.where(bucket == b, rb_ref[b, h], t)
    o_ref[...] = (t + add_ref[...]) * LOG2E


def attention_bias_table(rel_bias, seq):
    sub = seq // ATT_FAR_DIL
    bucket, addend = _attention_bias_layout(sub)
    width = ATT_NEAR + sub
    full = pl.BlockSpec((ATT_BLOCK, width), lambda h: (0, 0))
    return pl.pallas_call(
        _attn_bias_kernel,
        grid=(ATT_HEADS,),
        in_specs=[pl.BlockSpec(memory_space=pltpu.SMEM), full, full],
        out_specs=pl.BlockSpec((None, ATT_BLOCK, width), lambda h: (h, 0, 0)),
        out_shape=jax.ShapeDtypeStruct((ATT_HEADS, ATT_BLOCK, width), F32),
        compiler_params=_params("arbitrary"),
        name="attn_bias",
    )(rel_bias, jnp.asarray(bucket), jnp.asarray(addend))


ATT_HEADS_PER_BLOCK = LANES // ATT_HEAD_DIM
ATT_GROUP = 4


def _nt_dot(a, b):
    return lax.dot_general(a, b, (((1,), (1,)), ((), ())), preferred_element_type=F32)


def _attn_kernel(q_ref, k_ref, v_ref, bt_ref, *rest, seq, n_cast):
    cast_in, o_ref, cast_out = rest[:n_cast], rest[n_cast], rest[n_cast + 1:2 * n_cast + 1]
    kb_ref, vb_ref, m3_ref, l3_ref, a3_ref = rest[2 * n_cast + 1:]
    sub = seq // ATT_FAR_DIL
    pad = ATT_NEAR - ATT_BLOCK
    scale = ATT_HEAD_DIM ** -0.5 * LOG2E
    lane = lax.broadcasted_iota(jnp.int32, (1, LANES), 1)
    in_head = [(lane >= hh * ATT_HEAD_DIM) & (lane < (hh + 1) * ATT_HEAD_DIM)
               for hh in range(ATT_HEADS_PER_BLOCK)]

    def by_head(vals):
        out = vals[-1]
        for hh in range(ATT_HEADS_PER_BLOCK - 2, -1, -1):
            out = jnp.where(in_head[hh], vals[hh], out)
        return out

    def softmax_pv(tiles):
        heads = range(ATT_HEADS_PER_BLOCK)
        s = [[_nt_dot(jnp.where(in_head[hh], qs, 0.0).astype(BF16), kb) + bias_of(hh) for hh in heads]
             for qs, kb, _, bias_of in tiles]
        m = [[jnp.max(s_h, axis=-1, keepdims=True) for s_h in s_t] for s_t in s]
        p = [[jnp.exp2(s_h - m_h) for s_h, m_h in zip(s_t, m_t)] for s_t, m_t in zip(s, m)]
        l = [[jnp.sum(p_h, axis=-1, keepdims=True) for p_h in p_t] for p_t in p]
        acc = [[jnp.dot(p_h.astype(BF16), tile[2], preferred_element_type=F32) for p_h in p_t]
               for p_t, tile in zip(p, tiles)]
        out = []
        for tile, m_t, l_t, acc_t in zip(tiles, m, l, acc):
            full = (tile[0].shape[0], LANES)
            out.append((by_head([jnp.broadcast_to(x, full) for x in m_t]),
                        by_head([jnp.broadcast_to(x, full) for x in l_t]), by_head(acc_t)))
        return out

    for src, dst in zip(cast_in, cast_out):
        dst[...] = src[...].astype(BF16)

    kb_ref[...] = k_ref[...].astype(BF16)
    vb_ref[...] = v_ref[...].astype(BF16)

    for g0 in range(0, ATT_FAR_DIL, ATT_GROUP):
        group = [pl.ds(r, sub, stride=ATT_FAR_DIL) for r in range(g0, g0 + ATT_GROUP)]
        far_bias = lambda hh: bt_ref[hh, 0:sub, ATT_NEAR:ATT_NEAR + sub]
        tiles = [(q_ref[rows, :] * scale, k_ref[rows, :].astype(BF16), v_ref[rows, :].astype(BF16), far_bias)
                 for rows in group]
        for rows, (m, l, acc) in zip(group, softmax_pv(tiles)):
            m3_ref[rows, :] = m
            l3_ref[rows, :] = l
            a3_ref[rows, :] = acc

    for g0 in range(0, seq // ATT_BLOCK, ATT_GROUP):
        group, tiles = [], []
        for i in range(g0, g0 + ATT_GROUP):
            r0 = i * ATT_BLOCK
            lo = max(0, r0 - pad)
            c0 = ATT_NEAR - (r0 + ATT_BLOCK - lo)
            group.append(slice(r0, r0 + ATT_BLOCK))
            tiles.append((q_ref[group[-1], :] * scale, kb_ref[lo:r0 + ATT_BLOCK, :], vb_ref[lo:r0 + ATT_BLOCK, :],
                          lambda hh, c0=c0: bt_ref[hh, :, c0:ATT_NEAR]))
        merged = []
        for rows, (m, l, acc) in zip(group, softmax_pv(tiles)):
            m3 = m3_ref[rows, :]
            mm = jnp.maximum(m, m3)
            wn = jnp.exp2(m - mm)
            wf = jnp.exp2(m3 - mm)
            num = acc * wn + a3_ref[rows, :] * wf
            den = l * wn + l3_ref[rows, :] * wf
            merged.append(num / den)
        for rows, o in zip(group, merged):
            o_ref[rows, :] = o


def dilated_attention(proj, bias_table, casts, bsz, seq):
    assert seq % ATT_BLOCK == 0 and seq // ATT_FAR_DIL <= ATT_BLOCK
    nhp = ATT_HEADS // ATT_HEADS_PER_BLOCK
    steps = bsz * nhp
    qcol = 4 * GROUP_W // LANES
    kcol = 5 * GROUP_W // LANES
    vcol = 6 * GROUP_W // LANES
    width = bias_table.shape[-1]
    stat = pltpu.VMEM((seq, LANES), F32)
    half = pltpu.VMEM((seq, LANES), BF16)
    cast_in, cast_out, cast_shapes = [], [], []
    for w, layer in casts:
        rows, cols = w.shape[1:]
        assert rows % steps == 0
        cast_in.append(pl.BlockSpec((None, rows // steps, cols), lambda b, h, layer=layer: (layer, b * nhp + h, 0)))
        cast_out.append(pl.BlockSpec((rows // steps, cols), lambda b, h: (b * nhp + h, 0)))
        cast_shapes.append(jax.ShapeDtypeStruct((rows, cols), BF16))
    return pl.pallas_call(
        functools.partial(_attn_kernel, seq=seq, n_cast=len(casts)),
        grid=(bsz, nhp),
        in_specs=[
            pl.BlockSpec((seq, LANES), lambda b, h: (b, qcol + h)),
            pl.BlockSpec((seq, LANES), lambda b, h: (b, kcol + h)),
            pl.BlockSpec((seq, LANES), lambda b, h: (b, vcol + h)),
            pl.BlockSpec((ATT_HEADS_PER_BLOCK, ATT_BLOCK, width), lambda b, h: (h, 0, 0)),
        ] + cast_in,
        out_specs=[pl.BlockSpec((seq, LANES), lambda b, h: (b, h))] + cast_out,
        out_shape=[jax.ShapeDtypeStruct((bsz * seq, GROUP_W), F32)] + cast_shapes,
        scratch_shapes=[half, half, stat, stat, stat],
        compiler_params=_params("parallel", "parallel"),
        name="dilated_attention",
    )(proj, proj, proj, bias_table, *[w for w, _ in casts])


def _mix_out_kernel(ya_ref, yb_ref, yc_ref, yd_ref, g_ref, w_ref, h_ref, o_ref, wb_ref):
    @pl.when(pl.program_id(0) == 0)
    def _():
        wb_ref[...] = w_ref[...].astype(BF16)

    yn = [_rms(y_ref[...], g_ref[:, gi * GROUP_W:(gi + 1) * GROUP_W]).astype(BF16)
          for gi, y_ref in enumerate((ya_ref, yb_ref, yc_ref, yd_ref))]
    o_ref[...] = h_ref[...] + jnp.dot(jnp.concatenate(yn, axis=-1), wb_ref[...], preferred_element_type=F32)


def mix_out(ya, yb, yc, yd, g, w, h, layer, tm=512):
    m, d = h.shape
    yspec = pl.BlockSpec((tm, GROUP_W), lambda i: (i, 0))
    return pl.pallas_call(
        _mix_out_kernel,
        grid=(m // tm,),
        in_specs=[
            yspec, yspec, yspec, yspec,
            pl.BlockSpec((None, 1, d), lambda i: (layer, 0, 0)),
            pl.BlockSpec((None, d, d), lambda i: (layer, 0, 0), pipeline_mode=pl.Buffered(1)),
            pl.BlockSpec((tm, d), lambda i: (i, 0)),
        ],
        out_specs=pl.BlockSpec((tm, d), lambda i: (i, 0)),
        out_shape=jax.ShapeDtypeStruct((m, d), F32),
        scratch_shapes=[pltpu.VMEM((d, d), BF16)],
        compiler_params=_params("arbitrary"),
        name="mix_out",
    )(ya, yb, yc, yd, g, w, h)


def _mem_kv_kernel(mem_ref, g_ref, wk_ref, wv_ref, k_ref, v_ref):
    mn = _rms(mem_ref[...], g_ref[...]).astype(BF16)
    k_ref[...] = jnp.dot(mn, wk_ref[...].astype(BF16), preferred_element_type=F32).astype(BF16)
    v_ref[...] = jnp.dot(mn, wv_ref[...].astype(BF16), preferred_element_type=F32).astype(BF16)


def mem_kv(mem, g, wk, wv):
    m, d = mem.shape
    depth = wk.shape[0]
    wspec = pl.BlockSpec((None, d, X_WIDTH), lambda l: (l, 0, 0))
    ospec = pl.BlockSpec((None, m, X_WIDTH), lambda l: (l, 0, 0))
    return pl.pallas_call(
        _mem_kv_kernel,
        grid=(depth,),
        in_specs=[pl.BlockSpec((m, d), lambda l: (0, 0)), pl.BlockSpec((1, d), lambda l: (0, 0)),
                  wspec, wspec],
        out_specs=[ospec, ospec],
        out_shape=[jax.ShapeDtypeStruct((depth, m, X_WIDTH), BF16)] * 2,
        compiler_params=_params("arbitrary"),
        name="mem_kv",
    )(mem, g, wk, wv)


def _xattn_kernel(h_ref, g_ref, wq_ref, k_ref, v_ref, wo_ref, o_ref, wqb_ref, wob_ref):
    @pl.when((pl.program_id(0) == 0) & (pl.program_id(1) == 0))
    def _():
        wqb_ref[...] = wq_ref[...].astype(BF16)
        wob_ref[...] = wo_ref[...].astype(BF16)

    h = h_ref[...]
    hn = _rms(h, g_ref[...]).astype(BF16)
    q = jnp.dot(hn, wqb_ref[...], preferred_element_type=F32) * (X_HEAD_DIM ** -0.5 * LOG2E)
    cols = [slice(hd * X_HEAD_DIM, (hd + 1) * X_HEAD_DIM) for hd in range(X_HEADS)]
    qb = q.astype(BF16)
    s = [_nt_dot(qb[:, c], k_ref[:, c]) for c in cols]
    m = [jnp.max(x, axis=-1, keepdims=True) for x in s]
    p = [jnp.exp2(x - mx) for x, mx in zip(s, m)]
    den = [jnp.sum(x, axis=-1, keepdims=True) for x in p]
    p = [x / dx for x, dx in zip(p, den)]
    outs = [jnp.dot(x.astype(BF16), v_ref[:, c], preferred_element_type=F32) for x, c in zip(p, cols)]
    o = jnp.concatenate(outs, axis=-1).astype(BF16)
    o_ref[...] = h + jnp.dot(o, wob_ref[...], preferred_element_type=F32)


def cross_attention(h, g, wq, k, v, wo, layer, bsz, seq, tm=512):
    d = h.shape[-1]
    nl = seq // tm
    mlen = k.shape[1] // bsz
    return pl.pallas_call(
        _xattn_kernel,
        grid=(bsz, nl),
        in_specs=[
            pl.BlockSpec((tm, d), lambda b, l: (b * nl + l, 0)),
            pl.BlockSpec((None, 1, d), lambda b, l: (layer, 0, 0)),
            pl.BlockSpec((None, d, X_WIDTH), lambda b, l: (layer, 0, 0)),
            pl.BlockSpec((None, mlen, X_WIDTH), lambda b, l: (layer, b, 0)),
            pl.BlockSpec((None, mlen, X_WIDTH), lambda b, l: (layer, b, 0)),
            pl.BlockSpec((None, X_WIDTH, d), lambda b, l: (layer, 0, 0)),
        ],
        out_specs=pl.BlockSpec((tm, d), lambda b, l: (b * nl + l, 0)),
        out_shape=jax.ShapeDtypeStruct(h.shape, F32),
        scratch_shapes=[pltpu.VMEM((d, X_WIDTH), BF16), pltpu.VMEM((X_WIDTH, d), BF16)],
        compiler_params=_params("arbitrary", "arbitrary"),
        name="cross_attention",
    )(h, g, wq, k, v, wo)


def _mlp_kernel(h_ref, g_ref, wu_ref, wd_ref, *rest, out_norm):
    o_ref, hn_ref = rest[-2:]
    j = pl.program_id(1)

    @pl.when(j == 0)
    def _():
        h = h_ref[...]
        hn_ref[...] = _rms(h, g_ref[...]).astype(BF16)
        o_ref[...] = h

    a = jnp.dot(hn_ref[...], wu_ref[...], preferred_element_type=F32)
    a = jnp.square(jnp.maximum(a, 0.0)).astype(BF16)
    o_ref[...] += jnp.dot(a, wd_ref[...], preferred_element_type=F32)

    if out_norm:
        @pl.when(j == pl.num_programs(1) - 1)
        def _():
            o_ref[...] = _rms(o_ref[...], rest[0][...])


def mlp(h, g, wu, wd, layer, out_norm_g=None, tm=1024, tf=512):
    m, d = h.shape
    f = wu.shape[-1]
    out_norm = out_norm_g is not None
    extra_specs = [pl.BlockSpec((1, d), lambda i, j: (0, 0))] if out_norm else []
    extra_args = (out_norm_g,) if out_norm else ()
    return pl.pallas_call(
        functools.partial(_mlp_kernel, out_norm=out_norm),
        grid=(m // tm, f // tf),
        in_specs=[
            pl.BlockSpec((tm, d), lambda i, j: (i, 0)),
            pl.BlockSpec((None, 1, d), lambda i, j: (layer, 0, 0)),
            pl.BlockSpec((d, tf), lambda i, j: (0, j)),
            pl.BlockSpec((tf, d), lambda i, j: (j, 0)),
        ] + extra_specs,
        out_specs=pl.BlockSpec((tm, d), lambda i, j: (i, 0)),
        out_shape=jax.ShapeDtypeStruct((m, d), F32),
        scratch_shapes=[pltpu.VMEM((tm, d), BF16)],
        compiler_params=_params("parallel", "arbitrary"),
        name="mlp",
    )(h, g, wu, wd, *extra_args)


def kernel(x, mem, rel_bias, mem_norm_g, norm_mix_g, w_in, s5_lam_re, s5_lam_im, s5_log_dt, s5_b_re, s5_b_im, s5_c_re, s5_c_im, s5_d, s5_w_glu, pool_w, pool_scale, conv_w_dw, conv_b_dw, conv_ln_g, conv_ln_b, conv_w_pw, grp_norm_g, w_out, norm_x_g, w_xq, w_xk, w_xv, w_xo, norm_mlp_g, w_up, w_down, norm_final_g):
    bsz, seq, d = x.shape
    depth = w_in.shape[0]
    vec = lambda a: a[:, None, :]

    k_mem, v_mem = mem_kv(mem.reshape(bsz * mem.shape[1], d), mem_norm_g[None, :], w_xk, w_xv)
    pow_r, pow_i, bmat, cmat = s5_params(s5_lam_re, s5_lam_im, s5_log_dt, s5_b_re, s5_b_im, s5_c_re, s5_c_im,
                                         chunk=S5_BLOCK // SUBLANES)
    coef_r, coef_i = s5_scan_coefficients(pow_r, pow_i)
    bias_table = attention_bias_table(rel_bias, seq)

    norm_mix, s5_dv, pool_sc = vec(norm_mix_g), vec(s5_d), vec(pool_scale)
    b_dw, ln_g, ln_b = vec(conv_b_dw), vec(conv_ln_g), vec(conv_ln_b)
    grp_g, norm_x, norm_mlp = vec(grp_norm_g), vec(norm_x_g), vec(norm_mlp_g)

    h = x.reshape(bsz * seq, d)
    w_in_b = w_in[0].astype(BF16)
    for l in range(depth):
        proj = norm_matmul(h, norm_mix, w_in_b, l)
        y_a = s5_mixer(proj, bmat, cmat, coef_r, coef_i, s5_dv, s5_w_glu, l, bsz, seq)
        y_b = pool_mixer(proj, pool_w, pool_sc, l, bsz, seq)
        y_c = conv_mixer(proj, conv_w_dw, b_dw, ln_g, ln_b, conv_w_pw, l, bsz, seq)
        casts = [(w_up, l), (w_down, l)] + ([(w_in, l + 1)] if l + 1 < depth else [])
        y_d, w_up_b, w_down_b, *nxt = dilated_attention(proj, bias_table, casts, bsz, seq)
        w_in_b = nxt[0] if nxt else None
        h = mix_out(y_a, y_b, y_c, y_d, grp_g, w_out, h, l)
        h = cross_attention(h, norm_x, w_xq, k_mem, v_mem, w_xo, l, bsz, seq)
        h = mlp(h, norm_mlp, w_up_b, w_down_b, l,
                out_norm_g=norm_final_g[None, :] if l == depth - 1 else None)
    return h.reshape(bsz, seq, d)
```

```python
import functools
import math

import numpy as np
import jax
import jax.numpy as jnp
from jax import lax
from jax.experimental import pallas as pl
from jax.experimental.pallas import tpu as pltpu

F32 = jnp.float32
BF16 = jnp.bfloat16

D_MODEL = 2048
DEPTH = 4
GROUP_W = 512
S5_GROUPS = 32
S5_CH = 16
S5_STATE = 64
S5_NSTATE = S5_GROUPS * S5_STATE
S5_SCAN_LANE_BLOCKS = 8
S5_BLOCK = 512
S5_SCAN_UNROLL = 4
POOL_WINDOWS = (2, 4, 8, 16)
POOL_CH = 128
CONV_WIDTH = 31
ATT_HEADS = 8
ATT_HEAD_DIM = 64
DILATED_PATTERNS = ((128, 1), (512, 4), (2048, 16))
ATT_BLOCK = 128
REL_BUCKETS = 32
REL_MAX_DIST = 2048
MEM_LEN = 256
X_HEADS = 4
X_HEAD_DIM = 128
X_WIDTH = 512
D_FF = 4 * D_MODEL
NORM_EPS = 1e-6
NEG_INF = -1e30
LOG2E = math.log2(math.e)
IN_WIDTH = 7 * GROUP_W

SUBLANES = 8
LANES = 128
VMEM_LIMIT = 52 * 1024 * 1024


def _params(*sem):
    return pltpu.CompilerParams(dimension_semantics=sem, vmem_limit_bytes=VMEM_LIMIT)


def _rms(x, g):
    return x * lax.rsqrt(jnp.mean(x * x, axis=-1, keepdims=True) + NORM_EPS) * g


def _norm_matmul_kernel(x_ref, g_ref, w_ref, o_ref):
    xn = _rms(x_ref[...], g_ref[...]).astype(BF16)
    o_ref[...] = jnp.dot(xn, w_ref[...], preferred_element_type=F32)


def norm_matmul(x, g, w, layer, tm=512):
    m, d = x.shape
    n = w.shape[-1]
    return pl.pallas_call(
        _norm_matmul_kernel,
        grid=(m // tm,),
        in_specs=[
            pl.BlockSpec((tm, d), lambda i: (i, 0)),
            pl.BlockSpec((None, 1, d), lambda i: (layer, 0, 0)),
            pl.BlockSpec((d, n), lambda i: (0, 0), pipeline_mode=pl.Buffered(1)),
        ],
        out_specs=pl.BlockSpec((tm, n), lambda i: (i, 0)),
        out_shape=jax.ShapeDtypeStruct((m, n), F32),
        compiler_params=_params("parallel"),
        name="norm_matmul",
    )(x, g, w)


def _s5_param_kernel(lr_ref, li_ref, ldt_ref, lrx_ref, lix_ref, ldtx_ref, br_ref, bi_ref, cr_ref, ci_ref,
                     pr_ref, pi_ref, bmat_ref, cmat_ref, *, chunk):
    def discretise(lr, li, ldt):
        dt = jnp.exp(ldt)
        mag = jnp.exp(lr * dt)
        return mag * jnp.cos(li * dt), mag * jnp.sin(li * dt)

    cr, ci = discretise(lr_ref[...], li_ref[...], ldt_ref[...])
    slot = {1: 0, chunk: 1, 2 * chunk: 2, 4 * chunk: 3}
    e = 1
    while True:
        if e in slot:
            pr_ref[slot[e]] = cr
            pi_ref[slot[e]] = ci
        if e == 4 * chunk:
            break
        cr, ci = cr * cr - ci * ci, 2.0 * (cr * ci)
        e *= 2

    lr, li = lrx_ref[...], lix_ref[...]
    ar, ai = discretise(lr, li, ldtx_ref[...])
    den = lr * lr + li * li
    nr, ni = ar - 1.0, ai
    f_r = (nr * lr + ni * li) / den
    f_i = (ni * lr - nr * li) / den
    b_r, b_i = br_ref[...], bi_ref[...]
    bb_r = f_r * b_r - f_i * b_i
    bb_i = f_r * b_i + f_i * b_r

    def block_diagonal(x, row_shift, col_shift):
        rows, w = x.shape
        cols = (rows >> row_shift) << col_shift
        sel = ((lax.broadcasted_iota(jnp.int32, (w, cols), 1) & (w - 1))
               == lax.broadcasted_iota(jnp.int32, (w, cols), 0))
        tiled = jnp.dot(x.astype(BF16), jnp.where(sel, 1.0, 0.0).astype(BF16), preferred_element_type=F32)
        same = ((lax.broadcasted_iota(jnp.int32, (rows, cols), 0) >> row_shift)
                == (lax.broadcasted_iota(jnp.int32, (rows, cols), 1) >> col_shift))
        return jnp.where(same, tiled, 0.0).astype(BF16)

    ch_shift, st_shift = S5_CH.bit_length() - 1, S5_STATE.bit_length() - 1
    bmat_ref[:, 0:S5_NSTATE] = block_diagonal(bb_r, ch_shift, st_shift)
    bmat_ref[:, S5_NSTATE:] = block_diagonal(bb_i, ch_shift, st_shift)
    cmat_ref[0:S5_NSTATE, :] = block_diagonal(cr_ref[...], st_shift, ch_shift)
    cmat_ref[S5_NSTATE:, :] = block_diagonal(-ci_ref[...], st_shift, ch_shift)


S5_POWERS = 4


def s5_params(lam_re, lam_im, log_dt, b_re, b_im, c_re, c_im, chunk):
    depth = lam_re.shape[0]
    gc = S5_GROUPS * S5_CH
    rep = lambda a: jnp.repeat(a, S5_CH, axis=1)
    ldt = log_dt[..., None]
    bt = lambda b: jnp.transpose(b, (0, 1, 3, 2)).reshape(depth, gc, S5_STATE)
    ct = lambda c: jnp.transpose(c, (0, 1, 3, 2)).reshape(depth, S5_NSTATE, S5_CH)
    small = pl.BlockSpec((None, S5_GROUPS, S5_STATE), lambda l: (l, 0, 0))
    small1 = pl.BlockSpec((None, S5_GROUPS, 1), lambda l: (l, 0, 0))
    big = pl.BlockSpec((None, gc, S5_STATE), lambda l: (l, 0, 0))
    big1 = pl.BlockSpec((None, gc, 1), lambda l: (l, 0, 0))
    cspec = pl.BlockSpec((None, S5_NSTATE, S5_CH), lambda l: (l, 0, 0))
    powspec = pl.BlockSpec((None, S5_POWERS, S5_GROUPS, S5_STATE), lambda l: (l, 0, 0, 0))
    return pl.pallas_call(
        functools.partial(_s5_param_kernel, chunk=chunk),
        grid=(depth,),
        in_specs=[small, small, small1, big, big, big1, big, big, cspec, cspec],
        out_specs=[powspec, powspec,
                   pl.BlockSpec((None, gc, 2 * S5_NSTATE), lambda l: (l, 0, 0)),
                   pl.BlockSpec((None, 2 * S5_NSTATE, gc), lambda l: (l, 0, 0))],
        out_shape=[
            jax.ShapeDtypeStruct((depth, S5_POWERS, S5_GROUPS, S5_STATE), F32),
            jax.ShapeDtypeStruct((depth, S5_POWERS, S5_GROUPS, S5_STATE), F32),
            jax.ShapeDtypeStruct((depth, gc, 2 * S5_NSTATE), BF16),
            jax.ShapeDtypeStruct((depth, 2 * S5_NSTATE, gc), BF16),
        ],
        compiler_params=_params("arbitrary"),
        name="s5_params",
    )(lam_re, lam_im, ldt, rep(lam_re), rep(lam_im), rep(ldt), bt(b_re), bt(b_im), ct(c_re), ct(c_im))


def s5_scan_coefficients(pow_r, pow_i):
    depth = pow_r.shape[0]
    row = jnp.arange(SUBLANES)[None, :, None]

    def coef(p):
        p = p.reshape(depth, S5_POWERS, 1, S5_NSTATE)
        full = [jnp.broadcast_to(p[:, e], (depth, SUBLANES, S5_NSTATE)) for e in (0, 1)]
        masked = [jnp.where(row >= s, p[:, e], 0.0) for e, s in ((1, 1), (2, 2), (3, 4))]
        return jnp.stack(full + masked, axis=1)

    return coef(pow_r), coef(pow_i)


MXU_COLS = 256
S5_COEF_TILES = 5


def _s5_kernel(u0_ref, u1_ref, u2_ref, u3_ref, bmat_ref, cmat_ref, cfr_ref, cfi_ref, d_ref, wglu_ref, o_ref,
               up_ref, uperm_ref, xs_ref, cin_ref, yp_ref, *, tl):
    n = S5_NSTATE
    tc = tl // SUBLANES
    pitch = tc + SUBLANES
    u_refs = (u0_ref, u1_ref, u2_ref, u3_ref)
    l = pl.program_id(1)

    @pl.when(l == 0)
    def _():
        cin_ref[...] = jnp.zeros(cin_ref.shape, F32)

    for j, u_ref in enumerate(u_refs):
        for c in range(SUBLANES):
            up_ref[j, c * pitch:c * pitch + tc, :] = u_ref[c * tc:(c + 1) * tc, :]

    def gather(t, carry):
        r = pl.multiple_of(t * SUBLANES, SUBLANES)
        for j in range(len(u_refs)):
            uperm_ref[pl.ds(r, SUBLANES), j * LANES:(j + 1) * LANES] = up_ref[j, pl.ds(t, SUBLANES, stride=pitch), :]
        return carry

    lax.fori_loop(0, tc, gather, 0)

    ub = uperm_ref[...].astype(BF16)
    blocks_per_part = n // MXU_COLS
    for jb in range(2 * blocks_per_part):
        k = (jb % blocks_per_part) * MXU_COLS // S5_STATE * S5_CH // LANES
        xs_ref[:, jb * MXU_COLS:(jb + 1) * MXU_COLS] = jnp.dot(
            ub[:, k * LANES:(k + 1) * LANES], bmat_ref[k * LANES:(k + 1) * LANES, jb * MXU_COLS:(jb + 1) * MXU_COLS],
            preferred_element_type=F32)

    row = lax.broadcasted_iota(jnp.int32, (SUBLANES, LANES), 0)

    def cmul_add(ar, ai, xr, xi, br, bi):
        return (ar * xr - ai * xi) + br, (ar * xi + ai * xr) + bi

    def scan_lanes(lb0):
        cols = [(slice(lb * LANES, (lb + 1) * LANES), slice(n + lb * LANES, n + (lb + 1) * LANES))
                for lb in range(lb0, lb0 + S5_SCAN_LANE_BLOCKS)]
        a = [(cfr_ref[0, :, cr_], cfi_ref[0, :, cr_]) for cr_, _ in cols]

        def recurrence(init, store):
            def step(t, x):
                base = pl.multiple_of(t * SUBLANES, SUBLANES)
                new = []
                for q, (cr_, ci_) in enumerate(cols):
                    nr, ni = cmul_add(a[q][0], a[q][1], x[2 * q], x[2 * q + 1],
                                      xs_ref[pl.ds(base, SUBLANES), cr_], xs_ref[pl.ds(base, SUBLANES), ci_])
                    if store:
                        xs_ref[pl.ds(base, SUBLANES), cr_] = nr
                        xs_ref[pl.ds(base, SUBLANES), ci_] = ni
                    new += [nr, ni]
                return tuple(new)
            return lax.fori_loop(0, tc, step, tuple(init), unroll=S5_SCAN_UNROLL)

        zero = jnp.zeros((SUBLANES, LANES), F32)
        ends = recurrence([zero] * (2 * len(cols)), store=False)

        starts = []
        for q, (cr_, ci_) in enumerate(cols):
            er, ei = ends[2 * q], ends[2 * q + 1]
            wr = jnp.where(row == 0, cin_ref[:, cr_], pltpu.roll(er, 1, 0))
            wi = jnp.where(row == 0, cin_ref[:, ci_], pltpu.roll(ei, 1, 0))
            for k, s in ((2, 1), (3, 2), (4, 4)):
                wr, wi = cmul_add(cfr_ref[k, :, cr_], cfi_ref[k, :, cr_],
                                  pltpu.roll(wr, s, 0), pltpu.roll(wi, s, 0), wr, wi)
            tr, ti = cmul_add(cfr_ref[1, :, cr_], cfi_ref[1, :, cr_], wr, wi, er, ei)
            cin_ref[:, cr_] = pltpu.roll(tr, 1, 0)
            cin_ref[:, ci_] = pltpu.roll(ti, 1, 0)
            starts += [wr, wi]
        recurrence(starts, store=True)

    for lb0 in range(0, n // LANES, S5_SCAN_LANE_BLOCKS):
        scan_lanes(lb0)

    ys = []
    for m in range(GROUP_W // MXU_COLS):
        k0 = m * MXU_COLS // S5_CH * S5_STATE
        k1 = (m + 1) * MXU_COLS // S5_CH * S5_STATE
        cols = slice(m * MXU_COLS, (m + 1) * MXU_COLS)
        ys.append(jnp.dot(xs_ref[:, k0:k1].astype(BF16), cmat_ref[k0:k1, cols], preferred_element_type=F32)
                  + jnp.dot(xs_ref[:, n + k0:n + k1].astype(BF16), cmat_ref[n + k0:n + k1, cols],
                            preferred_element_type=F32))
    y = jnp.concatenate(ys, axis=-1) + d_ref[...] * uperm_ref[...]
    g = jax.nn.gelu(y)
    gate = jnp.dot(g.astype(BF16), wglu_ref[...].astype(BF16), preferred_element_type=F32)
    out = g * jax.nn.sigmoid(gate)
    for j in range(len(u_refs)):
        yp_ref[j] = out[:, j * LANES:(j + 1) * LANES]

    tiles_per_chunk = tc // SUBLANES
    shift = tiles_per_chunk.bit_length() - 1

    def scatter(idx, carry):
        c = idx >> shift
        m = idx & (tiles_per_chunk - 1)
        r = pl.multiple_of(idx * SUBLANES, SUBLANES)
        src = m * (SUBLANES * SUBLANES) + c
        for j in range(len(u_refs)):
            o_ref[pl.ds(r, SUBLANES), j * LANES:(j + 1) * LANES] = yp_ref[j, pl.ds(src, SUBLANES, stride=SUBLANES), :]
        return carry

    lax.fori_loop(0, SUBLANES * tiles_per_chunk, scatter, 0)


def s5_mixer(proj, bmat, cmat, coef_r, coef_i, d_skip, w_glu, layer, bsz, seq, tl=S5_BLOCK):
    nl = seq // tl
    n2 = 2 * S5_NSTATE
    tc = tl // SUBLANES
    assert tc % SUBLANES == 0 and tc & (tc - 1) == 0
    nu = GROUP_W // LANES
    uspecs = [pl.BlockSpec((tl, LANES), lambda b, l, j=j: (b * nl + l, j)) for j in range(nu)]
    coef = pl.BlockSpec((None, S5_COEF_TILES, SUBLANES, S5_NSTATE), lambda b, l: (layer, 0, 0, 0))
    return pl.pallas_call(
        functools.partial(_s5_kernel, tl=tl),
        grid=(bsz, nl),
        in_specs=uspecs + [
            pl.BlockSpec((None, GROUP_W, n2), lambda b, l: (layer, 0, 0)),
            pl.BlockSpec((None, n2, GROUP_W), lambda b, l: (layer, 0, 0)),
            coef, coef,
            pl.BlockSpec((None, 1, GROUP_W), lambda b, l: (layer, 0, 0)),
            pl.BlockSpec((None, GROUP_W, GROUP_W), lambda b, l: (layer, 0, 0)),
        ],
        out_specs=pl.BlockSpec((tl, GROUP_W), lambda b, l: (b * nl + l, 0)),
        out_shape=jax.ShapeDtypeStruct((bsz * seq, GROUP_W), F32),
        scratch_shapes=[
            pltpu.VMEM((nu, SUBLANES * (tc + SUBLANES), LANES), F32),
            pltpu.VMEM((tl, GROUP_W), F32),
            pltpu.VMEM((tl, n2), F32),
            pltpu.VMEM((SUBLANES, n2), F32),
            pltpu.VMEM((nu, tl, LANES), F32),
        ],
        compiler_params=_params("parallel", "arbitrary"),
        name="s5_mixer",
    )(*([proj] * nu), bmat, cmat, coef_r, coef_i, d_skip, w_glu)


POOL_HALO = 16


def _pool_kernel(u_ref, pw_ref, sc_ref, o_ref, ext_ref, *, tl):
    l = pl.program_id(1)

    @pl.when(l == 0)
    def _():
        ext_ref[0:POOL_HALO, :] = jnp.zeros((POOL_HALO, GROUP_W), F32)

    @pl.when(l > 0)
    def _():
        ext_ref[0:POOL_HALO, :] = ext_ref[tl:tl + POOL_HALO, :]

    ext_ref[POOL_HALO:, :] = u_ref[...]
    t = l * tl + lax.broadcasted_iota(jnp.int32, (tl, 1), 0)
    for gi, w in enumerate(POOL_WINDOWS):
        cols = slice(gi * POOL_CH, (gi + 1) * POOL_CH)
        acc = ext_ref[pl.ds(POOL_HALO, tl), cols]
        for k in range(1, w):
            acc = acc + ext_ref[pl.ds(POOL_HALO - k, tl), cols]
        cnt = jnp.minimum(t + 1, w).astype(F32)
        p = acc / cnt - u_ref[:, cols]
        y = jnp.dot(p.astype(BF16), pw_ref[gi].astype(BF16), preferred_element_type=F32)
        o_ref[:, cols] = y * sc_ref[:, cols]


def pool_mixer(proj, pool_w, pool_scale, layer, bsz, seq, tl=512):
    nl = seq // tl
    ng = len(POOL_WINDOWS)
    return pl.pallas_call(
        functools.partial(_pool_kernel, tl=tl),
        grid=(bsz, nl),
        in_specs=[
            pl.BlockSpec((tl, GROUP_W), lambda b, l: (b * nl + l, 1)),
            pl.BlockSpec((None, ng, POOL_CH, POOL_CH), lambda b, l: (layer, 0, 0, 0)),
            pl.BlockSpec((None, 1, GROUP_W), lambda b, l: (layer, 0, 0)),
        ],
        out_specs=pl.BlockSpec((tl, GROUP_W), lambda b, l: (b * nl + l, 0)),
        out_shape=jax.ShapeDtypeStruct((bsz * seq, GROUP_W), F32),
        scratch_shapes=[pltpu.VMEM((POOL_HALO + tl, GROUP_W), F32)],
        compiler_params=_params("parallel", "arbitrary"),
        name="pool_mixer",
    )(proj, pool_w, pool_scale)


CONV_HALO = 32
CONV_ROWS = 32


def _conv_kernel(val_ref, gate_ref, wdw_ref, bdw_ref, lng_ref, lnb_ref, wpw_ref, o_ref,
                 sh_ref, cv_ref, wb_ref, *, tl):
    l = pl.program_id(1)
    n = tl + CONV_HALO

    @pl.when(l == 0)
    def _():
        sh_ref[0, 0:CONV_HALO, :] = jnp.zeros((CONV_HALO, GROUP_W), F32)

    @pl.when(l > 0)
    def _():
        sh_ref[0, 0:CONV_HALO, :] = sh_ref[0, tl:n, :]

    sh_ref[0, CONV_HALO:n, :] = val_ref[...] * jax.nn.sigmoid(gate_ref[...])
    sh_ref[0, n:, :] = jnp.zeros((SUBLANES, GROUP_W), F32)
    for s in range(1, SUBLANES):
        sh_ref[s, 0:n, :] = sh_ref[0, s:s + n, :]

    off = CONV_HALO - (CONV_WIDTH - 1)

    for k in range(CONV_WIDTH):
        wb_ref[k] = jnp.broadcast_to(wdw_ref[k:k + 1, :], (SUBLANES, GROUP_W))
    wb_ref[CONV_WIDTH] = jnp.broadcast_to(bdw_ref[...], (SUBLANES, GROUP_W))
    tiles = CONV_ROWS // SUBLANES

    def rows(c, carry):
        r0 = pl.multiple_of(c * CONV_ROWS, CONV_ROWS)
        acc = jnp.broadcast_to(wb_ref[CONV_WIDTH][None], (tiles, SUBLANES, GROUP_W))
        for k in range(CONV_WIDTH):
            q, s = divmod(off + k, SUBLANES)
            x = sh_ref[s, pl.ds(r0 + q * SUBLANES, CONV_ROWS), :].reshape(tiles, SUBLANES, GROUP_W)
            acc = acc + wb_ref[k][None] * x
        cv_ref[pl.ds(r0, CONV_ROWS), :] = acc.reshape(CONV_ROWS, GROUP_W)
        return carry

    lax.fori_loop(0, tl // CONV_ROWS, rows, 0)

    h = cv_ref[...]
    hc = h - jnp.mean(h, axis=-1, keepdims=True)
    y = hc * lax.rsqrt(jnp.mean(hc * hc, axis=-1, keepdims=True) + NORM_EPS)
    y = y * lng_ref[...] + lnb_ref[...]
    y = jax.nn.silu(y)
    o_ref[...] = jnp.dot(y.astype(BF16), wpw_ref[...].astype(BF16), preferred_element_type=F32)


def conv_mixer(proj, w_dw, b_dw, ln_g, ln_b, w_pw, layer, bsz, seq, tl=512):
    nl = seq // tl
    vec = pl.BlockSpec((None, 1, GROUP_W), lambda b, l: (layer, 0, 0))
    return pl.pallas_call(
        functools.partial(_conv_kernel, tl=tl),
        grid=(bsz, nl),
        in_specs=[
            pl.BlockSpec((tl, GROUP_W), lambda b, l: (b * nl + l, 2)),
            pl.BlockSpec((tl, GROUP_W), lambda b, l: (b * nl + l, 3)),
            pl.BlockSpec((None, CONV_WIDTH, GROUP_W), lambda b, l: (layer, 0, 0)),
            vec, vec, vec,
            pl.BlockSpec((None, GROUP_W, GROUP_W), lambda b, l: (layer, 0, 0)),
        ],
        out_specs=pl.BlockSpec((tl, GROUP_W), lambda b, l: (b * nl + l, 0)),
        out_shape=jax.ShapeDtypeStruct((bsz * seq, GROUP_W), F32),
        scratch_shapes=[pltpu.VMEM((SUBLANES, CONV_HALO + tl + SUBLANES, GROUP_W), F32),
                        pltpu.VMEM((tl, GROUP_W), F32),
                        pltpu.VMEM((CONV_WIDTH + 1, SUBLANES, GROUP_W), F32)],
        compiler_params=_params("parallel", "arbitrary"),
        name="conv_mixer",
    )(proj, proj, w_dw, b_dw, ln_g, ln_b, w_pw)


def _t5_bucket(dist):
    n = np.maximum(dist, 0)
    max_exact = REL_BUCKETS // 2
    large = max_exact + (np.log(np.maximum(n, 1) / max_exact) / np.log(REL_MAX_DIST / max_exact)
                         * (REL_BUCKETS - max_exact)).astype(np.int64)
    large = np.minimum(large, REL_BUCKETS - 1)
    return np.where(n < max_exact, n, large).astype(np.int32)


ATT_NEAR_BLOCKS = 5
ATT_NEAR = ATT_NEAR_BLOCKS * ATT_BLOCK
ATT_FAR_DIL = 16


def _attention_bias_layout(sub):
    a = np.arange(ATT_BLOCK)[:, None]
    d_near = (ATT_NEAR - ATT_BLOCK) + a - np.arange(ATT_NEAR)[None, :]
    mult = np.zeros(d_near.shape, np.int64)
    for window, dil in DILATED_PATTERNS:
        if dil != ATT_FAR_DIL:
            mult += (d_near >= 0) & (d_near % dil == 0) & (d_near // dil <= window // dil)
    d_far = ATT_FAR_DIL * (a - np.arange(sub)[None, :])
    mult_far = ((d_far >= 0) & (d_far // ATT_FAR_DIL <= ATT_BLOCK)).astype(np.int64)
    d = np.concatenate([d_near, d_far], axis=1)
    mult = np.concatenate([mult, mult_far], axis=1)
    addend = np.where(mult > 0, np.log(np.maximum(mult, 1)), NEG_INF).astype(np.float32)
    return _t5_bucket(d), addend


def _attn_bias_kernel(rb_ref, bucket_ref, add_ref, o_ref):
    h = pl.program_id(0)
    bucket = bucket_ref[...]
    t = jnp.zeros(bucket.shape, F32)
    for b in range(REL_BUCKETS):
        t = jnp.where(bucket == b, rb_ref[b, h], t)
    o_ref[...] = (t + add_ref[...]) * LOG2E


def attention_bias_table(rel_bias, seq):
    sub = seq // ATT_FAR_DIL
    bucket, addend = _attention_bias_layout(sub)
    width = ATT_NEAR + sub
    full = pl.BlockSpec((ATT_BLOCK, width), lambda h: (0, 0))
    return pl.pallas_call(
        _attn_bias_kernel,
        grid=(ATT_HEADS,),
        in_specs=[pl.BlockSpec(memory_space=pltpu.SMEM), full, full],
        out_specs=pl.BlockSpec((None, ATT_BLOCK, width), lambda h: (h, 0, 0)),
        out_shape=jax.ShapeDtypeStruct((ATT_HEADS, ATT_BLOCK, width), F32),
        compiler_params=_params("arbitrary"),
        name="attn_bias",
    )(rel_bias, jnp.asarray(bucket), jnp.asarray(addend))


ATT_HEADS_PER_BLOCK = LANES // ATT_HEAD_DIM
ATT_GROUP = 8


def _nt_dot(a, b):
    return lax.dot_general(a, b, (((1,), (1,)), ((), ())), preferred_element_type=F32)


def _attn_kernel(q_ref, k_ref, v_ref, bt_ref, *rest, seq, n_cast):
    cast_in, o_ref, cast_out = rest[:n_cast], rest[n_cast], rest[n_cast + 1:2 * n_cast + 1]
    kb_ref, vb_ref, m3_ref, l3_ref, a3_ref = rest[2 * n_cast + 1:]
    sub = seq // ATT_FAR_DIL
    pad = ATT_NEAR - ATT_BLOCK
    scale = ATT_HEAD_DIM ** -0.5 * LOG2E
    lane = lax.broadcasted_iota(jnp.int32, (1, LANES), 1)
    in_head = [(lane >= hh * ATT_HEAD_DIM) & (lane < (hh + 1) * ATT_HEAD_DIM)
               for hh in range(ATT_HEADS_PER_BLOCK)]

    def by_head(vals):
        out = vals[-1]
        for hh in range(ATT_HEADS_PER_BLOCK - 2, -1, -1):
            out = jnp.where(in_head[hh], vals[hh], out)
        return out

    def softmax_pv(tiles):
        heads = range(ATT_HEADS_PER_BLOCK)
        s = [[_nt_dot(jnp.where(in_head[hh], qs, 0.0).astype(BF16), kb) + bias_of(hh) for hh in heads]
             for qs, kb, _, bias_of in tiles]
        m = [[jnp.max(s_h, axis=-1, keepdims=True) for s_h in s_t] for s_t in s]
        p = [[jnp.exp2(s_h - m_h) for s_h, m_h in zip(s_t, m_t)] for s_t, m_t in zip(s, m)]
        l = [[jnp.sum(p_h, axis=-1, keepdims=True) for p_h in p_t] for p_t in p]
        acc = [[jnp.dot(p_h.astype(BF16), tile[2], preferred_element_type=F32) for p_h in p_t]
               for p_t, tile in zip(p, tiles)]
        out = []
        for tile, m_t, l_t, acc_t in zip(tiles, m, l, acc):
            full = (tile[0].shape[0], LANES)
            out.append((by_head([jnp.broadcast_to(x, full) for x in m_t]),
                        by_head([jnp.broadcast_to(x, full) for x in l_t]), by_head(acc_t)))
        return out

    for src, dst in zip(cast_in, cast_out):
        dst[...] = src[...].astype(BF16)

    kb_ref[...] = k_ref[...].astype(BF16)
    vb_ref[...] = v_ref[...].astype(BF16)

    for g0 in range(0, ATT_FAR_DIL, ATT_GROUP):
        group = [pl.ds(r, sub, stride=ATT_FAR_DIL) for r in range(g0, g0 + ATT_GROUP)]
        far_bias = lambda hh: bt_ref[hh, 0:sub, ATT_NEAR:ATT_NEAR + sub]
        tiles = [(q_ref[rows, :] * scale, k_ref[rows, :].astype(BF16), v_ref[rows, :].astype(BF16), far_bias)
                 for rows in group]
        for rows, (m, l, acc) in zip(group, softmax_pv(tiles)):
            m3_ref[rows, :] = m
            l3_ref[rows, :] = l
            a3_ref[rows, :] = acc

    nblocks = seq // ATT_BLOCK
    for g0 in range(0, nblocks, ATT_GROUP):
        group, tiles = [], []
        for i in range(g0, min(g0 + ATT_GROUP, nblocks)):
            r0 = i * ATT_BLOCK
            lo = max(0, r0 - pad)
            c0 = ATT_NEAR - (r0 + ATT_BLOCK - lo)
            group.append(slice(r0, r0 + ATT_BLOCK))
            tiles.append((q_ref[group[-1], :] * scale, kb_ref[lo:r0 + ATT_BLOCK, :], vb_ref[lo:r0 + ATT_BLOCK, :],
                          lambda hh, c0=c0: bt_ref[hh, :, c0:ATT_NEAR]))
        merged = []
        for rows, (m, l, acc) in zip(group, softmax_pv(tiles)):
            m3 = m3_ref[rows, :]
            mm = jnp.maximum(m, m3)
            wn = jnp.exp2(m - mm)
            wf = jnp.exp2(m3 - mm)
            num = acc * wn + a3_ref[rows, :] * wf
            den = l * wn + l3_ref[rows, :] * wf
            merged.append(num / den)
        for rows, o in zip(group, merged):
            o_ref[rows, :] = o


def dilated_attention(proj, bias_table, casts, bsz, seq):
    assert seq % ATT_BLOCK == 0 and seq // ATT_FAR_DIL <= ATT_BLOCK
    nhp = ATT_HEADS // ATT_HEADS_PER_BLOCK
    steps = bsz * nhp
    qcol = 4 * GROUP_W // LANES
    kcol = 5 * GROUP_W // LANES
    vcol = 6 * GROUP_W // LANES
    width = bias_table.shape[-1]
    stat = pltpu.VMEM((seq, LANES), F32)
    half = pltpu.VMEM((seq, LANES), BF16)
    cast_in, cast_out, cast_shapes = [], [], []
    for w, layer in casts:
        rows, cols = w.shape[1:]
        assert rows % steps == 0
        cast_in.append(pl.BlockSpec((None, rows // steps, cols), lambda b, h, layer=layer: (layer, b * nhp + h, 0)))
        cast_out.append(pl.BlockSpec((rows // steps, cols), lambda b, h: (b * nhp + h, 0)))
        cast_shapes.append(jax.ShapeDtypeStruct((rows, cols), BF16))
    return pl.pallas_call(
        functools.partial(_attn_kernel, seq=seq, n_cast=len(casts)),
        grid=(bsz, nhp),
        in_specs=[
            pl.BlockSpec((seq, LANES), lambda b, h: (b, qcol + h)),
            pl.BlockSpec((seq, LANES), lambda b, h: (b, kcol + h)),
            pl.BlockSpec((seq, LANES), lambda b, h: (b, vcol + h)),
            pl.BlockSpec((ATT_HEADS_PER_BLOCK, ATT_BLOCK, width), lambda b, h: (h, 0, 0)),
        ] + cast_in,
        out_specs=[pl.BlockSpec((seq, LANES), lambda b, h: (b, h))] + cast_out,
        out_shape=[jax.ShapeDtypeStruct((bsz * seq, GROUP_W), F32)] + cast_shapes,
        scratch_shapes=[half, half, stat, stat, stat],
        compiler_params=_params("parallel", "parallel"),
        name="dilated_attention",
    )(proj, proj, proj, bias_table, *[w for w, _ in casts])


def _mix_out_kernel(ya_ref, yb_ref, yc_ref, yd_ref, g_ref, w_ref, h_ref, o_ref, wb_ref):
    @pl.when(pl.program_id(0) == 0)
    def _():
        wb_ref[...] = w_ref[...].astype(BF16)

    yn = [_rms(y_ref[...], g_ref[:, gi * GROUP_W:(gi + 1) * GROUP_W]).astype(BF16)
          for gi, y_ref in enumerate((ya_ref, yb_ref, yc_ref, yd_ref))]
    o_ref[...] = h_ref[...] + jnp.dot(jnp.concatenate(yn, axis=-1), wb_ref[...], preferred_element_type=F32)


def mix_out(ya, yb, yc, yd, g, w, h, layer, tm=512):
    m, d = h.shape
    yspec = pl.BlockSpec((tm, GROUP_W), lambda i: (i, 0))
    return pl.pallas_call(
        _mix_out_kernel,
        grid=(m // tm,),
        in_specs=[
            yspec, yspec, yspec, yspec,
            pl.BlockSpec((None, 1, d), lambda i: (layer, 0, 0)),
            pl.BlockSpec((None, d, d), lambda i: (layer, 0, 0), pipeline_mode=pl.Buffered(1)),
            pl.BlockSpec((tm, d), lambda i: (i, 0)),
        ],
        out_specs=pl.BlockSpec((tm, d), lambda i: (i, 0)),
        out_shape=jax.ShapeDtypeStruct((m, d), F32),
        scratch_shapes=[pltpu.VMEM((d, d), BF16)],
        compiler_params=_params("arbitrary"),
        name="mix_out",
    )(ya, yb, yc, yd, g, w, h)


def _mem_kv_kernel(mem_ref, g_ref, wk_ref, wv_ref, k_ref, v_ref):
    mn = _rms(mem_ref[...], g_ref[...]).astype(BF16)
    k_ref[...] = jnp.dot(mn, wk_ref[...].astype(BF16), preferred_element_type=F32).astype(BF16)
    v_ref[...] = jnp.dot(mn, wv_ref[...].astype(BF16), preferred_element_type=F32).astype(BF16)


def mem_kv(mem, g, wk, wv):
    m, d = mem.shape
    depth = wk.shape[0]
    wspec = pl.BlockSpec((None, d, X_WIDTH), lambda l: (l, 0, 0))
    ospec = pl.BlockSpec((None, m, X_WIDTH), lambda l: (l, 0, 0))
    return pl.pallas_call(
        _mem_kv_kernel,
        grid=(depth,),
        in_specs=[pl.BlockSpec((m, d), lambda l: (0, 0)), pl.BlockSpec((1, d), lambda l: (0, 0)),
                  wspec, wspec],
        out_specs=[ospec, ospec],
        out_shape=[jax.ShapeDtypeStruct((depth, m, X_WIDTH), BF16)] * 2,
        compiler_params=_params("arbitrary"),
        name="mem_kv",
    )(mem, g, wk, wv)


def _xattn_kernel(h_ref, g_ref, wq_ref, k_ref, v_ref, wo_ref, o_ref, wqb_ref, wob_ref):
    @pl.when((pl.program_id(0) == 0) & (pl.program_id(1) == 0))
    def _():
        wqb_ref[...] = wq_ref[...].astype(BF16)
        wob_ref[...] = wo_ref[...].astype(BF16)

    h = h_ref[...]
    hn = _rms(h, g_ref[...]).astype(BF16)
    q = jnp.dot(hn, wqb_ref[...], preferred_element_type=F32) * (X_HEAD_DIM ** -0.5 * LOG2E)
    cols = [slice(hd * X_HEAD_DIM, (hd + 1) * X_HEAD_DIM) for hd in range(X_HEADS)]
    qb = q.astype(BF16)
    s = [_nt_dot(qb[:, c], k_ref[:, c]) for c in cols]
    m = [jnp.max(x, axis=-1, keepdims=True) for x in s]
    p = [jnp.exp2(x - mx) for x, mx in zip(s, m)]
    den = [jnp.sum(x, axis=-1, keepdims=True) for x in p]
    p = [x / dx for x, dx in zip(p, den)]
    outs = [jnp.dot(x.astype(BF16), v_ref[:, c], preferred_element_type=F32) for x, c in zip(p, cols)]
    o = jnp.concatenate(outs, axis=-1).astype(BF16)
    o_ref[...] = h + jnp.dot(o, wob_ref[...], preferred_element_type=F32)


def cross_attention(h, g, wq, k, v, wo, layer, bsz, seq, tm=512):
    d = h.shape[-1]
    nl = seq // tm
    mlen = k.shape[1] // bsz
    return pl.pallas_call(
        _xattn_kernel,
        grid=(bsz, nl),
        in_specs=[
            pl.BlockSpec((tm, d), lambda b, l: (b * nl + l, 0)),
            pl.BlockSpec((None, 1, d), lambda b, l: (layer, 0, 0)),
            pl.BlockSpec((None, d, X_WIDTH), lambda b, l: (layer, 0, 0)),
            pl.BlockSpec((None, mlen, X_WIDTH), lambda b, l: (layer, b, 0)),
            pl.BlockSpec((None, mlen, X_WIDTH), lambda b, l: (layer, b, 0)),
            pl.BlockSpec((None, X_WIDTH, d), lambda b, l: (layer, 0, 0)),
        ],
        out_specs=pl.BlockSpec((tm, d), lambda b, l: (b * nl + l, 0)),
        out_shape=jax.ShapeDtypeStruct(h.shape, F32),
        scratch_shapes=[pltpu.VMEM((d, X_WIDTH), BF16), pltpu.VMEM((X_WIDTH, d), BF16)],
        compiler_params=_params("arbitrary", "arbitrary"),
        name="cross_attention",
    )(h, g, wq, k, v, wo)


def _mlp_kernel(h_ref, g_ref, wu_ref, wd_ref, *rest, out_norm):
    o_ref, hn_ref = rest[-2:]
    j = pl.program_id(1)

    @pl.when(j == 0)
    def _():
        h = h_ref[...]
        hn_ref[...] = _rms(h, g_ref[...]).astype(BF16)
        o_ref[...] = h

    a = jnp.dot(hn_ref[...], wu_ref[...], preferred_element_type=F32)
    a = jnp.square(jnp.maximum(a, 0.0)).astype(BF16)
    o_ref[...] += jnp.dot(a, wd_ref[...], preferred_element_type=F32)

    if out_norm:
        @pl.when(j == pl.num_programs(1) - 1)
        def _():
            o_ref[...] = _rms(o_ref[...], rest[0][...])


def mlp(h, g, wu, wd, layer, out_norm_g=None, tm=1024, tf=512):
    m, d = h.shape
    f = wu.shape[-1]
    out_norm = out_norm_g is not None
    extra_specs = [pl.BlockSpec((1, d), lambda i, j: (0, 0))] if out_norm else []
    extra_args = (out_norm_g,) if out_norm else ()
    return pl.pallas_call(
        functools.partial(_mlp_kernel, out_norm=out_norm),
        grid=(m // tm, f // tf),
        in_specs=[
            pl.BlockSpec((tm, d), lambda i, j: (i, 0)),
            pl.BlockSpec((None, 1, d), lambda i, j: (layer, 0, 0)),
            pl.BlockSpec((d, tf), lambda i, j: (0, j)),
            pl.BlockSpec((tf, d), lambda i, j: (j, 0)),
        ] + extra_specs,
        out_specs=pl.BlockSpec((tm, d), lambda i, j: (i, 0)),
        out_shape=jax.ShapeDtypeStruct((m, d), F32),
        scratch_shapes=[pltpu.VMEM((tm, d), BF16)],
        compiler_params=_params("parallel", "arbitrary"),
        name="mlp",
    )(h, g, wu, wd, *extra_args)


def kernel(x, mem, rel_bias, mem_norm_g, norm_mix_g, w_in, s5_lam_re, s5_lam_im, s5_log_dt, s5_b_re, s5_b_im, s5_c_re, s5_c_im, s5_d, s5_w_glu, pool_w, pool_scale, conv_w_dw, conv_b_dw, conv_ln_g, conv_ln_b, conv_w_pw, grp_norm_g, w_out, norm_x_g, w_xq, w_xk, w_xv, w_xo, norm_mlp_g, w_up, w_down, norm_final_g):
    bsz, seq, d = x.shape
    depth = w_in.shape[0]
    vec = lambda a: a[:, None, :]

    k_mem, v_mem = mem_kv(mem.reshape(bsz * mem.shape[1], d), mem_norm_g[None, :], w_xk, w_xv)
    pow_r, pow_i, bmat, cmat = s5_params(s5_lam_re, s5_lam_im, s5_log_dt, s5_b_re, s5_b_im, s5_c_re, s5_c_im,
                                         chunk=S5_BLOCK // SUBLANES)
    coef_r, coef_i = s5_scan_coefficients(pow_r, pow_i)
    bias_table = attention_bias_table(rel_bias, seq)

    norm_mix, s5_dv, pool_sc = vec(norm_mix_g), vec(s5_d), vec(pool_scale)
    b_dw, ln_g, ln_b = vec(conv_b_dw), vec(conv_ln_g), vec(conv_ln_b)
    grp_g, norm_x, norm_mlp = vec(grp_norm_g), vec(norm_x_g), vec(norm_mlp_g)

    h = x.reshape(bsz * seq, d)
    w_in_b = w_in[0].astype(BF16)
    for l in range(depth):
        proj = norm_matmul(h, norm_mix, w_in_b, l)
        y_a = s5_mixer(proj, bmat, cmat, coef_r, coef_i, s5_dv, s5_w_glu, l, bsz, seq)
        y_b = pool_mixer(proj, pool_w, pool_sc, l, bsz, seq)
        y_c = conv_mixer(proj, conv_w_dw, b_dw, ln_g, ln_b, conv_w_pw, l, bsz, seq)
        casts = [(w_up, l), (w_down, l)] + ([(w_in, l + 1)] if l + 1 < depth else [])
        y_d, w_up_b, w_down_b, *nxt = dilated_attention(proj, bias_table, casts, bsz, seq)
        w_in_b = nxt[0] if nxt else None
        h = mix_out(y_a, y_b, y_c, y_d, grp_g, w_out, h, l)
        h = cross_attention(h, norm_x, w_xq, k_mem, v_mem, w_xo, l, bsz, seq)
        h = mlp(h, norm_mlp, w_up_b, w_down_b, l,
                out_norm_g=norm_final_g[None, :] if l == depth - 1 else None)
    return h.reshape(bsz, seq, d)
```

```python
import functools
import math

import numpy as np
import jax
import jax.numpy as jnp
from jax import lax
from jax.experimental import pallas as pl
from jax.experimental.pallas import tpu as pltpu

F32 = jnp.float32
BF16 = jnp.bfloat16

D_MODEL = 2048
DEPTH = 4
GROUP_W = 512
S5_GROUPS = 32
S5_CH = 16
S5_STATE = 64
S5_NSTATE = S5_GROUPS * S5_STATE
S5_SCAN_LANE_BLOCKS = 8
S5_BLOCK = 512
S5_SCAN_UNROLL = 4
POOL_WINDOWS = (2, 4, 8, 16)
POOL_CH = 128
CONV_WIDTH = 31
ATT_HEADS = 8
ATT_HEAD_DIM = 64
DILATED_PATTERNS = ((128, 1), (512, 4), (2048, 16))
ATT_BLOCK = 128
REL_BUCKETS = 32
REL_MAX_DIST = 2048
MEM_LEN = 256
X_HEADS = 4
X_HEAD_DIM = 128
X_WIDTH = 512
D_FF = 4 * D_MODEL
NORM_EPS = 1e-6
NEG_INF = -1e30
LOG2E = math.log2(math.e)
IN_WIDTH = 7 * GROUP_W

SUBLANES = 8
LANES = 128
VMEM_LIMIT = 58 * 1024 * 1024


def _params(*sem):
    return pltpu.CompilerParams(dimension_semantics=sem, vmem_limit_bytes=VMEM_LIMIT)


def _rms(x, g):
    return x * lax.rsqrt(jnp.mean(x * x, axis=-1, keepdims=True) + NORM_EPS) * g


def _norm_matmul_kernel(x_ref, g_ref, w_ref, o_ref):
    xn = _rms(x_ref[...], g_ref[...]).astype(BF16)
    o_ref[...] = jnp.dot(xn, w_ref[...], preferred_element_type=F32)


def norm_matmul(x, g, w, layer, tm=512):
    m, d = x.shape
    n = w.shape[-1]
    return pl.pallas_call(
        _norm_matmul_kernel,
        grid=(m // tm,),
        in_specs=[
            pl.BlockSpec((tm, d), lambda i: (i, 0)),
            pl.BlockSpec((None, 1, d), lambda i: (layer, 0, 0)),
            pl.BlockSpec((d, n), lambda i: (0, 0), pipeline_mode=pl.Buffered(1)),
        ],
        out_specs=pl.BlockSpec((tm, n), lambda i: (i, 0)),
        out_shape=jax.ShapeDtypeStruct((m, n), F32),
        compiler_params=_params("parallel"),
        name="norm_matmul",
    )(x, g, w)


def _proj_conv_kernel(x_ref, g_ref, w_ref, wdw_ref, bdw_ref, lng_ref, lnb_ref, wpw_ref, pw_ref, psc_ref,
                      proj_ref, yc_ref, yb_ref, sh_ref, cv_ref, wb_ref, hnext_ref, xn_ref, ext_ref, unext_ref,
                      *, tl, nl):
    step = pl.program_id(0)
    n = tl + CONV_HALO

    @pl.when(step == 0)
    def _():
        hnext_ref[...] = jnp.zeros(hnext_ref.shape, F32)
        unext_ref[...] = jnp.zeros(unext_ref.shape, F32)

    opens = (step == 0) | (lax.rem(step - 1, nl) == 0)

    @pl.when(opens)
    def _():
        sh_ref[0, 0:CONV_HALO, :] = jnp.zeros((CONV_HALO, GROUP_W), F32)
        ext_ref[0:POOL_HALO, :] = jnp.zeros((POOL_HALO, GROUP_W), F32)

    @pl.when(jnp.logical_not(opens))
    def _():
        sh_ref[0, 0:CONV_HALO, :] = sh_ref[0, tl:n, :]
        ext_ref[0:POOL_HALO, :] = ext_ref[tl:tl + POOL_HALO, :]

    ext_ref[POOL_HALO:, :] = unext_ref[...]

    sh_ref[0, CONV_HALO:n, :] = hnext_ref[...]
    sh_ref[0, n:, :] = jnp.zeros((SUBLANES, GROUP_W), F32)
    for s in range(1, SUBLANES):
        sh_ref[s, 0:n, :] = sh_ref[0, s:s + n, :]

    for k in range(CONV_WIDTH):
        wb_ref[k] = jnp.broadcast_to(wdw_ref[k:k + 1, :], (SUBLANES, GROUP_W))
    wb_ref[CONV_WIDTH] = jnp.broadcast_to(bdw_ref[...], (SUBLANES, GROUP_W))

    xn_ref[...] = _rms(x_ref[...], g_ref[...]).astype(BF16)
    slice_cols = 2 * MXU_COLS
    n_slices = proj_ref.shape[-1] // slice_cols
    n_chunks = tl // CONV_ROWS
    per_slice = n_chunks // n_slices
    tiles = CONV_ROWS // SUBLANES
    off = CONV_HALO - (CONV_WIDTH - 1)

    def conv_chunk(c):
        r0 = c * CONV_ROWS
        acc = jnp.broadcast_to(wb_ref[CONV_WIDTH][None], (tiles, SUBLANES, GROUP_W))
        for k in range(CONV_WIDTH):
            q, s = divmod(off + k, SUBLANES)
            x = sh_ref[s, r0 + q * SUBLANES:r0 + q * SUBLANES + CONV_ROWS, :]
            acc = acc + wb_ref[k][None] * x.reshape(tiles, SUBLANES, GROUP_W)
        cv_ref[r0:r0 + CONV_ROWS, :] = acc.reshape(CONV_ROWS, GROUP_W)

    def pool_tile():
        t = lax.rem(jnp.maximum(step - 1, 0), nl) * tl + lax.broadcasted_iota(jnp.int32, (tl, 1), 0)
        for gi, w in enumerate(POOL_WINDOWS):
            pc = slice(gi * POOL_CH, (gi + 1) * POOL_CH)
            acc = ext_ref[POOL_HALO:POOL_HALO + tl, pc]
            for k in range(1, w):
                acc = acc + ext_ref[POOL_HALO - k:POOL_HALO - k + tl, pc]
            p = acc / jnp.minimum(t + 1, w).astype(F32) - ext_ref[POOL_HALO:POOL_HALO + tl, pc]
            y = jnp.dot(p.astype(BF16), pw_ref[gi].astype(BF16), preferred_element_type=F32)
            yb_ref[:, pc] = y * psc_ref[:, pc]

    for j in range(n_slices):
        cols = slice(j * slice_cols, (j + 1) * slice_cols)
        proj_ref[:, cols] = jnp.dot(xn_ref[...], w_ref[:, cols], preferred_element_type=F32)
        if j == 0:
            pool_tile()
        for c in range(j * per_slice, (j + 1) * per_slice):
            conv_chunk(c)
        if per_slice:
            r0 = ((j + 1) * per_slice - 1) * CONV_ROWS
            zero = pltpu.bitcast(lax.shift_right_logical(
                pltpu.bitcast(cv_ref[r0:r0 + 2 * SUBLANES, 0:LANES], jnp.uint32), jnp.uint32(32)), F32)
            xn_ref[0:2 * SUBLANES, 0:LANES] = xn_ref[0:2 * SUBLANES, 0:LANES] + zero.astype(BF16)
    for c in range(n_slices * per_slice, n_chunks):
        conv_chunk(c)

    h = cv_ref[...]
    hc = h - jnp.mean(h, axis=-1, keepdims=True)
    y = hc * lax.rsqrt(jnp.mean(hc * hc, axis=-1, keepdims=True) + NORM_EPS)
    y = y * lng_ref[...] + lnb_ref[...]
    y = jax.nn.silu(y)
    yc_ref[...] = jnp.dot(y.astype(BF16), wpw_ref[...].astype(BF16), preferred_element_type=F32)

    unext_ref[...] = proj_ref[:, GROUP_W:2 * GROUP_W]
    hnext_ref[...] = proj_ref[:, 2 * GROUP_W:3 * GROUP_W] * jax.nn.sigmoid(proj_ref[:, 3 * GROUP_W:4 * GROUP_W])


def proj_conv(x, g, w, w_dw, b_dw, ln_g, ln_b, w_pw, pool_w, pool_scale, layer, bsz, seq, tl=512):
    m, d = x.shape
    n = w.shape[-1]
    nl = seq // tl
    nt = m // tl
    vec = pl.BlockSpec((None, 1, GROUP_W), lambda s: (layer, 0, 0))
    cur = lambda s: (jnp.minimum(s, nt - 1), 0)
    prev = lambda s: (jnp.maximum(s - 1, 0), 0)
    return pl.pallas_call(
        functools.partial(_proj_conv_kernel, tl=tl, nl=nl),
        grid=(nt + 1,),
        in_specs=[
            pl.BlockSpec((tl, d), cur),
            pl.BlockSpec((None, 1, d), lambda s: (layer, 0, 0)),
            pl.BlockSpec((d, n), lambda s: (0, 0), pipeline_mode=pl.Buffered(1)),
            pl.BlockSpec((None, CONV_WIDTH, GROUP_W), lambda s: (layer, 0, 0)),
            vec, vec, vec,
            pl.BlockSpec((None, GROUP_W, GROUP_W), lambda s: (layer, 0, 0)),
            pl.BlockSpec((None, len(POOL_WINDOWS), POOL_CH, POOL_CH), lambda s: (layer, 0, 0, 0)),
            vec,
        ],
        out_specs=[pl.BlockSpec((tl, n), cur), pl.BlockSpec((tl, GROUP_W), prev),
                   pl.BlockSpec((tl, GROUP_W), prev)],
        out_shape=[jax.ShapeDtypeStruct((m, n), F32), jax.ShapeDtypeStruct((m, GROUP_W), F32),
                   jax.ShapeDtypeStruct((m, GROUP_W), F32)],
        scratch_shapes=[pltpu.VMEM((SUBLANES, CONV_HALO + tl + SUBLANES, GROUP_W), F32),
                        pltpu.VMEM((tl, GROUP_W), F32),
                        pltpu.VMEM((CONV_WIDTH + 1, SUBLANES, GROUP_W), F32),
                        pltpu.VMEM((tl, GROUP_W), F32),
                        pltpu.VMEM((tl, d), BF16),
                        pltpu.VMEM((POOL_HALO + tl, GROUP_W), F32),
                        pltpu.VMEM((tl, GROUP_W), F32)],
        compiler_params=_params("arbitrary"),
        name="proj_conv",
    )(x, g, w, w_dw, b_dw, ln_g, ln_b, w_pw, pool_w, pool_scale)


def _s5_param_kernel(lr_ref, li_ref, ldt_ref, lrx_ref, lix_ref, ldtx_ref, br_ref, bi_ref, cr_ref, ci_ref,
                     pr_ref, pi_ref, bmat_ref, cmat_ref, *, chunk):
    def discretise(lr, li, ldt):
        dt = jnp.exp(ldt)
        mag = jnp.exp(lr * dt)
        return mag * jnp.cos(li * dt), mag * jnp.sin(li * dt)

    cr, ci = discretise(lr_ref[...], li_ref[...], ldt_ref[...])
    slot = {1: 0, chunk: 1, 2 * chunk: 2, 4 * chunk: 3}
    e = 1
    while True:
        if e in slot:
            pr_ref[slot[e]] = cr
            pi_ref[slot[e]] = ci
        if e == 4 * chunk:
            break
        cr, ci = cr * cr - ci * ci, 2.0 * (cr * ci)
        e *= 2

    lr, li = lrx_ref[...], lix_ref[...]
    ar, ai = discretise(lr, li, ldtx_ref[...])
    den = lr * lr + li * li
    nr, ni = ar - 1.0, ai
    f_r = (nr * lr + ni * li) / den
    f_i = (ni * lr - nr * li) / den
    b_r, b_i = br_ref[...], bi_ref[...]
    bb_r = f_r * b_r - f_i * b_i
    bb_i = f_r * b_i + f_i * b_r

    def block_diagonal(x, row_shift, col_shift):
        rows, w = x.shape
        cols = (rows >> row_shift) << col_shift
        sel = ((lax.broadcasted_iota(jnp.int32, (w, cols), 1) & (w - 1))
               == lax.broadcasted_iota(jnp.int32, (w, cols), 0))
        tiled = jnp.dot(x.astype(BF16), jnp.where(sel, 1.0, 0.0).astype(BF16), preferred_element_type=F32)
        same = ((lax.broadcasted_iota(jnp.int32, (rows, cols), 0) >> row_shift)
                == (lax.broadcasted_iota(jnp.int32, (rows, cols), 1) >> col_shift))
        return jnp.where(same, tiled, 0.0).astype(BF16)

    ch_shift, st_shift = S5_CH.bit_length() - 1, S5_STATE.bit_length() - 1
    bmat_ref[:, 0:S5_NSTATE] = block_diagonal(bb_r, ch_shift, st_shift)
    bmat_ref[:, S5_NSTATE:] = block_diagonal(bb_i, ch_shift, st_shift)
    cmat_ref[0:S5_NSTATE, :] = block_diagonal(cr_ref[...], st_shift, ch_shift)
    cmat_ref[S5_NSTATE:, :] = block_diagonal(-ci_ref[...], st_shift, ch_shift)


S5_POWERS = 4


def s5_params(lam_re, lam_im, log_dt, b_re, b_im, c_re, c_im, chunk):
    depth = lam_re.shape[0]
    gc = S5_GROUPS * S5_CH
    rep = lambda a: jnp.repeat(a, S5_CH, axis=1)
    ldt = log_dt[..., None]
    bt = lambda b: jnp.transpose(b, (0, 1, 3, 2)).reshape(depth, gc, S5_STATE)
    ct = lambda c: jnp.transpose(c, (0, 1, 3, 2)).reshape(depth, S5_NSTATE, S5_CH)
    small = pl.BlockSpec((None, S5_GROUPS, S5_STATE), lambda l: (l, 0, 0))
    small1 = pl.BlockSpec((None, S5_GROUPS, 1), lambda l: (l, 0, 0))
    big = pl.BlockSpec((None, gc, S5_STATE), lambda l: (l, 0, 0))
    big1 = pl.BlockSpec((None, gc, 1), lambda l: (l, 0, 0))
    cspec = pl.BlockSpec((None, S5_NSTATE, S5_CH), lambda l: (l, 0, 0))
    powspec = pl.BlockSpec((None, S5_POWERS, S5_GROUPS, S5_STATE), lambda l: (l, 0, 0, 0))
    return pl.pallas_call(
        functools.partial(_s5_param_kernel, chunk=chunk),
        grid=(depth,),
        in_specs=[small, small, small1, big, big, big1, big, big, cspec, cspec],
        out_specs=[powspec, powspec,
                   pl.BlockSpec((None, gc, 2 * S5_NSTATE), lambda l: (l, 0, 0)),
                   pl.BlockSpec((None, 2 * S5_NSTATE, gc), lambda l: (l, 0, 0))],
        out_shape=[
            jax.ShapeDtypeStruct((depth, S5_POWERS, S5_GROUPS, S5_STATE), F32),
            jax.ShapeDtypeStruct((depth, S5_POWERS, S5_GROUPS, S5_STATE), F32),
            jax.ShapeDtypeStruct((depth, gc, 2 * S5_NSTATE), BF16),
            jax.ShapeDtypeStruct((depth, 2 * S5_NSTATE, gc), BF16),
        ],
        compiler_params=_params("arbitrary"),
        name="s5_params",
    )(lam_re, lam_im, ldt, rep(lam_re), rep(lam_im), rep(ldt), bt(b_re), bt(b_im), ct(c_re), ct(c_im))


def s5_scan_coefficients(pow_r, pow_i):
    depth = pow_r.shape[0]
    row = jnp.arange(SUBLANES)[None, :, None]

    def coef(p):
        p = p.reshape(depth, S5_POWERS, 1, S5_NSTATE)
        full = [jnp.broadcast_to(p[:, e], (depth, SUBLANES, S5_NSTATE)) for e in (0, 1)]
        masked = [jnp.where(row >= s, p[:, e], 0.0) for e, s in ((1, 1), (2, 2), (3, 4))]
        return jnp.stack(full + masked, axis=1)

    return coef(pow_r), coef(pow_i)


MXU_COLS = 256
S5_COEF_TILES = 5


def _s5_kernel(u0_ref, u1_ref, u2_ref, u3_ref, bmat_ref, cmat_ref, cfr_ref, cfi_ref, d_ref, wglu_ref, o_ref,
               up_ref, uperm_ref, xs_ref, cin_ref, yp_ref, *, tl):
    n = S5_NSTATE
    tc = tl // SUBLANES
    pitch = tc + SUBLANES
    u_refs = (u0_ref, u1_ref, u2_ref, u3_ref)
    l = pl.program_id(1)

    @pl.when(l == 0)
    def _():
        cin_ref[...] = jnp.zeros(cin_ref.shape, F32)

    for j, u_ref in enumerate(u_refs):
        for c in range(SUBLANES):
            up_ref[j, c * pitch:c * pitch + tc, :] = u_ref[c * tc:(c + 1) * tc, :]

    def gather(t, carry):
        r = pl.multiple_of(t * SUBLANES, SUBLANES)
        for j in range(len(u_refs)):
            uperm_ref[pl.ds(r, SUBLANES), j * LANES:(j + 1) * LANES] = up_ref[j, pl.ds(t, SUBLANES, stride=pitch), :]
        return carry

    lax.fori_loop(0, tc, gather, 0)

    ub = uperm_ref[...].astype(BF16)
    blocks_per_part = n // MXU_COLS
    for jb in range(2 * blocks_per_part):
        k = (jb % blocks_per_part) * MXU_COLS // S5_STATE * S5_CH // LANES
        xs_ref[:, jb * MXU_COLS:(jb + 1) * MXU_COLS] = jnp.dot(
            ub[:, k * LANES:(k + 1) * LANES], bmat_ref[k * LANES:(k + 1) * LANES, jb * MXU_COLS:(jb + 1) * MXU_COLS],
            preferred_element_type=F32)

    row = lax.broadcasted_iota(jnp.int32, (SUBLANES, LANES), 0)

    def cmul_add(ar, ai, xr, xi, br, bi):
        return (ar * xr - ai * xi) + br, (ar * xi + ai * xr) + bi

    def scan_lanes(lb0):
        cols = [(slice(lb * LANES, (lb + 1) * LANES), slice(n + lb * LANES, n + (lb + 1) * LANES))
                for lb in range(lb0, lb0 + S5_SCAN_LANE_BLOCKS)]
        a = [(cfr_ref[0, :, cr_], cfi_ref[0, :, cr_]) for cr_, _ in cols]

        def recurrence(init, store):
            def step(t, x):
                base = pl.multiple_of(t * SUBLANES, SUBLANES)
                new = []
                for q, (cr_, ci_) in enumerate(cols):
                    nr, ni = cmul_add(a[q][0], a[q][1], x[2 * q], x[2 * q + 1],
                                      xs_ref[pl.ds(base, SUBLANES), cr_], xs_ref[pl.ds(base, SUBLANES), ci_])
                    if store:
                        xs_ref[pl.ds(base, SUBLANES), cr_] = nr
                        xs_ref[pl.ds(base, SUBLANES), ci_] = ni
                    new += [nr, ni]
                return tuple(new)
            return lax.fori_loop(0, tc, step, tuple(init), unroll=S5_SCAN_UNROLL)

        zero = jnp.zeros((SUBLANES, LANES), F32)
        ends = recurrence([zero] * (2 * len(cols)), store=False)

        starts = []
        for q, (cr_, ci_) in enumerate(cols):
            er, ei = ends[2 * q], ends[2 * q + 1]
            wr = jnp.where(row == 0, cin_ref[:, cr_], pltpu.roll(er, 1, 0))
            wi = jnp.where(row == 0, cin_ref[:, ci_], pltpu.roll(ei, 1, 0))
            for k, s in ((2, 1), (3, 2), (4, 4)):
                wr, wi = cmul_add(cfr_ref[k, :, cr_], cfi_ref[k, :, cr_],
                                  pltpu.roll(wr, s, 0), pltpu.roll(wi, s, 0), wr, wi)
            tr, ti = cmul_add(cfr_ref[1, :, cr_], cfi_ref[1, :, cr_], wr, wi, er, ei)
            cin_ref[:, cr_] = pltpu.roll(tr, 1, 0)
            cin_ref[:, ci_] = pltpu.roll(ti, 1, 0)
            starts += [wr, wi]
        recurrence(starts, store=True)

    for lb0 in range(0, n // LANES, S5_SCAN_LANE_BLOCKS):
        scan_lanes(lb0)

    ys = []
    for m in range(GROUP_W // MXU_COLS):
        k0 = m * MXU_COLS // S5_CH * S5_STATE
        k1 = (m + 1) * MXU_COLS // S5_CH * S5_STATE
        cols = slice(m * MXU_COLS, (m + 1) * MXU_COLS)
        ys.append(jnp.dot(xs_ref[:, k0:k1].astype(BF16), cmat_ref[k0:k1, cols], preferred_element_type=F32)
                  + jnp.dot(xs_ref[:, n + k0:n + k1].astype(BF16), cmat_ref[n + k0:n + k1, cols],
                            preferred_element_type=F32))
    y = jnp.concatenate(ys, axis=-1) + d_ref[...] * uperm_ref[...]
    g = jax.nn.gelu(y)
    gate = jnp.dot(g.astype(BF16), wglu_ref[...].astype(BF16), preferred_element_type=F32)
    out = g * jax.nn.sigmoid(gate)
    for j in range(len(u_refs)):
        yp_ref[j] = out[:, j * LANES:(j + 1) * LANES]

    tiles_per_chunk = tc // SUBLANES
    shift = tiles_per_chunk.bit_length() - 1

    def scatter(idx, carry):
        c = idx >> shift
        m = idx & (tiles_per_chunk - 1)
        r = pl.multiple_of(idx * SUBLANES, SUBLANES)
        src = m * (SUBLANES * SUBLANES) + c
        for j in range(len(u_refs)):
            o_ref[pl.ds(r, SUBLANES), j * LANES:(j + 1) * LANES] = yp_ref[j, pl.ds(src, SUBLANES, stride=SUBLANES), :]
        return carry

    lax.fori_loop(0, SUBLANES * tiles_per_chunk, scatter, 0)


def s5_mixer(proj, bmat, cmat, coef_r, coef_i, d_skip, w_glu, layer, bsz, seq, tl=S5_BLOCK):
    nl = seq // tl
    n2 = 2 * S5_NSTATE
    tc = tl // SUBLANES
    assert tc % SUBLANES == 0 and tc & (tc - 1) == 0
    nu = GROUP_W // LANES
    uspecs = [pl.BlockSpec((tl, LANES), lambda b, l, j=j: (b * nl + l, j)) for j in range(nu)]
    coef = pl.BlockSpec((None, S5_COEF_TILES, SUBLANES, S5_NSTATE), lambda b, l: (layer, 0, 0, 0))
    return pl.pallas_call(
        functools.partial(_s5_kernel, tl=tl),
        grid=(bsz, nl),
        in_specs=uspecs + [
            pl.BlockSpec((None, GROUP_W, n2), lambda b, l: (layer, 0, 0)),
            pl.BlockSpec((None, n2, GROUP_W), lambda b, l: (layer, 0, 0)),
            coef, coef,
            pl.BlockSpec((None, 1, GROUP_W), lambda b, l: (layer, 0, 0)),
            pl.BlockSpec((None, GROUP_W, GROUP_W), lambda b, l: (layer, 0, 0)),
        ],
        out_specs=pl.BlockSpec((tl, GROUP_W), lambda b, l: (b * nl + l, 0)),
        out_shape=jax.ShapeDtypeStruct((bsz * seq, GROUP_W), F32),
        scratch_shapes=[
            pltpu.VMEM((nu, SUBLANES * (tc + SUBLANES), LANES), F32),
            pltpu.VMEM((tl, GROUP_W), F32),
            pltpu.VMEM((tl, n2), F32),
            pltpu.VMEM((SUBLANES, n2), F32),
            pltpu.VMEM((nu, tl, LANES), F32),
        ],
        compiler_params=_params("parallel", "arbitrary"),
        name="s5_mixer",
    )(*([proj] * nu), bmat, cmat, coef_r, coef_i, d_skip, w_glu)


POOL_HALO = 16


def _pool_kernel(u_ref, pw_ref, sc_ref, o_ref, ext_ref, *, tl):
    l = pl.program_id(1)

    @pl.when(l == 0)
    def _():
        ext_ref[0:POOL_HALO, :] = jnp.zeros((POOL_HALO, GROUP_W), F32)

    @pl.when(l > 0)
    def _():
        ext_ref[0:POOL_HALO, :] = ext_ref[tl:tl + POOL_HALO, :]

    ext_ref[POOL_HALO:, :] = u_ref[...]
    t = l * tl + lax.broadcasted_iota(jnp.int32, (tl, 1), 0)
    for gi, w in enumerate(POOL_WINDOWS):
        cols = slice(gi * POOL_CH, (gi + 1) * POOL_CH)
        acc = ext_ref[pl.ds(POOL_HALO, tl), cols]
        for k in range(1, w):
            acc = acc + ext_ref[pl.ds(POOL_HALO - k, tl), cols]
        cnt = jnp.minimum(t + 1, w).astype(F32)
        p = acc / cnt - u_ref[:, cols]
        y = jnp.dot(p.astype(BF16), pw_ref[gi].astype(BF16), preferred_element_type=F32)
        o_ref[:, cols] = y * sc_ref[:, cols]


def pool_mixer(proj, pool_w, pool_scale, layer, bsz, seq, tl=512):
    nl = seq // tl
    ng = len(POOL_WINDOWS)
    return pl.pallas_call(
        functools.partial(_pool_kernel, tl=tl),
        grid=(bsz, nl),
        in_specs=[
            pl.BlockSpec((tl, GROUP_W), lambda b, l: (b * nl + l, 1)),
            pl.BlockSpec((None, ng, POOL_CH, POOL_CH), lambda b, l: (layer, 0, 0, 0)),
            pl.BlockSpec((None, 1, GROUP_W), lambda b, l: (layer, 0, 0)),
        ],
        out_specs=pl.BlockSpec((tl, GROUP_W), lambda b, l: (b * nl + l, 0)),
        out_shape=jax.ShapeDtypeStruct((bsz * seq, GROUP_W), F32),
        scratch_shapes=[pltpu.VMEM((POOL_HALO + tl, GROUP_W), F32)],
        compiler_params=_params("parallel", "arbitrary"),
        name="pool_mixer",
    )(proj, pool_w, pool_scale)


CONV_HALO = 32
CONV_ROWS = 32


def _conv_kernel(val_ref, gate_ref, wdw_ref, bdw_ref, lng_ref, lnb_ref, wpw_ref, o_ref,
                 sh_ref, cv_ref, wb_ref, *, tl):
    l = pl.program_id(1)
    n = tl + CONV_HALO

    @pl.when(l == 0)
    def _():
        sh_ref[0, 0:CONV_HALO, :] = jnp.zeros((CONV_HALO, GROUP_W), F32)

    @pl.when(l > 0)
    def _():
        sh_ref[0, 0:CONV_HALO, :] = sh_ref[0, tl:n, :]

    sh_ref[0, CONV_HALO:n, :] = val_ref[...] * jax.nn.sigmoid(gate_ref[...])
    sh_ref[0, n:, :] = jnp.zeros((SUBLANES, GROUP_W), F32)
    for s in range(1, SUBLANES):
        sh_ref[s, 0:n, :] = sh_ref[0, s:s + n, :]

    off = CONV_HALO - (CONV_WIDTH - 1)

    for k in range(CONV_WIDTH):
        wb_ref[k] = jnp.broadcast_to(wdw_ref[k:k + 1, :], (SUBLANES, GROUP_W))
    wb_ref[CONV_WIDTH] = jnp.broadcast_to(bdw_ref[...], (SUBLANES, GROUP_W))
    tiles = CONV_ROWS // SUBLANES

    def rows(c, carry):
        r0 = pl.multiple_of(c * CONV_ROWS, CONV_ROWS)
        acc = jnp.broadcast_to(wb_ref[CONV_WIDTH][None], (tiles, SUBLANES, GROUP_W))
        for k in range(CONV_WIDTH):
            q, s = divmod(off + k, SUBLANES)
            x = sh_ref[s, pl.ds(r0 + q * SUBLANES, CONV_ROWS), :].reshape(tiles, SUBLANES, GROUP_W)
            acc = acc + wb_ref[k][None] * x
        cv_ref[pl.ds(r0, CONV_ROWS), :] = acc.reshape(CONV_ROWS, GROUP_W)
        return carry

    lax.fori_loop(0, tl // CONV_ROWS, rows, 0)

    h = cv_ref[...]
    hc = h - jnp.mean(h, axis=-1, keepdims=True)
    y = hc * lax.rsqrt(jnp.mean(hc * hc, axis=-1, keepdims=True) + NORM_EPS)
    y = y * lng_ref[...] + lnb_ref[...]
    y = jax.nn.silu(y)
    o_ref[...] = jnp.dot(y.astype(BF16), wpw_ref[...].astype(BF16), preferred_element_type=F32)


def conv_mixer(proj, w_dw, b_dw, ln_g, ln_b, w_pw, layer, bsz, seq, tl=512):
    nl = seq // tl
    vec = pl.BlockSpec((None, 1, GROUP_W), lambda b, l: (layer, 0, 0))
    return pl.pallas_call(
        functools.partial(_conv_kernel, tl=tl),
        grid=(bsz, nl),
        in_specs=[
            pl.BlockSpec((tl, GROUP_W), lambda b, l: (b * nl + l, 2)),
            pl.BlockSpec((tl, GROUP_W), lambda b, l: (b * nl + l, 3)),
            pl.BlockSpec((None, CONV_WIDTH, GROUP_W), lambda b, l: (layer, 0, 0)),
            vec, vec, vec,
            pl.BlockSpec((None, GROUP_W, GROUP_W), lambda b, l: (layer, 0, 0)),
        ],
        out_specs=pl.BlockSpec((tl, GROUP_W), lambda b, l: (b * nl + l, 0)),
        out_shape=jax.ShapeDtypeStruct((bsz * seq, GROUP_W), F32),
        scratch_shapes=[pltpu.VMEM((SUBLANES, CONV_HALO + tl + SUBLANES, GROUP_W), F32),
                        pltpu.VMEM((tl, GROUP_W), F32),
                        pltpu.VMEM((CONV_WIDTH + 1, SUBLANES, GROUP_W), F32)],
        compiler_params=_params("parallel", "arbitrary"),
        name="conv_mixer",
    )(proj, proj, w_dw, b_dw, ln_g, ln_b, w_pw)


def _t5_bucket(dist):
    n = np.maximum(dist, 0)
    max_exact = REL_BUCKETS // 2
    large = max_exact + (np.log(np.maximum(n, 1) / max_exact) / np.log(REL_MAX_DIST / max_exact)
                         * (REL_BUCKETS - max_exact)).astype(np.int64)
    large = np.minimum(large, REL_BUCKETS - 1)
    return np.where(n < max_exact, n, large).astype(np.int32)


ATT_NEAR_BLOCKS = 5
ATT_NEAR = ATT_NEAR_BLOCKS * ATT_BLOCK
ATT_FAR_DIL = 16


def _attention_bias_layout(sub):
    a = np.arange(ATT_BLOCK)[:, None]
    d_near = (ATT_NEAR - ATT_BLOCK) + a - np.arange(ATT_NEAR)[None, :]
    mult = np.zeros(d_near.shape, np.int64)
    for window, dil in DILATED_PATTERNS:
        if dil != ATT_FAR_DIL:
            mult += (d_near >= 0) & (d_near % dil == 0) & (d_near // dil <= window // dil)
    d_far = ATT_FAR_DIL * (a - np.arange(sub)[None, :])
    mult_far = ((d_far >= 0) & (d_far // ATT_FAR_DIL <= ATT_BLOCK)).astype(np.int64)
    d = np.concatenate([d_near, d_far], axis=1)
    mult = np.concatenate([mult, mult_far], axis=1)
    addend = np.where(mult > 0, np.log(np.maximum(mult, 1)), NEG_INF).astype(np.float32)
    return _t5_bucket(d), addend


def _attn_bias_kernel(rb_ref, bucket_ref, add_ref, o_ref):
    h = pl.program_id(0)
    bucket = bucket_ref[...]
    t = jnp.zeros(bucket.shape, F32)
    for b in range(REL_BUCKETS):
        t = jnp.where(bucket == b, rb_ref[b, h], t)
    o_ref[...] = (t + add_ref[...]) * LOG2E


def attention_bias_table(rel_bias, seq):
    sub = seq // ATT_FAR_DIL
    bucket, addend = _attention_bias_layout(sub)
    width = ATT_NEAR + sub
    full = pl.BlockSpec((ATT_BLOCK, width), lambda h: (0, 0))
    return pl.pallas_call(
        _attn_bias_kernel,
        grid=(ATT_HEADS,),
        in_specs=[pl.BlockSpec(memory_space=pltpu.SMEM), full, full],
        out_specs=pl.BlockSpec((None, ATT_BLOCK, width), lambda h: (h, 0, 0)),
        out_shape=jax.ShapeDtypeStruct((ATT_HEADS, ATT_BLOCK, width), F32),
        compiler_params=_params("arbitrary"),
        name="attn_bias",
    )(rel_bias, jnp.asarray(bucket), jnp.asarray(addend))


ATT_HEADS_PER_BLOCK = LANES // ATT_HEAD_DIM
ATT_GROUP = 8


def _nt_dot(a, b):
    return lax.dot_general(a, b, (((1,), (1,)), ((), ())), preferred_element_type=F32)


def _attn_kernel(q_ref, k_ref, v_ref, bt_ref, *rest, seq, n_cast):
    cast_in, o_ref, cast_out = rest[:n_cast], rest[n_cast], rest[n_cast + 1:2 * n_cast + 1]
    kb_ref, vb_ref, m3_ref, l3_ref, a3_ref = rest[2 * n_cast + 1:]
    sub = seq // ATT_FAR_DIL
    pad = ATT_NEAR - ATT_BLOCK
    scale = ATT_HEAD_DIM ** -0.5 * LOG2E
    lane = lax.broadcasted_iota(jnp.int32, (1, LANES), 1)
    in_head = [(lane >= hh * ATT_HEAD_DIM) & (lane < (hh + 1) * ATT_HEAD_DIM)
               for hh in range(ATT_HEADS_PER_BLOCK)]

    def by_head(vals):
        out = vals[-1]
        for hh in range(ATT_HEADS_PER_BLOCK - 2, -1, -1):
            out = jnp.where(in_head[hh], vals[hh], out)
        return out

    def softmax_pv(tiles):
        heads = range(ATT_HEADS_PER_BLOCK)
        s = [[_nt_dot(jnp.where(in_head[hh], qs, 0.0).astype(BF16), kb) + bias_of(hh) for hh in heads]
             for qs, kb, _, bias_of in tiles]
        m = [[jnp.max(s_h, axis=-1, keepdims=True) for s_h in s_t] for s_t in s]
        p = [[jnp.exp2(s_h - m_h) for s_h, m_h in zip(s_t, m_t)] for s_t, m_t in zip(s, m)]
        l = [[jnp.sum(p_h, axis=-1, keepdims=True) for p_h in p_t] for p_t in p]
        acc = [[jnp.dot(p_h.astype(BF16), tile[2], preferred_element_type=F32) for p_h in p_t]
               for p_t, tile in zip(p, tiles)]
        out = []
        for tile, m_t, l_t, acc_t in zip(tiles, m, l, acc):
            full = (tile[0].shape[0], LANES)
            out.append((by_head([jnp.broadcast_to(x, full) for x in m_t]),
                        by_head([jnp.broadcast_to(x, full) for x in l_t]), by_head(acc_t)))
        return out

    for src, dst in zip(cast_in, cast_out):
        dst[...] = src[...].astype(BF16)

    kb_ref[...] = k_ref[...].astype(BF16)
    vb_ref[...] = v_ref[...].astype(BF16)

    for g0 in range(0, ATT_FAR_DIL, ATT_GROUP):
        group = [pl.ds(r, sub, stride=ATT_FAR_DIL) for r in range(g0, g0 + ATT_GROUP)]
        far_bias = lambda hh: bt_ref[hh, 0:sub, ATT_NEAR:ATT_NEAR + sub]
        tiles = [(q_ref[rows, :] * scale, k_ref[rows, :].astype(BF16), v_ref[rows, :].astype(BF16), far_bias)
                 for rows in group]
        for rows, (m, l, acc) in zip(group, softmax_pv(tiles)):
            m3_ref[rows, :] = m
            l3_ref[rows, :] = l
            a3_ref[rows, :] = acc

    nblocks = seq // ATT_BLOCK
    for g0 in range(0, nblocks, ATT_GROUP):
        group, tiles = [], []
        for i in range(g0, min(g0 + ATT_GROUP, nblocks)):
            r0 = i * ATT_BLOCK
            lo = max(0, r0 - pad)
            c0 = ATT_NEAR - (r0 + ATT_BLOCK - lo)
            group.append(slice(r0, r0 + ATT_BLOCK))
            tiles.append((q_ref[group[-1], :] * scale, kb_ref[lo:r0 + ATT_BLOCK, :], vb_ref[lo:r0 + ATT_BLOCK, :],
                          lambda hh, c0=c0: bt_ref[hh, :, c0:ATT_NEAR]))
        merged = []
        for rows, (m, l, acc) in zip(group, softmax_pv(tiles)):
            m3 = m3_ref[rows, :]
            mm = jnp.maximum(m, m3)
            wn = jnp.exp2(m - mm)
            wf = jnp.exp2(m3 - mm)
            num = acc * wn + a3_ref[rows, :] * wf
            den = l * wn + l3_ref[rows, :] * wf
            merged.append(num / den)
        for rows, o in zip(group, merged):
            o_ref[rows, :] = o


def dilated_attention(proj, bias_table, casts, bsz, seq):
    assert seq % ATT_BLOCK == 0 and seq // ATT_FAR_DIL <= ATT_BLOCK
    nhp = ATT_HEADS // ATT_HEADS_PER_BLOCK
    steps = bsz * nhp
    qcol = 4 * GROUP_W // LANES
    kcol = 5 * GROUP_W // LANES
    vcol = 6 * GROUP_W // LANES
    width = bias_table.shape[-1]
    stat = pltpu.VMEM((seq, LANES), F32)
    half = pltpu.VMEM((seq, LANES), BF16)
    cast_in, cast_out, cast_shapes = [], [], []
    for w, layer in casts:
        rows, cols = w.shape[1:]
        assert rows % steps == 0
        cast_in.append(pl.BlockSpec((None, rows // steps, cols), lambda b, h, layer=layer: (layer, b * nhp + h, 0)))
        cast_out.append(pl.BlockSpec((rows // steps, cols), lambda b, h: (b * nhp + h, 0)))
        cast_shapes.append(jax.ShapeDtypeStruct((rows, cols), BF16))
    return pl.pallas_call(
        functools.partial(_attn_kernel, seq=seq, n_cast=len(casts)),
        grid=(bsz, nhp),
        in_specs=[
            pl.BlockSpec((seq, LANES), lambda b, h: (b, qcol + h)),
            pl.BlockSpec((seq, LANES), lambda b, h: (b, kcol + h)),
            pl.BlockSpec((seq, LANES), lambda b, h: (b, vcol + h)),
            pl.BlockSpec((ATT_HEADS_PER_BLOCK, ATT_BLOCK, width), lambda b, h: (h, 0, 0)),
        ] + cast_in,
        out_specs=[pl.BlockSpec((seq, LANES), lambda b, h: (b, h))] + cast_out,
        out_shape=[jax.ShapeDtypeStruct((bsz * seq, GROUP_W), F32)] + cast_shapes,
        scratch_shapes=[half, half, stat, stat, stat],
        compiler_params=_params("parallel", "parallel"),
        name="dilated_attention",
    )(proj, proj, proj, bias_table, *[w for w, _ in casts])


def _mix_out_kernel(ya_ref, yb_ref, yc_ref, yd_ref, g_ref, w_ref, h_ref, o_ref, wb_ref):
    @pl.when(pl.program_id(0) == 0)
    def _():
        wb_ref[...] = w_ref[...].astype(BF16)

    yn = [_rms(y_ref[...], g_ref[:, gi * GROUP_W:(gi + 1) * GROUP_W]).astype(BF16)
          for gi, y_ref in enumerate((ya_ref, yb_ref, yc_ref, yd_ref))]
    o_ref[...] = h_ref[...] + jnp.dot(jnp.concatenate(yn, axis=-1), wb_ref[...], preferred_element_type=F32)


def mix_out(ya, yb, yc, yd, g, w, h, layer, tm=512):
    m, d = h.shape
    yspec = pl.BlockSpec((tm, GROUP_W), lambda i: (i, 0))
    return pl.pallas_call(
        _mix_out_kernel,
        grid=(m // tm,),
        in_specs=[
            yspec, yspec, yspec, yspec,
            pl.BlockSpec((None, 1, d), lambda i: (layer, 0, 0)),
            pl.BlockSpec((None, d, d), lambda i: (layer, 0, 0), pipeline_mode=pl.Buffered(1)),
            pl.BlockSpec((tm, d), lambda i: (i, 0)),
        ],
        out_specs=pl.BlockSpec((tm, d), lambda i: (i, 0)),
        out_shape=jax.ShapeDtypeStruct((m, d), F32),
        scratch_shapes=[pltpu.VMEM((d, d), BF16)],
        compiler_params=_params("arbitrary"),
        name="mix_out",
    )(ya, yb, yc, yd, g, w, h)


def _mem_kv_kernel(mem_ref, g_ref, wk_ref, wv_ref, k_ref, v_ref):
    mn = _rms(mem_ref[...], g_ref[...]).astype(BF16)
    k_ref[...] = jnp.dot(mn, wk_ref[...].astype(BF16), preferred_element_type=F32).astype(BF16)
    v_ref[...] = jnp.dot(mn, wv_ref[...].astype(BF16), preferred_element_type=F32).astype(BF16)


def mem_kv(mem, g, wk, wv):
    m, d = mem.shape
    depth = wk.shape[0]
    wspec = pl.BlockSpec((None, d, X_WIDTH), lambda l: (l, 0, 0))
    ospec = pl.BlockSpec((None, m, X_WIDTH), lambda l: (l, 0, 0))
    return pl.pallas_call(
        _mem_kv_kernel,
        grid=(depth,),
        in_specs=[pl.BlockSpec((m, d), lambda l: (0, 0)), pl.BlockSpec((1, d), lambda l: (0, 0)),
                  wspec, wspec],
        out_specs=[ospec, ospec],
        out_shape=[jax.ShapeDtypeStruct((depth, m, X_WIDTH), BF16)] * 2,
        compiler_params=_params("arbitrary"),
        name="mem_kv",
    )(mem, g, wk, wv)


def _xattn_kernel(h_ref, g_ref, wq_ref, k_ref, v_ref, wo_ref, o_ref, wqb_ref, wob_ref):
    @pl.when((pl.program_id(0) == 0) & (pl.program_id(1) == 0))
    def _():
        wqb_ref[...] = wq_ref[...].astype(BF16)
        wob_ref[...] = wo_ref[...].astype(BF16)

    h = h_ref[...]
    hn = _rms(h, g_ref[...]).astype(BF16)
    q = jnp.dot(hn, wqb_ref[...], preferred_element_type=F32) * (X_HEAD_DIM ** -0.5 * LOG2E)
    cols = [slice(hd * X_HEAD_DIM, (hd + 1) * X_HEAD_DIM) for hd in range(X_HEADS)]
    qb = q.astype(BF16)
    s = [_nt_dot(qb[:, c], k_ref[:, c]) for c in cols]
    m = [jnp.max(x, axis=-1, keepdims=True) for x in s]
    p = [jnp.exp2(x - mx) for x, mx in zip(s, m)]
    den = [jnp.sum(x, axis=-1, keepdims=True) for x in p]
    p = [x / dx for x, dx in zip(p, den)]
    outs = [jnp.dot(x.astype(BF16), v_ref[:, c], preferred_element_type=F32) for x, c in zip(p, cols)]
    o = jnp.concatenate(outs, axis=-1).astype(BF16)
    o_ref[...] = h + jnp.dot(o, wob_ref[...], preferred_element_type=F32)


def cross_attention(h, g, wq, k, v, wo, layer, bsz, seq, tm=512):
    d = h.shape[-1]
    nl = seq // tm
    mlen = k.shape[1] // bsz
    return pl.pallas_call(
        _xattn_kernel,
        grid=(bsz, nl),
        in_specs=[
            pl.BlockSpec((tm, d), lambda b, l: (b * nl + l, 0)),
            pl.BlockSpec((None, 1, d), lambda b, l: (layer, 0, 0)),
            pl.BlockSpec((None, d, X_WIDTH), lambda b, l: (layer, 0, 0)),
            pl.BlockSpec((None, mlen, X_WIDTH), lambda b, l: (layer, b, 0)),
            pl.BlockSpec((None, mlen, X_WIDTH), lambda b, l: (layer, b, 0)),
            pl.BlockSpec((None, X_WIDTH, d), lambda b, l: (layer, 0, 0)),
        ],
        out_specs=pl.BlockSpec((tm, d), lambda b, l: (b * nl + l, 0)),
        out_shape=jax.ShapeDtypeStruct(h.shape, F32),
        scratch_shapes=[pltpu.VMEM((d, X_WIDTH), BF16), pltpu.VMEM((X_WIDTH, d), BF16)],
        compiler_params=_params("arbitrary", "arbitrary"),
        name="cross_attention",
    )(h, g, wq, k, v, wo)


def _mlp_kernel(h_ref, g_ref, wu_ref, wd_ref, *rest, out_norm):
    o_ref, hn_ref = rest[-2:]
    j = pl.program_id(1)

    @pl.when(j == 0)
    def _():
        h = h_ref[...]
        hn_ref[...] = _rms(h, g_ref[...]).astype(BF16)
        o_ref[...] = h

    a = jnp.dot(hn_ref[...], wu_ref[...], preferred_element_type=F32)
    a = jnp.square(jnp.maximum(a, 0.0)).astype(BF16)
    o_ref[...] += jnp.dot(a, wd_ref[...], preferred_element_type=F32)

    if out_norm:
        @pl.when(j == pl.num_programs(1) - 1)
        def _():
            o_ref[...] = _rms(o_ref[...], rest[0][...])


def mlp(h, g, wu, wd, layer, out_norm_g=None, tm=1024, tf=512):
    m, d = h.shape
    f = wu.shape[-1]
    out_norm = out_norm_g is not None
    extra_specs = [pl.BlockSpec((1, d), lambda i, j: (0, 0))] if out_norm else []
    extra_args = (out_norm_g,) if out_norm else ()
    return pl.pallas_call(
        functools.partial(_mlp_kernel, out_norm=out_norm),
        grid=(m // tm, f // tf),
        in_specs=[
            pl.BlockSpec((tm, d), lambda i, j: (i, 0)),
            pl.BlockSpec((None, 1, d), lambda i, j: (layer, 0, 0)),
            pl.BlockSpec((d, tf), lambda i, j: (0, j)),
            pl.BlockSpec((tf, d), lambda i, j: (j, 0)),
        ] + extra_specs,
        out_specs=pl.BlockSpec((tm, d), lambda i, j: (i, 0)),
        out_shape=jax.ShapeDtypeStruct((m, d), F32),
        scratch_shapes=[pltpu.VMEM((tm, d), BF16)],
        compiler_params=_params("parallel", "arbitrary"),
        name="mlp",
    )(h, g, wu, wd, *extra_args)


def kernel(x, mem, rel_bias, mem_norm_g, norm_mix_g, w_in, s5_lam_re, s5_lam_im, s5_log_dt, s5_b_re, s5_b_im, s5_c_re, s5_c_im, s5_d, s5_w_glu, pool_w, pool_scale, conv_w_dw, conv_b_dw, conv_ln_g, conv_ln_b, conv_w_pw, grp_norm_g, w_out, norm_x_g, w_xq, w_xk, w_xv, w_xo, norm_mlp_g, w_up, w_down, norm_final_g):
    bsz, seq, d = x.shape
    depth = w_in.shape[0]
    vec = lambda a: a[:, None, :]

    k_mem, v_mem = mem_kv(mem.reshape(bsz * mem.shape[1], d), mem_norm_g[None, :], w_xk, w_xv)
    pow_r, pow_i, bmat, cmat = s5_params(s5_lam_re, s5_lam_im, s5_log_dt, s5_b_re, s5_b_im, s5_c_re, s5_c_im,
                                         chunk=S5_BLOCK // SUBLANES)
    coef_r, coef_i = s5_scan_coefficients(pow_r, pow_i)
    bias_table = attention_bias_table(rel_bias, seq)

    norm_mix, s5_dv, pool_sc = vec(norm_mix_g), vec(s5_d), vec(pool_scale)
    b_dw, ln_g, ln_b = vec(conv_b_dw), vec(conv_ln_g), vec(conv_ln_b)
    grp_g, norm_x, norm_mlp = vec(grp_norm_g), vec(norm_x_g), vec(norm_mlp_g)

    h = x.reshape(bsz * seq, d)
    w_in_b = w_in[0].astype(BF16)
    for l in range(depth):
        proj, y_c, y_b = proj_conv(h, norm_mix, w_in_b, conv_w_dw, b_dw, ln_g, ln_b, conv_w_pw,
                                   pool_w, pool_sc, l, bsz, seq)
        y_a = s5_mixer(proj, bmat, cmat, coef_r, coef_i, s5_dv, s5_w_glu, l, bsz, seq)
        casts = [(w_up, l), (w_down, l)] + ([(w_in, l + 1)] if l + 1 < depth else [])
        y_d, w_up_b, w_down_b, *nxt = dilated_attention(proj, bias_table, casts, bsz, seq)
        w_in_b = nxt[0] if nxt else None
        h = mix_out(y_a, y_b, y_c, y_d, grp_g, w_out, h, l)
        h = cross_attention(h, norm_x, w_xq, k_mem, v_mem, w_xo, l, bsz, seq)
        h = mlp(h, norm_mlp, w_up_b, w_down_b, l,
                out_norm_g=norm_final_g[None, :] if l == depth - 1 else None)
    return h.reshape(bsz, seq, d)
```

```python
import functools
import math

import numpy as np
import jax
import jax.numpy as jnp
from jax import lax
from jax.experimental import pallas as pl
from jax.experimental.pallas import tpu as pltpu

F32 = jnp.float32
BF16 = jnp.bfloat16

GROUP_W = 512
S5_GROUPS = 32
S5_CH = 16
S5_STATE = 64
S5_NSTATE = S5_GROUPS * S5_STATE
S5_SCAN_LANE_BLOCKS = 8
S5_BLOCK = 512
S5_SCAN_UNROLL = 4
POOL_WINDOWS = (2, 4, 8, 16)
POOL_CH = 128
CONV_WIDTH = 31
ATT_HEADS = 8
ATT_HEAD_DIM = 64
DILATED_PATTERNS = ((128, 1), (512, 4), (2048, 16))
ATT_BLOCK = 128
REL_BUCKETS = 32
REL_MAX_DIST = 2048
X_HEADS = 4
X_HEAD_DIM = 128
X_WIDTH = 512
NORM_EPS = 1e-6
NEG_INF = -1e30
LOG2E = math.log2(math.e)

SUBLANES = 8
LANES = 128
VMEM_LIMIT = 58 * 1024 * 1024


def _params(*sem):
    return pltpu.CompilerParams(dimension_semantics=sem, vmem_limit_bytes=VMEM_LIMIT)


def _rms(x, g):
    return x * lax.rsqrt(jnp.mean(x * x, axis=-1, keepdims=True) + NORM_EPS) * g


MXU_COLS = 256
POOL_HALO = 16
CONV_HALO = 32
CONV_ROWS = 32


def _proj_conv_kernel(x_ref, g_ref, w_ref, wdw_ref, bdw_ref, lng_ref, lnb_ref, wpw_ref, pw_ref, psc_ref,
                      proj_ref, yc_ref, yb_ref, sh_ref, cv_ref, wb_ref, xn_ref, ext_ref, *, tl, nl):
    step = pl.program_id(0)
    n = tl + CONV_HALO

    @pl.when(step == 0)
    def _():
        sh_ref[0] = jnp.zeros(sh_ref.shape[1:], F32)
        ext_ref[...] = jnp.zeros(ext_ref.shape, F32)

    @pl.when(lax.rem(step - 1, nl) == 0)
    def _():
        sh_ref[0, 0:CONV_HALO, :] = jnp.zeros((CONV_HALO, GROUP_W), F32)
        ext_ref[0:POOL_HALO, :] = jnp.zeros((POOL_HALO, GROUP_W), F32)

    sh_ref[0, n:, :] = jnp.zeros((SUBLANES, GROUP_W), F32)
    for s in range(1, SUBLANES):
        sh_ref[s, 0:n, :] = sh_ref[0, s:s + n, :]

    for k in range(CONV_WIDTH):
        wb_ref[k] = jnp.broadcast_to(wdw_ref[k:k + 1, :], (SUBLANES, GROUP_W))
    wb_ref[CONV_WIDTH] = jnp.broadcast_to(bdw_ref[...], (SUBLANES, GROUP_W))

    xn_ref[...] = _rms(x_ref[...], g_ref[...]).astype(BF16)
    slice_cols = 2 * MXU_COLS
    n_slices = proj_ref.shape[-1] // slice_cols
    n_chunks = tl // CONV_ROWS
    per_slice = n_chunks // n_slices
    tiles = CONV_ROWS // SUBLANES
    off = CONV_HALO - (CONV_WIDTH - 1)

    def conv_chunk(c):
        r0 = c * CONV_ROWS
        acc = jnp.broadcast_to(wb_ref[CONV_WIDTH][None], (tiles, SUBLANES, GROUP_W))
        for k in range(CONV_WIDTH):
            q, s = divmod(off + k, SUBLANES)
            x = sh_ref[s, r0 + q * SUBLANES:r0 + q * SUBLANES + CONV_ROWS, :]
            acc = acc + wb_ref[k][None] * x.reshape(tiles, SUBLANES, GROUP_W)
        cv_ref[r0:r0 + CONV_ROWS, :] = acc.reshape(CONV_ROWS, GROUP_W)

    def pool_tile():
        t = lax.rem(jnp.maximum(step - 1, 0), nl) * tl + lax.broadcasted_iota(jnp.int32, (tl, 1), 0)
        for gi, w in enumerate(POOL_WINDOWS):
            pc = slice(gi * POOL_CH, (gi + 1) * POOL_CH)
            acc = ext_ref[POOL_HALO:POOL_HALO + tl, pc]
            for k in range(1, w):
                acc = acc + ext_ref[POOL_HALO - k:POOL_HALO - k + tl, pc]
            p = acc / jnp.minimum(t + 1, w).astype(F32) - ext_ref[POOL_HALO:POOL_HALO + tl, pc]
            y = jnp.dot(p.astype(BF16), pw_ref[gi].astype(BF16), preferred_element_type=F32)
            yb_ref[:, pc] = y * psc_ref[:, pc]

    for j in range(n_slices):
        cols = slice(j * slice_cols, (j + 1) * slice_cols)
        proj_ref[:, cols] = jnp.dot(xn_ref[...], w_ref[:, cols], preferred_element_type=F32)
        if j == 0:
            pool_tile()
        for c in range(j * per_slice, (j + 1) * per_slice):
            conv_chunk(c)
        if per_slice:
            r0 = ((j + 1) * per_slice - 1) * CONV_ROWS
            zero = pltpu.bitcast(lax.shift_right_logical(
                pltpu.bitcast(cv_ref[r0:r0 + 2 * SUBLANES, 0:LANES], jnp.uint32), jnp.uint32(32)), F32)
            xn_ref[0:2 * SUBLANES, 0:LANES] = xn_ref[0:2 * SUBLANES, 0:LANES] + zero.astype(BF16)
    for c in range(n_slices * per_slice, n_chunks):
        conv_chunk(c)

    h = cv_ref[...]
    hc = h - jnp.mean(h, axis=-1, keepdims=True)
    y = hc * lax.rsqrt(jnp.mean(hc * hc, axis=-1, keepdims=True) + NORM_EPS)
    y = y * lng_ref[...] + lnb_ref[...]
    y = jax.nn.silu(y)
    yc_ref[...] = jnp.dot(y.astype(BF16), wpw_ref[...].astype(BF16), preferred_element_type=F32)

    ext_ref[0:POOL_HALO, :] = ext_ref[tl:tl + POOL_HALO, :]
    ext_ref[POOL_HALO:, :] = proj_ref[:, GROUP_W:2 * GROUP_W]
    sh_ref[0, 0:CONV_HALO, :] = sh_ref[0, tl:n, :]
    sh_ref[0, CONV_HALO:n, :] = (proj_ref[:, 2 * GROUP_W:3 * GROUP_W]
                                 * jax.nn.sigmoid(proj_ref[:, 3 * GROUP_W:4 * GROUP_W]))


def proj_conv(x, g, w, w_dw, b_dw, ln_g, ln_b, w_pw, pool_w, pool_scale, layer, bsz, seq, tl=512):
    m, d = x.shape
    n = w.shape[-1]
    nl = seq // tl
    nt = m // tl
    vec = pl.BlockSpec((None, 1, GROUP_W), lambda s: (layer, 0, 0))
    cur = lambda s: (jnp.minimum(s, nt - 1), 0)
    prev = lambda s: (jnp.maximum(s - 1, 0), 0)
    return pl.pallas_call(
        functools.partial(_proj_conv_kernel, tl=tl, nl=nl),
        grid=(nt + 1,),
        in_specs=[
            pl.BlockSpec((tl, d), cur),
            pl.BlockSpec((None, 1, d), lambda s: (layer, 0, 0)),
            pl.BlockSpec((d, n), lambda s: (0, 0), pipeline_mode=pl.Buffered(1)),
            pl.BlockSpec((None, CONV_WIDTH, GROUP_W), lambda s: (layer, 0, 0)),
            vec, vec, vec,
            pl.BlockSpec((None, GROUP_W, GROUP_W), lambda s: (layer, 0, 0)),
            pl.BlockSpec((None, len(POOL_WINDOWS), POOL_CH, POOL_CH), lambda s: (layer, 0, 0, 0)),
            vec,
        ],
        out_specs=[pl.BlockSpec((tl, n), cur), pl.BlockSpec((tl, GROUP_W), prev),
                   pl.BlockSpec((tl, GROUP_W), prev)],
        out_shape=[jax.ShapeDtypeStruct((m, n), F32), jax.ShapeDtypeStruct((m, GROUP_W), F32),
                   jax.ShapeDtypeStruct((m, GROUP_W), F32)],
        scratch_shapes=[pltpu.VMEM((SUBLANES, CONV_HALO + tl + SUBLANES, GROUP_W), F32),
                        pltpu.VMEM((tl, GROUP_W), F32),
                        pltpu.VMEM((CONV_WIDTH + 1, SUBLANES, GROUP_W), F32),
                        pltpu.VMEM((tl, d), BF16),
                        pltpu.VMEM((POOL_HALO + tl, GROUP_W), F32)],
        compiler_params=_params("arbitrary"),
        name="proj_conv",
    )(x, g, w, w_dw, b_dw, ln_g, ln_b, w_pw, pool_w, pool_scale)


def _s5_param_kernel(lr_ref, li_ref, ldt_ref, lrx_ref, lix_ref, ldtx_ref, br_ref, bi_ref, cr_ref, ci_ref,
                     pr_ref, pi_ref, bmat_ref, cmat_ref, *, chunk):
    def discretise(lr, li, ldt):
        dt = jnp.exp(ldt)
        mag = jnp.exp(lr * dt)
        return mag * jnp.cos(li * dt), mag * jnp.sin(li * dt)

    cr, ci = discretise(lr_ref[...], li_ref[...], ldt_ref[...])
    slot = {1: 0, chunk: 1, 2 * chunk: 2, 4 * chunk: 3}
    e = 1
    while True:
        if e in slot:
            pr_ref[slot[e]] = cr
            pi_ref[slot[e]] = ci
        if e == 4 * chunk:
            break
        cr, ci = cr * cr - ci * ci, 2.0 * (cr * ci)
        e *= 2

    lr, li = lrx_ref[...], lix_ref[...]
    ar, ai = discretise(lr, li, ldtx_ref[...])
    den = lr * lr + li * li
    nr, ni = ar - 1.0, ai
    f_r = (nr * lr + ni * li) / den
    f_i = (ni * lr - nr * li) / den
    b_r, b_i = br_ref[...], bi_ref[...]
    bb_r = f_r * b_r - f_i * b_i
    bb_i = f_r * b_i + f_i * b_r

    def block_diagonal(x, row_shift, col_shift):
        rows, w = x.shape
        cols = (rows >> row_shift) << col_shift
        sel = ((lax.broadcasted_iota(jnp.int32, (w, cols), 1) & (w - 1))
               == lax.broadcasted_iota(jnp.int32, (w, cols), 0))
        tiled = jnp.dot(x.astype(BF16), jnp.where(sel, 1.0, 0.0).astype(BF16), preferred_element_type=F32)
        same = ((lax.broadcasted_iota(jnp.int32, (rows, cols), 0) >> row_shift)
                == (lax.broadcasted_iota(jnp.int32, (rows, cols), 1) >> col_shift))
        return jnp.where(same, tiled, 0.0).astype(BF16)

    ch_shift, st_shift = S5_CH.bit_length() - 1, S5_STATE.bit_length() - 1
    bmat_ref[:, 0:S5_NSTATE] = block_diagonal(bb_r, ch_shift, st_shift)
    bmat_ref[:, S5_NSTATE:] = block_diagonal(bb_i, ch_shift, st_shift)
    cmat_ref[0:S5_NSTATE, :] = block_diagonal(cr_ref[...], st_shift, ch_shift)
    cmat_ref[S5_NSTATE:, :] = block_diagonal(-ci_ref[...], st_shift, ch_shift)


S5_POWERS = 4


def s5_params(lam_re, lam_im, log_dt, b_re, b_im, c_re, c_im, chunk):
    depth = lam_re.shape[0]
    gc = S5_GROUPS * S5_CH
    rep = lambda a: jnp.repeat(a, S5_CH, axis=1)
    ldt = log_dt[..., None]
    bt = lambda b: jnp.transpose(b, (0, 1, 3, 2)).reshape(depth, gc, S5_STATE)
    ct = lambda c: jnp.transpose(c, (0, 1, 3, 2)).reshape(depth, S5_NSTATE, S5_CH)
    small = pl.BlockSpec((None, S5_GROUPS, S5_STATE), lambda l: (l, 0, 0))
    small1 = pl.BlockSpec((None, S5_GROUPS, 1), lambda l: (l, 0, 0))
    big = pl.BlockSpec((None, gc, S5_STATE), lambda l: (l, 0, 0))
    big1 = pl.BlockSpec((None, gc, 1), lambda l: (l, 0, 0))
    cspec = pl.BlockSpec((None, S5_NSTATE, S5_CH), lambda l: (l, 0, 0))
    powspec = pl.BlockSpec((None, S5_POWERS, S5_GROUPS, S5_STATE), lambda l: (l, 0, 0, 0))
    return pl.pallas_call(
        functools.partial(_s5_param_kernel, chunk=chunk),
        grid=(depth,),
        in_specs=[small, small, small1, big, big, big1, big, big, cspec, cspec],
        out_specs=[powspec, powspec,
                   pl.BlockSpec((None, gc, 2 * S5_NSTATE), lambda l: (l, 0, 0)),
                   pl.BlockSpec((None, 2 * S5_NSTATE, gc), lambda l: (l, 0, 0))],
        out_shape=[
            jax.ShapeDtypeStruct((depth, S5_POWERS, S5_GROUPS, S5_STATE), F32),
            jax.ShapeDtypeStruct((depth, S5_POWERS, S5_GROUPS, S5_STATE), F32),
            jax.ShapeDtypeStruct((depth, gc, 2 * S5_NSTATE), BF16),
            jax.ShapeDtypeStruct((depth, 2 * S5_NSTATE, gc), BF16),
        ],
        compiler_params=_params("arbitrary"),
        name="s5_params",
    )(lam_re, lam_im, ldt, rep(lam_re), rep(lam_im), rep(ldt), bt(b_re), bt(b_im), ct(c_re), ct(c_im))


def s5_scan_coefficients(pow_r, pow_i):
    depth = pow_r.shape[0]
    row = jnp.arange(SUBLANES)[None, :, None]

    def coef(p):
        p = p.reshape(depth, S5_POWERS, 1, S5_NSTATE)
        full = [jnp.broadcast_to(p[:, e], (depth, SUBLANES, S5_NSTATE)) for e in (0, 1)]
        masked = [jnp.where(row >= s, p[:, e], 0.0) for e, s in ((1, 1), (2, 2), (3, 4))]
        return jnp.stack(full + masked, axis=1)

    return coef(pow_r), coef(pow_i)


S5_COEF_TILES = 5


def _s5_kernel(u0_ref, u1_ref, u2_ref, u3_ref, bmat_ref, cmat_ref, cfr_ref, cfi_ref, d_ref, wglu_ref, o_ref,
               up_ref, uperm_ref, xs_ref, cin_ref, yp_ref, *, tl):
    n = S5_NSTATE
    tc = tl // SUBLANES
    pitch = tc + SUBLANES
    u_refs = (u0_ref, u1_ref, u2_ref, u3_ref)
    l = pl.program_id(1)

    @pl.when(l == 0)
    def _():
        cin_ref[...] = jnp.zeros(cin_ref.shape, F32)

    for j, u_ref in enumerate(u_refs):
        for c in range(SUBLANES):
            up_ref[j, c * pitch:c * pitch + tc, :] = u_ref[c * tc:(c + 1) * tc, :]

    def gather(t, carry):
        r = pl.multiple_of(t * SUBLANES, SUBLANES)
        for j in range(len(u_refs)):
            uperm_ref[pl.ds(r, SUBLANES), j * LANES:(j + 1) * LANES] = up_ref[j, pl.ds(t, SUBLANES, stride=pitch), :]
        return carry

    lax.fori_loop(0, tc, gather, 0)

    ub = uperm_ref[...].astype(BF16)
    blocks_per_part = n // MXU_COLS
    for jb in range(2 * blocks_per_part):
        k = (jb % blocks_per_part) * MXU_COLS // S5_STATE * S5_CH // LANES
        xs_ref[:, jb * MXU_COLS:(jb + 1) * MXU_COLS] = jnp.dot(
            ub[:, k * LANES:(k + 1) * LANES], bmat_ref[k * LANES:(k + 1) * LANES, jb * MXU_COLS:(jb + 1) * MXU_COLS],
            preferred_element_type=F32)

    row = lax.broadcasted_iota(jnp.int32, (SUBLANES, LANES), 0)

    def cmul_add(ar, ai, xr, xi, br, bi):
        return (ar * xr - ai * xi) + br, (ar * xi + ai * xr) + bi

    def scan_lanes(lb0):
        cols = [(slice(lb * LANES, (lb + 1) * LANES), slice(n + lb * LANES, n + (lb + 1) * LANES))
                for lb in range(lb0, lb0 + S5_SCAN_LANE_BLOCKS)]
        a = [(cfr_ref[0, :, cr_], cfi_ref[0, :, cr_]) for cr_, _ in cols]

        def recurrence(init, store):
            def step(t, x):
                base = pl.multiple_of(t * SUBLANES, SUBLANES)
                new = []
                for q, (cr_, ci_) in enumerate(cols):
                    nr, ni = cmul_add(a[q][0], a[q][1], x[2 * q], x[2 * q + 1],
                                      xs_ref[pl.ds(base, SUBLANES), cr_], xs_ref[pl.ds(base, SUBLANES), ci_])
                    if store:
                        xs_ref[pl.ds(base, SUBLANES), cr_] = nr
                        xs_ref[pl.ds(base, SUBLANES), ci_] = ni
                    new += [nr, ni]
                return tuple(new)
            return lax.fori_loop(0, tc, step, tuple(init), unroll=S5_SCAN_UNROLL)

        zero = jnp.zeros((SUBLANES, LANES), F32)
        ends = recurrence([zero] * (2 * len(cols)), store=False)

        starts = []
        for q, (cr_, ci_) in enumerate(cols):
            er, ei = ends[2 * q], ends[2 * q + 1]
            wr = jnp.where(row == 0, cin_ref[:, cr_], pltpu.roll(er, 1, 0))
            wi = jnp.where(row == 0, cin_ref[:, ci_], pltpu.roll(ei, 1, 0))
            for k, s in ((2, 1), (3, 2), (4, 4)):
                wr, wi = cmul_add(cfr_ref[k, :, cr_], cfi_ref[k, :, cr_],
                                  pltpu.roll(wr, s, 0), pltpu.roll(wi, s, 0), wr, wi)
            tr, ti = cmul_add(cfr_ref[1, :, cr_], cfi_ref[1, :, cr_], wr, wi, er, ei)
            cin_ref[:, cr_] = pltpu.roll(tr, 1, 0)
            cin_ref[:, ci_] = pltpu.roll(ti, 1, 0)
            starts += [wr, wi]
        recurrence(starts, store=True)

    for lb0 in range(0, n // LANES, S5_SCAN_LANE_BLOCKS):
        scan_lanes(lb0)

    ys = []
    for m in range(GROUP_W // MXU_COLS):
        k0 = m * MXU_COLS // S5_CH * S5_STATE
        k1 = (m + 1) * MXU_COLS // S5_CH * S5_STATE
        cols = slice(m * MXU_COLS, (m + 1) * MXU_COLS)
        ys.append(jnp.dot(xs_ref[:, k0:k1].astype(BF16), cmat_ref[k0:k1, cols], preferred_element_type=F32)
                  + jnp.dot(xs_ref[:, n + k0:n + k1].astype(BF16), cmat_ref[n + k0:n + k1, cols],
                            preferred_element_type=F32))
    y = jnp.concatenate(ys, axis=-1) + d_ref[...] * uperm_ref[...]
    g = jax.nn.gelu(y)
    gate = jnp.dot(g.astype(BF16), wglu_ref[...].astype(BF16), preferred_element_type=F32)
    out = g * jax.nn.sigmoid(gate)
    for j in range(len(u_refs)):
        yp_ref[j] = out[:, j * LANES:(j + 1) * LANES]

    tiles_per_chunk = tc // SUBLANES
    shift = tiles_per_chunk.bit_length() - 1

    def scatter(idx, carry):
        c = idx >> shift
        m = idx & (tiles_per_chunk - 1)
        r = pl.multiple_of(idx * SUBLANES, SUBLANES)
        src = m * (SUBLANES * SUBLANES) + c
        for j in range(len(u_refs)):
            o_ref[pl.ds(r, SUBLANES), j * LANES:(j + 1) * LANES] = yp_ref[j, pl.ds(src, SUBLANES, stride=SUBLANES), :]
        return carry

    lax.fori_loop(0, SUBLANES * tiles_per_chunk, scatter, 0)


def s5_mixer(proj, bmat, cmat, coef_r, coef_i, d_skip, w_glu, layer, bsz, seq, tl=S5_BLOCK):
    nl = seq // tl
    n2 = 2 * S5_NSTATE
    tc = tl // SUBLANES
    assert tc % SUBLANES == 0 and tc & (tc - 1) == 0
    nu = GROUP_W // LANES
    uspecs = [pl.BlockSpec((tl, LANES), lambda b, l, j=j: (b * nl + l, j)) for j in range(nu)]
    coef = pl.BlockSpec((None, S5_COEF_TILES, SUBLANES, S5_NSTATE), lambda b, l: (layer, 0, 0, 0))
    return pl.pallas_call(
        functools.partial(_s5_kernel, tl=tl),
        grid=(bsz, nl),
        in_specs=uspecs + [
            pl.BlockSpec((None, GROUP_W, n2), lambda b, l: (layer, 0, 0)),
            pl.BlockSpec((None, n2, GROUP_W), lambda b, l: (layer, 0, 0)),
            coef, coef,
            pl.BlockSpec((None, 1, GROUP_W), lambda b, l: (layer, 0, 0)),
            pl.BlockSpec((None, GROUP_W, GROUP_W), lambda b, l: (layer, 0, 0)),
        ],
        out_specs=pl.BlockSpec((tl, GROUP_W), lambda b, l: (b * nl + l, 0)),
        out_shape=jax.ShapeDtypeStruct((bsz * seq, GROUP_W), F32),
        scratch_shapes=[
            pltpu.VMEM((nu, SUBLANES * (tc + SUBLANES), LANES), F32),
            pltpu.VMEM((tl, GROUP_W), F32),
            pltpu.VMEM((tl, n2), F32),
            pltpu.VMEM((SUBLANES, n2), F32),
            pltpu.VMEM((nu, tl, LANES), F32),
        ],
        compiler_params=_params("parallel", "arbitrary"),
        name="s5_mixer",
    )(*([proj] * nu), bmat, cmat, coef_r, coef_i, d_skip, w_glu)


def _t5_bucket(dist):
    n = np.maximum(dist, 0)
    max_exact = REL_BUCKETS // 2
    large = max_exact + (np.log(np.maximum(n, 1) / max_exact) / np.log(REL_MAX_DIST / max_exact)
                         * (REL_BUCKETS - max_exact)).astype(np.int64)
    large = np.minimum(large, REL_BUCKETS - 1)
    return np.where(n < max_exact, n, large).astype(np.int32)


ATT_NEAR_BLOCKS = 5
ATT_NEAR = ATT_NEAR_BLOCKS * ATT_BLOCK
ATT_FAR_DIL = 16


def _attention_bias_layout(sub):
    a = np.arange(ATT_BLOCK)[:, None]
    d_near = (ATT_NEAR - ATT_BLOCK) + a - np.arange(ATT_NEAR)[None, :]
    mult = np.zeros(d_near.shape, np.int64)
    for window, dil in DILATED_PATTERNS:
        if dil != ATT_FAR_DIL:
            mult += (d_near >= 0) & (d_near % dil == 0) & (d_near // dil <= window // dil)
    d_far = ATT_FAR_DIL * (a - np.arange(sub)[None, :])
    mult_far = ((d_far >= 0) & (d_far // ATT_FAR_DIL <= ATT_BLOCK)).astype(np.int64)
    d = np.concatenate([d_near, d_far], axis=1)
    mult = np.concatenate([mult, mult_far], axis=1)
    addend = np.where(mult > 0, np.log(np.maximum(mult, 1)), NEG_INF).astype(np.float32)
    return _t5_bucket(d), addend


def _attn_bias_kernel(rb_ref, bucket_ref, add_ref, o_ref):
    h = pl.program_id(0)
    bucket = bucket_ref[...]
    t = jnp.zeros(bucket.shape, F32)
    for b in range(REL_BUCKETS):
        t = jnp.where(bucket == b, rb_ref[b, h], t)
    o_ref[...] = (t + add_ref[...]) * LOG2E


def attention_bias_table(rel_bias, seq):
    sub = seq // ATT_FAR_DIL
    bucket, addend = _attention_bias_layout(sub)
    width = ATT_NEAR + sub
    full = pl.BlockSpec((ATT_BLOCK, width), lambda h: (0, 0))
    return pl.pallas_call(
        _attn_bias_kernel,
        grid=(ATT_HEADS,),
        in_specs=[pl.BlockSpec(memory_space=pltpu.SMEM), full, full],
        out_specs=pl.BlockSpec((None, ATT_BLOCK, width), lambda h: (h, 0, 0)),
        out_shape=jax.ShapeDtypeStruct((ATT_HEADS, ATT_BLOCK, width), F32),
        compiler_params=_params("arbitrary"),
        name="attn_bias",
    )(rel_bias, jnp.asarray(bucket), jnp.asarray(addend))


ATT_HEADS_PER_BLOCK = LANES // ATT_HEAD_DIM
ATT_GROUP = 8


def _nt_dot(a, b):
    return lax.dot_general(a, b, (((1,), (1,)), ((), ())), preferred_element_type=F32)


def _attn_kernel(q_ref, k_ref, v_ref, bt_ref, *rest, seq, n_cast):
    cast_in, o_ref, cast_out = rest[:n_cast], rest[n_cast], rest[n_cast + 1:2 * n_cast + 1]
    kb_ref, vb_ref, m3_ref, l3_ref, a3_ref = rest[2 * n_cast + 1:]
    sub = seq // ATT_FAR_DIL
    pad = ATT_NEAR - ATT_BLOCK
    scale = ATT_HEAD_DIM ** -0.5 * LOG2E
    lane = lax.broadcasted_iota(jnp.int32, (1, LANES), 1)
    in_head = [(lane >= hh * ATT_HEAD_DIM) & (lane < (hh + 1) * ATT_HEAD_DIM)
               for hh in range(ATT_HEADS_PER_BLOCK)]

    def by_head(vals):
        out = vals[-1]
        for hh in range(ATT_HEADS_PER_BLOCK - 2, -1, -1):
            out = jnp.where(in_head[hh], vals[hh], out)
        return out

    def softmax_pv(tiles):
        heads = range(ATT_HEADS_PER_BLOCK)
        s = [[_nt_dot(jnp.where(in_head[hh], qs, 0.0).astype(BF16), kb) + bias_of(hh) for hh in heads]
             for qs, kb, _, bias_of in tiles]
        m = [[jnp.max(s_h, axis=-1, keepdims=True) for s_h in s_t] for s_t in s]
        p = [[jnp.exp2(s_h - m_h) for s_h, m_h in zip(s_t, m_t)] for s_t, m_t in zip(s, m)]
        l = [[jnp.sum(p_h, axis=-1, keepdims=True) for p_h in p_t] for p_t in p]
        acc = [[jnp.dot(p_h.astype(BF16), tile[2], preferred_element_type=F32) for p_h in p_t]
               for p_t, tile in zip(p, tiles)]
        out = []
        for tile, m_t, l_t, acc_t in zip(tiles, m, l, acc):
            full = (tile[0].shape[0], LANES)
            out.append((by_head([jnp.broadcast_to(x, full) for x in m_t]),
                        by_head([jnp.broadcast_to(x, full) for x in l_t]), by_head(acc_t)))
        return out

    for src, dst in zip(cast_in, cast_out):
        dst[...] = src[...].astype(BF16)

    kb_ref[...] = k_ref[...].astype(BF16)
    vb_ref[...] = v_ref[...].astype(BF16)

    for g0 in range(0, ATT_FAR_DIL, ATT_GROUP):
        group = [pl.ds(r, sub, stride=ATT_FAR_DIL) for r in range(g0, g0 + ATT_GROUP)]
        far_bias = lambda hh: bt_ref[hh, 0:sub, ATT_NEAR:ATT_NEAR + sub]
        tiles = [(q_ref[rows, :] * scale, k_ref[rows, :].astype(BF16), v_ref[rows, :].astype(BF16), far_bias)
                 for rows in group]
        for rows, (m, l, acc) in zip(group, softmax_pv(tiles)):
            m3_ref[rows, :] = m
            l3_ref[rows, :] = l
            a3_ref[rows, :] = acc

    nblocks = seq // ATT_BLOCK
    for g0 in range(0, nblocks, ATT_GROUP):
        group, tiles = [], []
        for i in range(g0, min(g0 + ATT_GROUP, nblocks)):
            r0 = i * ATT_BLOCK
            lo = max(0, r0 - pad)
            c0 = ATT_NEAR - (r0 + ATT_BLOCK - lo)
            group.append(slice(r0, r0 + ATT_BLOCK))
            tiles.append((q_ref[group[-1], :] * scale, kb_ref[lo:r0 + ATT_BLOCK, :], vb_ref[lo:r0 + ATT_BLOCK, :],
                          lambda hh, c0=c0: bt_ref[hh, :, c0:ATT_NEAR]))
        merged = []
        for rows, (m, l, acc) in zip(group, softmax_pv(tiles)):
            m3 = m3_ref[rows, :]
            mm = jnp.maximum(m, m3)
            wn = jnp.exp2(m - mm)
            wf = jnp.exp2(m3 - mm)
            num = acc * wn + a3_ref[rows, :] * wf
            den = l * wn + l3_ref[rows, :] * wf
            merged.append(num / den)
        for rows, o in zip(group, merged):
            o_ref[rows, :] = o


def dilated_attention(proj, bias_table, casts, bsz, seq):
    assert seq % ATT_BLOCK == 0 and seq // ATT_FAR_DIL <= ATT_BLOCK
    nhp = ATT_HEADS // ATT_HEADS_PER_BLOCK
    steps = bsz * nhp
    qcol = 4 * GROUP_W // LANES
    kcol = 5 * GROUP_W // LANES
    vcol = 6 * GROUP_W // LANES
    width = bias_table.shape[-1]
    stat = pltpu.VMEM((seq, LANES), F32)
    half = pltpu.VMEM((seq, LANES), BF16)
    cast_in, cast_out, cast_shapes = [], [], []
    for w, layer in casts:
        rows, cols = w.shape[1:]
        assert rows % steps == 0
        cast_in.append(pl.BlockSpec((None, rows // steps, cols), lambda b, h, layer=layer: (layer, b * nhp + h, 0)))
        cast_out.append(pl.BlockSpec((rows // steps, cols), lambda b, h: (b * nhp + h, 0)))
        cast_shapes.append(jax.ShapeDtypeStruct((rows, cols), BF16))
    return pl.pallas_call(
        functools.partial(_attn_kernel, seq=seq, n_cast=len(casts)),
        grid=(bsz, nhp),
        in_specs=[
            pl.BlockSpec((seq, LANES), lambda b, h: (b, qcol + h)),
            pl.BlockSpec((seq, LANES), lambda b, h: (b, kcol + h)),
            pl.BlockSpec((seq, LANES), lambda b, h: (b, vcol + h)),
            pl.BlockSpec((ATT_HEADS_PER_BLOCK, ATT_BLOCK, width), lambda b, h: (h, 0, 0)),
        ] + cast_in,
        out_specs=[pl.BlockSpec((seq, LANES), lambda b, h: (b, h))] + cast_out,
        out_shape=[jax.ShapeDtypeStruct((bsz * seq, GROUP_W), F32)] + cast_shapes,
        scratch_shapes=[half, half, stat, stat, stat],
        compiler_params=_params("parallel", "parallel"),
        name="dilated_attention",
    )(proj, proj, proj, bias_table, *[w for w, _ in casts])


def _mix_out_kernel(ya_ref, yb_ref, yc_ref, yd_ref, g_ref, w_ref, h_ref, o_ref, wb_ref):
    @pl.when(pl.program_id(0) == 0)
    def _():
        wb_ref[...] = w_ref[...].astype(BF16)

    yn = [_rms(y_ref[...], g_ref[:, gi * GROUP_W:(gi + 1) * GROUP_W]).astype(BF16)
          for gi, y_ref in enumerate((ya_ref, yb_ref, yc_ref, yd_ref))]
    o_ref[...] = h_ref[...] + jnp.dot(jnp.concatenate(yn, axis=-1), wb_ref[...], preferred_element_type=F32)


def mix_out(ya, yb, yc, yd, g, w, h, layer, tm=512):
    m, d = h.shape
    yspec = pl.BlockSpec((tm, GROUP_W), lambda i: (i, 0))
    return pl.pallas_call(
        _mix_out_kernel,
        grid=(m // tm,),
        in_specs=[
            yspec, yspec, yspec, yspec,
            pl.BlockSpec((None, 1, d), lambda i: (layer, 0, 0)),
            pl.BlockSpec((None, d, d), lambda i: (layer, 0, 0), pipeline_mode=pl.Buffered(1)),
            pl.BlockSpec((tm, d), lambda i: (i, 0)),
        ],
        out_specs=pl.BlockSpec((tm, d), lambda i: (i, 0)),
        out_shape=jax.ShapeDtypeStruct((m, d), F32),
        scratch_shapes=[pltpu.VMEM((d, d), BF16)],
        compiler_params=_params("arbitrary"),
        name="mix_out",
    )(ya, yb, yc, yd, g, w, h)


def _mem_kv_kernel(mem_ref, g_ref, wk_ref, wv_ref, k_ref, v_ref):
    mn = _rms(mem_ref[...], g_ref[...]).astype(BF16)
    k_ref[...] = jnp.dot(mn, wk_ref[...].astype(BF16), preferred_element_type=F32).astype(BF16)
    v_ref[...] = jnp.dot(mn, wv_ref[...].astype(BF16), preferred_element_type=F32).astype(BF16)


def mem_kv(mem, g, wk, wv):
    m, d = mem.shape
    depth = wk.shape[0]
    wspec = pl.BlockSpec((None, d, X_WIDTH), lambda l: (l, 0, 0))
    ospec = pl.BlockSpec((None, m, X_WIDTH), lambda l: (l, 0, 0))
    return pl.pallas_call(
        _mem_kv_kernel,
        grid=(depth,),
        in_specs=[pl.BlockSpec((m, d), lambda l: (0, 0)), pl.BlockSpec((1, d), lambda l: (0, 0)),
                  wspec, wspec],
        out_specs=[ospec, ospec],
        out_shape=[jax.ShapeDtypeStruct((depth, m, X_WIDTH), BF16)] * 2,
        compiler_params=_params("arbitrary"),
        name="mem_kv",
    )(mem, g, wk, wv)


def _xattn_kernel(h_ref, g_ref, wq_ref, k_ref, v_ref, wo_ref, o_ref, wqb_ref, wob_ref):
    @pl.when((pl.program_id(0) == 0) & (pl.program_id(1) == 0))
    def _():
        wqb_ref[...] = wq_ref[...].astype(BF16)
        wob_ref[...] = wo_ref[...].astype(BF16)

    h = h_ref[...]
    hn = _rms(h, g_ref[...]).astype(BF16)
    q = jnp.dot(hn, wqb_ref[...], preferred_element_type=F32) * (X_HEAD_DIM ** -0.5 * LOG2E)
    cols = [slice(hd * X_HEAD_DIM, (hd + 1) * X_HEAD_DIM) for hd in range(X_HEADS)]
    qb = q.astype(BF16)
    s = [_nt_dot(qb[:, c], k_ref[:, c]) for c in cols]
    m = [jnp.max(x, axis=-1, keepdims=True) for x in s]
    p = [jnp.exp2(x - mx) for x, mx in zip(s, m)]
    den = [jnp.sum(x, axis=-1, keepdims=True) for x in p]
    p = [x / dx for x, dx in zip(p, den)]
    outs = [jnp.dot(x.astype(BF16), v_ref[:, c], preferred_element_type=F32) for x, c in zip(p, cols)]
    o = jnp.concatenate(outs, axis=-1).astype(BF16)
    o_ref[...] = h + jnp.dot(o, wob_ref[...], preferred_element_type=F32)


def cross_attention(h, g, wq, k, v, wo, layer, bsz, seq, tm=512):
    d = h.shape[-1]
    nl = seq // tm
    mlen = k.shape[1] // bsz
    return pl.pallas_call(
        _xattn_kernel,
        grid=(bsz, nl),
        in_specs=[
            pl.BlockSpec((tm, d), lambda b, l: (b * nl + l, 0)),
            pl.BlockSpec((None, 1, d), lambda b, l: (layer, 0, 0)),
            pl.BlockSpec((None, d, X_WIDTH), lambda b, l: (layer, 0, 0)),
            pl.BlockSpec((None, mlen, X_WIDTH), lambda b, l: (layer, b, 0)),
            pl.BlockSpec((None, mlen, X_WIDTH), lambda b, l: (layer, b, 0)),
            pl.BlockSpec((None, X_WIDTH, d), lambda b, l: (layer, 0, 0)),
        ],
        out_specs=pl.BlockSpec((tm, d), lambda b, l: (b * nl + l, 0)),
        out_shape=jax.ShapeDtypeStruct(h.shape, F32),
        scratch_shapes=[pltpu.VMEM((d, X_WIDTH), BF16), pltpu.VMEM((X_WIDTH, d), BF16)],
        compiler_params=_params("arbitrary", "arbitrary"),
        name="cross_attention",
    )(h, g, wq, k, v, wo)


def _mlp_kernel(h_ref, g_ref, wu_ref, wd_ref, *rest, out_norm):
    o_ref, hn_ref = rest[-2:]
    j = pl.program_id(1)

    @pl.when(j == 0)
    def _():
        h = h_ref[...]
        hn_ref[...] = _rms(h, g_ref[...]).astype(BF16)
        o_ref[...] = h

    a = jnp.dot(hn_ref[...], wu_ref[...], preferred_element_type=F32)
    a = jnp.square(jnp.maximum(a, 0.0)).astype(BF16)
    o_ref[...] += jnp.dot(a, wd_ref[...], preferred_element_type=F32)

    if out_norm:
        @pl.when(j == pl.num_programs(1) - 1)
        def _():
            o_ref[...] = _rms(o_ref[...], rest[0][...])


def mlp(h, g, wu, wd, layer, out_norm_g=None, tm=1024, tf=512):
    m, d = h.shape
    f = wu.shape[-1]
    out_norm = out_norm_g is not None
    extra_specs = [pl.BlockSpec((1, d), lambda i, j: (0, 0))] if out_norm else []
    extra_args = (out_norm_g,) if out_norm else ()
    return pl.pallas_call(
        functools.partial(_mlp_kernel, out_norm=out_norm),
        grid=(m // tm, f // tf),
        in_specs=[
            pl.BlockSpec((tm, d), lambda i, j: (i, 0)),
            pl.BlockSpec((None, 1, d), lambda i, j: (layer, 0, 0)),
            pl.BlockSpec((d, tf), lambda i, j: (0, j)),
            pl.BlockSpec((tf, d), lambda i, j: (j, 0)),
        ] + extra_specs,
        out_specs=pl.BlockSpec((tm, d), lambda i, j: (i, 0)),
        out_shape=jax.ShapeDtypeStruct((m, d), F32),
        scratch_shapes=[pltpu.VMEM((tm, d), BF16)],
        compiler_params=_params("parallel", "arbitrary"),
        name="mlp",
    )(h, g, wu, wd, *extra_args)


def kernel(x, mem, rel_bias, mem_norm_g, norm_mix_g, w_in, s5_lam_re, s5_lam_im, s5_log_dt, s5_b_re, s5_b_im, s5_c_re, s5_c_im, s5_d, s5_w_glu, pool_w, pool_scale, conv_w_dw, conv_b_dw, conv_ln_g, conv_ln_b, conv_w_pw, grp_norm_g, w_out, norm_x_g, w_xq, w_xk, w_xv, w_xo, norm_mlp_g, w_up, w_down, norm_final_g):
    bsz, seq, d = x.shape
    depth = w_in.shape[0]
    vec = lambda a: a[:, None, :]

    k_mem, v_mem = mem_kv(mem.reshape(bsz * mem.shape[1], d), mem_norm_g[None, :], w_xk, w_xv)
    pow_r, pow_i, bmat, cmat = s5_params(s5_lam_re, s5_lam_im, s5_log_dt, s5_b_re, s5_b_im, s5_c_re, s5_c_im,
                                         chunk=S5_BLOCK // SUBLANES)
    coef_r, coef_i = s5_scan_coefficients(pow_r, pow_i)
    bias_table = attention_bias_table(rel_bias, seq)

    norm_mix, s5_dv, pool_sc = vec(norm_mix_g), vec(s5_d), vec(pool_scale)
    b_dw, ln_g, ln_b = vec(conv_b_dw), vec(conv_ln_g), vec(conv_ln_b)
    grp_g, norm_x, norm_mlp = vec(grp_norm_g), vec(norm_x_g), vec(norm_mlp_g)

    h = x.reshape(bsz * seq, d)
    w_in_b = w_in[0].astype(BF16)
    for l in range(depth):
        proj, y_c, y_b = proj_conv(h, norm_mix, w_in_b, conv_w_dw, b_dw, ln_g, ln_b, conv_w_pw,
                                   pool_w, pool_sc, l, bsz, seq)
        y_a = s5_mixer(proj, bmat, cmat, coef_r, coef_i, s5_dv, s5_w_glu, l, bsz, seq)
        casts = [(w_up, l), (w_down, l)] + ([(w_in, l + 1)] if l + 1 < depth else [])
        y_d, w_up_b, w_down_b, *nxt = dilated_attention(proj, bias_table, casts, bsz, seq)
        w_in_b = nxt[0] if nxt else None
        h = mix_out(y_a, y_b, y_c, y_d, grp_g, w_out, h, l)
        h = cross_attention(h, norm_x, w_xq, k_mem, v_mem, w_xo, l, bsz, seq)
        h = mlp(h, norm_mlp, w_up_b, w_down_b, l,
                out_norm_g=norm_final_g[None, :] if l == depth - 1 else None)
    return h.reshape(bsz, seq, d)
```

```python
import functools
import math

import numpy as np
import jax
import jax.numpy as jnp
from jax import lax
from jax.experimental import pallas as pl
from jax.experimental.pallas import tpu as pltpu

F32 = jnp.float32
BF16 = jnp.bfloat16

GROUP_W = 512
S5_GROUPS = 32
S5_CH = 16
S5_STATE = 64
S5_NSTATE = S5_GROUPS * S5_STATE
S5_SCAN_LANE_BLOCKS = 8
S5_BLOCK = 512
S5_SCAN_UNROLL = 4
POOL_WINDOWS = (2, 4, 8, 16)
POOL_CH = 128
CONV_WIDTH = 31
ATT_HEADS = 8
ATT_HEAD_DIM = 64
DILATED_PATTERNS = ((128, 1), (512, 4), (2048, 16))
ATT_BLOCK = 128
REL_BUCKETS = 32
REL_MAX_DIST = 2048
X_HEADS = 4
X_HEAD_DIM = 128
X_WIDTH = 512
NORM_EPS = 1e-6
NEG_INF = -1e30
LOG2E = math.log2(math.e)

SUBLANES = 8
LANES = 128
VMEM_LIMIT = 58 * 1024 * 1024


def _params(*sem):
    return pltpu.CompilerParams(dimension_semantics=sem, vmem_limit_bytes=VMEM_LIMIT)


def _rms(x, g):
    return x * lax.rsqrt(jnp.mean(x * x, axis=-1, keepdims=True) + NORM_EPS) * g


MXU_COLS = 256
POOL_HALO = 16
CONV_HALO = 32
CONV_ROWS = 32


def _proj_conv_kernel(x_ref, g_ref, w_ref, wdw_ref, bdw_ref, lng_ref, lnb_ref, wpw_ref, pw_ref, psc_ref,
                      proj_ref, yc_ref, yb_ref, sh_ref, cv_ref, wb_ref, xn_ref, ext_ref, *, tl, nl):
    step = pl.program_id(0)
    n = tl + CONV_HALO

    @pl.when(step == 0)
    def _():
        sh_ref[0] = jnp.zeros(sh_ref.shape[1:], F32)
        ext_ref[...] = jnp.zeros(ext_ref.shape, F32)

    @pl.when(lax.rem(step - 1, nl) == 0)
    def _():
        sh_ref[0, 0:CONV_HALO, :] = jnp.zeros((CONV_HALO, GROUP_W), F32)
        ext_ref[0:POOL_HALO, :] = jnp.zeros((POOL_HALO, GROUP_W), F32)

    sh_ref[0, n:, :] = jnp.zeros((SUBLANES, GROUP_W), F32)
    for s in range(1, SUBLANES):
        sh_ref[s, 0:n, :] = sh_ref[0, s:s + n, :]

    for k in range(CONV_WIDTH):
        wb_ref[k] = jnp.broadcast_to(wdw_ref[k:k + 1, :], (SUBLANES, GROUP_W))
    wb_ref[CONV_WIDTH] = jnp.broadcast_to(bdw_ref[...], (SUBLANES, GROUP_W))

    xn_ref[...] = _rms(x_ref[...], g_ref[...]).astype(BF16)
    slice_cols = 2 * MXU_COLS
    n_slices = proj_ref.shape[-1] // slice_cols
    n_chunks = tl // CONV_ROWS
    per_slice = n_chunks // n_slices
    tiles = CONV_ROWS // SUBLANES
    off = CONV_HALO - (CONV_WIDTH - 1)

    def conv_chunk(c):
        r0 = c * CONV_ROWS
        acc = jnp.broadcast_to(wb_ref[CONV_WIDTH][None], (tiles, SUBLANES, GROUP_W))
        for k in range(CONV_WIDTH):
            q, s = divmod(off + k, SUBLANES)
            x = sh_ref[s, r0 + q * SUBLANES:r0 + q * SUBLANES + CONV_ROWS, :]
            acc = acc + wb_ref[k][None] * x.reshape(tiles, SUBLANES, GROUP_W)
        cv_ref[r0:r0 + CONV_ROWS, :] = acc.reshape(CONV_ROWS, GROUP_W)

    def pool_tile():
        t = lax.rem(jnp.maximum(step - 1, 0), nl) * tl + lax.broadcasted_iota(jnp.int32, (tl, 1), 0)
        for gi, w in enumerate(POOL_WINDOWS):
            pc = slice(gi * POOL_CH, (gi + 1) * POOL_CH)
            acc = ext_ref[POOL_HALO:POOL_HALO + tl, pc]
            for k in range(1, w):
                acc = acc + ext_ref[POOL_HALO - k:POOL_HALO - k + tl, pc]
            p = acc / jnp.minimum(t + 1, w).astype(F32) - ext_ref[POOL_HALO:POOL_HALO + tl, pc]
            y = jnp.dot(p.astype(BF16), pw_ref[gi].astype(BF16), preferred_element_type=F32)
            yb_ref[:, pc] = y * psc_ref[:, pc]

    for j in range(n_slices):
        cols = slice(j * slice_cols, (j + 1) * slice_cols)
        proj_ref[:, cols] = jnp.dot(xn_ref[...], w_ref[:, cols], preferred_element_type=F32)
        if j == 0:
            pool_tile()
        for c in range(j * per_slice, (j + 1) * per_slice):
            conv_chunk(c)
        if per_slice:
            r0 = ((j + 1) * per_slice - 1) * CONV_ROWS
            zero = pltpu.bitcast(lax.shift_right_logical(
                pltpu.bitcast(cv_ref[r0:r0 + 2 * SUBLANES, 0:LANES], jnp.uint32), jnp.uint32(32)), F32)
            xn_ref[0:2 * SUBLANES, 0:LANES] = xn_ref[0:2 * SUBLANES, 0:LANES] + zero.astype(BF16)
    for c in range(n_slices * per_slice, n_chunks):
        conv_chunk(c)

    h = cv_ref[...]
    hc = h - jnp.mean(h, axis=-1, keepdims=True)
    y = hc * lax.rsqrt(jnp.mean(hc * hc, axis=-1, keepdims=True) + NORM_EPS)
    y = y * lng_ref[...] + lnb_ref[...]
    y = jax.nn.silu(y)
    yc_ref[...] = jnp.dot(y.astype(BF16), wpw_ref[...].astype(BF16), preferred_element_type=F32)

    ext_ref[0:POOL_HALO, :] = ext_ref[tl:tl + POOL_HALO, :]
    ext_ref[POOL_HALO:, :] = proj_ref[:, GROUP_W:2 * GROUP_W]
    sh_ref[0, 0:CONV_HALO, :] = sh_ref[0, tl:n, :]
    sh_ref[0, CONV_HALO:n, :] = (proj_ref[:, 2 * GROUP_W:3 * GROUP_W]
                                 * jax.nn.sigmoid(proj_ref[:, 3 * GROUP_W:4 * GROUP_W]))


def proj_conv(x, g, w, w_dw, b_dw, ln_g, ln_b, w_pw, pool_w, pool_scale, layer, bsz, seq, tl=512):
    m, d = x.shape
    n = w.shape[-1]
    nl = seq // tl
    nt = m // tl
    vec = pl.BlockSpec((None, 1, GROUP_W), lambda s: (layer, 0, 0))
    cur = lambda s: (jnp.minimum(s, nt - 1), 0)
    prev = lambda s: (jnp.maximum(s - 1, 0), 0)
    return pl.pallas_call(
        functools.partial(_proj_conv_kernel, tl=tl, nl=nl),
        grid=(nt + 1,),
        in_specs=[
            pl.BlockSpec((tl, d), cur),
            pl.BlockSpec((None, 1, d), lambda s: (layer, 0, 0)),
            pl.BlockSpec((d, n), lambda s: (0, 0), pipeline_mode=pl.Buffered(1)),
            pl.BlockSpec((None, CONV_WIDTH, GROUP_W), lambda s: (layer, 0, 0)),
            vec, vec, vec,
            pl.BlockSpec((None, GROUP_W, GROUP_W), lambda s: (layer, 0, 0)),
            pl.BlockSpec((None, len(POOL_WINDOWS), POOL_CH, POOL_CH), lambda s: (layer, 0, 0, 0)),
            vec,
        ],
        out_specs=[pl.BlockSpec((tl, n), cur), pl.BlockSpec((tl, GROUP_W), prev),
                   pl.BlockSpec((tl, GROUP_W), prev)],
        out_shape=[jax.ShapeDtypeStruct((m, n), F32), jax.ShapeDtypeStruct((m, GROUP_W), F32),
                   jax.ShapeDtypeStruct((m, GROUP_W), F32)],
        scratch_shapes=[pltpu.VMEM((SUBLANES, CONV_HALO + tl + SUBLANES, GROUP_W), F32),
                        pltpu.VMEM((tl, GROUP_W), F32),
                        pltpu.VMEM((CONV_WIDTH + 1, SUBLANES, GROUP_W), F32),
                        pltpu.VMEM((tl, d), BF16),
                        pltpu.VMEM((POOL_HALO + tl, GROUP_W), F32)],
        compiler_params=_params("arbitrary"),
        name="proj_conv",
    )(x, g, w, w_dw, b_dw, ln_g, ln_b, w_pw, pool_w, pool_scale)


def _s5_param_kernel(lr_ref, li_ref, ldt_ref, lrx_ref, lix_ref, ldtx_ref, br_ref, bi_ref, cr_ref, ci_ref,
                     pr_ref, pi_ref, bmat_ref, cmat_ref, *, chunk):
    def discretise(lr, li, ldt):
        dt = jnp.exp(ldt)
        mag = jnp.exp(lr * dt)
        return mag * jnp.cos(li * dt), mag * jnp.sin(li * dt)

    cr, ci = discretise(lr_ref[...], li_ref[...], ldt_ref[...])
    slot = {1: 0, chunk: 1, 2 * chunk: 2, 4 * chunk: 3}
    e = 1
    while True:
        if e in slot:
            pr_ref[slot[e]] = cr
            pi_ref[slot[e]] = ci
        if e == 4 * chunk:
            break
        cr, ci = cr * cr - ci * ci, 2.0 * (cr * ci)
        e *= 2

    lr, li = lrx_ref[...], lix_ref[...]
    ar, ai = discretise(lr, li, ldtx_ref[...])
    den = lr * lr + li * li
    nr, ni = ar - 1.0, ai
    f_r = (nr * lr + ni * li) / den
    f_i = (ni * lr - nr * li) / den
    b_r, b_i = br_ref[...], bi_ref[...]
    bb_r = f_r * b_r - f_i * b_i
    bb_i = f_r * b_i + f_i * b_r

    def block_diagonal(x, row_shift, col_shift):
        rows, w = x.shape
        cols = (rows >> row_shift) << col_shift
        sel = ((lax.broadcasted_iota(jnp.int32, (w, cols), 1) & (w - 1))
               == lax.broadcasted_iota(jnp.int32, (w, cols), 0))
        tiled = jnp.dot(x.astype(BF16), jnp.where(sel, 1.0, 0.0).astype(BF16), preferred_element_type=F32)
        same = ((lax.broadcasted_iota(jnp.int32, (rows, cols), 0) >> row_shift)
                == (lax.broadcasted_iota(jnp.int32, (rows, cols), 1) >> col_shift))
        return jnp.where(same, tiled, 0.0).astype(BF16)

    ch_shift, st_shift = S5_CH.bit_length() - 1, S5_STATE.bit_length() - 1
    bmat_ref[:, 0:S5_NSTATE] = block_diagonal(bb_r, ch_shift, st_shift)
    bmat_ref[:, S5_NSTATE:] = block_diagonal(bb_i, ch_shift, st_shift)
    cmat_ref[0:S5_NSTATE, :] = block_diagonal(cr_ref[...], st_shift, ch_shift)
    cmat_ref[S5_NSTATE:, :] = block_diagonal(-ci_ref[...], st_shift, ch_shift)


S5_POWERS = 4


def s5_params(lam_re, lam_im, log_dt, b_re, b_im, c_re, c_im, chunk):
    depth = lam_re.shape[0]
    gc = S5_GROUPS * S5_CH
    rep = lambda a: jnp.repeat(a, S5_CH, axis=1)
    ldt = log_dt[..., None]
    bt = lambda b: jnp.transpose(b, (0, 1, 3, 2)).reshape(depth, gc, S5_STATE)
    ct = lambda c: jnp.transpose(c, (0, 1, 3, 2)).reshape(depth, S5_NSTATE, S5_CH)
    small = pl.BlockSpec((None, S5_GROUPS, S5_STATE), lambda l: (l, 0, 0))
    small1 = pl.BlockSpec((None, S5_GROUPS, 1), lambda l: (l, 0, 0))
    big = pl.BlockSpec((None, gc, S5_STATE), lambda l: (l, 0, 0))
    big1 = pl.BlockSpec((None, gc, 1), lambda l: (l, 0, 0))
    cspec = pl.BlockSpec((None, S5_NSTATE, S5_CH), lambda l: (l, 0, 0))
    powspec = pl.BlockSpec((None, S5_POWERS, S5_GROUPS, S5_STATE), lambda l: (l, 0, 0, 0))
    return pl.pallas_call(
        functools.partial(_s5_param_kernel, chunk=chunk),
        grid=(depth,),
        in_specs=[small, small, small1, big, big, big1, big, big, cspec, cspec],
        out_specs=[powspec, powspec,
                   pl.BlockSpec((None, gc, 2 * S5_NSTATE), lambda l: (l, 0, 0)),
                   pl.BlockSpec((None, 2 * S5_NSTATE, gc), lambda l: (l, 0, 0))],
        out_shape=[
            jax.ShapeDtypeStruct((depth, S5_POWERS, S5_GROUPS, S5_STATE), F32),
            jax.ShapeDtypeStruct((depth, S5_POWERS, S5_GROUPS, S5_STATE), F32),
            jax.ShapeDtypeStruct((depth, gc, 2 * S5_NSTATE), BF16),
            jax.ShapeDtypeStruct((depth, 2 * S5_NSTATE, gc), BF16),
        ],
        compiler_params=_params("arbitrary"),
        name="s5_params",
    )(lam_re, lam_im, ldt, rep(lam_re), rep(lam_im), rep(ldt), bt(b_re), bt(b_im), ct(c_re), ct(c_im))


def s5_scan_coefficients(pow_r, pow_i):
    depth = pow_r.shape[0]
    row = jnp.arange(SUBLANES)[None, :, None]

    def coef(p):
        p = p.reshape(depth, S5_POWERS, 1, S5_NSTATE)
        full = [jnp.broadcast_to(p[:, e], (depth, SUBLANES, S5_NSTATE)) for e in (0, 1)]
        masked = [jnp.where(row >= s, p[:, e], 0.0) for e, s in ((1, 1), (2, 2), (3, 4))]
        return jnp.stack(full + masked, axis=1)

    return coef(pow_r), coef(pow_i)


S5_COEF_TILES = 5


def _s5_kernel(u0_ref, u1_ref, u2_ref, u3_ref, bmat_ref, cmat_ref, cfr_ref, cfi_ref, d_ref, wglu_ref, o_ref,
               up_ref, uperm_ref, xs_ref, cin_ref, yp_ref, *, tl):
    n = S5_NSTATE
    tc = tl // SUBLANES
    pitch = tc + SUBLANES
    u_refs = (u0_ref, u1_ref, u2_ref, u3_ref)
    l = pl.program_id(1)

    @pl.when(l == 0)
    def _():
        cin_ref[...] = jnp.zeros(cin_ref.shape, F32)

    for j, u_ref in enumerate(u_refs):
        for c in range(SUBLANES):
            up_ref[j, c * pitch:c * pitch + tc, :] = u_ref[c * tc:(c + 1) * tc, :]

    def gather(t, carry):
        r = pl.multiple_of(t * SUBLANES, SUBLANES)
        for j in range(len(u_refs)):
            uperm_ref[pl.ds(r, SUBLANES), j * LANES:(j + 1) * LANES] = up_ref[j, pl.ds(t, SUBLANES, stride=pitch), :]
        return carry

    lax.fori_loop(0, tc, gather, 0)

    ub = uperm_ref[...].astype(BF16)
    blocks_per_part = n // MXU_COLS
    for jb in range(2 * blocks_per_part):
        k = (jb % blocks_per_part) * MXU_COLS // S5_STATE * S5_CH // LANES
        xs_ref[:, jb * MXU_COLS:(jb + 1) * MXU_COLS] = jnp.dot(
            ub[:, k * LANES:(k + 1) * LANES], bmat_ref[k * LANES:(k + 1) * LANES, jb * MXU_COLS:(jb + 1) * MXU_COLS],
            preferred_element_type=F32)

    row = lax.broadcasted_iota(jnp.int32, (SUBLANES, LANES), 0)

    def cmul_add(ar, ai, xr, xi, br, bi):
        return (ar * xr - ai * xi) + br, (ar * xi + ai * xr) + bi

    def scan_lanes(lb0):
        cols = [(slice(lb * LANES, (lb + 1) * LANES), slice(n + lb * LANES, n + (lb + 1) * LANES))
                for lb in range(lb0, lb0 + S5_SCAN_LANE_BLOCKS)]
        a = [(cfr_ref[0, :, cr_], cfi_ref[0, :, cr_]) for cr_, _ in cols]

        def recurrence(init, store):
            def step(t, x):
                base = pl.multiple_of(t * SUBLANES, SUBLANES)
                new = []
                for q, (cr_, ci_) in enumerate(cols):
                    nr, ni = cmul_add(a[q][0], a[q][1], x[2 * q], x[2 * q + 1],
                                      xs_ref[pl.ds(base, SUBLANES), cr_], xs_ref[pl.ds(base, SUBLANES), ci_])
                    if store:
                        xs_ref[pl.ds(base, SUBLANES), cr_] = nr
                        xs_ref[pl.ds(base, SUBLANES), ci_] = ni
                    new += [nr, ni]
                return tuple(new)
            return lax.fori_loop(0, tc, step, tuple(init), unroll=S5_SCAN_UNROLL)

        zero = jnp.zeros((SUBLANES, LANES), F32)
        ends = recurrence([zero] * (2 * len(cols)), store=False)

        starts = []
        for q, (cr_, ci_) in enumerate(cols):
            er, ei = ends[2 * q], ends[2 * q + 1]
            wr = jnp.where(row == 0, cin_ref[:, cr_], pltpu.roll(er, 1, 0))
            wi = jnp.where(row == 0, cin_ref[:, ci_], pltpu.roll(ei, 1, 0))
            for k, s in ((2, 1), (3, 2), (4, 4)):
                wr, wi = cmul_add(cfr_ref[k, :, cr_], cfi_ref[k, :, cr_],
                                  pltpu.roll(wr, s, 0), pltpu.roll(wi, s, 0), wr, wi)
            tr, ti = cmul_add(cfr_ref[1, :, cr_], cfi_ref[1, :, cr_], wr, wi, er, ei)
            cin_ref[:, cr_] = pltpu.roll(tr, 1, 0)
            cin_ref[:, ci_] = pltpu.roll(ti, 1, 0)
            starts += [wr, wi]
        recurrence(starts, store=True)

    for lb0 in range(0, n // LANES, S5_SCAN_LANE_BLOCKS):
        scan_lanes(lb0)

    ys = []
    for m in range(GROUP_W // MXU_COLS):
        k0 = m * MXU_COLS // S5_CH * S5_STATE
        k1 = (m + 1) * MXU_COLS // S5_CH * S5_STATE
        cols = slice(m * MXU_COLS, (m + 1) * MXU_COLS)
        ys.append(jnp.dot(xs_ref[:, k0:k1].astype(BF16), cmat_ref[k0:k1, cols], preferred_element_type=F32)
                  + jnp.dot(xs_ref[:, n + k0:n + k1].astype(BF16), cmat_ref[n + k0:n + k1, cols],
                            preferred_element_type=F32))
    y = jnp.concatenate(ys, axis=-1) + d_ref[...] * uperm_ref[...]
    g = jax.nn.gelu(y)
    gate = jnp.dot(g.astype(BF16), wglu_ref[...].astype(BF16), preferred_element_type=F32)
    out = g * jax.nn.sigmoid(gate)
    for j in range(len(u_refs)):
        yp_ref[j] = out[:, j * LANES:(j + 1) * LANES]

    tiles_per_chunk = tc // SUBLANES
    shift = tiles_per_chunk.bit_length() - 1

    def scatter(idx, carry):
        c = idx >> shift
        m = idx & (tiles_per_chunk - 1)
        r = pl.multiple_of(idx * SUBLANES, SUBLANES)
        src = m * (SUBLANES * SUBLANES) + c
        for j in range(len(u_refs)):
            o_ref[pl.ds(r, SUBLANES), j * LANES:(j + 1) * LANES] = yp_ref[j, pl.ds(src, SUBLANES, stride=SUBLANES), :]
        return carry

    lax.fori_loop(0, SUBLANES * tiles_per_chunk, scatter, 0)


def s5_mixer(proj, bmat, cmat, coef_r, coef_i, d_skip, w_glu, layer, bsz, seq, tl=S5_BLOCK):
    nl = seq // tl
    n2 = 2 * S5_NSTATE
    tc = tl // SUBLANES
    assert tc % SUBLANES == 0 and tc & (tc - 1) == 0
    nu = GROUP_W // LANES
    uspecs = [pl.BlockSpec((tl, LANES), lambda b, l, j=j: (b * nl + l, j)) for j in range(nu)]
    coef = pl.BlockSpec((None, S5_COEF_TILES, SUBLANES, S5_NSTATE), lambda b, l: (layer, 0, 0, 0))
    return pl.pallas_call(
        functools.partial(_s5_kernel, tl=tl),
        grid=(bsz, nl),
        in_specs=uspecs + [
            pl.BlockSpec((None, GROUP_W, n2), lambda b, l: (layer, 0, 0)),
            pl.BlockSpec((None, n2, GROUP_W), lambda b, l: (layer, 0, 0)),
            coef, coef,
            pl.BlockSpec((None, 1, GROUP_W), lambda b, l: (layer, 0, 0)),
            pl.BlockSpec((None, GROUP_W, GROUP_W), lambda b, l: (layer, 0, 0)),
        ],
        out_specs=pl.BlockSpec((tl, GROUP_W), lambda b, l: (b * nl + l, 0)),
        out_shape=jax.ShapeDtypeStruct((bsz * seq, GROUP_W), F32),
        scratch_shapes=[
            pltpu.VMEM((nu, SUBLANES * (tc + SUBLANES), LANES), F32),
            pltpu.VMEM((tl, GROUP_W), F32),
            pltpu.VMEM((tl, n2), F32),
            pltpu.VMEM((SUBLANES, n2), F32),
            pltpu.VMEM((nu, tl, LANES), F32),
        ],
        compiler_params=_params("parallel", "arbitrary"),
        name="s5_mixer",
    )(*([proj] * nu), bmat, cmat, coef_r, coef_i, d_skip, w_glu)


def _t5_bucket(dist):
    n = np.maximum(dist, 0)
    max_exact = REL_BUCKETS // 2
    large = max_exact + (np.log(np.maximum(n, 1) / max_exact) / np.log(REL_MAX_DIST / max_exact)
                         * (REL_BUCKETS - max_exact)).astype(np.int64)
    large = np.minimum(large, REL_BUCKETS - 1)
    return np.where(n < max_exact, n, large).astype(np.int32)


ATT_NEAR_BLOCKS = 5
ATT_NEAR = ATT_NEAR_BLOCKS * ATT_BLOCK
ATT_FAR_DIL = 16


def _attention_bias_layout(sub):
    a = np.arange(ATT_BLOCK)[:, None]
    d_near = (ATT_NEAR - ATT_BLOCK) + a - np.arange(ATT_NEAR)[None, :]
    mult = np.zeros(d_near.shape, np.int64)
    for window, dil in DILATED_PATTERNS:
        if dil != ATT_FAR_DIL:
            mult += (d_near >= 0) & (d_near % dil == 0) & (d_near // dil <= window // dil)
    d_far = ATT_FAR_DIL * (a - np.arange(sub)[None, :])
    mult_far = ((d_far >= 0) & (d_far // ATT_FAR_DIL <= ATT_BLOCK)).astype(np.int64)
    d = np.concatenate([d_near, d_far], axis=1)
    mult = np.concatenate([mult, mult_far], axis=1)
    addend = np.where(mult > 0, np.log(np.maximum(mult, 1)), NEG_INF).astype(np.float32)
    return _t5_bucket(d), addend


def _attn_bias_kernel(rb_ref, bucket_ref, add_ref, o_ref):
    h = pl.program_id(0)
    bucket = bucket_ref[...]
    t = jnp.zeros(bucket.shape, F32)
    for b in range(REL_BUCKETS):
        t = jnp.where(bucket == b, rb_ref[b, h], t)
    o_ref[...] = (t + add_ref[...]) * LOG2E


def attention_bias_table(rel_bias, seq):
    sub = seq // ATT_FAR_DIL
    bucket, addend = _attention_bias_layout(sub)
    width = ATT_NEAR + sub
    full = pl.BlockSpec((ATT_BLOCK, width), lambda h: (0, 0))
    return pl.pallas_call(
        _attn_bias_kernel,
        grid=(ATT_HEADS,),
        in_specs=[pl.BlockSpec(memory_space=pltpu.SMEM), full, full],
        out_specs=pl.BlockSpec((None, ATT_BLOCK, width), lambda h: (h, 0, 0)),
        out_shape=jax.ShapeDtypeStruct((ATT_HEADS, ATT_BLOCK, width), F32),
        compiler_params=_params("arbitrary"),
        name="attn_bias",
    )(rel_bias, jnp.asarray(bucket), jnp.asarray(addend))


ATT_HEADS_PER_BLOCK = LANES // ATT_HEAD_DIM
ATT_GROUP = 8


def _nt_dot(a, b):
    return lax.dot_general(a, b, (((1,), (1,)), ((), ())), preferred_element_type=F32)


def _attn_kernel(q_ref, k_ref, v_ref, bt_ref, *rest, seq, n_cast):
    cast_in, o_ref, cast_out = rest[:n_cast], rest[n_cast], rest[n_cast + 1:2 * n_cast + 1]
    kb_ref, vb_ref, m3_ref, l3_ref, a3_ref = rest[2 * n_cast + 1:]
    sub = seq // ATT_FAR_DIL
    pad = ATT_NEAR - ATT_BLOCK
    scale = ATT_HEAD_DIM ** -0.5 * LOG2E
    lane = lax.broadcasted_iota(jnp.int32, (1, LANES), 1)
    in_head = [(lane >= hh * ATT_HEAD_DIM) & (lane < (hh + 1) * ATT_HEAD_DIM)
               for hh in range(ATT_HEADS_PER_BLOCK)]

    def by_head(vals):
        out = vals[-1]
        for hh in range(ATT_HEADS_PER_BLOCK - 2, -1, -1):
            out = jnp.where(in_head[hh], vals[hh], out)
        return out

    def softmax_pv(tiles):
        heads = range(ATT_HEADS_PER_BLOCK)
        s = [[_nt_dot(jnp.where(in_head[hh], qs, 0.0).astype(BF16), kb) + bias_of(hh) for hh in heads]
             for qs, kb, _, bias_of in tiles]
        m = [[jnp.max(s_h, axis=-1, keepdims=True) for s_h in s_t] for s_t in s]
        p = [[jnp.exp2(s_h - m_h) for s_h, m_h in zip(s_t, m_t)] for s_t, m_t in zip(s, m)]
        acc = [[jnp.dot(p_h.astype(BF16), tile[2](hh), preferred_element_type=F32) for hh, p_h in enumerate(p_t)]
               for p_t, tile in zip(p, tiles)]
        out = []
        for tile, m_t, acc_t in zip(tiles, m, acc):
            full = (tile[0].shape[0], LANES)
            den = pltpu.roll(by_head(acc_t[::-1]), ATT_HEAD_DIM, 1)
            out.append((by_head([jnp.broadcast_to(x, full) for x in m_t]), den, by_head(acc_t)))
        return out

    for src, dst in zip(cast_in, cast_out):
        dst[...] = src[...].astype(BF16)

    def with_ones(v, hh):
        return jnp.where(in_head[hh], v, 1.0).astype(BF16)

    kb_ref[...] = k_ref[...].astype(BF16)
    for hh in range(ATT_HEADS_PER_BLOCK):
        vb_ref[hh] = with_ones(v_ref[...], hh)

    for g0 in range(0, ATT_FAR_DIL, ATT_GROUP):
        group = [pl.ds(r, sub, stride=ATT_FAR_DIL) for r in range(g0, g0 + ATT_GROUP)]
        far_bias = lambda hh: bt_ref[hh, 0:sub, ATT_NEAR:ATT_NEAR + sub]
        tiles = [(q_ref[rows, :] * scale, k_ref[rows, :].astype(BF16),
                  lambda hh, rows=rows: with_ones(v_ref[rows, :], hh), far_bias) for rows in group]
        for rows, (m, l, acc) in zip(group, softmax_pv(tiles)):
            m3_ref[rows, :] = m
            l3_ref[rows, :] = l
            a3_ref[rows, :] = acc

    nblocks = seq // ATT_BLOCK
    for g0 in range(0, nblocks, ATT_GROUP):
        group, tiles = [], []
        for i in range(g0, min(g0 + ATT_GROUP, nblocks)):
            r0 = i * ATT_BLOCK
            lo = max(0, r0 - pad)
            c0 = ATT_NEAR - (r0 + ATT_BLOCK - lo)
            group.append(slice(r0, r0 + ATT_BLOCK))
            tiles.append((q_ref[group[-1], :] * scale, kb_ref[lo:r0 + ATT_BLOCK, :],
                          lambda hh, lo=lo, r0=r0: vb_ref[hh, lo:r0 + ATT_BLOCK, :],
                          lambda hh, c0=c0: bt_ref[hh, :, c0:ATT_NEAR]))
        merged = []
        for rows, (m, l, acc) in zip(group, softmax_pv(tiles)):
            m3 = m3_ref[rows, :]
            mm = jnp.maximum(m, m3)
            wn = jnp.exp2(m - mm)
            wf = jnp.exp2(m3 - mm)
            num = acc * wn + a3_ref[rows, :] * wf
            den = l * wn + l3_ref[rows, :] * wf
            merged.append(num / den)
        for rows, o in zip(group, merged):
            o_ref[rows, :] = o


def dilated_attention(proj, bias_table, casts, bsz, seq):
    assert seq % ATT_BLOCK == 0 and seq // ATT_FAR_DIL <= ATT_BLOCK
    nhp = ATT_HEADS // ATT_HEADS_PER_BLOCK
    steps = bsz * nhp
    qcol = 4 * GROUP_W // LANES
    kcol = 5 * GROUP_W // LANES
    vcol = 6 * GROUP_W // LANES
    width = bias_table.shape[-1]
    stat = pltpu.VMEM((seq, LANES), F32)
    half = pltpu.VMEM((seq, LANES), BF16)
    vvar = pltpu.VMEM((ATT_HEADS_PER_BLOCK, seq, LANES), BF16)
    cast_in, cast_out, cast_shapes = [], [], []
    for w, layer in casts:
        rows, cols = w.shape[1:]
        assert rows % steps == 0
        cast_in.append(pl.BlockSpec((None, rows // steps, cols), lambda b, h, layer=layer: (layer, b * nhp + h, 0)))
        cast_out.append(pl.BlockSpec((rows // steps, cols), lambda b, h: (b * nhp + h, 0)))
        cast_shapes.append(jax.ShapeDtypeStruct((rows, cols), BF16))
    return pl.pallas_call(
        functools.partial(_attn_kernel, seq=seq, n_cast=len(casts)),
        grid=(bsz, nhp),
        in_specs=[
            pl.BlockSpec((seq, LANES), lambda b, h: (b, qcol + h)),
            pl.BlockSpec((seq, LANES), lambda b, h: (b, kcol + h)),
            pl.BlockSpec((seq, LANES), lambda b, h: (b, vcol + h)),
            pl.BlockSpec((ATT_HEADS_PER_BLOCK, ATT_BLOCK, width), lambda b, h: (h, 0, 0)),
        ] + cast_in,
        out_specs=[pl.BlockSpec((seq, LANES), lambda b, h: (b, h))] + cast_out,
        out_shape=[jax.ShapeDtypeStruct((bsz * seq, GROUP_W), F32)] + cast_shapes,
        scratch_shapes=[half, vvar, stat, stat, stat],
        compiler_params=_params("parallel", "parallel"),
        name="dilated_attention",
    )(proj, proj, proj, bias_table, *[w for w, _ in casts])


def _mix_out_kernel(ya_ref, yb_ref, yc_ref, yd_ref, g_ref, w_ref, h_ref, o_ref, wb_ref):
    @pl.when(pl.program_id(0) == 0)
    def _():
        wb_ref[...] = w_ref[...].astype(BF16)

    yn = [_rms(y_ref[...], g_ref[:, gi * GROUP_W:(gi + 1) * GROUP_W]).astype(BF16)
          for gi, y_ref in enumerate((ya_ref, yb_ref, yc_ref, yd_ref))]
    o_ref[...] = h_ref[...] + jnp.dot(jnp.concatenate(yn, axis=-1), wb_ref[...], preferred_element_type=F32)


def mix_out(ya, yb, yc, yd, g, w, h, layer, tm=512):
    m, d = h.shape
    yspec = pl.BlockSpec((tm, GROUP_W), lambda i: (i, 0))
    return pl.pallas_call(
        _mix_out_kernel,
        grid=(m // tm,),
        in_specs=[
            yspec, yspec, yspec, yspec,
            pl.BlockSpec((None, 1, d), lambda i: (layer, 0, 0)),
            pl.BlockSpec((None, d, d), lambda i: (layer, 0, 0), pipeline_mode=pl.Buffered(1)),
            pl.BlockSpec((tm, d), lambda i: (i, 0)),
        ],
        out_specs=pl.BlockSpec((tm, d), lambda i: (i, 0)),
        out_shape=jax.ShapeDtypeStruct((m, d), F32),
        scratch_shapes=[pltpu.VMEM((d, d), BF16)],
        compiler_params=_params("arbitrary"),
        name="mix_out",
    )(ya, yb, yc, yd, g, w, h)


def _mem_kv_kernel(mem_ref, g_ref, wk_ref, wv_ref, k_ref, v_ref):
    mn = _rms(mem_ref[...], g_ref[...]).astype(BF16)
    k_ref[...] = jnp.dot(mn, wk_ref[...].astype(BF16), preferred_element_type=F32).astype(BF16)
    v_ref[...] = jnp.dot(mn, wv_ref[...].astype(BF16), preferred_element_type=F32).astype(BF16)


def mem_kv(mem, g, wk, wv):
    m, d = mem.shape
    depth = wk.shape[0]
    wspec = pl.BlockSpec((None, d, X_WIDTH), lambda l: (l, 0, 0))
    ospec = pl.BlockSpec((None, m, X_WIDTH), lambda l: (l, 0, 0))
    return pl.pallas_call(
        _mem_kv_kernel,
        grid=(depth,),
        in_specs=[pl.BlockSpec((m, d), lambda l: (0, 0)), pl.BlockSpec((1, d), lambda l: (0, 0)),
                  wspec, wspec],
        out_specs=[ospec, ospec],
        out_shape=[jax.ShapeDtypeStruct((depth, m, X_WIDTH), BF16)] * 2,
        compiler_params=_params("arbitrary"),
        name="mem_kv",
    )(mem, g, wk, wv)


def _xattn_kernel(h_ref, g_ref, wq_ref, k_ref, v_ref, wo_ref, o_ref, wqb_ref, wob_ref):
    @pl.when((pl.program_id(0) == 0) & (pl.program_id(1) == 0))
    def _():
        wqb_ref[...] = wq_ref[...].astype(BF16)
        wob_ref[...] = wo_ref[...].astype(BF16)

    h = h_ref[...]
    hn = _rms(h, g_ref[...]).astype(BF16)
    q = jnp.dot(hn, wqb_ref[...], preferred_element_type=F32) * (X_HEAD_DIM ** -0.5 * LOG2E)
    cols = [slice(hd * X_HEAD_DIM, (hd + 1) * X_HEAD_DIM) for hd in range(X_HEADS)]
    qb = q.astype(BF16)
    s = [_nt_dot(qb[:, c], k_ref[:, c]) for c in cols]
    m = [jnp.max(x, axis=-1, keepdims=True) for x in s]
    p = [jnp.exp2(x - mx) for x, mx in zip(s, m)]
    den = [jnp.sum(x, axis=-1, keepdims=True) for x in p]
    p = [x / dx for x, dx in zip(p, den)]
    outs = [jnp.dot(x.astype(BF16), v_ref[:, c], preferred_element_type=F32) for x, c in zip(p, cols)]
    o = jnp.concatenate(outs, axis=-1).astype(BF16)
    o_ref[...] = h + jnp.dot(o, wob_ref[...], preferred_element_type=F32)


def cross_attention(h, g, wq, k, v, wo, layer, bsz, seq, tm=512):
    d = h.shape[-1]
    nl = seq // tm
    mlen = k.shape[1] // bsz
    return pl.pallas_call(
        _xattn_kernel,
        grid=(bsz, nl),
        in_specs=[
            pl.BlockSpec((tm, d), lambda b, l: (b * nl + l, 0)),
            pl.BlockSpec((None, 1, d), lambda b, l: (layer, 0, 0)),
            pl.BlockSpec((None, d, X_WIDTH), lambda b, l: (layer, 0, 0)),
            pl.BlockSpec((None, mlen, X_WIDTH), lambda b, l: (layer, b, 0)),
            pl.BlockSpec((None, mlen, X_WIDTH), lambda b, l: (layer, b, 0)),
            pl.BlockSpec((None, X_WIDTH, d), lambda b, l: (layer, 0, 0)),
        ],
        out_specs=pl.BlockSpec((tm, d), lambda b, l: (b * nl + l, 0)),
        out_shape=jax.ShapeDtypeStruct(h.shape, F32),
        scratch_shapes=[pltpu.VMEM((d, X_WIDTH), BF16), pltpu.VMEM((X_WIDTH, d), BF16)],
        compiler_params=_params("arbitrary", "arbitrary"),
        name="cross_attention",
    )(h, g, wq, k, v, wo)


def _mlp_kernel(h_ref, g_ref, wu_ref, wd_ref, *rest, out_norm):
    o_ref, hn_ref = rest[-2:]
    j = pl.program_id(1)

    @pl.when(j == 0)
    def _():
        h = h_ref[...]
        hn_ref[...] = _rms(h, g_ref[...]).astype(BF16)
        o_ref[...] = h

    a = jnp.dot(hn_ref[...], wu_ref[...], preferred_element_type=F32)
    a = jnp.square(jnp.maximum(a, 0.0)).astype(BF16)
    o_ref[...] += jnp.dot(a, wd_ref[...], preferred_element_type=F32)

    if out_norm:
        @pl.when(j == pl.num_programs(1) - 1)
        def _():
            o_ref[...] = _rms(o_ref[...], rest[0][...])


def mlp(h, g, wu, wd, layer, out_norm_g=None, tm=1024, tf=512):
    m, d = h.shape
    f = wu.shape[-1]
    out_norm = out_norm_g is not None
    extra_specs = [pl.BlockSpec((1, d), lambda i, j: (0, 0))] if out_norm else []
    extra_args = (out_norm_g,) if out_norm else ()
    return pl.pallas_call(
        functools.partial(_mlp_kernel, out_norm=out_norm),
        grid=(m // tm, f // tf),
        in_specs=[
            pl.BlockSpec((tm, d), lambda i, j: (i, 0)),
            pl.BlockSpec((None, 1, d), lambda i, j: (layer, 0, 0)),
            pl.BlockSpec((d, tf), lambda i, j: (0, j)),
            pl.BlockSpec((tf, d), lambda i, j: (j, 0)),
        ] + extra_specs,
        out_specs=pl.BlockSpec((tm, d), lambda i, j: (i, 0)),
        out_shape=jax.ShapeDtypeStruct((m, d), F32),
        scratch_shapes=[pltpu.VMEM((tm, d), BF16)],
        compiler_params=_params("parallel", "arbitrary"),
        name="mlp",
    )(h, g, wu, wd, *extra_args)


def kernel(x, mem, rel_bias, mem_norm_g, norm_mix_g, w_in, s5_lam_re, s5_lam_im, s5_log_dt, s5_b_re, s5_b_im, s5_c_re, s5_c_im, s5_d, s5_w_glu, pool_w, pool_scale, conv_w_dw, conv_b_dw, conv_ln_g, conv_ln_b, conv_w_pw, grp_norm_g, w_out, norm_x_g, w_xq, w_xk, w_xv, w_xo, norm_mlp_g, w_up, w_down, norm_final_g):
    bsz, seq, d = x.shape
    depth = w_in.shape[0]
    vec = lambda a: a[:, None, :]

    k_mem, v_mem = mem_kv(mem.reshape(bsz * mem.shape[1], d), mem_norm_g[None, :], w_xk, w_xv)
    pow_r, pow_i, bmat, cmat = s5_params(s5_lam_re, s5_lam_im, s5_log_dt, s5_b_re, s5_b_im, s5_c_re, s5_c_im,
                                         chunk=S5_BLOCK // SUBLANES)
    coef_r, coef_i = s5_scan_coefficients(pow_r, pow_i)
    bias_table = attention_bias_table(rel_bias, seq)

    norm_mix, s5_dv, pool_sc = vec(norm_mix_g), vec(s5_d), vec(pool_scale)
    b_dw, ln_g, ln_b = vec(conv_b_dw), vec(conv_ln_g), vec(conv_ln_b)
    grp_g, norm_x, norm_mlp = vec(grp_norm_g), vec(norm_x_g), vec(norm_mlp_g)

    h = x.reshape(bsz * seq, d)
    w_in_b = w_in[0].astype(BF16)
    for l in range(depth):
        proj, y_c, y_b = proj_conv(h, norm_mix, w_in_b, conv_w_dw, b_dw, ln_g, ln_b, conv_w_pw,
                                   pool_w, pool_sc, l, bsz, seq)
        y_a = s5_mixer(proj, bmat, cmat, coef_r, coef_i, s5_dv, s5_w_glu, l, bsz, seq)
        casts = [(w_up, l), (w_down, l)] + ([(w_in, l + 1)] if l + 1 < depth else [])
        y_d, w_up_b, w_down_b, *nxt = dilated_attention(proj, bias_table, casts, bsz, seq)
        w_in_b = nxt[0] if nxt else None
        h = mix_out(y_a, y_b, y_c, y_d, grp_g, w_out, h, l)
        h = cross_attention(h, norm_x, w_xq, k_mem, v_mem, w_xo, l, bsz, seq)
        h = mlp(h, norm_mlp, w_up_b, w_down_b, l,
                out_norm_g=norm_final_g[None, :] if l == depth - 1 else None)
    return h.reshape(bsz, seq, d)
```

```python
import functools
import math

import numpy as np
import jax
import jax.numpy as jnp
from jax import lax
from jax.experimental import pallas as pl
from jax.experimental.pallas import tpu as pltpu

F32 = jnp.float32
BF16 = jnp.bfloat16

GROUP_W = 512
S5_GROUPS = 32
S5_CH = 16
S5_STATE = 64
S5_NSTATE = S5_GROUPS * S5_STATE
S5_SCAN_LANE_BLOCKS = 8
S5_BLOCK = 512
S5_SCAN_UNROLL = 4
POOL_WINDOWS = (2, 4, 8, 16)
POOL_CH = 128
CONV_WIDTH = 31
ATT_HEADS = 8
ATT_HEAD_DIM = 64
DILATED_PATTERNS = ((128, 1), (512, 4), (2048, 16))
ATT_BLOCK = 128
REL_BUCKETS = 32
REL_MAX_DIST = 2048
X_HEADS = 4
X_HEAD_DIM = 128
X_WIDTH = 512
NORM_EPS = 1e-6
NEG_INF = -1e30
LOG2E = math.log2(math.e)

SUBLANES = 8
LANES = 128
VMEM_LIMIT = 58 * 1024 * 1024


def _params(*sem):
    return pltpu.CompilerParams(dimension_semantics=sem, vmem_limit_bytes=VMEM_LIMIT)


def _rms(x, g):
    return x * lax.rsqrt(jnp.mean(x * x, axis=-1, keepdims=True) + NORM_EPS) * g


MXU_COLS = 256
POOL_HALO = 16
CONV_HALO = 32
CONV_ROWS = 32


def _proj_conv_kernel(x_ref, g_ref, w_ref, wdw_ref, bdw_ref, lng_ref, lnb_ref, wpw_ref, pw_ref, psc_ref,
                      proj_ref, yc_ref, yb_ref, sh_ref, cv_ref, wb_ref, xn_ref, ext_ref, *, tl, nl):
    step = pl.program_id(0)
    n = tl + CONV_HALO

    @pl.when(step == 0)
    def _():
        sh_ref[0] = jnp.zeros(sh_ref.shape[1:], F32)
        ext_ref[...] = jnp.zeros(ext_ref.shape, F32)

    @pl.when(lax.rem(step - 1, nl) == 0)
    def _():
        sh_ref[0, 0:CONV_HALO, :] = jnp.zeros((CONV_HALO, GROUP_W), F32)
        ext_ref[0:POOL_HALO, :] = jnp.zeros((POOL_HALO, GROUP_W), F32)

    sh_ref[0, n:, :] = jnp.zeros((SUBLANES, GROUP_W), F32)
    for s in range(1, SUBLANES):
        sh_ref[s, 0:n, :] = sh_ref[0, s:s + n, :]

    for k in range(CONV_WIDTH):
        wb_ref[k] = jnp.broadcast_to(wdw_ref[k:k + 1, :], (SUBLANES, GROUP_W))
    wb_ref[CONV_WIDTH] = jnp.broadcast_to(bdw_ref[...], (SUBLANES, GROUP_W))

    xn_ref[...] = _rms(x_ref[...], g_ref[...]).astype(BF16)
    slice_cols = 2 * MXU_COLS
    n_slices = proj_ref.shape[-1] // slice_cols
    n_chunks = tl // CONV_ROWS
    per_slice = n_chunks // n_slices
    tiles = CONV_ROWS // SUBLANES
    off = CONV_HALO - (CONV_WIDTH - 1)

    def conv_chunk(c):
        r0 = c * CONV_ROWS
        acc = jnp.broadcast_to(wb_ref[CONV_WIDTH][None], (tiles, SUBLANES, GROUP_W))
        for k in range(CONV_WIDTH):
            q, s = divmod(off + k, SUBLANES)
            x = sh_ref[s, r0 + q * SUBLANES:r0 + q * SUBLANES + CONV_ROWS, :]
            acc = acc + wb_ref[k][None] * x.reshape(tiles, SUBLANES, GROUP_W)
        cv_ref[r0:r0 + CONV_ROWS, :] = acc.reshape(CONV_ROWS, GROUP_W)

    def pool_tile():
        t = lax.rem(jnp.maximum(step - 1, 0), nl) * tl + lax.broadcasted_iota(jnp.int32, (tl, 1), 0)
        for gi, w in enumerate(POOL_WINDOWS):
            pc = slice(gi * POOL_CH, (gi + 1) * POOL_CH)
            acc = ext_ref[POOL_HALO:POOL_HALO + tl, pc]
            for k in range(1, w):
                acc = acc + ext_ref[POOL_HALO - k:POOL_HALO - k + tl, pc]
            p = acc / jnp.minimum(t + 1, w).astype(F32) - ext_ref[POOL_HALO:POOL_HALO + tl, pc]
            y = jnp.dot(p.astype(BF16), pw_ref[gi].astype(BF16), preferred_element_type=F32)
            yb_ref[:, pc] = y * psc_ref[:, pc]

    for j in range(n_slices):
        cols = slice(j * slice_cols, (j + 1) * slice_cols)
        proj_ref[:, cols] = jnp.dot(xn_ref[...], w_ref[:, cols], preferred_element_type=F32)
        if j == 0:
            pool_tile()
        for c in range(j * per_slice, (j + 1) * per_slice):
            conv_chunk(c)
        if per_slice:
            r0 = ((j + 1) * per_slice - 1) * CONV_ROWS
            zero = pltpu.bitcast(lax.shift_right_logical(
                pltpu.bitcast(cv_ref[r0:r0 + 2 * SUBLANES, 0:LANES], jnp.uint32), jnp.uint32(32)), F32)
            xn_ref[0:2 * SUBLANES, 0:LANES] = xn_ref[0:2 * SUBLANES, 0:LANES] + zero.astype(BF16)
    for c in range(n_slices * per_slice, n_chunks):
        conv_chunk(c)

    h = cv_ref[...]
    hc = h - jnp.mean(h, axis=-1, keepdims=True)
    y = hc * lax.rsqrt(jnp.mean(hc * hc, axis=-1, keepdims=True) + NORM_EPS)
    y = y * lng_ref[...] + lnb_ref[...]
    y = jax.nn.silu(y)
    yc_ref[...] = jnp.dot(y.astype(BF16), wpw_ref[...].astype(BF16), preferred_element_type=F32)

    ext_ref[0:POOL_HALO, :] = ext_ref[tl:tl + POOL_HALO, :]
    ext_ref[POOL_HALO:, :] = proj_ref[:, GROUP_W:2 * GROUP_W]
    sh_ref[0, 0:CONV_HALO, :] = sh_ref[0, tl:n, :]
    sh_ref[0, CONV_HALO:n, :] = (proj_ref[:, 2 * GROUP_W:3 * GROUP_W]
                                 * jax.nn.sigmoid(proj_ref[:, 3 * GROUP_W:4 * GROUP_W]))


def proj_conv(x, g, w, w_dw, b_dw, ln_g, ln_b, w_pw, pool_w, pool_scale, layer, bsz, seq, tl=512):
    m, d = x.shape
    n = w.shape[-1]
    nl = seq // tl
    nt = m // tl
    vec = pl.BlockSpec((None, 1, GROUP_W), lambda s: (layer, 0, 0))
    cur = lambda s: (jnp.minimum(s, nt - 1), 0)
    prev = lambda s: (jnp.maximum(s - 1, 0), 0)
    return pl.pallas_call(
        functools.partial(_proj_conv_kernel, tl=tl, nl=nl),
        grid=(nt + 1,),
        in_specs=[
            pl.BlockSpec((tl, d), cur),
            pl.BlockSpec((None, 1, d), lambda s: (layer, 0, 0)),
            pl.BlockSpec((d, n), lambda s: (0, 0), pipeline_mode=pl.Buffered(1)),
            pl.BlockSpec((None, CONV_WIDTH, GROUP_W), lambda s: (layer, 0, 0)),
            vec, vec, vec,
            pl.BlockSpec((None, GROUP_W, GROUP_W), lambda s: (layer, 0, 0)),
            pl.BlockSpec((None, len(POOL_WINDOWS), POOL_CH, POOL_CH), lambda s: (layer, 0, 0, 0)),
            vec,
        ],
        out_specs=[pl.BlockSpec((tl, n), cur), pl.BlockSpec((tl, GROUP_W), prev),
                   pl.BlockSpec((tl, GROUP_W), prev)],
        out_shape=[jax.ShapeDtypeStruct((m, n), F32), jax.ShapeDtypeStruct((m, GROUP_W), F32),
                   jax.ShapeDtypeStruct((m, GROUP_W), F32)],
        scratch_shapes=[pltpu.VMEM((SUBLANES, CONV_HALO + tl + SUBLANES, GROUP_W), F32),
                        pltpu.VMEM((tl, GROUP_W), F32),
                        pltpu.VMEM((CONV_WIDTH + 1, SUBLANES, GROUP_W), F32),
                        pltpu.VMEM((tl, d), BF16),
                        pltpu.VMEM((POOL_HALO + tl, GROUP_W), F32)],
        compiler_params=_params("arbitrary"),
        name="proj_conv",
    )(x, g, w, w_dw, b_dw, ln_g, ln_b, w_pw, pool_w, pool_scale)


def _s5_param_kernel(lr_ref, li_ref, ldt_ref, lrx_ref, lix_ref, ldtx_ref, br_ref, bi_ref, cr_ref, ci_ref,
                     pr_ref, pi_ref, bmat_ref, cmat_ref, *, chunk):
    def discretise(lr, li, ldt):
        dt = jnp.exp(ldt)
        mag = jnp.exp(lr * dt)
        return mag * jnp.cos(li * dt), mag * jnp.sin(li * dt)

    cr, ci = discretise(lr_ref[...], li_ref[...], ldt_ref[...])
    slot = {1: 0, chunk: 1, 2 * chunk: 2, 4 * chunk: 3}
    e = 1
    while True:
        if e in slot:
            pr_ref[slot[e]] = cr
            pi_ref[slot[e]] = ci
        if e == 4 * chunk:
            break
        cr, ci = cr * cr - ci * ci, 2.0 * (cr * ci)
        e *= 2

    lr, li = lrx_ref[...], lix_ref[...]
    ar, ai = discretise(lr, li, ldtx_ref[...])
    den = lr * lr + li * li
    nr, ni = ar - 1.0, ai
    f_r = (nr * lr + ni * li) / den
    f_i = (ni * lr - nr * li) / den
    b_r, b_i = br_ref[...], bi_ref[...]
    bb_r = f_r * b_r - f_i * b_i
    bb_i = f_r * b_i + f_i * b_r

    def block_diagonal(x, row_shift, col_shift):
        rows, w = x.shape
        cols = (rows >> row_shift) << col_shift
        sel = ((lax.broadcasted_iota(jnp.int32, (w, cols), 1) & (w - 1))
               == lax.broadcasted_iota(jnp.int32, (w, cols), 0))
        tiled = jnp.dot(x.astype(BF16), jnp.where(sel, 1.0, 0.0).astype(BF16), preferred_element_type=F32)
        same = ((lax.broadcasted_iota(jnp.int32, (rows, cols), 0) >> row_shift)
                == (lax.broadcasted_iota(jnp.int32, (rows, cols), 1) >> col_shift))
        return jnp.where(same, tiled, 0.0).astype(BF16)

    ch_shift, st_shift = S5_CH.bit_length() - 1, S5_STATE.bit_length() - 1
    bmat_ref[:, 0:S5_NSTATE] = block_diagonal(bb_r, ch_shift, st_shift)
    bmat_ref[:, S5_NSTATE:] = block_diagonal(bb_i, ch_shift, st_shift)
    cmat_ref[0:S5_NSTATE, :] = block_diagonal(cr_ref[...], st_shift, ch_shift)
    cmat_ref[S5_NSTATE:, :] = block_diagonal(-ci_ref[...], st_shift, ch_shift)


S5_POWERS = 4


def s5_params(lam_re, lam_im, log_dt, b_re, b_im, c_re, c_im, chunk):
    depth = lam_re.shape[0]
    gc = S5_GROUPS * S5_CH
    rep = lambda a: jnp.repeat(a, S5_CH, axis=1)
    ldt = log_dt[..., None]
    bt = lambda b: jnp.transpose(b, (0, 1, 3, 2)).reshape(depth, gc, S5_STATE)
    ct = lambda c: jnp.transpose(c, (0, 1, 3, 2)).reshape(depth, S5_NSTATE, S5_CH)
    small = pl.BlockSpec((None, S5_GROUPS, S5_STATE), lambda l: (l, 0, 0))
    small1 = pl.BlockSpec((None, S5_GROUPS, 1), lambda l: (l, 0, 0))
    big = pl.BlockSpec((None, gc, S5_STATE), lambda l: (l, 0, 0))
    big1 = pl.BlockSpec((None, gc, 1), lambda l: (l, 0, 0))
    cspec = pl.BlockSpec((None, S5_NSTATE, S5_CH), lambda l: (l, 0, 0))
    powspec = pl.BlockSpec((None, S5_POWERS, S5_GROUPS, S5_STATE), lambda l: (l, 0, 0, 0))
    return pl.pallas_call(
        functools.partial(_s5_param_kernel, chunk=chunk),
        grid=(depth,),
        in_specs=[small, small, small1, big, big, big1, big, big, cspec, cspec],
        out_specs=[powspec, powspec,
                   pl.BlockSpec((None, gc, 2 * S5_NSTATE), lambda l: (l, 0, 0)),
                   pl.BlockSpec((None, 2 * S5_NSTATE, gc), lambda l: (l, 0, 0))],
        out_shape=[
            jax.ShapeDtypeStruct((depth, S5_POWERS, S5_GROUPS, S5_STATE), F32),
            jax.ShapeDtypeStruct((depth, S5_POWERS, S5_GROUPS, S5_STATE), F32),
            jax.ShapeDtypeStruct((depth, gc, 2 * S5_NSTATE), BF16),
            jax.ShapeDtypeStruct((depth, 2 * S5_NSTATE, gc), BF16),
        ],
        compiler_params=_params("arbitrary"),
        name="s5_params",
    )(lam_re, lam_im, ldt, rep(lam_re), rep(lam_im), rep(ldt), bt(b_re), bt(b_im), ct(c_re), ct(c_im))


def s5_scan_coefficients(pow_r, pow_i):
    depth = pow_r.shape[0]
    row = jnp.arange(SUBLANES)[None, :, None]

    def coef(p):
        p = p.reshape(depth, S5_POWERS, 1, S5_NSTATE)
        full = [jnp.broadcast_to(p[:, e], (depth, SUBLANES, S5_NSTATE)) for e in (0, 1)]
        masked = [jnp.where(row >= s, p[:, e], 0.0) for e, s in ((1, 1), (2, 2), (3, 4))]
        return jnp.stack(full + masked, axis=1)

    return coef(pow_r), coef(pow_i)


S5_COEF_TILES = 5


def _s5_kernel(u0_ref, u1_ref, u2_ref, u3_ref, bmat_ref, cmat_ref, cfr_ref, cfi_ref, d_ref, wglu_ref, o_ref,
               up_ref, uperm_ref, xs_ref, cin_ref, yp_ref, *, tl):
    n = S5_NSTATE
    tc = tl // SUBLANES
    pitch = tc + SUBLANES
    u_refs = (u0_ref, u1_ref, u2_ref, u3_ref)
    l = pl.program_id(1)

    @pl.when(l == 0)
    def _():
        cin_ref[...] = jnp.zeros(cin_ref.shape, F32)

    for j, u_ref in enumerate(u_refs):
        for c in range(SUBLANES):
            up_ref[j, c * pitch:c * pitch + tc, :] = u_ref[c * tc:(c + 1) * tc, :]

    def gather(t, carry):
        r = pl.multiple_of(t * SUBLANES, SUBLANES)
        for j in range(len(u_refs)):
            uperm_ref[pl.ds(r, SUBLANES), j * LANES:(j + 1) * LANES] = up_ref[j, pl.ds(t, SUBLANES, stride=pitch), :]
        return carry

    lax.fori_loop(0, tc, gather, 0, unroll=S5_SCAN_UNROLL)

    ub = uperm_ref[...].astype(BF16)
    blocks_per_part = n // MXU_COLS
    for jb in range(2 * blocks_per_part):
        k = (jb % blocks_per_part) * MXU_COLS // S5_STATE * S5_CH // LANES
        xs_ref[:, jb * MXU_COLS:(jb + 1) * MXU_COLS] = jnp.dot(
            ub[:, k * LANES:(k + 1) * LANES], bmat_ref[k * LANES:(k + 1) * LANES, jb * MXU_COLS:(jb + 1) * MXU_COLS],
            preferred_element_type=F32)

    row = lax.broadcasted_iota(jnp.int32, (SUBLANES, LANES), 0)

    def cmul_add(ar, ai, xr, xi, br, bi):
        return (ar * xr - ai * xi) + br, (ar * xi + ai * xr) + bi

    def scan_lanes(lb0):
        cols = [(slice(lb * LANES, (lb + 1) * LANES), slice(n + lb * LANES, n + (lb + 1) * LANES))
                for lb in range(lb0, lb0 + S5_SCAN_LANE_BLOCKS)]
        a = [(cfr_ref[0, :, cr_], cfi_ref[0, :, cr_]) for cr_, _ in cols]

        def recurrence(init, store):
            def step(t, x):
                base = pl.multiple_of(t * SUBLANES, SUBLANES)
                new = []
                for q, (cr_, ci_) in enumerate(cols):
                    nr, ni = cmul_add(a[q][0], a[q][1], x[2 * q], x[2 * q + 1],
                                      xs_ref[pl.ds(base, SUBLANES), cr_], xs_ref[pl.ds(base, SUBLANES), ci_])
                    if store:
                        xs_ref[pl.ds(base, SUBLANES), cr_] = nr
                        xs_ref[pl.ds(base, SUBLANES), ci_] = ni
                    new += [nr, ni]
                return tuple(new)
            return lax.fori_loop(0, tc, step, tuple(init), unroll=S5_SCAN_UNROLL)

        zero = jnp.zeros((SUBLANES, LANES), F32)
        ends = recurrence([zero] * (2 * len(cols)), store=False)

        starts = []
        for q, (cr_, ci_) in enumerate(cols):
            er, ei = ends[2 * q], ends[2 * q + 1]
            wr = jnp.where(row == 0, cin_ref[:, cr_], pltpu.roll(er, 1, 0))
            wi = jnp.where(row == 0, cin_ref[:, ci_], pltpu.roll(ei, 1, 0))
            for k, s in ((2, 1), (3, 2), (4, 4)):
                wr, wi = cmul_add(cfr_ref[k, :, cr_], cfi_ref[k, :, cr_],
                                  pltpu.roll(wr, s, 0), pltpu.roll(wi, s, 0), wr, wi)
            tr, ti = cmul_add(cfr_ref[1, :, cr_], cfi_ref[1, :, cr_], wr, wi, er, ei)
            cin_ref[:, cr_] = pltpu.roll(tr, 1, 0)
            cin_ref[:, ci_] = pltpu.roll(ti, 1, 0)
            starts += [wr, wi]
        recurrence(starts, store=True)

    for lb0 in range(0, n // LANES, S5_SCAN_LANE_BLOCKS):
        scan_lanes(lb0)

    ys = []
    for m in range(GROUP_W // MXU_COLS):
        k0 = m * MXU_COLS // S5_CH * S5_STATE
        k1 = (m + 1) * MXU_COLS // S5_CH * S5_STATE
        cols = slice(m * MXU_COLS, (m + 1) * MXU_COLS)
        ys.append(jnp.dot(xs_ref[:, k0:k1].astype(BF16), cmat_ref[k0:k1, cols], preferred_element_type=F32)
                  + jnp.dot(xs_ref[:, n + k0:n + k1].astype(BF16), cmat_ref[n + k0:n + k1, cols],
                            preferred_element_type=F32))
    y = jnp.concatenate(ys, axis=-1) + d_ref[...] * uperm_ref[...]
    g = jax.nn.gelu(y)
    gate = jnp.dot(g.astype(BF16), wglu_ref[...].astype(BF16), preferred_element_type=F32)
    out = g * jax.nn.sigmoid(gate)
    for j in range(len(u_refs)):
        yp_ref[j] = out[:, j * LANES:(j + 1) * LANES]

    tiles_per_chunk = tc // SUBLANES
    shift = tiles_per_chunk.bit_length() - 1

    def scatter(idx, carry):
        c = idx >> shift
        m = idx & (tiles_per_chunk - 1)
        r = pl.multiple_of(idx * SUBLANES, SUBLANES)
        src = m * (SUBLANES * SUBLANES) + c
        for j in range(len(u_refs)):
            o_ref[pl.ds(r, SUBLANES), j * LANES:(j + 1) * LANES] = yp_ref[j, pl.ds(src, SUBLANES, stride=SUBLANES), :]
        return carry

    lax.fori_loop(0, SUBLANES * tiles_per_chunk, scatter, 0, unroll=S5_SCAN_UNROLL)


def s5_mixer(proj, bmat, cmat, coef_r, coef_i, d_skip, w_glu, layer, bsz, seq, tl=S5_BLOCK):
    nl = seq // tl
    n2 = 2 * S5_NSTATE
    tc = tl // SUBLANES
    assert tc % SUBLANES == 0 and tc & (tc - 1) == 0
    nu = GROUP_W // LANES
    uspecs = [pl.BlockSpec((tl, LANES), lambda b, l, j=j: (b * nl + l, j)) for j in range(nu)]
    coef = pl.BlockSpec((None, S5_COEF_TILES, SUBLANES, S5_NSTATE), lambda b, l: (layer, 0, 0, 0))
    return pl.pallas_call(
        functools.partial(_s5_kernel, tl=tl),
        grid=(bsz, nl),
        in_specs=uspecs + [
            pl.BlockSpec((None, GROUP_W, n2), lambda b, l: (layer, 0, 0)),
            pl.BlockSpec((None, n2, GROUP_W), lambda b, l: (layer, 0, 0)),
            coef, coef,
            pl.BlockSpec((None, 1, GROUP_W), lambda b, l: (layer, 0, 0)),
            pl.BlockSpec((None, GROUP_W, GROUP_W), lambda b, l: (layer, 0, 0)),
        ],
        out_specs=pl.BlockSpec((tl, GROUP_W), lambda b, l: (b * nl + l, 0)),
        out_shape=jax.ShapeDtypeStruct((bsz * seq, GROUP_W), F32),
        scratch_shapes=[
            pltpu.VMEM((nu, SUBLANES * (tc + SUBLANES), LANES), F32),
            pltpu.VMEM((tl, GROUP_W), F32),
            pltpu.VMEM((tl, n2), F32),
            pltpu.VMEM((SUBLANES, n2), F32),
            pltpu.VMEM((nu, tl, LANES), F32),
        ],
        compiler_params=_params("parallel", "arbitrary"),
        name="s5_mixer",
    )(*([proj] * nu), bmat, cmat, coef_r, coef_i, d_skip, w_glu)


def _t5_bucket(dist):
    n = np.maximum(dist, 0)
    max_exact = REL_BUCKETS // 2
    large = max_exact + (np.log(np.maximum(n, 1) / max_exact) / np.log(REL_MAX_DIST / max_exact)
                         * (REL_BUCKETS - max_exact)).astype(np.int64)
    large = np.minimum(large, REL_BUCKETS - 1)
    return np.where(n < max_exact, n, large).astype(np.int32)


ATT_NEAR_BLOCKS = 5
ATT_NEAR = ATT_NEAR_BLOCKS * ATT_BLOCK
ATT_FAR_DIL = 16


def _attention_bias_layout(sub):
    a = np.arange(ATT_BLOCK)[:, None]
    d_near = (ATT_NEAR - ATT_BLOCK) + a - np.arange(ATT_NEAR)[None, :]
    mult = np.zeros(d_near.shape, np.int64)
    for window, dil in DILATED_PATTERNS:
        if dil != ATT_FAR_DIL:
            mult += (d_near >= 0) & (d_near % dil == 0) & (d_near // dil <= window // dil)
    d_far = ATT_FAR_DIL * (a - np.arange(sub)[None, :])
    mult_far = ((d_far >= 0) & (d_far // ATT_FAR_DIL <= ATT_BLOCK)).astype(np.int64)
    d = np.concatenate([d_near, d_far], axis=1)
    mult = np.concatenate([mult, mult_far], axis=1)
    addend = np.where(mult > 0, np.log(np.maximum(mult, 1)), NEG_INF).astype(np.float32)
    return _t5_bucket(d), addend


def _attn_bias_kernel(rb_ref, bucket_ref, add_ref, o_ref):
    h = pl.program_id(0)
    bucket = bucket_ref[...]
    t = jnp.zeros(bucket.shape, F32)
    for b in range(REL_BUCKETS):
        t = jnp.where(bucket == b, rb_ref[b, h], t)
    o_ref[...] = (t + add_ref[...]) * LOG2E


def attention_bias_table(rel_bias, seq):
    sub = seq // ATT_FAR_DIL
    bucket, addend = _attention_bias_layout(sub)
    width = ATT_NEAR + sub
    full = pl.BlockSpec((ATT_BLOCK, width), lambda h: (0, 0))
    return pl.pallas_call(
        _attn_bias_kernel,
        grid=(ATT_HEADS,),
        in_specs=[pl.BlockSpec(memory_space=pltpu.SMEM), full, full],
        out_specs=pl.BlockSpec((None, ATT_BLOCK, width), lambda h: (h, 0, 0)),
        out_shape=jax.ShapeDtypeStruct((ATT_HEADS, ATT_BLOCK, width), F32),
        compiler_params=_params("arbitrary"),
        name="attn_bias",
    )(rel_bias, jnp.asarray(bucket), jnp.asarray(addend))


ATT_HEADS_PER_BLOCK = LANES // ATT_HEAD_DIM
ATT_GROUP = 8


def _nt_dot(a, b):
    return lax.dot_general(a, b, (((1,), (1,)), ((), ())), preferred_element_type=F32)


def _attn_kernel(q_ref, k_ref, v_ref, bt_ref, *rest, seq, n_cast):
    cast_in, o_ref, cast_out = rest[:n_cast], rest[n_cast], rest[n_cast + 1:2 * n_cast + 1]
    kb_ref, vb_ref, m3_ref, l3_ref, a3_ref = rest[2 * n_cast + 1:]
    sub = seq // ATT_FAR_DIL
    pad = ATT_NEAR - ATT_BLOCK
    scale = ATT_HEAD_DIM ** -0.5 * LOG2E
    lane = lax.broadcasted_iota(jnp.int32, (1, LANES), 1)
    in_head = [(lane >= hh * ATT_HEAD_DIM) & (lane < (hh + 1) * ATT_HEAD_DIM)
               for hh in range(ATT_HEADS_PER_BLOCK)]

    def by_head(vals):
        out = vals[-1]
        for hh in range(ATT_HEADS_PER_BLOCK - 2, -1, -1):
            out = jnp.where(in_head[hh], vals[hh], out)
        return out

    def softmax_pv(tiles):
        heads = range(ATT_HEADS_PER_BLOCK)
        s = [[_nt_dot(jnp.where(in_head[hh], qs, 0.0).astype(BF16), kb) + bias_of(hh) for hh in heads]
             for qs, kb, _, bias_of in tiles]
        m = [[jnp.max(s_h, axis=-1, keepdims=True) for s_h in s_t] for s_t in s]
        p = [[jnp.exp2(s_h - m_h) for s_h, m_h in zip(s_t, m_t)] for s_t, m_t in zip(s, m)]
        acc = [[jnp.dot(p_h.astype(BF16), tile[2](hh), preferred_element_type=F32) for hh, p_h in enumerate(p_t)]
               for p_t, tile in zip(p, tiles)]
        out = []
        for tile, m_t, acc_t in zip(tiles, m, acc):
            full = (tile[0].shape[0], LANES)
            den = pltpu.roll(by_head(acc_t[::-1]), ATT_HEAD_DIM, 1)
            out.append((by_head([jnp.broadcast_to(x, full) for x in m_t]), den, by_head(acc_t)))
        return out

    for src, dst in zip(cast_in, cast_out):
        dst[...] = src[...].astype(BF16)

    def with_ones(v, hh):
        return jnp.where(in_head[hh], v, 1.0).astype(BF16)

    kb_ref[...] = k_ref[...].astype(BF16)
    for hh in range(ATT_HEADS_PER_BLOCK):
        vb_ref[hh] = with_ones(v_ref[...], hh)

    for g0 in range(0, ATT_FAR_DIL, ATT_GROUP):
        group = [pl.ds(r, sub, stride=ATT_FAR_DIL) for r in range(g0, g0 + ATT_GROUP)]
        far_bias = lambda hh: bt_ref[hh, 0:sub, ATT_NEAR:ATT_NEAR + sub]
        tiles = [(q_ref[rows, :] * scale, k_ref[rows, :].astype(BF16),
                  lambda hh, rows=rows: with_ones(v_ref[rows, :], hh), far_bias) for rows in group]
        for rows, (m, l, acc) in zip(group, softmax_pv(tiles)):
            m3_ref[rows, :] = m
            l3_ref[rows, :] = l
            a3_ref[rows, :] = acc

    nblocks = seq // ATT_BLOCK
    for g0 in range(0, nblocks, ATT_GROUP):
        group, tiles = [], []
        for i in range(g0, min(g0 + ATT_GROUP, nblocks)):
            r0 = i * ATT_BLOCK
            lo = max(0, r0 - pad)
            c0 = ATT_NEAR - (r0 + ATT_BLOCK - lo)
            group.append(slice(r0, r0 + ATT_BLOCK))
            tiles.append((q_ref[group[-1], :] * scale, kb_ref[lo:r0 + ATT_BLOCK, :],
                          lambda hh, lo=lo, r0=r0: vb_ref[hh, lo:r0 + ATT_BLOCK, :],
                          lambda hh, c0=c0: bt_ref[hh, :, c0:ATT_NEAR]))
        merged = []
        for rows, (m, l, acc) in zip(group, softmax_pv(tiles)):
            m3 = m3_ref[rows, :]
            mm = jnp.maximum(m, m3)
            wn = jnp.exp2(m - mm)
            wf = jnp.exp2(m3 - mm)
            num = acc * wn + a3_ref[rows, :] * wf
            den = l * wn + l3_ref[rows, :] * wf
            merged.append(num / den)
        for rows, o in zip(group, merged):
            o_ref[rows, :] = o


def dilated_attention(proj, bias_table, casts, bsz, seq):
    assert seq % ATT_BLOCK == 0 and seq // ATT_FAR_DIL <= ATT_BLOCK
    nhp = ATT_HEADS // ATT_HEADS_PER_BLOCK
    steps = bsz * nhp
    qcol = 4 * GROUP_W // LANES
    kcol = 5 * GROUP_W // LANES
    vcol = 6 * GROUP_W // LANES
    width = bias_table.shape[-1]
    stat = pltpu.VMEM((seq, LANES), F32)
    half = pltpu.VMEM((seq, LANES), BF16)
    vvar = pltpu.VMEM((ATT_HEADS_PER_BLOCK, seq, LANES), BF16)
    cast_in, cast_out, cast_shapes = [], [], []
    for w, layer in casts:
        rows, cols = w.shape[1:]
        assert rows % steps == 0
        cast_in.append(pl.BlockSpec((None, rows // steps, cols), lambda b, h, layer=layer: (layer, b * nhp + h, 0)))
        cast_out.append(pl.BlockSpec((rows // steps, cols), lambda b, h: (b * nhp + h, 0)))
        cast_shapes.append(jax.ShapeDtypeStruct((rows, cols), BF16))
    return pl.pallas_call(
        functools.partial(_attn_kernel, seq=seq, n_cast=len(casts)),
        grid=(bsz, nhp),
        in_specs=[
            pl.BlockSpec((seq, LANES), lambda b, h: (b, qcol + h)),
            pl.BlockSpec((seq, LANES), lambda b, h: (b, kcol + h)),
            pl.BlockSpec((seq, LANES), lambda b, h: (b, vcol + h)),
            pl.BlockSpec((ATT_HEADS_PER_BLOCK, ATT_BLOCK, width), lambda b, h: (h, 0, 0)),
        ] + cast_in,
        out_specs=[pl.BlockSpec((seq, LANES), lambda b, h: (b, h))] + cast_out,
        out_shape=[jax.ShapeDtypeStruct((bsz * seq, GROUP_W), F32)] + cast_shapes,
        scratch_shapes=[half, vvar, stat, stat, stat],
        compiler_params=_params("parallel", "parallel"),
        name="dilated_attention",
    )(proj, proj, proj, bias_table, *[w for w, _ in casts])


def _mix_out_kernel(ya_ref, yb_ref, yc_ref, yd_ref, g_ref, w_ref, h_ref, o_ref, wb_ref):
    @pl.when(pl.program_id(0) == 0)
    def _():
        wb_ref[...] = w_ref[...].astype(BF16)

    yn = [_rms(y_ref[...], g_ref[:, gi * GROUP_W:(gi + 1) * GROUP_W]).astype(BF16)
          for gi, y_ref in enumerate((ya_ref, yb_ref, yc_ref, yd_ref))]
    o_ref[...] = h_ref[...] + jnp.dot(jnp.concatenate(yn, axis=-1), wb_ref[...], preferred_element_type=F32)


def mix_out(ya, yb, yc, yd, g, w, h, layer, tm=512):
    m, d = h.shape
    yspec = pl.BlockSpec((tm, GROUP_W), lambda i: (i, 0))
    return pl.pallas_call(
        _mix_out_kernel,
        grid=(m // tm,),
        in_specs=[
            yspec, yspec, yspec, yspec,
            pl.BlockSpec((None, 1, d), lambda i: (layer, 0, 0)),
            pl.BlockSpec((None, d, d), lambda i: (layer, 0, 0), pipeline_mode=pl.Buffered(1)),
            pl.BlockSpec((tm, d), lambda i: (i, 0)),
        ],
        out_specs=pl.BlockSpec((tm, d), lambda i: (i, 0)),
        out_shape=jax.ShapeDtypeStruct((m, d), F32),
        scratch_shapes=[pltpu.VMEM((d, d), BF16)],
        compiler_params=_params("arbitrary"),
        name="mix_out",
    )(ya, yb, yc, yd, g, w, h)


def _mem_kv_kernel(mem_ref, g_ref, wk_ref, wv_ref, w0_ref, k_ref, v_ref, w0b_ref):
    mn = _rms(mem_ref[...], g_ref[...]).astype(BF16)
    k_ref[...] = jnp.dot(mn, wk_ref[...].astype(BF16), preferred_element_type=F32).astype(BF16)
    v_ref[...] = jnp.dot(mn, wv_ref[...].astype(BF16), preferred_element_type=F32).astype(BF16)
    w0b_ref[...] = w0_ref[...].astype(BF16)


def mem_kv(mem, g, wk, wv, w_in):
    m, d = mem.shape
    depth = wk.shape[0]
    rows, cols = w_in.shape[1:]
    assert rows % depth == 0
    wspec = pl.BlockSpec((None, d, X_WIDTH), lambda l: (l, 0, 0))
    ospec = pl.BlockSpec((None, m, X_WIDTH), lambda l: (l, 0, 0))
    return pl.pallas_call(
        _mem_kv_kernel,
        grid=(depth,),
        in_specs=[pl.BlockSpec((m, d), lambda l: (0, 0)), pl.BlockSpec((1, d), lambda l: (0, 0)),
                  wspec, wspec,
                  pl.BlockSpec((None, rows // depth, cols), lambda l: (0, l, 0))],
        out_specs=[ospec, ospec, pl.BlockSpec((rows // depth, cols), lambda l: (l, 0))],
        out_shape=[jax.ShapeDtypeStruct((depth, m, X_WIDTH), BF16)] * 2
        + [jax.ShapeDtypeStruct((rows, cols), BF16)],
        compiler_params=_params("arbitrary"),
        name="mem_kv",
    )(mem, g, wk, wv, w_in)


def _xattn_kernel(h_ref, g_ref, wq_ref, k_ref, v_ref, wo_ref, o_ref, wqb_ref, wob_ref):
    @pl.when((pl.program_id(0) == 0) & (pl.program_id(1) == 0))
    def _():
        wqb_ref[...] = wq_ref[...].astype(BF16)
        wob_ref[...] = wo_ref[...].astype(BF16)

    h = h_ref[...]
    hn = _rms(h, g_ref[...]).astype(BF16)
    q = jnp.dot(hn, wqb_ref[...], preferred_element_type=F32) * (X_HEAD_DIM ** -0.5 * LOG2E)
    cols = [slice(hd * X_HEAD_DIM, (hd + 1) * X_HEAD_DIM) for hd in range(X_HEADS)]
    qb = q.astype(BF16)
    s = [_nt_dot(qb[:, c], k_ref[:, c]) for c in cols]
    m = [jnp.max(x, axis=-1, keepdims=True) for x in s]
    p = [jnp.exp2(x - mx) for x, mx in zip(s, m)]
    den = [jnp.sum(x, axis=-1, keepdims=True) for x in p]
    p = [x / dx for x, dx in zip(p, den)]
    outs = [jnp.dot(x.astype(BF16), v_ref[:, c], preferred_element_type=F32) for x, c in zip(p, cols)]
    o = jnp.concatenate(outs, axis=-1).astype(BF16)
    o_ref[...] = h + jnp.dot(o, wob_ref[...], preferred_element_type=F32)


def cross_attention(h, g, wq, k, v, wo, layer, bsz, seq, tm=512):
    d = h.shape[-1]
    nl = seq // tm
    mlen = k.shape[1] // bsz
    return pl.pallas_call(
        _xattn_kernel,
        grid=(bsz, nl),
        in_specs=[
            pl.BlockSpec((tm, d), lambda b, l: (b * nl + l, 0)),
            pl.BlockSpec((None, 1, d), lambda b, l: (layer, 0, 0)),
            pl.BlockSpec((None, d, X_WIDTH), lambda b, l: (layer, 0, 0)),
            pl.BlockSpec((None, mlen, X_WIDTH), lambda b, l: (layer, b, 0)),
            pl.BlockSpec((None, mlen, X_WIDTH), lambda b, l: (layer, b, 0)),
            pl.BlockSpec((None, X_WIDTH, d), lambda b, l: (layer, 0, 0)),
        ],
        out_specs=pl.BlockSpec((tm, d), lambda b, l: (b * nl + l, 0)),
        out_shape=jax.ShapeDtypeStruct(h.shape, F32),
        scratch_shapes=[pltpu.VMEM((d, X_WIDTH), BF16), pltpu.VMEM((X_WIDTH, d), BF16)],
        compiler_params=_params("arbitrary", "arbitrary"),
        name="cross_attention",
    )(h, g, wq, k, v, wo)


def _mlp_kernel(h_ref, g_ref, wu_ref, wd_ref, *rest, out_norm):
    o_ref, hn_ref = rest[-2:]
    j = pl.program_id(1)

    @pl.when(j == 0)
    def _():
        h = h_ref[...]
        hn_ref[...] = _rms(h, g_ref[...]).astype(BF16)
        o_ref[...] = h

    a = jnp.dot(hn_ref[...], wu_ref[...], preferred_element_type=F32)
    a = jnp.square(jnp.maximum(a, 0.0)).astype(BF16)
    o_ref[...] += jnp.dot(a, wd_ref[...], preferred_element_type=F32)

    if out_norm:
        @pl.when(j == pl.num_programs(1) - 1)
        def _():
            o_ref[...] = _rms(o_ref[...], rest[0][...])


def mlp(h, g, wu, wd, layer, out_norm_g=None, tm=1024, tf=512):
    m, d = h.shape
    f = wu.shape[-1]
    out_norm = out_norm_g is not None
    extra_specs = [pl.BlockSpec((1, d), lambda i, j: (0, 0))] if out_norm else []
    extra_args = (out_norm_g,) if out_norm else ()
    return pl.pallas_call(
        functools.partial(_mlp_kernel, out_norm=out_norm),
        grid=(m // tm, f // tf),
        in_specs=[
            pl.BlockSpec((tm, d), lambda i, j: (i, 0)),
            pl.BlockSpec((None, 1, d), lambda i, j: (layer, 0, 0)),
            pl.BlockSpec((d, tf), lambda i, j: (0, j)),
            pl.BlockSpec((tf, d), lambda i, j: (j, 0)),
        ] + extra_specs,
        out_specs=pl.BlockSpec((tm, d), lambda i, j: (i, 0)),
        out_shape=jax.ShapeDtypeStruct((m, d), F32),
        scratch_shapes=[pltpu.VMEM((tm, d), BF16)],
        compiler_params=_params("parallel", "arbitrary"),
        name="mlp",
    )(h, g, wu, wd, *extra_args)


def kernel(x, mem, rel_bias, mem_norm_g, norm_mix_g, w_in, s5_lam_re, s5_lam_im, s5_log_dt, s5_b_re, s5_b_im, s5_c_re, s5_c_im, s5_d, s5_w_glu, pool_w, pool_scale, conv_w_dw, conv_b_dw, conv_ln_g, conv_ln_b, conv_w_pw, grp_norm_g, w_out, norm_x_g, w_xq, w_xk, w_xv, w_xo, norm_mlp_g, w_up, w_down, norm_final_g):
    bsz, seq, d = x.shape
    depth = w_in.shape[0]
    vec = lambda a: a[:, None, :]

    k_mem, v_mem, w_in_b = mem_kv(mem.reshape(bsz * mem.shape[1], d), mem_norm_g[None, :], w_xk, w_xv, w_in)
    pow_r, pow_i, bmat, cmat = s5_params(s5_lam_re, s5_lam_im, s5_log_dt, s5_b_re, s5_b_im, s5_c_re, s5_c_im,
                                         chunk=S5_BLOCK // SUBLANES)
    coef_r, coef_i = s5_scan_coefficients(pow_r, pow_i)
    bias_table = attention_bias_table(rel_bias, seq)

    norm_mix, s5_dv, pool_sc = vec(norm_mix_g), vec(s5_d), vec(pool_scale)
    b_dw, ln_g, ln_b = vec(conv_b_dw), vec(conv_ln_g), vec(conv_ln_b)
    grp_g, norm_x, norm_mlp = vec(grp_norm_g), vec(norm_x_g), vec(norm_mlp_g)

    h = x.reshape(bsz * seq, d)
    for l in range(depth):
        proj, y_c, y_b = proj_conv(h, norm_mix, w_in_b, conv_w_dw, b_dw, ln_g, ln_b, conv_w_pw,
                                   pool_w, pool_sc, l, bsz, seq)
        y_a = s5_mixer(proj, bmat, cmat, coef_r, coef_i, s5_dv, s5_w_glu, l, bsz, seq)
        casts = [(w_up, l), (w_down, l)] + ([(w_in, l + 1)] if l + 1 < depth else [])
        y_d, w_up_b, w_down_b, *nxt = dilated_attention(proj, bias_table, casts, bsz, seq)
        w_in_b = nxt[0] if nxt else None
        h = mix_out(y_a, y_b, y_c, y_d, grp_g, w_out, h, l)
        h = cross_attention(h, norm_x, w_xq, k_mem, v_mem, w_xo, l, bsz, seq)
        h = mlp(h, norm_mlp, w_up_b, w_down_b, l,
                out_norm_g=norm_final_g[None, :] if l == depth - 1 else None)
    return h.reshape(bsz, seq, d)
```

```python
import functools
import math

import numpy as np
import jax
import jax.numpy as jnp
from jax import lax
from jax.experimental import pallas as pl
from jax.experimental.pallas import tpu as pltpu

F32 = jnp.float32
BF16 = jnp.bfloat16

GROUP_W = 512
S5_GROUPS = 32
S5_CH = 16
S5_STATE = 64
S5_NSTATE = S5_GROUPS * S5_STATE
S5_SCAN_LANE_BLOCKS = 8
S5_BLOCK = 512
S5_SCAN_UNROLL = 4
POOL_WINDOWS = (2, 4, 8, 16)
POOL_CH = 128
CONV_WIDTH = 31
ATT_HEADS = 8
ATT_HEAD_DIM = 64
DILATED_PATTERNS = ((128, 1), (512, 4), (2048, 16))
ATT_BLOCK = 128
REL_BUCKETS = 32
REL_MAX_DIST = 2048
X_HEADS = 4
X_HEAD_DIM = 128
X_WIDTH = 512
NORM_EPS = 1e-6
NEG_INF = -1e30
LOG2E = math.log2(math.e)

SUBLANES = 8
LANES = 128
VMEM_LIMIT = 58 * 1024 * 1024


def _params(*sem):
    return pltpu.CompilerParams(dimension_semantics=sem, vmem_limit_bytes=VMEM_LIMIT)


def _rms(x, g):
    return x * lax.rsqrt(jnp.mean(x * x, axis=-1, keepdims=True) + NORM_EPS) * g


MXU_COLS = 256
POOL_HALO = 16
CONV_HALO = 32
CONV_ROWS = 32


def _proj_conv_kernel(x_ref, g_ref, w_ref, wdw_ref, bdw_ref, lng_ref, lnb_ref, wpw_ref, pw_ref, psc_ref,
                      proj_ref, yc_ref, yb_ref, sh_ref, cv_ref, wb_ref, xn_ref, ext_ref, *, tl, nl):
    step = pl.program_id(0)
    n = tl + CONV_HALO

    @pl.when(step == 0)
    def _():
        sh_ref[0] = jnp.zeros(sh_ref.shape[1:], F32)
        ext_ref[...] = jnp.zeros(ext_ref.shape, F32)

    @pl.when(lax.rem(step - 1, nl) == 0)
    def _():
        sh_ref[0, 0:CONV_HALO, :] = jnp.zeros((CONV_HALO, GROUP_W), F32)
        ext_ref[0:POOL_HALO, :] = jnp.zeros((POOL_HALO, GROUP_W), F32)

    sh_ref[0, n:, :] = jnp.zeros((SUBLANES, GROUP_W), F32)
    for s in range(1, SUBLANES):
        sh_ref[s, 0:n, :] = sh_ref[0, s:s + n, :]

    for k in range(CONV_WIDTH):
        wb_ref[k] = jnp.broadcast_to(wdw_ref[k:k + 1, :], (SUBLANES, GROUP_W))
    wb_ref[CONV_WIDTH] = jnp.broadcast_to(bdw_ref[...], (SUBLANES, GROUP_W))

    xn_ref[...] = _rms(x_ref[...], g_ref[...]).astype(BF16)
    slice_cols = 2 * MXU_COLS
    n_slices = proj_ref.shape[-1] // slice_cols
    n_chunks = tl // CONV_ROWS
    per_slice = n_chunks // n_slices
    tiles = CONV_ROWS // SUBLANES
    off = CONV_HALO - (CONV_WIDTH - 1)

    def conv_chunk(c):
        r0 = c * CONV_ROWS
        acc = jnp.broadcast_to(wb_ref[CONV_WIDTH][None], (tiles, SUBLANES, GROUP_W))
        for k in range(CONV_WIDTH):
            q, s = divmod(off + k, SUBLANES)
            x = sh_ref[s, r0 + q * SUBLANES:r0 + q * SUBLANES + CONV_ROWS, :]
            acc = acc + wb_ref[k][None] * x.reshape(tiles, SUBLANES, GROUP_W)
        cv_ref[r0:r0 + CONV_ROWS, :] = acc.reshape(CONV_ROWS, GROUP_W)

    def pool_tile():
        t = lax.rem(jnp.maximum(step - 1, 0), nl) * tl + lax.broadcasted_iota(jnp.int32, (tl, 1), 0)
        for gi, w in enumerate(POOL_WINDOWS):
            pc = slice(gi * POOL_CH, (gi + 1) * POOL_CH)
            acc = ext_ref[POOL_HALO:POOL_HALO + tl, pc]
            for k in range(1, w):
                acc = acc + ext_ref[POOL_HALO - k:POOL_HALO - k + tl, pc]
            p = acc / jnp.minimum(t + 1, w).astype(F32) - ext_ref[POOL_HALO:POOL_HALO + tl, pc]
            y = jnp.dot(p.astype(BF16), pw_ref[gi].astype(BF16), preferred_element_type=F32)
            yb_ref[:, pc] = y * psc_ref[:, pc]

    for j in range(n_slices):
        cols = slice(j * slice_cols, (j + 1) * slice_cols)
        proj_ref[:, cols] = jnp.dot(xn_ref[...], w_ref[:, cols], preferred_element_type=F32)
        if j == 0:
            pool_tile()
        for c in range(j * per_slice, (j + 1) * per_slice):
            conv_chunk(c)
        if per_slice:
            r0 = ((j + 1) * per_slice - 1) * CONV_ROWS
            zero = pltpu.bitcast(lax.shift_right_logical(
                pltpu.bitcast(cv_ref[r0:r0 + 2 * SUBLANES, 0:LANES], jnp.uint32), jnp.uint32(32)), F32)
            xn_ref[0:2 * SUBLANES, 0:LANES] = xn_ref[0:2 * SUBLANES, 0:LANES] + zero.astype(BF16)
    for c in range(n_slices * per_slice, n_chunks):
        conv_chunk(c)

    h = cv_ref[...]
    hc = h - jnp.mean(h, axis=-1, keepdims=True)
    y = hc * lax.rsqrt(jnp.mean(hc * hc, axis=-1, keepdims=True) + NORM_EPS)
    y = y * lng_ref[...] + lnb_ref[...]
    y = jax.nn.silu(y)
    yc_ref[...] = jnp.dot(y.astype(BF16), wpw_ref[...].astype(BF16), preferred_element_type=F32)

    ext_ref[0:POOL_HALO, :] = ext_ref[tl:tl + POOL_HALO, :]
    ext_ref[POOL_HALO:, :] = proj_ref[:, GROUP_W:2 * GROUP_W]
    sh_ref[0, 0:CONV_HALO, :] = sh_ref[0, tl:n, :]
    sh_ref[0, CONV_HALO:n, :] = (proj_ref[:, 2 * GROUP_W:3 * GROUP_W]
                                 * jax.nn.sigmoid(proj_ref[:, 3 * GROUP_W:4 * GROUP_W]))


def proj_conv(x, g, w, w_dw, b_dw, ln_g, ln_b, w_pw, pool_w, pool_scale, layer, bsz, seq, tl=512):
    m, d = x.shape
    n = w.shape[-1]
    nl = seq // tl
    nt = m // tl
    vec = pl.BlockSpec((None, 1, GROUP_W), lambda s: (layer, 0, 0))
    cur = lambda s: (jnp.minimum(s, nt - 1), 0)
    prev = lambda s: (jnp.maximum(s - 1, 0), 0)
    return pl.pallas_call(
        functools.partial(_proj_conv_kernel, tl=tl, nl=nl),
        grid=(nt + 1,),
        in_specs=[
            pl.BlockSpec((tl, d), cur),
            pl.BlockSpec((None, 1, d), lambda s: (layer, 0, 0)),
            pl.BlockSpec((d, n), lambda s: (0, 0), pipeline_mode=pl.Buffered(1)),
            pl.BlockSpec((None, CONV_WIDTH, GROUP_W), lambda s: (layer, 0, 0)),
            vec, vec, vec,
            pl.BlockSpec((None, GROUP_W, GROUP_W), lambda s: (layer, 0, 0)),
            pl.BlockSpec((None, len(POOL_WINDOWS), POOL_CH, POOL_CH), lambda s: (layer, 0, 0, 0)),
            vec,
        ],
        out_specs=[pl.BlockSpec((tl, n), cur), pl.BlockSpec((tl, GROUP_W), prev),
                   pl.BlockSpec((tl, GROUP_W), prev)],
        out_shape=[jax.ShapeDtypeStruct((m, n), F32), jax.ShapeDtypeStruct((m, GROUP_W), F32),
                   jax.ShapeDtypeStruct((m, GROUP_W), F32)],
        scratch_shapes=[pltpu.VMEM((SUBLANES, CONV_HALO + tl + SUBLANES, GROUP_W), F32),
                        pltpu.VMEM((tl, GROUP_W), F32),
                        pltpu.VMEM((CONV_WIDTH + 1, SUBLANES, GROUP_W), F32),
                        pltpu.VMEM((tl, d), BF16),
                        pltpu.VMEM((POOL_HALO + tl, GROUP_W), F32)],
        compiler_params=_params("arbitrary"),
        name="proj_conv",
    )(x, g, w, w_dw, b_dw, ln_g, ln_b, w_pw, pool_w, pool_scale)


def _s5_param_kernel(lr_ref, li_ref, ldt_ref, lrx_ref, lix_ref, ldtx_ref, br_ref, bi_ref, cr_ref, ci_ref,
                     pr_ref, pi_ref, bmat_ref, cmat_ref, *, chunk):
    def discretise(lr, li, ldt):
        dt = jnp.exp(ldt)
        mag = jnp.exp(lr * dt)
        return mag * jnp.cos(li * dt), mag * jnp.sin(li * dt)

    cr, ci = discretise(lr_ref[...], li_ref[...], ldt_ref[...])
    slot = {1: 0, chunk: 1, 2 * chunk: 2, 4 * chunk: 3}
    e = 1
    while True:
        if e in slot:
            pr_ref[slot[e]] = cr
            pi_ref[slot[e]] = ci
        if e == 4 * chunk:
            break
        cr, ci = cr * cr - ci * ci, 2.0 * (cr * ci)
        e *= 2

    lr, li = lrx_ref[...], lix_ref[...]
    ar, ai = discretise(lr, li, ldtx_ref[...])
    den = lr * lr + li * li
    nr, ni = ar - 1.0, ai
    f_r = (nr * lr + ni * li) / den
    f_i = (ni * lr - nr * li) / den
    b_r, b_i = br_ref[...], bi_ref[...]
    bb_r = f_r * b_r - f_i * b_i
    bb_i = f_r * b_i + f_i * b_r

    def block_diagonal(x, row_shift, col_shift):
        rows, w = x.shape
        cols = (rows >> row_shift) << col_shift
        sel = ((lax.broadcasted_iota(jnp.int32, (w, cols), 1) & (w - 1))
               == lax.broadcasted_iota(jnp.int32, (w, cols), 0))
        tiled = jnp.dot(x.astype(BF16), jnp.where(sel, 1.0, 0.0).astype(BF16), preferred_element_type=F32)
        same = ((lax.broadcasted_iota(jnp.int32, (rows, cols), 0) >> row_shift)
                == (lax.broadcasted_iota(jnp.int32, (rows, cols), 1) >> col_shift))
        return jnp.where(same, tiled, 0.0).astype(BF16)

    ch_shift, st_shift = S5_CH.bit_length() - 1, S5_STATE.bit_length() - 1
    bmat_ref[:, 0:S5_NSTATE] = block_diagonal(bb_r, ch_shift, st_shift)
    bmat_ref[:, S5_NSTATE:] = block_diagonal(bb_i, ch_shift, st_shift)
    cmat_ref[0:S5_NSTATE, :] = block_diagonal(cr_ref[...], st_shift, ch_shift)
    cmat_ref[S5_NSTATE:, :] = block_diagonal(-ci_ref[...], st_shift, ch_shift)


S5_POWERS = 4


def s5_params(lam_re, lam_im, log_dt, b_re, b_im, c_re, c_im, chunk):
    depth = lam_re.shape[0]
    gc = S5_GROUPS * S5_CH
    rep = lambda a: jnp.repeat(a, S5_CH, axis=1)
    ldt = log_dt[..., None]
    bt = lambda b: jnp.transpose(b, (0, 1, 3, 2)).reshape(depth, gc, S5_STATE)
    ct = lambda c: jnp.transpose(c, (0, 1, 3, 2)).reshape(depth, S5_NSTATE, S5_CH)
    small = pl.BlockSpec((None, S5_GROUPS, S5_STATE), lambda l: (l, 0, 0))
    small1 = pl.BlockSpec((None, S5_GROUPS, 1), lambda l: (l, 0, 0))
    big = pl.BlockSpec((None, gc, S5_STATE), lambda l: (l, 0, 0))
    big1 = pl.BlockSpec((None, gc, 1), lambda l: (l, 0, 0))
    cspec = pl.BlockSpec((None, S5_NSTATE, S5_CH), lambda l: (l, 0, 0))
    powspec = pl.BlockSpec((None, S5_POWERS, S5_GROUPS, S5_STATE), lambda l: (l, 0, 0, 0))
    return pl.pallas_call(
        functools.partial(_s5_param_kernel, chunk=chunk),
        grid=(depth,),
        in_specs=[small, small, small1, big, big, big1, big, big, cspec, cspec],
        out_specs=[powspec, powspec,
                   pl.BlockSpec((None, gc, 2 * S5_NSTATE), lambda l: (l, 0, 0)),
                   pl.BlockSpec((None, 2 * S5_NSTATE, gc), lambda l: (l, 0, 0))],
        out_shape=[
            jax.ShapeDtypeStruct((depth, S5_POWERS, S5_GROUPS, S5_STATE), F32),
            jax.ShapeDtypeStruct((depth, S5_POWERS, S5_GROUPS, S5_STATE), F32),
            jax.ShapeDtypeStruct((depth, gc, 2 * S5_NSTATE), BF16),
            jax.ShapeDtypeStruct((depth, 2 * S5_NSTATE, gc), BF16),
        ],
        compiler_params=_params("arbitrary"),
        name="s5_params",
    )(lam_re, lam_im, ldt, rep(lam_re), rep(lam_im), rep(ldt), bt(b_re), bt(b_im), ct(c_re), ct(c_im))


def s5_scan_coefficients(pow_r, pow_i):
    depth = pow_r.shape[0]
    row = jnp.arange(SUBLANES)[None, :, None]

    def coef(p):
        p = p.reshape(depth, S5_POWERS, 1, S5_NSTATE)
        full = [jnp.broadcast_to(p[:, e], (depth, SUBLANES, S5_NSTATE)) for e in (0, 1)]
        masked = [jnp.where(row >= s, p[:, e], 0.0) for e, s in ((1, 1), (2, 2), (3, 4))]
        return jnp.stack(full + masked, axis=1)

    return coef(pow_r), coef(pow_i)


S5_COEF_TILES = 5


def _s5_kernel(u0_ref, u1_ref, u2_ref, u3_ref, bmat_ref, cmat_ref, cfr_ref, cfi_ref, d_ref, wglu_ref, o_ref,
               up_ref, uperm_ref, xs_ref, cin_ref, yp_ref, *, tl):
    n = S5_NSTATE
    tc = tl // SUBLANES
    pitch = tc + SUBLANES
    u_refs = (u0_ref, u1_ref, u2_ref, u3_ref)
    l = pl.program_id(1)

    @pl.when(l == 0)
    def _():
        cin_ref[...] = jnp.zeros(cin_ref.shape, F32)

    for j, u_ref in enumerate(u_refs):
        for c in range(SUBLANES):
            up_ref[j, c * pitch:c * pitch + tc, :] = u_ref[c * tc:(c + 1) * tc, :]

    def gather(t, carry):
        r = pl.multiple_of(t * SUBLANES, SUBLANES)
        for j in range(len(u_refs)):
            uperm_ref[pl.ds(r, SUBLANES), j * LANES:(j + 1) * LANES] = up_ref[j, pl.ds(t, SUBLANES, stride=pitch), :]
        return carry

    lax.fori_loop(0, tc, gather, 0, unroll=S5_SCAN_UNROLL)

    ub = uperm_ref[...].astype(BF16)
    blocks_per_part = n // MXU_COLS
    for jb in range(2 * blocks_per_part):
        k = (jb % blocks_per_part) * MXU_COLS // S5_STATE * S5_CH // LANES
        xs_ref[:, jb * MXU_COLS:(jb + 1) * MXU_COLS] = jnp.dot(
            ub[:, k * LANES:(k + 1) * LANES], bmat_ref[k * LANES:(k + 1) * LANES, jb * MXU_COLS:(jb + 1) * MXU_COLS],
            preferred_element_type=F32)

    row = lax.broadcasted_iota(jnp.int32, (SUBLANES, LANES), 0)

    def cmul_add(ar, ai, xr, xi, br, bi):
        return (ar * xr - ai * xi) + br, (ar * xi + ai * xr) + bi

    def scan_lanes(lb0):
        cols = [(slice(lb * LANES, (lb + 1) * LANES), slice(n + lb * LANES, n + (lb + 1) * LANES))
                for lb in range(lb0, lb0 + S5_SCAN_LANE_BLOCKS)]
        a = [(cfr_ref[0, :, cr_], cfi_ref[0, :, cr_]) for cr_, _ in cols]

        def recurrence(init, store):
            def step(t, x):
                base = pl.multiple_of(t * SUBLANES, SUBLANES)
                new = []
                for q, (cr_, ci_) in enumerate(cols):
                    nr, ni = cmul_add(a[q][0], a[q][1], x[2 * q], x[2 * q + 1],
                                      xs_ref[pl.ds(base, SUBLANES), cr_], xs_ref[pl.ds(base, SUBLANES), ci_])
                    if store:
                        xs_ref[pl.ds(base, SUBLANES), cr_] = nr
                        xs_ref[pl.ds(base, SUBLANES), ci_] = ni
                    new += [nr, ni]
                return tuple(new)
            return lax.fori_loop(0, tc, step, tuple(init), unroll=S5_SCAN_UNROLL)

        zero = jnp.zeros((SUBLANES, LANES), F32)
        ends = recurrence([zero] * (2 * len(cols)), store=False)

        starts = []
        for q, (cr_, ci_) in enumerate(cols):
            er, ei = ends[2 * q], ends[2 * q + 1]
            wr = jnp.where(row == 0, cin_ref[:, cr_], pltpu.roll(er, 1, 0))
            wi = jnp.where(row == 0, cin_ref[:, ci_], pltpu.roll(ei, 1, 0))
            for k, s in ((2, 1), (3, 2), (4, 4)):
                wr, wi = cmul_add(cfr_ref[k, :, cr_], cfi_ref[k, :, cr_],
                                  pltpu.roll(wr, s, 0), pltpu.roll(wi, s, 0), wr, wi)
            tr, ti = cmul_add(cfr_ref[1, :, cr_], cfi_ref[1, :, cr_], wr, wi, er, ei)
            cin_ref[:, cr_] = pltpu.roll(tr, 1, 0)
            cin_ref[:, ci_] = pltpu.roll(ti, 1, 0)
            starts += [wr, wi]
        recurrence(starts, store=True)

    for lb0 in range(0, n // LANES, S5_SCAN_LANE_BLOCKS):
        scan_lanes(lb0)

    ys = []
    for m in range(GROUP_W // MXU_COLS):
        k0 = m * MXU_COLS // S5_CH * S5_STATE
        k1 = (m + 1) * MXU_COLS // S5_CH * S5_STATE
        cols = slice(m * MXU_COLS, (m + 1) * MXU_COLS)
        ys.append(jnp.dot(xs_ref[:, k0:k1].astype(BF16), cmat_ref[k0:k1, cols], preferred_element_type=F32)
                  + jnp.dot(xs_ref[:, n + k0:n + k1].astype(BF16), cmat_ref[n + k0:n + k1, cols],
                            preferred_element_type=F32))
    y = jnp.concatenate(ys, axis=-1) + d_ref[...] * uperm_ref[...]
    g = jax.nn.gelu(y)
    gate = jnp.dot(g.astype(BF16), wglu_ref[...].astype(BF16), preferred_element_type=F32)
    out = g * jax.nn.sigmoid(gate)
    for j in range(len(u_refs)):
        yp_ref[j] = out[:, j * LANES:(j + 1) * LANES]

    tiles_per_chunk = tc // SUBLANES
    shift = tiles_per_chunk.bit_length() - 1

    def scatter(idx, carry):
        c = idx >> shift
        m = idx & (tiles_per_chunk - 1)
        r = pl.multiple_of(idx * SUBLANES, SUBLANES)
        src = m * (SUBLANES * SUBLANES) + c
        for j in range(len(u_refs)):
            o_ref[pl.ds(r, SUBLANES), j * LANES:(j + 1) * LANES] = yp_ref[j, pl.ds(src, SUBLANES, stride=SUBLANES), :]
        return carry

    lax.fori_loop(0, SUBLANES * tiles_per_chunk, scatter, 0, unroll=S5_SCAN_UNROLL)


def s5_mixer(proj, bmat, cmat, coef_r, coef_i, d_skip, w_glu, layer, bsz, seq, tl=S5_BLOCK):
    nl = seq // tl
    n2 = 2 * S5_NSTATE
    tc = tl // SUBLANES
    assert tc % SUBLANES == 0 and tc & (tc - 1) == 0
    nu = GROUP_W // LANES
    uspecs = [pl.BlockSpec((tl, LANES), lambda b, l, j=j: (b * nl + l, j)) for j in range(nu)]
    coef = pl.BlockSpec((None, S5_COEF_TILES, SUBLANES, S5_NSTATE), lambda b, l: (layer, 0, 0, 0))
    return pl.pallas_call(
        functools.partial(_s5_kernel, tl=tl),
        grid=(bsz, nl),
        in_specs=uspecs + [
            pl.BlockSpec((None, GROUP_W, n2), lambda b, l: (layer, 0, 0)),
            pl.BlockSpec((None, n2, GROUP_W), lambda b, l: (layer, 0, 0)),
            coef, coef,
            pl.BlockSpec((None, 1, GROUP_W), lambda b, l: (layer, 0, 0)),
            pl.BlockSpec((None, GROUP_W, GROUP_W), lambda b, l: (layer, 0, 0)),
        ],
        out_specs=pl.BlockSpec((tl, GROUP_W), lambda b, l: (b * nl + l, 0)),
        out_shape=jax.ShapeDtypeStruct((bsz * seq, GROUP_W), F32),
        scratch_shapes=[
            pltpu.VMEM((nu, SUBLANES * (tc + SUBLANES), LANES), F32),
            pltpu.VMEM((tl, GROUP_W), F32),
            pltpu.VMEM((tl, n2), F32),
            pltpu.VMEM((SUBLANES, n2), F32),
            pltpu.VMEM((nu, tl, LANES), F32),
        ],
        compiler_params=_params("parallel", "arbitrary"),
        name="s5_mixer",
    )(*([proj] * nu), bmat, cmat, coef_r, coef_i, d_skip, w_glu)


def _t5_bucket(dist):
    n = np.maximum(dist, 0)
    max_exact = REL_BUCKETS // 2
    large = max_exact + (np.log(np.maximum(n, 1) / max_exact) / np.log(REL_MAX_DIST / max_exact)
                         * (REL_BUCKETS - max_exact)).astype(np.int64)
    large = np.minimum(large, REL_BUCKETS - 1)
    return np.where(n < max_exact, n, large).astype(np.int32)


ATT_NEAR_BLOCKS = 5
ATT_NEAR = ATT_NEAR_BLOCKS * ATT_BLOCK
ATT_FAR_DIL = 16


def _attention_bias_layout(sub):
    a = np.arange(ATT_BLOCK)[:, None]
    d_near = (ATT_NEAR - ATT_BLOCK) + a - np.arange(ATT_NEAR)[None, :]
    mult = np.zeros(d_near.shape, np.int64)
    for window, dil in DILATED_PATTERNS:
        if dil != ATT_FAR_DIL:
            mult += (d_near >= 0) & (d_near % dil == 0) & (d_near // dil <= window // dil)
    d_far = ATT_FAR_DIL * (a - np.arange(sub)[None, :])
    mult_far = ((d_far >= 0) & (d_far // ATT_FAR_DIL <= ATT_BLOCK)).astype(np.int64)
    d = np.concatenate([d_near, d_far], axis=1)
    mult = np.concatenate([mult, mult_far], axis=1)
    addend = np.where(mult > 0, np.log(np.maximum(mult, 1)), NEG_INF).astype(np.float32)
    return _t5_bucket(d), addend


def _attn_bias_kernel(rb_ref, bucket_ref, add_ref, o_ref):
    h = pl.program_id(0)
    bucket = bucket_ref[...]
    t = jnp.zeros(bucket.shape, F32)
    for b in range(REL_BUCKETS):
        t = jnp.where(bucket == b, rb_ref[b, h], t)
    o_ref[...] = (t + add_ref[...]) * LOG2E


def attention_bias_table(rel_bias, seq):
    sub = seq // ATT_FAR_DIL
    bucket, addend = _attention_bias_layout(sub)
    width = ATT_NEAR + sub
    full = pl.BlockSpec((ATT_BLOCK, width), lambda h: (0, 0))
    return pl.pallas_call(
        _attn_bias_kernel,
        grid=(ATT_HEADS,),
        in_specs=[pl.BlockSpec(memory_space=pltpu.SMEM), full, full],
        out_specs=pl.BlockSpec((None, ATT_BLOCK, width), lambda h: (h, 0, 0)),
        out_shape=jax.ShapeDtypeStruct((ATT_HEADS, ATT_BLOCK, width), F32),
        compiler_params=_params("arbitrary"),
        name="attn_bias",
    )(rel_bias, jnp.asarray(bucket), jnp.asarray(addend))


ATT_HEADS_PER_BLOCK = LANES // ATT_HEAD_DIM
ATT_GROUP = 8


def _nt_dot(a, b):
    return lax.dot_general(a, b, (((1,), (1,)), ((), ())), preferred_element_type=F32)


def _attn_kernel(q_ref, k_ref, v_ref, bt_ref, *rest, seq, n_cast):
    cast_in, o_ref, cast_out = rest[:n_cast], rest[n_cast], rest[n_cast + 1:2 * n_cast + 1]
    kb_ref, vb_ref, m3_ref, l3_ref, a3_ref = rest[2 * n_cast + 1:]
    sub = seq // ATT_FAR_DIL
    pad = ATT_NEAR - ATT_BLOCK
    scale = ATT_HEAD_DIM ** -0.5 * LOG2E
    lane = lax.broadcasted_iota(jnp.int32, (1, LANES), 1)
    in_head = [(lane >= hh * ATT_HEAD_DIM) & (lane < (hh + 1) * ATT_HEAD_DIM)
               for hh in range(ATT_HEADS_PER_BLOCK)]

    def by_head(vals):
        out = vals[-1]
        for hh in range(ATT_HEADS_PER_BLOCK - 2, -1, -1):
            out = jnp.where(in_head[hh], vals[hh], out)
        return out

    def softmax_pv(tiles):
        heads = range(ATT_HEADS_PER_BLOCK)
        s = [[_nt_dot(jnp.where(in_head[hh], qs, 0.0).astype(BF16), kb) + bias_of(hh) for hh in heads]
             for qs, kb, _, bias_of in tiles]
        m = [[jnp.max(s_h, axis=-1, keepdims=True) for s_h in s_t] for s_t in s]
        p = [[jnp.exp2(s_h - m_h) for s_h, m_h in zip(s_t, m_t)] for s_t, m_t in zip(s, m)]
        acc = [[jnp.dot(p_h.astype(BF16), tile[2](hh), preferred_element_type=F32) for hh, p_h in enumerate(p_t)]
               for p_t, tile in zip(p, tiles)]
        out = []
        for tile, m_t, acc_t in zip(tiles, m, acc):
            full = (tile[0].shape[0], LANES)
            den = pltpu.roll(by_head(acc_t[::-1]), ATT_HEAD_DIM, 1)
            out.append((by_head([jnp.broadcast_to(x, full) for x in m_t]), den, by_head(acc_t)))
        return out

    for src, dst in zip(cast_in, cast_out):
        dst[...] = src[...].astype(BF16)

    def with_ones(v, hh):
        return jnp.where(in_head[hh], v, 1.0).astype(BF16)

    kb_ref[...] = k_ref[...].astype(BF16)
    for hh in range(ATT_HEADS_PER_BLOCK):
        vb_ref[hh] = with_ones(v_ref[...], hh)

    for g0 in range(0, ATT_FAR_DIL, ATT_GROUP):
        group = [pl.ds(r, sub, stride=ATT_FAR_DIL) for r in range(g0, g0 + ATT_GROUP)]
        far_bias = lambda hh: bt_ref[hh, 0:sub, ATT_NEAR:ATT_NEAR + sub]
        tiles = [(q_ref[rows, :] * scale, k_ref[rows, :].astype(BF16),
                  lambda hh, rows=rows: with_ones(v_ref[rows, :], hh), far_bias) for rows in group]
        for rows, (m, l, acc) in zip(group, softmax_pv(tiles)):
            m3_ref[rows, :] = m
            l3_ref[rows, :] = l
            a3_ref[rows, :] = acc

    nblocks = seq // ATT_BLOCK
    for g0 in range(0, nblocks, ATT_GROUP):
        group, tiles = [], []
        for i in range(g0, min(g0 + ATT_GROUP, nblocks)):
            r0 = i * ATT_BLOCK
            lo = max(0, r0 - pad)
            c0 = ATT_NEAR - (r0 + ATT_BLOCK - lo)
            group.append(slice(r0, r0 + ATT_BLOCK))
            tiles.append((q_ref[group[-1], :] * scale, kb_ref[lo:r0 + ATT_BLOCK, :],
                          lambda hh, lo=lo, r0=r0: vb_ref[hh, lo:r0 + ATT_BLOCK, :],
                          lambda hh, c0=c0: bt_ref[hh, :, c0:ATT_NEAR]))
        merged = []
        for rows, (m, l, acc) in zip(group, softmax_pv(tiles)):
            m3 = m3_ref[rows, :]
            mm = jnp.maximum(m, m3)
            wn = jnp.exp2(m - mm)
            wf = jnp.exp2(m3 - mm)
            num = acc * wn + a3_ref[rows, :] * wf
            den = l * wn + l3_ref[rows, :] * wf
            merged.append(num / den)
        for rows, o in zip(group, merged):
            o_ref[rows, :] = o


def dilated_attention(proj, bias_table, casts, bsz, seq):
    assert seq % ATT_BLOCK == 0 and seq // ATT_FAR_DIL <= ATT_BLOCK
    nhp = ATT_HEADS // ATT_HEADS_PER_BLOCK
    steps = bsz * nhp
    qcol = 4 * GROUP_W // LANES
    kcol = 5 * GROUP_W // LANES
    vcol = 6 * GROUP_W // LANES
    width = bias_table.shape[-1]
    stat = pltpu.VMEM((seq, LANES), F32)
    half = pltpu.VMEM((seq, LANES), BF16)
    vvar = pltpu.VMEM((ATT_HEADS_PER_BLOCK, seq, LANES), BF16)
    cast_in, cast_out, cast_shapes = [], [], []
    for w, layer in casts:
        rows, cols = w.shape[1:]
        assert rows % steps == 0
        cast_in.append(pl.BlockSpec((None, rows // steps, cols), lambda b, h, layer=layer: (layer, b * nhp + h, 0)))
        cast_out.append(pl.BlockSpec((rows // steps, cols), lambda b, h: (b * nhp + h, 0)))
        cast_shapes.append(jax.ShapeDtypeStruct((rows, cols), BF16))
    return pl.pallas_call(
        functools.partial(_attn_kernel, seq=seq, n_cast=len(casts)),
        grid=(bsz, nhp),
        in_specs=[
            pl.BlockSpec((seq, LANES), lambda b, h: (b, qcol + h)),
            pl.BlockSpec((seq, LANES), lambda b, h: (b, kcol + h)),
            pl.BlockSpec((seq, LANES), lambda b, h: (b, vcol + h)),
            pl.BlockSpec((ATT_HEADS_PER_BLOCK, ATT_BLOCK, width), lambda b, h: (h, 0, 0)),
        ] + cast_in,
        out_specs=[pl.BlockSpec((seq, LANES), lambda b, h: (b, h))] + cast_out,
        out_shape=[jax.ShapeDtypeStruct((bsz * seq, GROUP_W), F32)] + cast_shapes,
        scratch_shapes=[half, vvar, stat, stat, stat],
        compiler_params=_params("parallel", "parallel"),
        name="dilated_attention",
    )(proj, proj, proj, bias_table, *[w for w, _ in casts])


def _mem_kv_kernel(mem_ref, g_ref, wk_ref, wv_ref, w0_ref, k_ref, v_ref, w0b_ref):
    mn = _rms(mem_ref[...], g_ref[...]).astype(BF16)
    k_ref[...] = jnp.dot(mn, wk_ref[...].astype(BF16), preferred_element_type=F32).astype(BF16)
    v_ref[...] = jnp.dot(mn, wv_ref[...].astype(BF16), preferred_element_type=F32).astype(BF16)
    w0b_ref[...] = w0_ref[...].astype(BF16)


def mem_kv(mem, g, wk, wv, w_in):
    m, d = mem.shape
    depth = wk.shape[0]
    rows, cols = w_in.shape[1:]
    assert rows % depth == 0
    wspec = pl.BlockSpec((None, d, X_WIDTH), lambda l: (l, 0, 0))
    ospec = pl.BlockSpec((None, m, X_WIDTH), lambda l: (l, 0, 0))
    return pl.pallas_call(
        _mem_kv_kernel,
        grid=(depth,),
        in_specs=[pl.BlockSpec((m, d), lambda l: (0, 0)), pl.BlockSpec((1, d), lambda l: (0, 0)),
                  wspec, wspec,
                  pl.BlockSpec((None, rows // depth, cols), lambda l: (0, l, 0))],
        out_specs=[ospec, ospec, pl.BlockSpec((rows // depth, cols), lambda l: (l, 0))],
        out_shape=[jax.ShapeDtypeStruct((depth, m, X_WIDTH), BF16)] * 2
        + [jax.ShapeDtypeStruct((rows, cols), BF16)],
        compiler_params=_params("arbitrary"),
        name="mem_kv",
    )(mem, g, wk, wv, w_in)


def _mix_xattn_kernel(ya_ref, yb_ref, yc_ref, yd_ref, gg_ref, wout_ref, h_ref, gx_ref, wq_ref, k_ref, v_ref,
                      wo_ref, o_ref):
    yn = [_rms(y_ref[...], gg_ref[:, gi * GROUP_W:(gi + 1) * GROUP_W]).astype(BF16)
          for gi, y_ref in enumerate((ya_ref, yb_ref, yc_ref, yd_ref))]
    h = h_ref[...] + jnp.dot(jnp.concatenate(yn, axis=-1), wout_ref[...], preferred_element_type=F32)

    hn = _rms(h, gx_ref[...]).astype(BF16)
    q = jnp.dot(hn, wq_ref[...], preferred_element_type=F32) * (X_HEAD_DIM ** -0.5 * LOG2E)
    cols = [slice(hd * X_HEAD_DIM, (hd + 1) * X_HEAD_DIM) for hd in range(X_HEADS)]
    qb = q.astype(BF16)
    s = [_nt_dot(qb[:, c], k_ref[:, c]) for c in cols]
    m = [jnp.max(x, axis=-1, keepdims=True) for x in s]
    p = [jnp.exp2(x - mx) for x, mx in zip(s, m)]
    den = [jnp.sum(x, axis=-1, keepdims=True) for x in p]
    p = [x / dx for x, dx in zip(p, den)]
    outs = [jnp.dot(x.astype(BF16), v_ref[:, c], preferred_element_type=F32) for x, c in zip(p, cols)]
    o = jnp.concatenate(outs, axis=-1).astype(BF16)
    o_ref[...] = h + jnp.dot(o, wo_ref[...], preferred_element_type=F32)


def mix_xattn(ya, yb, yc, yd, grp_g, w_out, h, norm_g, wq, k, v, wo, layer, bsz, seq, tm=512):
    d = h.shape[-1]
    nl = seq // tm
    mlen = k.shape[1] // bsz
    row = lambda b, l: (b * nl + l, 0)
    yspec = pl.BlockSpec((tm, GROUP_W), row)
    gain = pl.BlockSpec((None, 1, d), lambda b, l: (layer, 0, 0))
    whole = lambda shape: pl.BlockSpec(shape, lambda b, l: (0, 0), pipeline_mode=pl.Buffered(1))
    kv = pl.BlockSpec((None, mlen, X_WIDTH), lambda b, l: (layer, b, 0))
    return pl.pallas_call(
        _mix_xattn_kernel,
        grid=(bsz, nl),
        in_specs=[yspec, yspec, yspec, yspec, gain, whole((d, d)), pl.BlockSpec((tm, d), row),
                  gain, whole((d, X_WIDTH)), kv, kv, whole((X_WIDTH, d))],
        out_specs=pl.BlockSpec((tm, d), row),
        out_shape=jax.ShapeDtypeStruct(h.shape, F32),
        compiler_params=_params("parallel", "parallel"),
        name="mix_xattn",
    )(ya, yb, yc, yd, grp_g, w_out, h, norm_g, wq, k, v, wo)


def _mlp_kernel(h_ref, g_ref, wu_ref, wd_ref, *rest, out_norm):
    o_ref, hn_ref = rest[-2:]
    j = pl.program_id(1)

    @pl.when(j == 0)
    def _():
        h = h_ref[...]
        hn_ref[...] = _rms(h, g_ref[...]).astype(BF16)
        o_ref[...] = h

    a = jnp.dot(hn_ref[...], wu_ref[...], preferred_element_type=F32)
    a = jnp.square(jnp.maximum(a, 0.0)).astype(BF16)
    o_ref[...] += jnp.dot(a, wd_ref[...], preferred_element_type=F32)

    if out_norm:
        @pl.when(j == pl.num_programs(1) - 1)
        def _():
            o_ref[...] = _rms(o_ref[...], rest[0][...])


def mlp(h, g, wu, wd, layer, out_norm_g=None, tm=1024, tf=512):
    m, d = h.shape
    f = wu.shape[-1]
    out_norm = out_norm_g is not None
    extra_specs = [pl.BlockSpec((1, d), lambda i, j: (0, 0))] if out_norm else []
    extra_args = (out_norm_g,) if out_norm else ()
    return pl.pallas_call(
        functools.partial(_mlp_kernel, out_norm=out_norm),
        grid=(m // tm, f // tf),
        in_specs=[
            pl.BlockSpec((tm, d), lambda i, j: (i, 0)),
            pl.BlockSpec((None, 1, d), lambda i, j: (layer, 0, 0)),
            pl.BlockSpec((d, tf), lambda i, j: (0, j)),
            pl.BlockSpec((tf, d), lambda i, j: (j, 0)),
        ] + extra_specs,
        out_specs=pl.BlockSpec((tm, d), lambda i, j: (i, 0)),
        out_shape=jax.ShapeDtypeStruct((m, d), F32),
        scratch_shapes=[pltpu.VMEM((tm, d), BF16)],
        compiler_params=_params("parallel", "arbitrary"),
        name="mlp",
    )(h, g, wu, wd, *extra_args)


def kernel(x, mem, rel_bias, mem_norm_g, norm_mix_g, w_in, s5_lam_re, s5_lam_im, s5_log_dt, s5_b_re, s5_b_im, s5_c_re, s5_c_im, s5_d, s5_w_glu, pool_w, pool_scale, conv_w_dw, conv_b_dw, conv_ln_g, conv_ln_b, conv_w_pw, grp_norm_g, w_out, norm_x_g, w_xq, w_xk, w_xv, w_xo, norm_mlp_g, w_up, w_down, norm_final_g):
    bsz, seq, d = x.shape
    depth = w_in.shape[0]
    vec = lambda a: a[:, None, :]

    k_mem, v_mem, w_in_b = mem_kv(mem.reshape(bsz * mem.shape[1], d), mem_norm_g[None, :], w_xk, w_xv, w_in)
    pow_r, pow_i, bmat, cmat = s5_params(s5_lam_re, s5_lam_im, s5_log_dt, s5_b_re, s5_b_im, s5_c_re, s5_c_im,
                                         chunk=S5_BLOCK // SUBLANES)
    coef_r, coef_i = s5_scan_coefficients(pow_r, pow_i)
    bias_table = attention_bias_table(rel_bias, seq)

    norm_mix, s5_dv, pool_sc = vec(norm_mix_g), vec(s5_d), vec(pool_scale)
    b_dw, ln_g, ln_b = vec(conv_b_dw), vec(conv_ln_g), vec(conv_ln_b)
    grp_g, norm_x, norm_mlp = vec(grp_norm_g), vec(norm_x_g), vec(norm_mlp_g)

    h = x.reshape(bsz * seq, d)
    for l in range(depth):
        proj, y_c, y_b = proj_conv(h, norm_mix, w_in_b, conv_w_dw, b_dw, ln_g, ln_b, conv_w_pw,
                                   pool_w, pool_sc, l, bsz, seq)
        y_a = s5_mixer(proj, bmat, cmat, coef_r, coef_i, s5_dv, s5_w_glu, l, bsz, seq)
        casts = ([(w_out, l), (w_xq, l), (w_xo, l), (w_up, l), (w_down, l)]
                 + ([(w_in, l + 1)] if l + 1 < depth else []))
        y_d, w_out_b, w_xq_b, w_xo_b, w_up_b, w_down_b, *nxt = dilated_attention(proj, bias_table, casts, bsz, seq)
        w_in_b = nxt[0] if nxt else None
        h = mix_xattn(y_a, y_b, y_c, y_d, grp_g, w_out_b, h, norm_x, w_xq_b, k_mem, v_mem, w_xo_b, l, bsz, seq)
        h = mlp(h, norm_mlp, w_up_b, w_down_b, l,
                out_norm_g=norm_final_g[None, :] if l == depth - 1 else None)
    return h.reshape(bsz, seq, d)
```

```python
import functools
import math

import numpy as np
import jax
import jax.numpy as jnp
from jax import lax
from jax.experimental import pallas as pl
from jax.experimental.pallas import tpu as pltpu

F32 = jnp.float32
BF16 = jnp.bfloat16

GROUP_W = 512
S5_GROUPS = 32
S5_CH = 16
S5_STATE = 64
S5_NSTATE = S5_GROUPS * S5_STATE
S5_SCAN_LANE_BLOCKS = 8
S5_BLOCK = 1024
S5_SCAN_UNROLL = 4
POOL_WINDOWS = (2, 4, 8, 16)
POOL_CH = 128
CONV_WIDTH = 31
ATT_HEADS = 8
ATT_HEAD_DIM = 64
DILATED_PATTERNS = ((128, 1), (512, 4), (2048, 16))
ATT_BLOCK = 128
REL_BUCKETS = 32
REL_MAX_DIST = 2048
X_HEADS = 4
X_HEAD_DIM = 128
X_WIDTH = 512
NORM_EPS = 1e-6
NEG_INF = -1e30
LOG2E = math.log2(math.e)

SUBLANES = 8
LANES = 128
VMEM_LIMIT = 58 * 1024 * 1024


def _params(*sem):
    return pltpu.CompilerParams(dimension_semantics=sem, vmem_limit_bytes=VMEM_LIMIT)


def _rms(x, g):
    return x * lax.rsqrt(jnp.mean(x * x, axis=-1, keepdims=True) + NORM_EPS) * g


MXU_COLS = 256
POOL_HALO = 16
CONV_HALO = 32
CONV_ROWS = 32


def _proj_conv_kernel(x_ref, g_ref, w_ref, wdw_ref, bdw_ref, lng_ref, lnb_ref, wpw_ref, pw_ref, psc_ref,
                      proj_ref, yc_ref, yb_ref, sh_ref, cv_ref, wb_ref, xn_ref, ext_ref, *, tl, nl):
    step = pl.program_id(0)
    n = tl + CONV_HALO

    @pl.when(step == 0)
    def _():
        sh_ref[0] = jnp.zeros(sh_ref.shape[1:], F32)
        ext_ref[...] = jnp.zeros(ext_ref.shape, F32)

    @pl.when(lax.rem(step - 1, nl) == 0)
    def _():
        sh_ref[0, 0:CONV_HALO, :] = jnp.zeros((CONV_HALO, GROUP_W), F32)
        ext_ref[0:POOL_HALO, :] = jnp.zeros((POOL_HALO, GROUP_W), F32)

    sh_ref[0, n:, :] = jnp.zeros((SUBLANES, GROUP_W), F32)
    for s in range(1, SUBLANES):
        sh_ref[s, 0:n, :] = sh_ref[0, s:s + n, :]

    for k in range(CONV_WIDTH):
        wb_ref[k] = jnp.broadcast_to(wdw_ref[k:k + 1, :], (SUBLANES, GROUP_W))
    wb_ref[CONV_WIDTH] = jnp.broadcast_to(bdw_ref[...], (SUBLANES, GROUP_W))

    xn_ref[...] = _rms(x_ref[...], g_ref[...]).astype(BF16)
    slice_cols = 2 * MXU_COLS
    n_slices = proj_ref.shape[-1] // slice_cols
    n_chunks = tl // CONV_ROWS
    per_slice = n_chunks // n_slices
    tiles = CONV_ROWS // SUBLANES
    off = CONV_HALO - (CONV_WIDTH - 1)

    def conv_chunk(c):
        r0 = c * CONV_ROWS
        acc = jnp.broadcast_to(wb_ref[CONV_WIDTH][None], (tiles, SUBLANES, GROUP_W))
        for k in range(CONV_WIDTH):
            q, s = divmod(off + k, SUBLANES)
            x = sh_ref[s, r0 + q * SUBLANES:r0 + q * SUBLANES + CONV_ROWS, :]
            acc = acc + wb_ref[k][None] * x.reshape(tiles, SUBLANES, GROUP_W)
        cv_ref[r0:r0 + CONV_ROWS, :] = acc.reshape(CONV_ROWS, GROUP_W)

    def pool_tile():
        t = lax.rem(jnp.maximum(step - 1, 0), nl) * tl + lax.broadcasted_iota(jnp.int32, (tl, 1), 0)
        for gi, w in enumerate(POOL_WINDOWS):
            pc = slice(gi * POOL_CH, (gi + 1) * POOL_CH)
            acc = ext_ref[POOL_HALO:POOL_HALO + tl, pc]
            for k in range(1, w):
                acc = acc + ext_ref[POOL_HALO - k:POOL_HALO - k + tl, pc]
            p = acc / jnp.minimum(t + 1, w).astype(F32) - ext_ref[POOL_HALO:POOL_HALO + tl, pc]
            y = jnp.dot(p.astype(BF16), pw_ref[gi].astype(BF16), preferred_element_type=F32)
            yb_ref[:, pc] = y * psc_ref[:, pc]

    for j in range(n_slices):
        cols = slice(j * slice_cols, (j + 1) * slice_cols)
        proj_ref[:, cols] = jnp.dot(xn_ref[...], w_ref[:, cols], preferred_element_type=F32)
        if j == 0:
            pool_tile()
        for c in range(j * per_slice, (j + 1) * per_slice):
            conv_chunk(c)
        if per_slice:
            r0 = ((j + 1) * per_slice - 1) * CONV_ROWS
            zero = pltpu.bitcast(lax.shift_right_logical(
                pltpu.bitcast(cv_ref[r0:r0 + 2 * SUBLANES, 0:LANES], jnp.uint32), jnp.uint32(32)), F32)
            xn_ref[0:2 * SUBLANES, 0:LANES] = xn_ref[0:2 * SUBLANES, 0:LANES] + zero.astype(BF16)
    for c in range(n_slices * per_slice, n_chunks):
        conv_chunk(c)

    h = cv_ref[...]
    hc = h - jnp.mean(h, axis=-1, keepdims=True)
    y = hc * lax.rsqrt(jnp.mean(hc * hc, axis=-1, keepdims=True) + NORM_EPS)
    y = y * lng_ref[...] + lnb_ref[...]
    y = jax.nn.silu(y)
    yc_ref[...] = jnp.dot(y.astype(BF16), wpw_ref[...].astype(BF16), preferred_element_type=F32)

    ext_ref[0:POOL_HALO, :] = ext_ref[tl:tl + POOL_HALO, :]
    ext_ref[POOL_HALO:, :] = proj_ref[:, GROUP_W:2 * GROUP_W]
    sh_ref[0, 0:CONV_HALO, :] = sh_ref[0, tl:n, :]
    sh_ref[0, CONV_HALO:n, :] = (proj_ref[:, 2 * GROUP_W:3 * GROUP_W]
                                 * jax.nn.sigmoid(proj_ref[:, 3 * GROUP_W:4 * GROUP_W]))


def proj_conv(x, g, w, w_dw, b_dw, ln_g, ln_b, w_pw, pool_w, pool_scale, layer, bsz, seq, tl=512):
    m, d = x.shape
    n = w.shape[-1]
    nl = seq // tl
    nt = m // tl
    vec = pl.BlockSpec((None, 1, GROUP_W), lambda s: (layer, 0, 0))
    cur = lambda s: (jnp.minimum(s, nt - 1), 0)
    prev = lambda s: (jnp.maximum(s - 1, 0), 0)
    return pl.pallas_call(
        functools.partial(_proj_conv_kernel, tl=tl, nl=nl),
        grid=(nt + 1,),
        in_specs=[
            pl.BlockSpec((tl, d), cur),
            pl.BlockSpec((None, 1, d), lambda s: (layer, 0, 0)),
            pl.BlockSpec((d, n), lambda s: (0, 0), pipeline_mode=pl.Buffered(1)),
            pl.BlockSpec((None, CONV_WIDTH, GROUP_W), lambda s: (layer, 0, 0)),
            vec, vec, vec,
            pl.BlockSpec((None, GROUP_W, GROUP_W), lambda s: (layer, 0, 0)),
            pl.BlockSpec((None, len(POOL_WINDOWS), POOL_CH, POOL_CH), lambda s: (layer, 0, 0, 0)),
            vec,
        ],
        out_specs=[pl.BlockSpec((tl, n), cur), pl.BlockSpec((tl, GROUP_W), prev),
                   pl.BlockSpec((tl, GROUP_W), prev)],
        out_shape=[jax.ShapeDtypeStruct((m, n), F32), jax.ShapeDtypeStruct((m, GROUP_W), F32),
                   jax.ShapeDtypeStruct((m, GROUP_W), F32)],
        scratch_shapes=[pltpu.VMEM((SUBLANES, CONV_HALO + tl + SUBLANES, GROUP_W), F32),
                        pltpu.VMEM((tl, GROUP_W), F32),
                        pltpu.VMEM((CONV_WIDTH + 1, SUBLANES, GROUP_W), F32),
                        pltpu.VMEM((tl, d), BF16),
                        pltpu.VMEM((POOL_HALO + tl, GROUP_W), F32)],
        compiler_params=_params("arbitrary"),
        name="proj_conv",
    )(x, g, w, w_dw, b_dw, ln_g, ln_b, w_pw, pool_w, pool_scale)


def _s5_param_kernel(lr_ref, li_ref, ldt_ref, lrx_ref, lix_ref, ldtx_ref, br_ref, bi_ref, cr_ref, ci_ref,
                     pr_ref, pi_ref, bmat_ref, cmat_ref, *, chunk):
    def discretise(lr, li, ldt):
        dt = jnp.exp(ldt)
        mag = jnp.exp(lr * dt)
        return mag * jnp.cos(li * dt), mag * jnp.sin(li * dt)

    cr, ci = discretise(lr_ref[...], li_ref[...], ldt_ref[...])
    slot = {1: 0, chunk: 1, 2 * chunk: 2, 4 * chunk: 3}
    e = 1
    while True:
        if e in slot:
            pr_ref[slot[e]] = cr
            pi_ref[slot[e]] = ci
        if e == 4 * chunk:
            break
        cr, ci = cr * cr - ci * ci, 2.0 * (cr * ci)
        e *= 2

    lr, li = lrx_ref[...], lix_ref[...]
    ar, ai = discretise(lr, li, ldtx_ref[...])
    den = lr * lr + li * li
    nr, ni = ar - 1.0, ai
    f_r = (nr * lr + ni * li) / den
    f_i = (ni * lr - nr * li) / den
    b_r, b_i = br_ref[...], bi_ref[...]
    bb_r = f_r * b_r - f_i * b_i
    bb_i = f_r * b_i + f_i * b_r

    def block_diagonal(x, row_shift, col_shift):
        rows, w = x.shape
        cols = (rows >> row_shift) << col_shift
        sel = ((lax.broadcasted_iota(jnp.int32, (w, cols), 1) & (w - 1))
               == lax.broadcasted_iota(jnp.int32, (w, cols), 0))
        tiled = jnp.dot(x.astype(BF16), jnp.where(sel, 1.0, 0.0).astype(BF16), preferred_element_type=F32)
        same = ((lax.broadcasted_iota(jnp.int32, (rows, cols), 0) >> row_shift)
                == (lax.broadcasted_iota(jnp.int32, (rows, cols), 1) >> col_shift))
        return jnp.where(same, tiled, 0.0).astype(BF16)

    ch_shift, st_shift = S5_CH.bit_length() - 1, S5_STATE.bit_length() - 1
    bmat_ref[:, 0:S5_NSTATE] = block_diagonal(bb_r, ch_shift, st_shift)
    bmat_ref[:, S5_NSTATE:] = block_diagonal(bb_i, ch_shift, st_shift)
    cmat_ref[0:S5_NSTATE, :] = block_diagonal(cr_ref[...], st_shift, ch_shift)
    cmat_ref[S5_NSTATE:, :] = block_diagonal(-ci_ref[...], st_shift, ch_shift)


S5_POWERS = 4


def s5_params(lam_re, lam_im, log_dt, b_re, b_im, c_re, c_im, chunk):
    depth = lam_re.shape[0]
    gc = S5_GROUPS * S5_CH
    rep = lambda a: jnp.repeat(a, S5_CH, axis=1)
    ldt = log_dt[..., None]
    bt = lambda b: jnp.transpose(b, (0, 1, 3, 2)).reshape(depth, gc, S5_STATE)
    ct = lambda c: jnp.transpose(c, (0, 1, 3, 2)).reshape(depth, S5_NSTATE, S5_CH)
    small = pl.BlockSpec((None, S5_GROUPS, S5_STATE), lambda l: (l, 0, 0))
    small1 = pl.BlockSpec((None, S5_GROUPS, 1), lambda l: (l, 0, 0))
    big = pl.BlockSpec((None, gc, S5_STATE), lambda l: (l, 0, 0))
    big1 = pl.BlockSpec((None, gc, 1), lambda l: (l, 0, 0))
    cspec = pl.BlockSpec((None, S5_NSTATE, S5_CH), lambda l: (l, 0, 0))
    powspec = pl.BlockSpec((None, S5_POWERS, S5_GROUPS, S5_STATE), lambda l: (l, 0, 0, 0))
    return pl.pallas_call(
        functools.partial(_s5_param_kernel, chunk=chunk),
        grid=(depth,),
        in_specs=[small, small, small1, big, big, big1, big, big, cspec, cspec],
        out_specs=[powspec, powspec,
                   pl.BlockSpec((None, gc, 2 * S5_NSTATE), lambda l: (l, 0, 0)),
                   pl.BlockSpec((None, 2 * S5_NSTATE, gc), lambda l: (l, 0, 0))],
        out_shape=[
            jax.ShapeDtypeStruct((depth, S5_POWERS, S5_GROUPS, S5_STATE), F32),
            jax.ShapeDtypeStruct((depth, S5_POWERS, S5_GROUPS, S5_STATE), F32),
            jax.ShapeDtypeStruct((depth, gc, 2 * S5_NSTATE), BF16),
            jax.ShapeDtypeStruct((depth, 2 * S5_NSTATE, gc), BF16),
        ],
        compiler_params=_params("arbitrary"),
        name="s5_params",
    )(lam_re, lam_im, ldt, rep(lam_re), rep(lam_im), rep(ldt), bt(b_re), bt(b_im), ct(c_re), ct(c_im))


def s5_scan_coefficients(pow_r, pow_i):
    depth = pow_r.shape[0]
    row = jnp.arange(SUBLANES)[None, :, None]

    def coef(p):
        p = p.reshape(depth, S5_POWERS, 1, S5_NSTATE)
        full = [jnp.broadcast_to(p[:, e], (depth, SUBLANES, S5_NSTATE)) for e in (0, 1)]
        masked = [jnp.where(row >= s, p[:, e], 0.0) for e, s in ((1, 1), (2, 2), (3, 4))]
        return jnp.stack(full + masked, axis=1)

    return coef(pow_r), coef(pow_i)


S5_COEF_TILES = 5


def _s5_kernel(u0_ref, u1_ref, u2_ref, u3_ref, bmat_ref, cmat_ref, cfr_ref, cfi_ref, d_ref, wglu_ref, o_ref,
               up_ref, uperm_ref, xs_ref, cin_ref, yp_ref, *, tl):
    n = S5_NSTATE
    tc = tl // SUBLANES
    pitch = tc + SUBLANES
    u_refs = (u0_ref, u1_ref, u2_ref, u3_ref)
    l = pl.program_id(1)

    @pl.when(l == 0)
    def _():
        cin_ref[...] = jnp.zeros(cin_ref.shape, F32)

    for j, u_ref in enumerate(u_refs):
        for c in range(SUBLANES):
            up_ref[j, c * pitch:c * pitch + tc, :] = u_ref[c * tc:(c + 1) * tc, :]

    def gather(t, carry):
        r = pl.multiple_of(t * SUBLANES, SUBLANES)
        for j in range(len(u_refs)):
            uperm_ref[pl.ds(r, SUBLANES), j * LANES:(j + 1) * LANES] = up_ref[j, pl.ds(t, SUBLANES, stride=pitch), :]
        return carry

    lax.fori_loop(0, tc, gather, 0, unroll=S5_SCAN_UNROLL)

    ub = uperm_ref[...].astype(BF16)
    blocks_per_part = n // MXU_COLS
    for jb in range(2 * blocks_per_part):
        k = (jb % blocks_per_part) * MXU_COLS // S5_STATE * S5_CH // LANES
        xs_ref[:, jb * MXU_COLS:(jb + 1) * MXU_COLS] = jnp.dot(
            ub[:, k * LANES:(k + 1) * LANES], bmat_ref[k * LANES:(k + 1) * LANES, jb * MXU_COLS:(jb + 1) * MXU_COLS],
            preferred_element_type=F32)

    row = lax.broadcasted_iota(jnp.int32, (SUBLANES, LANES), 0)

    def cmul_add(ar, ai, xr, xi, br, bi):
        return (ar * xr - ai * xi) + br, (ar * xi + ai * xr) + bi

    def scan_lanes(lb0):
        cols = [(slice(lb * LANES, (lb + 1) * LANES), slice(n + lb * LANES, n + (lb + 1) * LANES))
                for lb in range(lb0, lb0 + S5_SCAN_LANE_BLOCKS)]
        a = [(cfr_ref[0, :, cr_], cfi_ref[0, :, cr_]) for cr_, _ in cols]

        def recurrence(init, store):
            def step(t, x):
                base = pl.multiple_of(t * SUBLANES, SUBLANES)
                new = []
                for q, (cr_, ci_) in enumerate(cols):
                    nr, ni = cmul_add(a[q][0], a[q][1], x[2 * q], x[2 * q + 1],
                                      xs_ref[pl.ds(base, SUBLANES), cr_], xs_ref[pl.ds(base, SUBLANES), ci_])
                    if store:
                        xs_ref[pl.ds(base, SUBLANES), cr_] = nr
                        xs_ref[pl.ds(base, SUBLANES), ci_] = ni
                    new += [nr, ni]
                return tuple(new)
            return lax.fori_loop(0, tc, step, tuple(init), unroll=S5_SCAN_UNROLL)

        zero = jnp.zeros((SUBLANES, LANES), F32)
        ends = recurrence([zero] * (2 * len(cols)), store=False)

        starts = []
        for q, (cr_, ci_) in enumerate(cols):
            er, ei = ends[2 * q], ends[2 * q + 1]
            wr = jnp.where(row == 0, cin_ref[:, cr_], pltpu.roll(er, 1, 0))
            wi = jnp.where(row == 0, cin_ref[:, ci_], pltpu.roll(ei, 1, 0))
            for k, s in ((2, 1), (3, 2), (4, 4)):
                wr, wi = cmul_add(cfr_ref[k, :, cr_], cfi_ref[k, :, cr_],
                                  pltpu.roll(wr, s, 0), pltpu.roll(wi, s, 0), wr, wi)
            tr, ti = cmul_add(cfr_ref[1, :, cr_], cfi_ref[1, :, cr_], wr, wi, er, ei)
            cin_ref[:, cr_] = pltpu.roll(tr, 1, 0)
            cin_ref[:, ci_] = pltpu.roll(ti, 1, 0)
            starts += [wr, wi]
        recurrence(starts, store=True)

    for lb0 in range(0, n // LANES, S5_SCAN_LANE_BLOCKS):
        scan_lanes(lb0)

    ys = []
    for m in range(GROUP_W // MXU_COLS):
        k0 = m * MXU_COLS // S5_CH * S5_STATE
        k1 = (m + 1) * MXU_COLS // S5_CH * S5_STATE
        cols = slice(m * MXU_COLS, (m + 1) * MXU_COLS)
        ys.append(jnp.dot(xs_ref[:, k0:k1].astype(BF16), cmat_ref[k0:k1, cols], preferred_element_type=F32)
                  + jnp.dot(xs_ref[:, n + k0:n + k1].astype(BF16), cmat_ref[n + k0:n + k1, cols],
                            preferred_element_type=F32))
    y = jnp.concatenate(ys, axis=-1) + d_ref[...] * uperm_ref[...]
    g = jax.nn.gelu(y)
    gate = jnp.dot(g.astype(BF16), wglu_ref[...].astype(BF16), preferred_element_type=F32)
    out = g * jax.nn.sigmoid(gate)
    for j in range(len(u_refs)):
        yp_ref[j] = out[:, j * LANES:(j + 1) * LANES]

    tiles_per_chunk = tc // SUBLANES
    shift = tiles_per_chunk.bit_length() - 1

    def scatter(idx, carry):
        c = idx >> shift
        m = idx & (tiles_per_chunk - 1)
        r = pl.multiple_of(idx * SUBLANES, SUBLANES)
        src = m * (SUBLANES * SUBLANES) + c
        for j in range(len(u_refs)):
            o_ref[pl.ds(r, SUBLANES), j * LANES:(j + 1) * LANES] = yp_ref[j, pl.ds(src, SUBLANES, stride=SUBLANES), :]
        return carry

    lax.fori_loop(0, SUBLANES * tiles_per_chunk, scatter, 0, unroll=S5_SCAN_UNROLL)


def s5_mixer(proj, bmat, cmat, coef_r, coef_i, d_skip, w_glu, layer, bsz, seq, tl=S5_BLOCK):
    nl = seq // tl
    n2 = 2 * S5_NSTATE
    tc = tl // SUBLANES
    assert tc % SUBLANES == 0 and tc & (tc - 1) == 0
    nu = GROUP_W // LANES
    uspecs = [pl.BlockSpec((tl, LANES), lambda b, l, j=j: (b * nl + l, j)) for j in range(nu)]
    coef = pl.BlockSpec((None, S5_COEF_TILES, SUBLANES, S5_NSTATE), lambda b, l: (layer, 0, 0, 0))
    return pl.pallas_call(
        functools.partial(_s5_kernel, tl=tl),
        grid=(bsz, nl),
        in_specs=uspecs + [
            pl.BlockSpec((None, GROUP_W, n2), lambda b, l: (layer, 0, 0)),
            pl.BlockSpec((None, n2, GROUP_W), lambda b, l: (layer, 0, 0)),
            coef, coef,
            pl.BlockSpec((None, 1, GROUP_W), lambda b, l: (layer, 0, 0)),
            pl.BlockSpec((None, GROUP_W, GROUP_W), lambda b, l: (layer, 0, 0)),
        ],
        out_specs=pl.BlockSpec((tl, GROUP_W), lambda b, l: (b * nl + l, 0)),
        out_shape=jax.ShapeDtypeStruct((bsz * seq, GROUP_W), F32),
        scratch_shapes=[
            pltpu.VMEM((nu, SUBLANES * (tc + SUBLANES), LANES), F32),
            pltpu.VMEM((tl, GROUP_W), F32),
            pltpu.VMEM((tl, n2), F32),
            pltpu.VMEM((SUBLANES, n2), F32),
            pltpu.VMEM((nu, tl, LANES), F32),
        ],
        compiler_params=_params("parallel", "arbitrary"),
        name="s5_mixer",
    )(*([proj] * nu), bmat, cmat, coef_r, coef_i, d_skip, w_glu)


def _t5_bucket(dist):
    n = np.maximum(dist, 0)
    max_exact = REL_BUCKETS // 2
    large = max_exact + (np.log(np.maximum(n, 1) / max_exact) / np.log(REL_MAX_DIST / max_exact)
                         * (REL_BUCKETS - max_exact)).astype(np.int64)
    large = np.minimum(large, REL_BUCKETS - 1)
    return np.where(n < max_exact, n, large).astype(np.int32)


ATT_NEAR_BLOCKS = 5
ATT_NEAR = ATT_NEAR_BLOCKS * ATT_BLOCK
ATT_FAR_DIL = 16


def _attention_bias_layout(sub):
    a = np.arange(ATT_BLOCK)[:, None]
    d_near = (ATT_NEAR - ATT_BLOCK) + a - np.arange(ATT_NEAR)[None, :]
    mult = np.zeros(d_near.shape, np.int64)
    for window, dil in DILATED_PATTERNS:
        if dil != ATT_FAR_DIL:
            mult += (d_near >= 0) & (d_near % dil == 0) & (d_near // dil <= window // dil)
    d_far = ATT_FAR_DIL * (a - np.arange(sub)[None, :])
    mult_far = ((d_far >= 0) & (d_far // ATT_FAR_DIL <= ATT_BLOCK)).astype(np.int64)
    d = np.concatenate([d_near, d_far], axis=1)
    mult = np.concatenate([mult, mult_far], axis=1)
    addend = np.where(mult > 0, np.log(np.maximum(mult, 1)), NEG_INF).astype(np.float32)
    return _t5_bucket(d), addend


def _attn_bias_kernel(rb_ref, bucket_ref, add_ref, o_ref):
    h = pl.program_id(0)
    bucket = bucket_ref[...]
    t = jnp.zeros(bucket.shape, F32)
    for b in range(REL_BUCKETS):
        t = jnp.where(bucket == b, rb_ref[b, h], t)
    o_ref[...] = (t + add_ref[...]) * LOG2E


def attention_bias_table(rel_bias, seq):
    sub = seq // ATT_FAR_DIL
    bucket, addend = _attention_bias_layout(sub)
    width = ATT_NEAR + sub
    full = pl.BlockSpec((ATT_BLOCK, width), lambda h: (0, 0))
    return pl.pallas_call(
        _attn_bias_kernel,
        grid=(ATT_HEADS,),
        in_specs=[pl.BlockSpec(memory_space=pltpu.SMEM), full, full],
        out_specs=pl.BlockSpec((None, ATT_BLOCK, width), lambda h: (h, 0, 0)),
        out_shape=jax.ShapeDtypeStruct((ATT_HEADS, ATT_BLOCK, width), F32),
        compiler_params=_params("arbitrary"),
        name="attn_bias",
    )(rel_bias, jnp.asarray(bucket), jnp.asarray(addend))


ATT_HEADS_PER_BLOCK = LANES // ATT_HEAD_DIM
ATT_GROUP = 8


def _nt_dot(a, b):
    return lax.dot_general(a, b, (((1,), (1,)), ((), ())), preferred_element_type=F32)


def _attn_kernel(q_ref, k_ref, v_ref, bt_ref, *rest, seq, n_cast):
    cast_in, o_ref, cast_out = rest[:n_cast], rest[n_cast], rest[n_cast + 1:2 * n_cast + 1]
    kb_ref, vb_ref, m3_ref, l3_ref, a3_ref = rest[2 * n_cast + 1:]
    sub = seq // ATT_FAR_DIL
    pad = ATT_NEAR - ATT_BLOCK
    scale = ATT_HEAD_DIM ** -0.5 * LOG2E
    lane = lax.broadcasted_iota(jnp.int32, (1, LANES), 1)
    in_head = [(lane >= hh * ATT_HEAD_DIM) & (lane < (hh + 1) * ATT_HEAD_DIM)
               for hh in range(ATT_HEADS_PER_BLOCK)]

    def by_head(vals):
        out = vals[-1]
        for hh in range(ATT_HEADS_PER_BLOCK - 2, -1, -1):
            out = jnp.where(in_head[hh], vals[hh], out)
        return out

    def softmax_pv(tiles):
        heads = range(ATT_HEADS_PER_BLOCK)
        s = [[_nt_dot(jnp.where(in_head[hh], qs, 0.0).astype(BF16), kb) + bias_of(hh) for hh in heads]
             for qs, kb, _, bias_of in tiles]
        m = [[jnp.max(s_h, axis=-1, keepdims=True) for s_h in s_t] for s_t in s]
        p = [[jnp.exp2(s_h - m_h) for s_h, m_h in zip(s_t, m_t)] for s_t, m_t in zip(s, m)]
        acc = [[jnp.dot(p_h.astype(BF16), tile[2](hh), preferred_element_type=F32) for hh, p_h in enumerate(p_t)]
               for p_t, tile in zip(p, tiles)]
        out = []
        for tile, m_t, acc_t in zip(tiles, m, acc):
            full = (tile[0].shape[0], LANES)
            den = pltpu.roll(by_head(acc_t[::-1]), ATT_HEAD_DIM, 1)
            out.append((by_head([jnp.broadcast_to(x, full) for x in m_t]), den, by_head(acc_t)))
        return out

    for src, dst in zip(cast_in, cast_out):
        dst[...] = src[...].astype(BF16)

    def with_ones(v, hh):
        return jnp.where(in_head[hh], v, 1.0).astype(BF16)

    kb_ref[...] = k_ref[...].astype(BF16)
    for hh in range(ATT_HEADS_PER_BLOCK):
        vb_ref[hh] = with_ones(v_ref[...], hh)

    for g0 in range(0, ATT_FAR_DIL, ATT_GROUP):
        group = [pl.ds(r, sub, stride=ATT_FAR_DIL) for r in range(g0, g0 + ATT_GROUP)]
        far_bias = lambda hh: bt_ref[hh, 0:sub, ATT_NEAR:ATT_NEAR + sub]
        tiles = [(q_ref[rows, :] * scale, k_ref[rows, :].astype(BF16),
                  lambda hh, rows=rows: with_ones(v_ref[rows, :], hh), far_bias) for rows in group]
        for rows, (m, l, acc) in zip(group, softmax_pv(tiles)):
            m3_ref[rows, :] = m
            l3_ref[rows, :] = l
            a3_ref[rows, :] = acc

    nblocks = seq // ATT_BLOCK
    for g0 in range(0, nblocks, ATT_GROUP):
        group, tiles = [], []
        for i in range(g0, min(g0 + ATT_GROUP, nblocks)):
            r0 = i * ATT_BLOCK
            lo = max(0, r0 - pad)
            c0 = ATT_NEAR - (r0 + ATT_BLOCK - lo)
            group.append(slice(r0, r0 + ATT_BLOCK))
            tiles.append((q_ref[group[-1], :] * scale, kb_ref[lo:r0 + ATT_BLOCK, :],
                          lambda hh, lo=lo, r0=r0: vb_ref[hh, lo:r0 + ATT_BLOCK, :],
                          lambda hh, c0=c0: bt_ref[hh, :, c0:ATT_NEAR]))
        merged = []
        for rows, (m, l, acc) in zip(group, softmax_pv(tiles)):
            m3 = m3_ref[rows, :]
            mm = jnp.maximum(m, m3)
            wn = jnp.exp2(m - mm)
            wf = jnp.exp2(m3 - mm)
            num = acc * wn + a3_ref[rows, :] * wf
            den = l * wn + l3_ref[rows, :] * wf
            merged.append(num / den)
        for rows, o in zip(group, merged):
            o_ref[rows, :] = o


def dilated_attention(proj, bias_table, casts, bsz, seq):
    assert seq % ATT_BLOCK == 0 and seq // ATT_FAR_DIL <= ATT_BLOCK
    nhp = ATT_HEADS // ATT_HEADS_PER_BLOCK
    steps = bsz * nhp
    qcol = 4 * GROUP_W // LANES
    kcol = 5 * GROUP_W // LANES
    vcol = 6 * GROUP_W // LANES
    width = bias_table.shape[-1]
    stat = pltpu.VMEM((seq, LANES), F32)
    half = pltpu.VMEM((seq, LANES), BF16)
    vvar = pltpu.VMEM((ATT_HEADS_PER_BLOCK, seq, LANES), BF16)
    cast_in, cast_out, cast_shapes = [], [], []
    for w, layer in casts:
        rows, cols = w.shape[1:]
        assert rows % steps == 0
        cast_in.append(pl.BlockSpec((None, rows // steps, cols), lambda b, h, layer=layer: (layer, b * nhp + h, 0)))
        cast_out.append(pl.BlockSpec((rows // steps, cols), lambda b, h: (b * nhp + h, 0)))
        cast_shapes.append(jax.ShapeDtypeStruct((rows, cols), BF16))
    return pl.pallas_call(
        functools.partial(_attn_kernel, seq=seq, n_cast=len(casts)),
        grid=(bsz, nhp),
        in_specs=[
            pl.BlockSpec((seq, LANES), lambda b, h: (b, qcol + h)),
            pl.BlockSpec((seq, LANES), lambda b, h: (b, kcol + h)),
            pl.BlockSpec((seq, LANES), lambda b, h: (b, vcol + h)),
            pl.BlockSpec((ATT_HEADS_PER_BLOCK, ATT_BLOCK, width), lambda b, h: (h, 0, 0)),
        ] + cast_in,
        out_specs=[pl.BlockSpec((seq, LANES), lambda b, h: (b, h))] + cast_out,
        out_shape=[jax.ShapeDtypeStruct((bsz * seq, GROUP_W), F32)] + cast_shapes,
        scratch_shapes=[half, vvar, stat, stat, stat],
        compiler_params=_params("parallel", "parallel"),
        name="dilated_attention",
    )(proj, proj, proj, bias_table, *[w for w, _ in casts])


def _mem_kv_kernel(mem_ref, g_ref, wk_ref, wv_ref, w0_ref, k_ref, v_ref, w0b_ref):
    mn = _rms(mem_ref[...], g_ref[...]).astype(BF16)
    k_ref[...] = jnp.dot(mn, wk_ref[...].astype(BF16), preferred_element_type=F32).astype(BF16)
    v_ref[...] = jnp.dot(mn, wv_ref[...].astype(BF16), preferred_element_type=F32).astype(BF16)
    w0b_ref[...] = w0_ref[...].astype(BF16)


def mem_kv(mem, g, wk, wv, w_in):
    m, d = mem.shape
    depth = wk.shape[0]
    rows, cols = w_in.shape[1:]
    assert rows % depth == 0
    wspec = pl.BlockSpec((None, d, X_WIDTH), lambda l: (l, 0, 0))
    ospec = pl.BlockSpec((None, m, X_WIDTH), lambda l: (l, 0, 0))
    return pl.pallas_call(
        _mem_kv_kernel,
        grid=(depth,),
        in_specs=[pl.BlockSpec((m, d), lambda l: (0, 0)), pl.BlockSpec((1, d), lambda l: (0, 0)),
                  wspec, wspec,
                  pl.BlockSpec((None, rows // depth, cols), lambda l: (0, l, 0))],
        out_specs=[ospec, ospec, pl.BlockSpec((rows // depth, cols), lambda l: (l, 0))],
        out_shape=[jax.ShapeDtypeStruct((depth, m, X_WIDTH), BF16)] * 2
        + [jax.ShapeDtypeStruct((rows, cols), BF16)],
        compiler_params=_params("arbitrary"),
        name="mem_kv",
    )(mem, g, wk, wv, w_in)


def _mix_xattn_kernel(ya_ref, yb_ref, yc_ref, yd_ref, gg_ref, wout_ref, h_ref, gx_ref, wq_ref, k_ref, v_ref,
                      wo_ref, o_ref):
    yn = [_rms(y_ref[...], gg_ref[:, gi * GROUP_W:(gi + 1) * GROUP_W]).astype(BF16)
          for gi, y_ref in enumerate((ya_ref, yb_ref, yc_ref, yd_ref))]
    h = h_ref[...] + jnp.dot(jnp.concatenate(yn, axis=-1), wout_ref[...], preferred_element_type=F32)

    hn = _rms(h, gx_ref[...]).astype(BF16)
    q = jnp.dot(hn, wq_ref[...], preferred_element_type=F32) * (X_HEAD_DIM ** -0.5 * LOG2E)
    cols = [slice(hd * X_HEAD_DIM, (hd + 1) * X_HEAD_DIM) for hd in range(X_HEADS)]
    qb = q.astype(BF16)
    s = [_nt_dot(qb[:, c], k_ref[:, c]) for c in cols]
    m = [jnp.max(x, axis=-1, keepdims=True) for x in s]
    p = [jnp.exp2(x - mx) for x, mx in zip(s, m)]
    den = [jnp.sum(x, axis=-1, keepdims=True) for x in p]
    outs = [jnp.dot(x.astype(BF16), v_ref[:, c], preferred_element_type=F32) / dx for x, dx, c in zip(p, den, cols)]
    o = jnp.concatenate(outs, axis=-1).astype(BF16)
    o_ref[...] = h + jnp.dot(o, wo_ref[...], preferred_element_type=F32)


def mix_xattn(ya, yb, yc, yd, grp_g, w_out, h, norm_g, wq, k, v, wo, layer, bsz, seq, tm=512):
    d = h.shape[-1]
    nl = seq // tm
    mlen = k.shape[1] // bsz
    row = lambda b, l: (b * nl + l, 0)
    yspec = pl.BlockSpec((tm, GROUP_W), row)
    gain = pl.BlockSpec((None, 1, d), lambda b, l: (layer, 0, 0))
    whole = lambda shape: pl.BlockSpec(shape, lambda b, l: (0, 0), pipeline_mode=pl.Buffered(1))
    kv = pl.BlockSpec((None, mlen, X_WIDTH), lambda b, l: (layer, b, 0))
    return pl.pallas_call(
        _mix_xattn_kernel,
        grid=(bsz, nl),
        in_specs=[yspec, yspec, yspec, yspec, gain, whole((d, d)), pl.BlockSpec((tm, d), row),
                  gain, whole((d, X_WIDTH)), kv, kv, whole((X_WIDTH, d))],
        out_specs=pl.BlockSpec((tm, d), row),
        out_shape=jax.ShapeDtypeStruct(h.shape, F32),
        compiler_params=_params("parallel", "parallel"),
        name="mix_xattn",
    )(ya, yb, yc, yd, grp_g, w_out, h, norm_g, wq, k, v, wo)


def _mlp_kernel(h_ref, g_ref, wu_ref, wd_ref, *rest, out_norm):
    o_ref, hn_ref = rest[-2:]
    j = pl.program_id(1)

    @pl.when(j == 0)
    def _():
        h = h_ref[...]
        hn_ref[...] = _rms(h, g_ref[...]).astype(BF16)
        o_ref[...] = h

    a = jnp.dot(hn_ref[...], wu_ref[...], preferred_element_type=F32)
    a = jnp.square(jnp.maximum(a, 0.0)).astype(BF16)
    o_ref[...] += jnp.dot(a, wd_ref[...], preferred_element_type=F32)

    if out_norm:
        @pl.when(j == pl.num_programs(1) - 1)
        def _():
            o_ref[...] = _rms(o_ref[...], rest[0][...])


def mlp(h, g, wu, wd, layer, out_norm_g=None, tm=1024, tf=512):
    m, d = h.shape
    f = wu.shape[-1]
    out_norm = out_norm_g is not None
    extra_specs = [pl.BlockSpec((1, d), lambda i, j: (0, 0))] if out_norm else []
    extra_args = (out_norm_g,) if out_norm else ()
    return pl.pallas_call(
        functools.partial(_mlp_kernel, out_norm=out_norm),
        grid=(m // tm, f // tf),
        in_specs=[
            pl.BlockSpec((tm, d), lambda i, j: (i, 0)),
            pl.BlockSpec((None, 1, d), lambda i, j: (layer, 0, 0)),
            pl.BlockSpec((d, tf), lambda i, j: (0, j)),
            pl.BlockSpec((tf, d), lambda i, j: (j, 0)),
        ] + extra_specs,
        out_specs=pl.BlockSpec((tm, d), lambda i, j: (i, 0)),
        out_shape=jax.ShapeDtypeStruct((m, d), F32),
        scratch_shapes=[pltpu.VMEM((tm, d), BF16)],
        compiler_params=_params("parallel", "arbitrary"),
        name="mlp",
    )(h, g, wu, wd, *extra_args)


def kernel(x, mem, rel_bias, mem_norm_g, norm_mix_g, w_in, s5_lam_re, s5_lam_im, s5_log_dt, s5_b_re, s5_b_im, s5_c_re, s5_c_im, s5_d, s5_w_glu, pool_w, pool_scale, conv_w_dw, conv_b_dw, conv_ln_g, conv_ln_b, conv_w_pw, grp_norm_g, w_out, norm_x_g, w_xq, w_xk, w_xv, w_xo, norm_mlp_g, w_up, w_down, norm_final_g):
    bsz, seq, d = x.shape
    depth = w_in.shape[0]
    vec = lambda a: a[:, None, :]

    k_mem, v_mem, w_in_b = mem_kv(mem.reshape(bsz * mem.shape[1], d), mem_norm_g[None, :], w_xk, w_xv, w_in)
    pow_r, pow_i, bmat, cmat = s5_params(s5_lam_re, s5_lam_im, s5_log_dt, s5_b_re, s5_b_im, s5_c_re, s5_c_im,
                                         chunk=S5_BLOCK // SUBLANES)
    coef_r, coef_i = s5_scan_coefficients(pow_r, pow_i)
    bias_table = attention_bias_table(rel_bias, seq)

    norm_mix, s5_dv, pool_sc = vec(norm_mix_g), vec(s5_d), vec(pool_scale)
    b_dw, ln_g, ln_b = vec(conv_b_dw), vec(conv_ln_g), vec(conv_ln_b)
    grp_g, norm_x, norm_mlp = vec(grp_norm_g), vec(norm_x_g), vec(norm_mlp_g)

    h = x.reshape(bsz * seq, d)
    for l in range(depth):
        proj, y_c, y_b = proj_conv(h, norm_mix, w_in_b, conv_w_dw, b_dw, ln_g, ln_b, conv_w_pw,
                                   pool_w, pool_sc, l, bsz, seq)
        y_a = s5_mixer(proj, bmat, cmat, coef_r, coef_i, s5_dv, s5_w_glu, l, bsz, seq)
        casts = ([(w_out, l), (w_xq, l), (w_xo, l), (w_up, l), (w_down, l)]
                 + ([(w_in, l + 1)] if l + 1 < depth else []))
        y_d, w_out_b, w_xq_b, w_xo_b, w_up_b, w_down_b, *nxt = dilated_attention(proj, bias_table, casts, bsz, seq)
        w_in_b = nxt[0] if nxt else None
        h = mix_xattn(y_a, y_b, y_c, y_d, grp_g, w_out_b, h, norm_x, w_xq_b, k_mem, v_mem, w_xo_b, l, bsz, seq)
        h = mlp(h, norm_mlp, w_up_b, w_down_b, l,
                out_norm_g=norm_final_g[None, :] if l == depth - 1 else None)
    return h.reshape(bsz, seq, d)
```

```python
import functools
import math

import numpy as np
import jax
import jax.numpy as jnp
from jax import lax
from jax.experimental import pallas as pl
from jax.experimental.pallas import tpu as pltpu

F32 = jnp.float32
BF16 = jnp.bfloat16

GROUP_W = 512
S5_GROUPS = 32
S5_CH = 16
S5_STATE = 64
S5_NSTATE = S5_GROUPS * S5_STATE
S5_SCAN_LANE_BLOCKS = 8
S5_BLOCK = 1024
S5_SCAN_UNROLL = 4
POOL_WINDOWS = (2, 4, 8, 16)
POOL_CH = 128
CONV_WIDTH = 31
ATT_HEADS = 8
ATT_HEAD_DIM = 64
DILATED_PATTERNS = ((128, 1), (512, 4), (2048, 16))
ATT_BLOCK = 128
REL_BUCKETS = 32
REL_MAX_DIST = 2048
X_HEADS = 4
X_HEAD_DIM = 128
X_WIDTH = 512
NORM_EPS = 1e-6
NEG_INF = -1e30
LOG2E = math.log2(math.e)

SUBLANES = 8
LANES = 128
VMEM_LIMIT = 58 * 1024 * 1024


def _params(*sem):
    return pltpu.CompilerParams(dimension_semantics=sem, vmem_limit_bytes=VMEM_LIMIT)


def _rms(x, g):
    return x * lax.rsqrt(jnp.mean(x * x, axis=-1, keepdims=True) + NORM_EPS) * g


MXU_COLS = 256
POOL_HALO = 16
CONV_HALO = 32
CONV_ROWS = 32


def _proj_conv_kernel(x_ref, g_ref, w_ref, wdw_ref, bdw_ref, lng_ref, lnb_ref, wpw_ref, pw_ref, psc_ref,
                      proj_ref, yc_ref, yb_ref, sh_ref, cv_ref, wb_ref, xn_ref, ext_ref, mid_ref, *, tl, nl):
    step = pl.program_id(0)
    n = tl + CONV_HALO

    @pl.when(step == 0)
    def _():
        sh_ref[0] = jnp.zeros(sh_ref.shape[1:], F32)
        ext_ref[...] = jnp.zeros(ext_ref.shape, F32)

    @pl.when(lax.rem(step - 1, nl) == 0)
    def _():
        sh_ref[0, 0:CONV_HALO, :] = jnp.zeros((CONV_HALO, GROUP_W), F32)
        ext_ref[0:POOL_HALO, :] = jnp.zeros((POOL_HALO, GROUP_W), F32)

    sh_ref[0, n:, :] = jnp.zeros((SUBLANES, GROUP_W), F32)
    for s in range(1, SUBLANES):
        sh_ref[s, 0:n, :] = sh_ref[0, s:s + n, :]

    for k in range(CONV_WIDTH):
        wb_ref[k] = jnp.broadcast_to(wdw_ref[k:k + 1, :], (SUBLANES, GROUP_W))
    wb_ref[CONV_WIDTH] = jnp.broadcast_to(bdw_ref[...], (SUBLANES, GROUP_W))

    xn_ref[...] = _rms(x_ref[...], g_ref[...]).astype(BF16)
    slice_cols = 2 * MXU_COLS
    n_slices = w_ref.shape[-1] // slice_cols
    n_chunks = tl // CONV_ROWS
    per_slice = n_chunks // n_slices
    tiles = CONV_ROWS // SUBLANES
    off = CONV_HALO - (CONV_WIDTH - 1)

    def conv_chunk(c):
        r0 = c * CONV_ROWS
        acc = jnp.broadcast_to(wb_ref[CONV_WIDTH][None], (tiles, SUBLANES, GROUP_W))
        for k in range(CONV_WIDTH):
            q, s = divmod(off + k, SUBLANES)
            x = sh_ref[s, r0 + q * SUBLANES:r0 + q * SUBLANES + CONV_ROWS, :]
            acc = acc + wb_ref[k][None] * x.reshape(tiles, SUBLANES, GROUP_W)
        cv_ref[r0:r0 + CONV_ROWS, :] = acc.reshape(CONV_ROWS, GROUP_W)

    def pool_tile():
        t = lax.rem(jnp.maximum(step - 1, 0), nl) * tl + lax.broadcasted_iota(jnp.int32, (tl, 1), 0)
        for gi, w in enumerate(POOL_WINDOWS):
            pc = slice(gi * POOL_CH, (gi + 1) * POOL_CH)
            acc = ext_ref[POOL_HALO:POOL_HALO + tl, pc]
            for k in range(1, w):
                acc = acc + ext_ref[POOL_HALO - k:POOL_HALO - k + tl, pc]
            p = acc / jnp.minimum(t + 1, w).astype(F32) - ext_ref[POOL_HALO:POOL_HALO + tl, pc]
            y = jnp.dot(p.astype(BF16), pw_ref[gi].astype(BF16), preferred_element_type=F32)
            yb_ref[:, pc] = y * psc_ref[:, pc]

    for j in range(n_slices):
        cols = slice(j * slice_cols, (j + 1) * slice_cols)
        res = jnp.dot(xn_ref[...], w_ref[:, cols], preferred_element_type=F32)
        c0 = j * slice_cols
        if c0 < GROUP_W:
            proj_ref[:, c0:c0 + slice_cols] = res
        elif c0 < 4 * GROUP_W:
            mid_ref[:, c0 - GROUP_W:c0 - GROUP_W + slice_cols] = res
        else:
            proj_ref[:, c0 - 3 * GROUP_W:c0 - 3 * GROUP_W + slice_cols] = res
        if j == 0:
            pool_tile()
        for c in range(j * per_slice, (j + 1) * per_slice):
            conv_chunk(c)
        if per_slice:
            r0 = ((j + 1) * per_slice - 1) * CONV_ROWS
            zero = pltpu.bitcast(lax.shift_right_logical(
                pltpu.bitcast(cv_ref[r0:r0 + 2 * SUBLANES, 0:LANES], jnp.uint32), jnp.uint32(32)), F32)
            xn_ref[0:2 * SUBLANES, 0:LANES] = xn_ref[0:2 * SUBLANES, 0:LANES] + zero.astype(BF16)
    for c in range(n_slices * per_slice, n_chunks):
        conv_chunk(c)

    h = cv_ref[...]
    hc = h - jnp.mean(h, axis=-1, keepdims=True)
    y = hc * lax.rsqrt(jnp.mean(hc * hc, axis=-1, keepdims=True) + NORM_EPS)
    y = y * lng_ref[...] + lnb_ref[...]
    y = jax.nn.silu(y)
    yc_ref[...] = jnp.dot(y.astype(BF16), wpw_ref[...].astype(BF16), preferred_element_type=F32)

    ext_ref[0:POOL_HALO, :] = ext_ref[tl:tl + POOL_HALO, :]
    ext_ref[POOL_HALO:, :] = mid_ref[:, 0:GROUP_W]
    sh_ref[0, 0:CONV_HALO, :] = sh_ref[0, tl:n, :]
    sh_ref[0, CONV_HALO:n, :] = (mid_ref[:, GROUP_W:2 * GROUP_W]
                                 * jax.nn.sigmoid(mid_ref[:, 2 * GROUP_W:3 * GROUP_W]))


def proj_conv(x, g, w, w_dw, b_dw, ln_g, ln_b, w_pw, pool_w, pool_scale, layer, bsz, seq, tl=512):
    m, d = x.shape
    n = w.shape[-1]
    nl = seq // tl
    nt = m // tl
    vec = pl.BlockSpec((None, 1, GROUP_W), lambda s: (layer, 0, 0))
    cur = lambda s: (jnp.minimum(s, nt - 1), 0)
    prev = lambda s: (jnp.maximum(s - 1, 0), 0)
    return pl.pallas_call(
        functools.partial(_proj_conv_kernel, tl=tl, nl=nl),
        grid=(nt + 1,),
        in_specs=[
            pl.BlockSpec((tl, d), cur),
            pl.BlockSpec((None, 1, d), lambda s: (layer, 0, 0)),
            pl.BlockSpec((d, n), lambda s: (0, 0), pipeline_mode=pl.Buffered(1)),
            pl.BlockSpec((None, CONV_WIDTH, GROUP_W), lambda s: (layer, 0, 0)),
            vec, vec, vec,
            pl.BlockSpec((None, GROUP_W, GROUP_W), lambda s: (layer, 0, 0)),
            pl.BlockSpec((None, len(POOL_WINDOWS), POOL_CH, POOL_CH), lambda s: (layer, 0, 0, 0)),
            vec,
        ],
        out_specs=[pl.BlockSpec((tl, n - 3 * GROUP_W), cur), pl.BlockSpec((tl, GROUP_W), prev),
                   pl.BlockSpec((tl, GROUP_W), prev)],
        out_shape=[jax.ShapeDtypeStruct((m, n - 3 * GROUP_W), F32), jax.ShapeDtypeStruct((m, GROUP_W), F32),
                   jax.ShapeDtypeStruct((m, GROUP_W), F32)],
        scratch_shapes=[pltpu.VMEM((SUBLANES, CONV_HALO + tl + SUBLANES, GROUP_W), F32),
                        pltpu.VMEM((tl, GROUP_W), F32),
                        pltpu.VMEM((CONV_WIDTH + 1, SUBLANES, GROUP_W), F32),
                        pltpu.VMEM((tl, d), BF16),
                        pltpu.VMEM((POOL_HALO + tl, GROUP_W), F32),
                        pltpu.VMEM((tl, 3 * GROUP_W), F32)],
        compiler_params=_params("arbitrary"),
        name="proj_conv",
    )(x, g, w, w_dw, b_dw, ln_g, ln_b, w_pw, pool_w, pool_scale)


def _s5_param_kernel(lr_ref, li_ref, ldt_ref, lrx_ref, lix_ref, ldtx_ref, br_ref, bi_ref, cr_ref, ci_ref,
                     pr_ref, pi_ref, bmat_ref, cmat_ref, *, chunk):
    def discretise(lr, li, ldt):
        dt = jnp.exp(ldt)
        mag = jnp.exp(lr * dt)
        return mag * jnp.cos(li * dt), mag * jnp.sin(li * dt)

    cr, ci = discretise(lr_ref[...], li_ref[...], ldt_ref[...])
    slot = {1: 0, chunk: 1, 2 * chunk: 2, 4 * chunk: 3}
    e = 1
    while True:
        if e in slot:
            pr_ref[slot[e]] = cr
            pi_ref[slot[e]] = ci
        if e == 4 * chunk:
            break
        cr, ci = cr * cr - ci * ci, 2.0 * (cr * ci)
        e *= 2

    lr, li = lrx_ref[...], lix_ref[...]
    ar, ai = discretise(lr, li, ldtx_ref[...])
    den = lr * lr + li * li
    nr, ni = ar - 1.0, ai
    f_r = (nr * lr + ni * li) / den
    f_i = (ni * lr - nr * li) / den
    b_r, b_i = br_ref[...], bi_ref[...]
    bb_r = f_r * b_r - f_i * b_i
    bb_i = f_r * b_i + f_i * b_r

    def block_diagonal(x, row_shift, col_shift):
        rows, w = x.shape
        cols = (rows >> row_shift) << col_shift
        sel = ((lax.broadcasted_iota(jnp.int32, (w, cols), 1) & (w - 1))
               == lax.broadcasted_iota(jnp.int32, (w, cols), 0))
        tiled = jnp.dot(x.astype(BF16), jnp.where(sel, 1.0, 0.0).astype(BF16), preferred_element_type=F32)
        same = ((lax.broadcasted_iota(jnp.int32, (rows, cols), 0) >> row_shift)
                == (lax.broadcasted_iota(jnp.int32, (rows, cols), 1) >> col_shift))
        return jnp.where(same, tiled, 0.0).astype(BF16)

    ch_shift, st_shift = S5_CH.bit_length() - 1, S5_STATE.bit_length() - 1
    bmat_ref[:, 0:S5_NSTATE] = block_diagonal(bb_r, ch_shift, st_shift)
    bmat_ref[:, S5_NSTATE:] = block_diagonal(bb_i, ch_shift, st_shift)
    cmat_ref[0:S5_NSTATE, :] = block_diagonal(cr_ref[...], st_shift, ch_shift)
    cmat_ref[S5_NSTATE:, :] = block_diagonal(-ci_ref[...], st_shift, ch_shift)


S5_POWERS = 4


def s5_params(lam_re, lam_im, log_dt, b_re, b_im, c_re, c_im, chunk):
    depth = lam_re.shape[0]
    gc = S5_GROUPS * S5_CH
    rep = lambda a: jnp.repeat(a, S5_CH, axis=1)
    ldt = log_dt[..., None]
    bt = lambda b: jnp.transpose(b, (0, 1, 3, 2)).reshape(depth, gc, S5_STATE)
    ct = lambda c: jnp.transpose(c, (0, 1, 3, 2)).reshape(depth, S5_NSTATE, S5_CH)
    small = pl.BlockSpec((None, S5_GROUPS, S5_STATE), lambda l: (l, 0, 0))
    small1 = pl.BlockSpec((None, S5_GROUPS, 1), lambda l: (l, 0, 0))
    big = pl.BlockSpec((None, gc, S5_STATE), lambda l: (l, 0, 0))
    big1 = pl.BlockSpec((None, gc, 1), lambda l: (l, 0, 0))
    cspec = pl.BlockSpec((None, S5_NSTATE, S5_CH), lambda l: (l, 0, 0))
    powspec = pl.BlockSpec((None, S5_POWERS, S5_GROUPS, S5_STATE), lambda l: (l, 0, 0, 0))
    return pl.pallas_call(
        functools.partial(_s5_param_kernel, chunk=chunk),
        grid=(depth,),
        in_specs=[small, small, small1, big, big, big1, big, big, cspec, cspec],
        out_specs=[powspec, powspec,
                   pl.BlockSpec((None, gc, 2 * S5_NSTATE), lambda l: (l, 0, 0)),
                   pl.BlockSpec((None, 2 * S5_NSTATE, gc), lambda l: (l, 0, 0))],
        out_shape=[
            jax.ShapeDtypeStruct((depth, S5_POWERS, S5_GROUPS, S5_STATE), F32),
            jax.ShapeDtypeStruct((depth, S5_POWERS, S5_GROUPS, S5_STATE), F32),
            jax.ShapeDtypeStruct((depth, gc, 2 * S5_NSTATE), BF16),
            jax.ShapeDtypeStruct((depth, 2 * S5_NSTATE, gc), BF16),
        ],
        compiler_params=_params("arbitrary"),
        name="s5_params",
    )(lam_re, lam_im, ldt, rep(lam_re), rep(lam_im), rep(ldt), bt(b_re), bt(b_im), ct(c_re), ct(c_im))


def s5_scan_coefficients(pow_r, pow_i):
    depth = pow_r.shape[0]
    row = jnp.arange(SUBLANES)[None, :, None]

    def coef(p):
        p = p.reshape(depth, S5_POWERS, 1, S5_NSTATE)
        full = [jnp.broadcast_to(p[:, e], (depth, SUBLANES, S5_NSTATE)) for e in (0, 1)]
        masked = [jnp.where(row >= s, p[:, e], 0.0) for e, s in ((1, 1), (2, 2), (3, 4))]
        return jnp.stack(full + masked, axis=1)

    return coef(pow_r), coef(pow_i)


S5_COEF_TILES = 5


def _s5_kernel(u0_ref, u1_ref, u2_ref, u3_ref, bmat_ref, cmat_ref, cfr_ref, cfi_ref, d_ref, wglu_ref, o_ref,
               up_ref, uperm_ref, xs_ref, cin_ref, yp_ref, *, tl):
    n = S5_NSTATE
    tc = tl // SUBLANES
    pitch = tc + SUBLANES
    u_refs = (u0_ref, u1_ref, u2_ref, u3_ref)
    l = pl.program_id(1)

    @pl.when(l == 0)
    def _():
        cin_ref[...] = jnp.zeros(cin_ref.shape, F32)

    for j, u_ref in enumerate(u_refs):
        for c in range(SUBLANES):
            up_ref[j, c * pitch:c * pitch + tc, :] = u_ref[c * tc:(c + 1) * tc, :]

    def gather(t, carry):
        r = pl.multiple_of(t * SUBLANES, SUBLANES)
        for j in range(len(u_refs)):
            uperm_ref[pl.ds(r, SUBLANES), j * LANES:(j + 1) * LANES] = up_ref[j, pl.ds(t, SUBLANES, stride=pitch), :]
        return carry

    lax.fori_loop(0, tc, gather, 0, unroll=S5_SCAN_UNROLL)

    ub = uperm_ref[...].astype(BF16)
    blocks_per_part = n // MXU_COLS
    for jb in range(2 * blocks_per_part):
        k = (jb % blocks_per_part) * MXU_COLS // S5_STATE * S5_CH // LANES
        xs_ref[:, jb * MXU_COLS:(jb + 1) * MXU_COLS] = jnp.dot(
            ub[:, k * LANES:(k + 1) * LANES], bmat_ref[k * LANES:(k + 1) * LANES, jb * MXU_COLS:(jb + 1) * MXU_COLS],
            preferred_element_type=F32)

    row = lax.broadcasted_iota(jnp.int32, (SUBLANES, LANES), 0)

    def cmul_add(ar, ai, xr, xi, br, bi):
        return (ar * xr - ai * xi) + br, (ar * xi + ai * xr) + bi

    def scan_lanes(lb0):
        cols = [(slice(lb * LANES, (lb + 1) * LANES), slice(n + lb * LANES, n + (lb + 1) * LANES))
                for lb in range(lb0, lb0 + S5_SCAN_LANE_BLOCKS)]
        a = [(cfr_ref[0, :, cr_], cfi_ref[0, :, cr_]) for cr_, _ in cols]

        def recurrence(init, store):
            def step(t, x):
                base = pl.multiple_of(t * SUBLANES, SUBLANES)
                new = []
                for q, (cr_, ci_) in enumerate(cols):
                    nr, ni = cmul_add(a[q][0], a[q][1], x[2 * q], x[2 * q + 1],
                                      xs_ref[pl.ds(base, SUBLANES), cr_], xs_ref[pl.ds(base, SUBLANES), ci_])
                    if store:
                        xs_ref[pl.ds(base, SUBLANES), cr_] = nr
                        xs_ref[pl.ds(base, SUBLANES), ci_] = ni
                    new += [nr, ni]
                return tuple(new)
            return lax.fori_loop(0, tc, step, tuple(init), unroll=S5_SCAN_UNROLL)

        zero = jnp.zeros((SUBLANES, LANES), F32)
        ends = recurrence([zero] * (2 * len(cols)), store=False)

        starts = []
        for q, (cr_, ci_) in enumerate(cols):
            er, ei = ends[2 * q], ends[2 * q + 1]
            wr = jnp.where(row == 0, cin_ref[:, cr_], pltpu.roll(er, 1, 0))
            wi = jnp.where(row == 0, cin_ref[:, ci_], pltpu.roll(ei, 1, 0))
            for k, s in ((2, 1), (3, 2), (4, 4)):
                wr, wi = cmul_add(cfr_ref[k, :, cr_], cfi_ref[k, :, cr_],
                                  pltpu.roll(wr, s, 0), pltpu.roll(wi, s, 0), wr, wi)
            tr, ti = cmul_add(cfr_ref[1, :, cr_], cfi_ref[1, :, cr_], wr, wi, er, ei)
            cin_ref[:, cr_] = pltpu.roll(tr, 1, 0)
            cin_ref[:, ci_] = pltpu.roll(ti, 1, 0)
            starts += [wr, wi]
        recurrence(starts, store=True)

    for lb0 in range(0, n // LANES, S5_SCAN_LANE_BLOCKS):
        scan_lanes(lb0)

    ys = []
    for m in range(GROUP_W // MXU_COLS):
        k0 = m * MXU_COLS // S5_CH * S5_STATE
        k1 = (m + 1) * MXU_COLS // S5_CH * S5_STATE
        cols = slice(m * MXU_COLS, (m + 1) * MXU_COLS)
        ys.append(jnp.dot(xs_ref[:, k0:k1].astype(BF16), cmat_ref[k0:k1, cols], preferred_element_type=F32)
                  + jnp.dot(xs_ref[:, n + k0:n + k1].astype(BF16), cmat_ref[n + k0:n + k1, cols],
                            preferred_element_type=F32))
    y = jnp.concatenate(ys, axis=-1) + d_ref[...] * uperm_ref[...]
    g = jax.nn.gelu(y)
    gate = jnp.dot(g.astype(BF16), wglu_ref[...].astype(BF16), preferred_element_type=F32)
    out = g * jax.nn.sigmoid(gate)
    for j in range(len(u_refs)):
        yp_ref[j] = out[:, j * LANES:(j + 1) * LANES]

    tiles_per_chunk = tc // SUBLANES
    shift = tiles_per_chunk.bit_length() - 1

    def scatter(idx, carry):
        c = idx >> shift
        m = idx & (tiles_per_chunk - 1)
        r = pl.multiple_of(idx * SUBLANES, SUBLANES)
        src = m * (SUBLANES * SUBLANES) + c
        for j in range(len(u_refs)):
            o_ref[pl.ds(r, SUBLANES), j * LANES:(j + 1) * LANES] = yp_ref[j, pl.ds(src, SUBLANES, stride=SUBLANES), :]
        return carry

    lax.fori_loop(0, SUBLANES * tiles_per_chunk, scatter, 0, unroll=S5_SCAN_UNROLL)


def s5_mixer(proj, bmat, cmat, coef_r, coef_i, d_skip, w_glu, layer, bsz, seq, tl=S5_BLOCK):
    nl = seq // tl
    n2 = 2 * S5_NSTATE
    tc = tl // SUBLANES
    assert tc % SUBLANES == 0 and tc & (tc - 1) == 0
    nu = GROUP_W // LANES
    uspecs = [pl.BlockSpec((tl, LANES), lambda b, l, j=j: (b * nl + l, j)) for j in range(nu)]
    coef = pl.BlockSpec((None, S5_COEF_TILES, SUBLANES, S5_NSTATE), lambda b, l: (layer, 0, 0, 0))
    return pl.pallas_call(
        functools.partial(_s5_kernel, tl=tl),
        grid=(bsz, nl),
        in_specs=uspecs + [
            pl.BlockSpec((None, GROUP_W, n2), lambda b, l: (layer, 0, 0)),
            pl.BlockSpec((None, n2, GROUP_W), lambda b, l: (layer, 0, 0)),
            coef, coef,
            pl.BlockSpec((None, 1, GROUP_W), lambda b, l: (layer, 0, 0)),
            pl.BlockSpec((None, GROUP_W, GROUP_W), lambda b, l: (layer, 0, 0)),
        ],
        out_specs=pl.BlockSpec((tl, GROUP_W), lambda b, l: (b * nl + l, 0)),
        out_shape=jax.ShapeDtypeStruct((bsz * seq, GROUP_W), F32),
        scratch_shapes=[
            pltpu.VMEM((nu, SUBLANES * (tc + SUBLANES), LANES), F32),
            pltpu.VMEM((tl, GROUP_W), F32),
            pltpu.VMEM((tl, n2), F32),
            pltpu.VMEM((SUBLANES, n2), F32),
            pltpu.VMEM((nu, tl, LANES), F32),
        ],
        compiler_params=_params("parallel", "arbitrary"),
        name="s5_mixer",
    )(*([proj] * nu), bmat, cmat, coef_r, coef_i, d_skip, w_glu)


def _t5_bucket(dist):
    n = np.maximum(dist, 0)
    max_exact = REL_BUCKETS // 2
    large = max_exact + (np.log(np.maximum(n, 1) / max_exact) / np.log(REL_MAX_DIST / max_exact)
                         * (REL_BUCKETS - max_exact)).astype(np.int64)
    large = np.minimum(large, REL_BUCKETS - 1)
    return np.where(n < max_exact, n, large).astype(np.int32)


ATT_NEAR_BLOCKS = 5
ATT_NEAR = ATT_NEAR_BLOCKS * ATT_BLOCK
ATT_FAR_DIL = 16


def _attention_bias_layout(sub):
    a = np.arange(ATT_BLOCK)[:, None]
    d_near = (ATT_NEAR - ATT_BLOCK) + a - np.arange(ATT_NEAR)[None, :]
    mult = np.zeros(d_near.shape, np.int64)
    for window, dil in DILATED_PATTERNS:
        if dil != ATT_FAR_DIL:
            mult += (d_near >= 0) & (d_near % dil == 0) & (d_near // dil <= window // dil)
    d_far = ATT_FAR_DIL * (a - np.arange(sub)[None, :])
    mult_far = ((d_far >= 0) & (d_far // ATT_FAR_DIL <= ATT_BLOCK)).astype(np.int64)
    d = np.concatenate([d_near, d_far], axis=1)
    mult = np.concatenate([mult, mult_far], axis=1)
    addend = np.where(mult > 0, np.log(np.maximum(mult, 1)), NEG_INF).astype(np.float32)
    return _t5_bucket(d), addend


def _attn_bias_kernel(rb_ref, bucket_ref, add_ref, o_ref):
    h = pl.program_id(0)
    bucket = bucket_ref[...]
    t = jnp.zeros(bucket.shape, F32)
    for b in range(REL_BUCKETS):
        t = jnp.where(bucket == b, rb_ref[b, h], t)
    o_ref[...] = (t + add_ref[...]) * LOG2E


def attention_bias_table(rel_bias, seq):
    sub = seq // ATT_FAR_DIL
    bucket, addend = _attention_bias_layout(sub)
    width = ATT_NEAR + sub
    full = pl.BlockSpec((ATT_BLOCK, width), lambda h: (0, 0))
    return pl.pallas_call(
        _attn_bias_kernel,
        grid=(ATT_HEADS,),
        in_specs=[pl.BlockSpec(memory_space=pltpu.SMEM), full, full],
        out_specs=pl.BlockSpec((None, ATT_BLOCK, width), lambda h: (h, 0, 0)),
        out_shape=jax.ShapeDtypeStruct((ATT_HEADS, ATT_BLOCK, width), F32),
        compiler_params=_params("arbitrary"),
        name="attn_bias",
    )(rel_bias, jnp.asarray(bucket), jnp.asarray(addend))


ATT_HEADS_PER_BLOCK = LANES // ATT_HEAD_DIM
ATT_GROUP = 8


def _nt_dot(a, b):
    return lax.dot_general(a, b, (((1,), (1,)), ((), ())), preferred_element_type=F32)


def _attn_kernel(q_ref, k_ref, v_ref, bt_ref, *rest, seq, n_cast):
    cast_in, o_ref, cast_out = rest[:n_cast], rest[n_cast], rest[n_cast + 1:2 * n_cast + 1]
    kb_ref, vb_ref, m3_ref, l3_ref, a3_ref = rest[2 * n_cast + 1:]
    sub = seq // ATT_FAR_DIL
    pad = ATT_NEAR - ATT_BLOCK
    scale = ATT_HEAD_DIM ** -0.5 * LOG2E
    lane = lax.broadcasted_iota(jnp.int32, (1, LANES), 1)
    in_head = [(lane >= hh * ATT_HEAD_DIM) & (lane < (hh + 1) * ATT_HEAD_DIM)
               for hh in range(ATT_HEADS_PER_BLOCK)]

    def by_head(vals):
        out = vals[-1]
        for hh in range(ATT_HEADS_PER_BLOCK - 2, -1, -1):
            out = jnp.where(in_head[hh], vals[hh], out)
        return out

    def softmax_pv(tiles):
        heads = range(ATT_HEADS_PER_BLOCK)
        s = [[_nt_dot(jnp.where(in_head[hh], qs, 0.0).astype(BF16), kb) + bias_of(hh) for hh in heads]
             for qs, kb, _, bias_of in tiles]
        m = [[jnp.max(s_h, axis=-1, keepdims=True) for s_h in s_t] for s_t in s]
        p = [[jnp.exp2(s_h - m_h) for s_h, m_h in zip(s_t, m_t)] for s_t, m_t in zip(s, m)]
        acc = [[jnp.dot(p_h.astype(BF16), tile[2](hh), preferred_element_type=F32) for hh, p_h in enumerate(p_t)]
               for p_t, tile in zip(p, tiles)]
        out = []
        for tile, m_t, acc_t in zip(tiles, m, acc):
            full = (tile[0].shape[0], LANES)
            den = pltpu.roll(by_head(acc_t[::-1]), ATT_HEAD_DIM, 1)
            out.append((by_head([jnp.broadcast_to(x, full) for x in m_t]), den, by_head(acc_t)))
        return out

    for src, dst in zip(cast_in, cast_out):
        dst[...] = src[...].astype(BF16)

    def with_ones(v, hh):
        return jnp.where(in_head[hh], v, 1.0).astype(BF16)

    kb_ref[...] = k_ref[...].astype(BF16)
    for hh in range(ATT_HEADS_PER_BLOCK):
        vb_ref[hh] = with_ones(v_ref[...], hh)

    for g0 in range(0, ATT_FAR_DIL, ATT_GROUP):
        group = [pl.ds(r, sub, stride=ATT_FAR_DIL) for r in range(g0, g0 + ATT_GROUP)]
        far_bias = lambda hh: bt_ref[hh, 0:sub, ATT_NEAR:ATT_NEAR + sub]
        tiles = [(q_ref[rows, :] * scale, k_ref[rows, :].astype(BF16),
                  lambda hh, rows=rows: with_ones(v_ref[rows, :], hh), far_bias) for rows in group]
        for rows, (m, l, acc) in zip(group, softmax_pv(tiles)):
            m3_ref[rows, :] = m
            l3_ref[rows, :] = l
            a3_ref[rows, :] = acc

    nblocks = seq // ATT_BLOCK
    for g0 in range(0, nblocks, ATT_GROUP):
        group, tiles = [], []
        for i in range(g0, min(g0 + ATT_GROUP, nblocks)):
            r0 = i * ATT_BLOCK
            lo = max(0, r0 - pad)
            c0 = ATT_NEAR - (r0 + ATT_BLOCK - lo)
            group.append(slice(r0, r0 + ATT_BLOCK))
            tiles.append((q_ref[group[-1], :] * scale, kb_ref[lo:r0 + ATT_BLOCK, :],
                          lambda hh, lo=lo, r0=r0: vb_ref[hh, lo:r0 + ATT_BLOCK, :],
                          lambda hh, c0=c0: bt_ref[hh, :, c0:ATT_NEAR]))
        merged = []
        for rows, (m, l, acc) in zip(group, softmax_pv(tiles)):
            m3 = m3_ref[rows, :]
            mm = jnp.maximum(m, m3)
            wn = jnp.exp2(m - mm)
            wf = jnp.exp2(m3 - mm)
            num = acc * wn + a3_ref[rows, :] * wf
            den = l * wn + l3_ref[rows, :] * wf
            merged.append(num / den)
        for rows, o in zip(group, merged):
            o_ref[rows, :] = o


def dilated_attention(proj, bias_table, casts, bsz, seq):
    assert seq % ATT_BLOCK == 0 and seq // ATT_FAR_DIL <= ATT_BLOCK
    nhp = ATT_HEADS // ATT_HEADS_PER_BLOCK
    steps = bsz * nhp
    qcol = 1 * GROUP_W // LANES
    kcol = 2 * GROUP_W // LANES
    vcol = 3 * GROUP_W // LANES
    width = bias_table.shape[-1]
    stat = pltpu.VMEM((seq, LANES), F32)
    half = pltpu.VMEM((seq, LANES), BF16)
    vvar = pltpu.VMEM((ATT_HEADS_PER_BLOCK, seq, LANES), BF16)
    cast_in, cast_out, cast_shapes = [], [], []
    for w, layer in casts:
        rows, cols = w.shape[1:]
        assert rows % steps == 0
        cast_in.append(pl.BlockSpec((None, rows // steps, cols), lambda b, h, layer=layer: (layer, b * nhp + h, 0)))
        cast_out.append(pl.BlockSpec((rows // steps, cols), lambda b, h: (b * nhp + h, 0)))
        cast_shapes.append(jax.ShapeDtypeStruct((rows, cols), BF16))
    return pl.pallas_call(
        functools.partial(_attn_kernel, seq=seq, n_cast=len(casts)),
        grid=(bsz, nhp),
        in_specs=[
            pl.BlockSpec((seq, LANES), lambda b, h: (b, qcol + h)),
            pl.BlockSpec((seq, LANES), lambda b, h: (b, kcol + h)),
            pl.BlockSpec((seq, LANES), lambda b, h: (b, vcol + h)),
            pl.BlockSpec((ATT_HEADS_PER_BLOCK, ATT_BLOCK, width), lambda b, h: (h, 0, 0)),
        ] + cast_in,
        out_specs=[pl.BlockSpec((seq, LANES), lambda b, h: (b, h))] + cast_out,
        out_shape=[jax.ShapeDtypeStruct((bsz * seq, GROUP_W), F32)] + cast_shapes,
        scratch_shapes=[half, vvar, stat, stat, stat],
        compiler_params=_params("parallel", "parallel"),
        name="dilated_attention",
    )(proj, proj, proj, bias_table, *[w for w, _ in casts])


def _mem_kv_kernel(mem_ref, g_ref, wk_ref, wv_ref, w0_ref, k_ref, v_ref, w0b_ref):
    mn = _rms(mem_ref[...], g_ref[...]).astype(BF16)
    k_ref[...] = jnp.dot(mn, wk_ref[...].astype(BF16), preferred_element_type=F32).astype(BF16)
    v_ref[...] = jnp.dot(mn, wv_ref[...].astype(BF16), preferred_element_type=F32).astype(BF16)
    w0b_ref[...] = w0_ref[...].astype(BF16)


def mem_kv(mem, g, wk, wv, w_in):
    m, d = mem.shape
    depth = wk.shape[0]
    rows, cols = w_in.shape[1:]
    assert rows % depth == 0
    wspec = pl.BlockSpec((None, d, X_WIDTH), lambda l: (l, 0, 0))
    ospec = pl.BlockSpec((None, m, X_WIDTH), lambda l: (l, 0, 0))
    return pl.pallas_call(
        _mem_kv_kernel,
        grid=(depth,),
        in_specs=[pl.BlockSpec((m, d), lambda l: (0, 0)), pl.BlockSpec((1, d), lambda l: (0, 0)),
                  wspec, wspec,
                  pl.BlockSpec((None, rows // depth, cols), lambda l: (0, l, 0))],
        out_specs=[ospec, ospec, pl.BlockSpec((rows // depth, cols), lambda l: (l, 0))],
        out_shape=[jax.ShapeDtypeStruct((depth, m, X_WIDTH), BF16)] * 2
        + [jax.ShapeDtypeStruct((rows, cols), BF16)],
        compiler_params=_params("arbitrary"),
        name="mem_kv",
    )(mem, g, wk, wv, w_in)


def _mix_xattn_kernel(ya_ref, yb_ref, yc_ref, yd_ref, gg_ref, wout_ref, h_ref, gx_ref, wq_ref, k_ref, v_ref,
                      wo_ref, o_ref):
    yn = [_rms(y_ref[...], gg_ref[:, gi * GROUP_W:(gi + 1) * GROUP_W]).astype(BF16)
          for gi, y_ref in enumerate((ya_ref, yb_ref, yc_ref, yd_ref))]
    h = h_ref[...] + jnp.dot(jnp.concatenate(yn, axis=-1), wout_ref[...], preferred_element_type=F32)

    hn = _rms(h, gx_ref[...]).astype(BF16)
    q = jnp.dot(hn, wq_ref[...], preferred_element_type=F32) * (X_HEAD_DIM ** -0.5 * LOG2E)
    cols = [slice(hd * X_HEAD_DIM, (hd + 1) * X_HEAD_DIM) for hd in range(X_HEADS)]
    qb = q.astype(BF16)
    s = [_nt_dot(qb[:, c], k_ref[:, c]) for c in cols]
    m = [jnp.max(x, axis=-1, keepdims=True) for x in s]
    p = [jnp.exp2(x - mx) for x, mx in zip(s, m)]
    den = [jnp.sum(x, axis=-1, keepdims=True) for x in p]
    outs = [jnp.dot(x.astype(BF16), v_ref[:, c], preferred_element_type=F32) / dx for x, dx, c in zip(p, den, cols)]
    o = jnp.concatenate(outs, axis=-1).astype(BF16)
    o_ref[...] = h + jnp.dot(o, wo_ref[...], preferred_element_type=F32)


def mix_xattn(ya, yb, yc, yd, grp_g, w_out, h, norm_g, wq, k, v, wo, layer, bsz, seq, tm=512):
    d = h.shape[-1]
    nl = seq // tm
    mlen = k.shape[1] // bsz
    row = lambda b, l: (b * nl + l, 0)
    yspec = pl.BlockSpec((tm, GROUP_W), row)
    gain = pl.BlockSpec((None, 1, d), lambda b, l: (layer, 0, 0))
    whole = lambda shape: pl.BlockSpec(shape, lambda b, l: (0, 0), pipeline_mode=pl.Buffered(1))
    kv = pl.BlockSpec((None, mlen, X_WIDTH), lambda b, l: (layer, b, 0))
    return pl.pallas_call(
        _mix_xattn_kernel,
        grid=(bsz, nl),
        in_specs=[yspec, yspec, yspec, yspec, gain, whole((d, d)), pl.BlockSpec((tm, d), row),
                  gain, whole((d, X_WIDTH)), kv, kv, whole((X_WIDTH, d))],
        out_specs=pl.BlockSpec((tm, d), row),
        out_shape=jax.ShapeDtypeStruct(h.shape, F32),
        compiler_params=_params("parallel", "parallel"),
        name="mix_xattn",
    )(ya, yb, yc, yd, grp_g, w_out, h, norm_g, wq, k, v, wo)


def _mlp_kernel(h_ref, g_ref, wu_ref, wd_ref, *rest, out_norm):
    o_ref, hn_ref = rest[-2:]
    j = pl.program_id(1)

    @pl.when(j == 0)
    def _():
        h = h_ref[...]
        hn_ref[...] = _rms(h, g_ref[...]).astype(BF16)
        o_ref[...] = h

    a = jnp.dot(hn_ref[...], wu_ref[...], preferred_element_type=F32)
    a = jnp.square(jnp.maximum(a, 0.0)).astype(BF16)
    o_ref[...] += jnp.dot(a, wd_ref[...], preferred_element_type=F32)

    if out_norm:
        @pl.when(j == pl.num_programs(1) - 1)
        def _():
            o_ref[...] = _rms(o_ref[...], rest[0][...])


def mlp(h, g, wu, wd, layer, out_norm_g=None, tm=1024, tf=512):
    m, d = h.shape
    f = wu.shape[-1]
    out_norm = out_norm_g is not None
    extra_specs = [pl.BlockSpec((1, d), lambda i, j: (0, 0))] if out_norm else []
    extra_args = (out_norm_g,) if out_norm else ()
    return pl.pallas_call(
        functools.partial(_mlp_kernel, out_norm=out_norm),
        grid=(m // tm, f // tf),
        in_specs=[
            pl.BlockSpec((tm, d), lambda i, j: (i, 0)),
            pl.BlockSpec((None, 1, d), lambda i, j: (layer, 0, 0)),
            pl.BlockSpec((d, tf), lambda i, j: (0, j)),
            pl.BlockSpec((tf, d), lambda i, j: (j, 0)),
        ] + extra_specs,
        out_specs=pl.BlockSpec((tm, d), lambda i, j: (i, 0)),
        out_shape=jax.ShapeDtypeStruct((m, d), F32),
        scratch_shapes=[pltpu.VMEM((tm, d), BF16)],
        compiler_params=_params("parallel", "arbitrary"),
        name="mlp",
    )(h, g, wu, wd, *extra_args)


def kernel(x, mem, rel_bias, mem_norm_g, norm_mix_g, w_in, s5_lam_re, s5_lam_im, s5_log_dt, s5_b_re, s5_b_im, s5_c_re, s5_c_im, s5_d, s5_w_glu, pool_w, pool_scale, conv_w_dw, conv_b_dw, conv_ln_g, conv_ln_b, conv_w_pw, grp_norm_g, w_out, norm_x_g, w_xq, w_xk, w_xv, w_xo, norm_mlp_g, w_up, w_down, norm_final_g):
    bsz, seq, d = x.shape
    depth = w_in.shape[0]
    vec = lambda a: a[:, None, :]

    k_mem, v_mem, w_in_b = mem_kv(mem.reshape(bsz * mem.shape[1], d), mem_norm_g[None, :], w_xk, w_xv, w_in)
    pow_r, pow_i, bmat, cmat = s5_params(s5_lam_re, s5_lam_im, s5_log_dt, s5_b_re, s5_b_im, s5_c_re, s5_c_im,
                                         chunk=S5_BLOCK // SUBLANES)
    coef_r, coef_i = s5_scan_coefficients(pow_r, pow_i)
    bias_table = attention_bias_table(rel_bias, seq)

    norm_mix, s5_dv, pool_sc = vec(norm_mix_g), vec(s5_d), vec(pool_scale)
    b_dw, ln_g, ln_b = vec(conv_b_dw), vec(conv_ln_g), vec(conv_ln_b)
    grp_g, norm_x, norm_mlp = vec(grp_norm_g), vec(norm_x_g), vec(norm_mlp_g)

    h = x.reshape(bsz * seq, d)
    for l in range(depth):
        proj, y_c, y_b = proj_conv(h, norm_mix, w_in_b, conv_w_dw, b_dw, ln_g, ln_b, conv_w_pw,
                                   pool_w, pool_sc, l, bsz, seq)
        y_a = s5_mixer(proj, bmat, cmat, coef_r, coef_i, s5_dv, s5_w_glu, l, bsz, seq)
        casts = ([(w_out, l), (w_xq, l), (w_xo, l), (w_up, l), (w_down, l)]
                 + ([(w_in, l + 1)] if l + 1 < depth else []))
        y_d, w_out_b, w_xq_b, w_xo_b, w_up_b, w_down_b, *nxt = dilated_attention(proj, bias_table, casts, bsz, seq)
        w_in_b = nxt[0] if nxt else None
        h = mix_xattn(y_a, y_b, y_c, y_d, grp_g, w_out_b, h, norm_x, w_xq_b, k_mem, v_mem, w_xo_b, l, bsz, seq)
        h = mlp(h, norm_mlp, w_up_b, w_down_b, l,
                out_norm_g=norm_final_g[None, :] if l == depth - 1 else None)
    return h.reshape(bsz, seq, d)
```
